```python
import math
import jax
import jax.numpy as jnp
from jax import lax
import numpy as np

D_MODEL = 1024
BATCH = 16
SEQ = 2048
DEPTH = 1

PLE_DIM = 256
SSM_D_INNER = 1024
SSM_HEAD_DIM = 64
SSM_HEADS = SSM_D_INNER // SSM_HEAD_DIM
SSM_GROUPS = 4
SSM_STATE = 128
SSM_CONV = 4
SSM_CHUNK = 128
SSM_XBC_DIM = SSM_D_INNER + 2 * SSM_GROUPS * SSM_STATE
LSTM_HEADS = 8
LSTM_QK_DIM = 64
LSTM_V_DIM = 128
LSTM_D_QK = LSTM_HEADS * LSTM_QK_DIM
LSTM_D_V = LSTM_HEADS * LSTM_V_DIM
LSTM_CHUNK = 128
MOE_GROUPS = 8
MOE_EXPERTS_PER_GROUP = 8
MOE_EXPERTS = MOE_GROUPS * MOE_EXPERTS_PER_GROUP
MOE_TOP_K = 2
MOE_D_FF = 512
MOE_BLOCK = 256
NORM_EPS = 1e-5
DEEPNORM_ALPHA = (2.0 * DEPTH) ** 0.25
DEEPNORM_BETA = (8.0 * DEPTH) ** -0.25
IN_PROJ_SIZES = (SSM_D_INNER, SSM_XBC_DIM, SSM_HEADS,
                 LSTM_D_QK, LSTM_D_QK, LSTM_D_V, LSTM_D_V, LSTM_HEADS, LSTM_HEADS,
                 D_MODEL, D_MODEL)
IN_PROJ_DIM = sum(IN_PROJ_SIZES)
IN_PROJ_SPLITS = tuple(int(v) for v in np.cumsum(IN_PROJ_SIZES)[:-1])

kernel_name = 'hybrid_ssd_mlstm_hmoe_deepnorm_block'


def layer_norm(x, g, b):
    xf = x.astype(jnp.float32)
    mu = jnp.mean(xf, -1, keepdims=True)
    xc = xf - mu
    var = jnp.mean(xc * xc, -1, keepdims=True)
    y = xc * lax.rsqrt(var + NORM_EPS) * g.astype(jnp.float32) + b.astype(jnp.float32)
    return y.astype(x.dtype)


def group_rms_norm(x, w, n_groups):
    shp = x.shape
    xg = x.reshape(shp[:-1] + (n_groups, shp[-1] // n_groups))
    xg = xg * lax.rsqrt(jnp.mean(xg * xg, -1, keepdims=True) + NORM_EPS)
    return xg.reshape(shp) * w.astype(jnp.float32)


def causal_depthwise_conv(u, w, b):
    y = lax.conv_general_dilated(u, w[:, None, :].astype(u.dtype), window_strides=(1,),
                                 padding=((SSM_CONV - 1, 0),),
                                 dimension_numbers=('NWC', 'WIO', 'NWC'),
                                 feature_group_count=u.shape[-1])
    return y + b.astype(u.dtype)


def ssd_chunked(xh, dt, a, bm, cm):
    b, s, h, p = xh.shape
    g, n = bm.shape[2], bm.shape[3]
    j = h // g
    L = SSM_CHUNK
    c = s // L
    xdt = (xh * dt[..., None]).reshape(b, c, L, g, j, p)
    a_cs = jnp.cumsum(jnp.moveaxis((dt * a).reshape(b, c, L, g, j), 2, -1), axis=-1)
    bc = bm.reshape(b, c, L, g, n)
    cc = cm.reshape(b, c, L, g, n)
    causal = jnp.tril(jnp.ones((L, L), bool))
    decay = jnp.exp(jnp.where(causal, a_cs[..., :, None] - a_cs[..., None, :], -jnp.inf))
    cb = jnp.einsum('bclgn,bcsgn->bcgls', cc, bc)
    y_diag = jnp.einsum('bcgjls,bcsgjp->bclgjp', cb[:, :, :, None] * decay, xdt)
    decay_to_end = jnp.exp(a_cs[..., -1:] - a_cs)
    chunk_states = jnp.einsum('bcsgn,bcgjs,bcsgjp->bcgjpn', bc, decay_to_end, xdt)
    chunk_decay = jnp.exp(a_cs[..., -1])

    def chunk_step(state, inp):
        st, dec = inp
        return state * dec[..., None, None] + st, state

    init = jnp.zeros((b, g, j, p, n), xh.dtype)
    _, prev = lax.scan(chunk_step, init, (jnp.moveaxis(chunk_states, 1, 0), jnp.moveaxis(chunk_decay, 1, 0)))
    prev = jnp.moveaxis(prev, 0, 1)
    y_off = jnp.einsum('bclgn,bcgjpn,bcgjl->bclgjp', cc, prev, jnp.exp(a_cs))
    return (y_diag + y_off).reshape(b, s, h, p)


def mamba2_mixer(z, xbc, dt_raw, conv_w, conv_b, dt_bias, a_log, d_skip, norm_w):
    f32 = jnp.float32
    b, s, _ = z.shape
    xbc = jax.nn.silu(causal_depthwise_conv(xbc, conv_w, conv_b)).astype(f32)
    xs, bm, cm = jnp.split(xbc, (SSM_D_INNER, SSM_D_INNER + SSM_GROUPS * SSM_STATE), axis=-1)
    xh = xs.reshape(b, s, SSM_HEADS, SSM_HEAD_DIM)
    dt = jax.nn.softplus(dt_raw.astype(f32) + dt_bias.astype(f32))
    a = -jnp.exp(a_log.astype(f32))
    y = ssd_chunked(xh, dt, a, bm.reshape(b, s, SSM_GROUPS, SSM_STATE), cm.reshape(b, s, SSM_GROUPS, SSM_STATE))
    y = y + d_skip.astype(f32)[:, None] * xh
    y = y.reshape(b, s, SSM_D_INNER) * jax.nn.silu(z.astype(f32))
    return group_rms_norm(y, norm_w, SSM_GROUPS).astype(z.dtype)


def mlstm_mixer(q, k, v, o_raw, i_raw, f_raw, i_bias, f_bias, norm_w):
    f32 = jnp.float32
    b, s, _ = q.shape
    H, L = LSTM_HEADS, LSTM_CHUNK
    c = s // L
    q = q.astype(f32).reshape(b, c, L, H, LSTM_QK_DIM) * (LSTM_QK_DIM ** -0.5)
    k = k.astype(f32).reshape(b, c, L, H, LSTM_QK_DIM)
    v = v.astype(f32).reshape(b, c, L, H, LSTM_V_DIM)
    log_i = jnp.swapaxes((i_raw.astype(f32) + i_bias.astype(f32)).reshape(b, c, L, H), 2, 3)
    log_f = jnp.swapaxes(jax.nn.log_sigmoid(f_raw.astype(f32) + f_bias.astype(f32)).reshape(b, c, L, H), 2, 3)
    fcum = jnp.cumsum(log_f, -1)
    causal = jnp.tril(jnp.ones((L, L), bool))
    log_d = jnp.where(causal, fcum[..., :, None] - fcum[..., None, :] + log_i[..., None, :], -jnp.inf)
    a_end = fcum[..., -1:] - fcum + log_i
    m_loc = jnp.max(a_end, -1)
    w_end = jnp.exp(a_end - m_loc[..., None])
    c_loc = jnp.einsum('bchs,bcshk,bcshv->bchkv', w_end, k, v)
    n_loc = jnp.einsum('bchs,bcshk->bchk', w_end, k)
    f_tot = fcum[..., -1]

    def chunk_step(carry, inp):
        c_st, n_st, m_st = carry
        c_l, n_l, m_l, f_l = inp
        m_new = jnp.maximum(f_l + m_st, m_l)
        s_prev = jnp.exp(f_l + m_st - m_new)
        s_loc = jnp.exp(m_l - m_new)
        c_new = s_prev[..., None, None] * c_st + s_loc[..., None, None] * c_l
        n_new = s_prev[..., None] * n_st + s_loc[..., None] * n_l
        return (c_new, n_new, m_new), (c_st, n_st, m_st)

    init = (jnp.zeros((b, H, LSTM_QK_DIM, LSTM_V_DIM), f32),
            jnp.zeros((b, H, LSTM_QK_DIM), f32),
            jnp.full((b, H), -jnp.inf, f32))
    _, (c_prev, n_prev, m_prev) = lax.scan(
        chunk_step, init,
        (jnp.moveaxis(c_loc, 1, 0), jnp.moveaxis(n_loc, 1, 0), jnp.moveaxis(m_loc, 1, 0), jnp.moveaxis(f_tot, 1, 0)))
    c_prev = jnp.moveaxis(c_prev, 0, 1)
    n_prev = jnp.moveaxis(n_prev, 0, 1)
    m_prev = jnp.moveaxis(m_prev, 0, 1)
    inter_log = fcum + m_prev[..., None]
    m_t = jnp.maximum(inter_log, jnp.max(log_d, -1))
    scores = jnp.einsum('bclhk,bcshk->bchls', q, k) * jnp.exp(log_d - m_t[..., None])
    inter_w = jnp.exp(inter_log - m_t)
    num = (jnp.einsum('bchls,bcshv->bclhv', scores, v)
           + jnp.einsum('bchl,bclhk,bchkv->bclhv', inter_w, q, c_prev))
    den = jnp.sum(scores, -1) + inter_w * jnp.einsum('bclhk,bchk->bchl', q, n_prev)
    denom = jnp.maximum(jnp.abs(den), jnp.exp(-m_t))
    h = (num / jnp.swapaxes(denom, 2, 3)[..., None]).reshape(b, s, LSTM_D_V)
    h = jax.nn.sigmoid(o_raw.astype(f32)) * group_rms_norm(h, norm_w, H)
    return h.astype(o_raw.dtype)


def hierarchical_moe(x, w_group, b_group, w_expert, b_expert, w_gate, w_up, w_down):
    f32 = jnp.float32
    b, s, d = x.shape
    t = x.reshape(b * s, d)
    n_tok = t.shape[0]
    grp_prob = jax.nn.softmax((t @ w_group).astype(f32) + b_group.astype(f32), -1)
    grp_p, grp_idx = lax.top_k(grp_prob, 1)
    exp_logits = ((t @ w_expert).astype(f32) + b_expert.astype(f32)).reshape(n_tok, MOE_GROUPS, MOE_EXPERTS_PER_GROUP)
    in_grp = jnp.take_along_axis(exp_logits, grp_idx[:, :, None], axis=1)[:, 0]
    top_logit, top_local = lax.top_k(in_grp, MOE_TOP_K)
    gate = jax.nn.softmax(top_logit, -1) * grp_p
    flat_e = (grp_idx * MOE_EXPERTS_PER_GROUP + top_local).reshape(-1)
    n_rows = flat_e.shape[0]
    order = jnp.argsort(flat_e)
    sorted_e = flat_e[order]
    sorted_tok = (order // MOE_TOP_K).astype(jnp.int32)
    sorted_gate = gate.reshape(-1)[order]
    counts = jnp.bincount(flat_e, length=MOE_EXPERTS)
    starts = jnp.cumsum(counts) - counts
    blocks = (counts + MOE_BLOCK - 1) // MOE_BLOCK
    block_end = jnp.cumsum(blocks)
    block_start = block_end - blocks
    slot = block_start[sorted_e] * MOE_BLOCK + jnp.arange(n_rows) - starts[sorted_e]
    n_blocks = -(-n_rows // MOE_BLOCK) + MOE_EXPERTS
    pad_tok = jnp.full((n_blocks * MOE_BLOCK,), n_tok, jnp.int32).at[slot].set(sorted_tok)
    pad_gate = jnp.zeros((n_blocks * MOE_BLOCK,), x.dtype).at[slot].set(sorted_gate.astype(x.dtype))
    block_e = jnp.minimum(jnp.searchsorted(block_end, jnp.arange(n_blocks), side='right'), MOE_EXPERTS - 1)
    t_pad = jnp.concatenate([t, jnp.zeros((1, d), t.dtype)], 0)
    x_blocks = t_pad[pad_tok].reshape(n_blocks, MOE_BLOCK, d)

    def expert_block(args):
        xb, e = args
        hid = jax.nn.silu(xb @ w_gate[e]) * (xb @ w_up[e])
        return hid @ w_down[e]

    y_blocks = lax.map(expert_block, (x_blocks, block_e)).reshape(n_blocks * MOE_BLOCK, d)
    out = jnp.zeros((n_tok + 1, d), x.dtype).at[pad_tok].add(y_blocks * pad_gate[:, None])
    return out[:n_tok].reshape(b, s, d)


def setup_inputs(seed: int = 0) -> dict:
    key = jax.random.key(seed)
    ks = jax.random.split(key, 32)
    f32 = jnp.float32

    def nrm(k, shape, scale):
        return jax.random.normal(k, shape, f32) * scale

    L = DEPTH
    x = nrm(ks[0], (BATCH, SEQ, D_MODEL), 1.0)
    p = nrm(ks[1], (DEPTH, BATCH, SEQ, PLE_DIM), 1.0)
    w_in = nrm(ks[2], (L, D_MODEL, IN_PROJ_DIM), D_MODEL ** -0.5)
    ssm_conv_w = nrm(ks[3], (L, SSM_CONV, SSM_XBC_DIM), SSM_CONV ** -0.5)
    ssm_conv_b = nrm(ks[4], (L, SSM_XBC_DIM), 0.02)
    dt0 = jnp.exp(jax.random.uniform(ks[5], (L, SSM_HEADS), f32, math.log(1e-3), math.log(1e-1)))
    ssm_dt_bias = dt0 + jnp.log(-jnp.expm1(-dt0))
    ssm_a_log = jnp.log(jax.random.uniform(ks[6], (L, SSM_HEADS), f32, 1.0, 16.0))
    ssm_d = 1.0 + nrm(ks[7], (L, SSM_HEADS), 0.1)
    ssm_norm_w = 1.0 + nrm(ks[8], (L, SSM_D_INNER), 0.02)
    lstm_i_bias = nrm(ks[9], (L, LSTM_HEADS), 0.1)
    lstm_f_bias = jnp.linspace(3.0, 6.0, LSTM_HEADS, dtype=f32)[None, :] + nrm(ks[10], (L, LSTM_HEADS), 0.1)
    lstm_norm_w = 1.0 + nrm(ks[11], (L, LSTM_D_V), 0.02)
    w_branch_ssm = nrm(ks[12], (L, SSM_D_INNER, D_MODEL), SSM_D_INNER ** -0.5 * DEEPNORM_BETA)
    w_branch_lstm = nrm(ks[13], (L, LSTM_D_V, D_MODEL), LSTM_D_V ** -0.5 * DEEPNORM_BETA)
    w_out = nrm(ks[14], (L, D_MODEL, D_MODEL), D_MODEL ** -0.5 * DEEPNORM_BETA)
    ln1_g = 1.0 + nrm(ks[15], (L, D_MODEL), 0.02)
    ln1_b = nrm(ks[16], (L, D_MODEL), 0.02)
    moe_w_group = nrm(ks[17], (L, D_MODEL, MOE_GROUPS), D_MODEL ** -0.5)
    moe_b_group = nrm(ks[18], (L, MOE_GROUPS), 0.01)
    moe_w_expert = nrm(ks[19], (L, D_MODEL, MOE_EXPERTS), D_MODEL ** -0.5)
    moe_b_expert = nrm(ks[20], (L, MOE_EXPERTS), 0.01)
    moe_w_gate = nrm(ks[21], (L, MOE_EXPERTS, D_MODEL, MOE_D_FF), D_MODEL ** -0.5 * DEEPNORM_BETA)
    moe_w_up = nrm(ks[22], (L, MOE_EXPERTS, D_MODEL, MOE_D_FF), D_MODEL ** -0.5 * DEEPNORM_BETA)
    moe_w_down = nrm(ks[23], (L, MOE_EXPERTS, MOE_D_FF, D_MODEL), MOE_D_FF ** -0.5 * DEEPNORM_BETA)
    ln2_g = 1.0 + nrm(ks[24], (L, D_MODEL), 0.02)
    ln2_b = nrm(ks[25], (L, D_MODEL), 0.02)
    ple_w_proj = nrm(ks[26], (L, PLE_DIM, D_MODEL), PLE_DIM ** -0.5)
    ple_w_gate = nrm(ks[27], (L, D_MODEL, D_MODEL), D_MODEL ** -0.5)
    return {'x': x, 'p': p, 'w_in': w_in,
            'ssm_conv_w': ssm_conv_w, 'ssm_conv_b': ssm_conv_b, 'ssm_dt_bias': ssm_dt_bias,
            'ssm_a_log': ssm_a_log, 'ssm_d': ssm_d, 'ssm_norm_w': ssm_norm_w,
            'lstm_i_bias': lstm_i_bias, 'lstm_f_bias': lstm_f_bias, 'lstm_norm_w': lstm_norm_w,
            'w_branch_ssm': w_branch_ssm, 'w_branch_lstm': w_branch_lstm, 'w_out': w_out,
            'ln1_g': ln1_g, 'ln1_b': ln1_b,
            'moe_w_group': moe_w_group, 'moe_b_group': moe_b_group,
            'moe_w_expert': moe_w_expert, 'moe_b_expert': moe_b_expert,
            'moe_w_gate': moe_w_gate, 'moe_w_up': moe_w_up, 'moe_w_down': moe_w_down,
            'ln2_g': ln2_g, 'ln2_b': ln2_b,
            'ple_w_proj': ple_w_proj, 'ple_w_gate': ple_w_gate}


def reference(x, p, w_in, ssm_conv_w, ssm_conv_b, ssm_dt_bias, ssm_a_log, ssm_d, ssm_norm_w,
              lstm_i_bias, lstm_f_bias, lstm_norm_w, w_branch_ssm, w_branch_lstm, w_out,
              ln1_g, ln1_b, moe_w_group, moe_b_group, moe_w_expert, moe_b_expert,
              moe_w_gate, moe_w_up, moe_w_down, ln2_g, ln2_b, ple_w_proj, ple_w_gate):
    for i in range(DEPTH):
        proj = x @ w_in[i]
        (z, xbc, dt_raw, q, k, v, o_raw, i_raw, f_raw, g_ssm, g_lstm) = jnp.split(proj, IN_PROJ_SPLITS, axis=-1)
        y_ssm = mamba2_mixer(z, xbc, dt_raw, ssm_conv_w[i], ssm_conv_b[i], ssm_dt_bias[i],
                             ssm_a_log[i], ssm_d[i], ssm_norm_w[i])
        y_lstm = mlstm_mixer(q, k, v, o_raw, i_raw, f_raw, lstm_i_bias[i], lstm_f_bias[i], lstm_norm_w[i])
        merged = (jax.nn.sigmoid(g_ssm) * (y_ssm @ w_branch_ssm[i])
                  + jax.nn.sigmoid(g_lstm) * (y_lstm @ w_branch_lstm[i]))
        x = layer_norm(DEEPNORM_ALPHA * x + merged @ w_out[i], ln1_g[i], ln1_b[i])
        moe = hierarchical_moe(x, moe_w_group[i], moe_b_group[i], moe_w_expert[i], moe_b_expert[i],
                               moe_w_gate[i], moe_w_up[i], moe_w_down[i])
        x = layer_norm(DEEPNORM_ALPHA * x + moe, ln2_g[i], ln2_b[i])
        x = x + jax.nn.sigmoid(x @ ple_w_gate[i]) * (p[i] @ ple_w_proj[i])
    return x
```

```python
import functools

import jax
import jax.numpy as jnp
from jax import lax
from jax.experimental import pallas as pl
from jax.experimental.pallas import tpu as pltpu

F32 = jnp.float32
BF16 = jnp.bfloat16

D_MODEL = 1024
PLE_DIM = 256
SSM_D_INNER = 1024
SSM_HEAD_DIM = 64
SSM_HEADS = 16
SSM_GROUPS = 4
SSM_STATE = 128
SSM_CONV = 4
SSM_XBC = SSM_D_INNER + 2 * SSM_GROUPS * SSM_STATE
LSTM_HEADS = 8
LSTM_QK = 64
LSTM_V = 128
LSTM_D_QK = LSTM_HEADS * LSTM_QK
LSTM_D_V = LSTM_HEADS * LSTM_V
CHUNK = 128
MOE_GROUPS = 8
MOE_PER_GROUP = 8
MOE_EXPERTS = 64
MOE_D_FF = 512
NORM_EPS = 1e-5
IN_PROJ_SIZES = (SSM_D_INNER, SSM_XBC, SSM_HEADS, LSTM_D_QK, LSTM_D_QK, LSTM_D_V, LSTM_D_V,
                 LSTM_HEADS, LSTM_HEADS, D_MODEL, D_MODEL)

LANES = 128
SUBLANES = 8
VMEM_LIMIT = 56 * 1024 * 1024

MIX_ROWS = 256
DENSE_ROWS = 512
MOE_TILE = 2048
MOE_ROWS = 128
ROUTE_PAD = 1024


def _dot(a, b):
    return jnp.dot(a, b, preferred_element_type=F32)


def _dot_nt(a, b):
    return lax.dot_general(a, b, (((1,), (1,)), ((), ())), preferred_element_type=F32)


def _dot_tn(a, b):
    return lax.dot_general(a, b, (((0,), (0,)), ((), ())), preferred_element_type=F32)


def _sigmoid(x):
    return 1.0 / (1.0 + jnp.exp(-x))


def _softplus(x):
    return jnp.maximum(x, 0.0) + jnp.log(1.0 + jnp.exp(-jnp.abs(x)))


def _split3(x):
    hi = x.astype(BF16)
    r1 = x - hi.astype(F32)
    mid = r1.astype(BF16)
    lo = (r1 - mid.astype(F32)).astype(BF16)
    return hi, mid, lo


def _cumsum_rows(tril, x):
    hi, mid, lo = _split3(x)
    return _dot(tril, hi) + _dot(tril, mid) + _dot(tril, lo)


def _causal_masks():
    r = lax.broadcasted_iota(jnp.int32, (CHUNK, CHUNK), 0)
    c = lax.broadcasted_iota(jnp.int32, (CHUNK, CHUNK), 1)
    causal = r >= c
    return causal, jnp.where(causal, 1.0, 0.0).astype(BF16)


def _pair_cols(v, ha, hb, rows, lane):
    a = jnp.broadcast_to(v[:, ha:ha + 1], (rows, LANES))
    b = jnp.broadcast_to(v[:, hb:hb + 1], (rows, LANES))
    return jnp.where(lane < SSM_HEAD_DIM, a, b)


def _ssd_kernel(x_ref, wz_ref, wxbc_ref, wdt_ref, cw_ref, cb_ref, dtb_ref, aneg_ref, dexp_ref, nw_ref,
                y_ref, ext_ref, st_ref, *, rows):
    @pl.when(pl.program_id(1) == 0)
    def _():
        ext_ref[0:SUBLANES, :] = jnp.zeros((SUBLANES, SSM_XBC), F32)
        st_ref[...] = jnp.zeros(st_ref.shape, F32)

    xb = x_ref[...].astype(BF16)
    z = _dot(xb, wz_ref[...])
    ext_ref[SUBLANES:SUBLANES + rows, :] = _dot(xb, wxbc_ref[...])
    dtr = _dot(xb, wdt_ref[...])

    conv = cb_ref[...]
    for k in range(SSM_CONV):
        conv = conv + cw_ref[k:k + 1, :] * ext_ref[pl.ds(SUBLANES - SSM_CONV + 1 + k, rows), :]
    ext_ref[0:SUBLANES, :] = ext_ref[rows:rows + SUBLANES, :]
    u = conv * _sigmoid(conv)

    dt = _softplus(dtr + dtb_ref[...])
    da = dt * aneg_ref[...]
    causal, tril = _causal_masks()
    lane = lax.broadcasted_iota(jnp.int32, (CHUNK, LANES), 1)
    lane1 = lax.broadcasted_iota(jnp.int32, (1, LANES), 1)
    gw = SSM_D_INNER // SSM_GROUPS

    for ci in range(rows // CHUNK):
        r0 = ci * CHUNK
        dt_c = dt[r0:r0 + CHUNK]
        a_cs = _cumsum_rows(tril, da[r0:r0 + CHUNK])
        a_cs_t = a_cs.T
        dt_t = dt_c.T
        a_last = a_cs[CHUNK - 1:CHUNK, :]
        ea = jnp.exp(a_cs)
        sw = dt_c * jnp.exp(a_last - a_cs)
        ea_last = jnp.exp(a_last)
        for g in range(SSM_GROUPS):
            bm_g = u[r0:r0 + CHUNK, SSM_D_INNER + g * SSM_STATE:SSM_D_INNER + (g + 1) * SSM_STATE].astype(BF16)
            c0 = SSM_D_INNER + SSM_GROUPS * SSM_STATE
            cm_g = u[r0:r0 + CHUNK, c0 + g * SSM_STATE:c0 + (g + 1) * SSM_STATE].astype(BF16)
            xs_g = u[r0:r0 + CHUNK, g * gw:(g + 1) * gw]
            cb = _dot_nt(cm_g, bm_g)
            st_g = st_ref[:, g * gw:(g + 1) * gw]
            y_off = _dot(cm_g, st_g.astype(BF16))
            yd, ea_x, sw_x, eal_x = [], [], [], []
            for pr in range(2):
                ha = 4 * g + 2 * pr
                hb = ha + 1
                xs_p = xs_g[:, pr * LANES:(pr + 1) * LANES].astype(BF16)
                rhs = jnp.concatenate([jnp.where(lane < SSM_HEAD_DIM, xs_p, 0).astype(BF16),
                                       jnp.where(lane >= SSM_HEAD_DIM, xs_p, 0).astype(BF16)], axis=0)
                gs = []
                for h in (ha, hb):
                    colb = jnp.broadcast_to(a_cs[:, h:h + 1], (CHUNK, CHUNK))
                    rowb = jnp.broadcast_to(a_cs_t[h:h + 1, :], (CHUNK, CHUNK))
                    dec = jnp.exp(jnp.where(causal, colb - rowb, -jnp.inf))
                    gs.append((cb * dec * jnp.broadcast_to(dt_t[h:h + 1, :], (CHUNK, CHUNK))).astype(BF16))
                yd.append(_dot(jnp.concatenate(gs, axis=1), rhs))
                ea_x.append(_pair_cols(ea, ha, hb, CHUNK, lane))
                sw_x.append(_pair_cols(sw, ha, hb, CHUNK, lane))
                eal_x.append(_pair_cols(ea_last, ha, hb, 1, lane1))
            y_g = (jnp.concatenate(yd, axis=1) + y_off * jnp.concatenate(ea_x, axis=1)
                   + dexp_ref[:, g * gw:(g + 1) * gw] * xs_g)
            xw = (xs_g * jnp.concatenate(sw_x, axis=1)).astype(BF16)
            st_ref[:, g * gw:(g + 1) * gw] = st_g * jnp.concatenate(eal_x, axis=1) + _dot_tn(bm_g, xw)
            zz = z[r0:r0 + CHUNK, g * gw:(g + 1) * gw]
            yy = y_g * (zz * _sigmoid(zz))
            ms = jnp.mean(yy * yy, axis=-1, keepdims=True)
            y_ref[r0:r0 + CHUNK, g * gw:(g + 1) * gw] = (
                yy * lax.rsqrt(ms + NORM_EPS) * nw_ref[:, g * gw:(g + 1) * gw]).astype(BF16)


def _ssd_mixer(x, wz, wxbc, wdt, cw, cb, dtb, aneg, dexp, nw):
    b, s, d = x.shape
    rows = MIX_ROWS
    const = lambda shape: pl.BlockSpec(shape, lambda i, j: (0,) * len(shape))
    return pl.pallas_call(
        functools.partial(_ssd_kernel, rows=rows),
        grid=(b, s // rows),
        in_specs=[pl.BlockSpec((None, rows, d), lambda i, j: (i, j, 0)),
                  const(wz.shape), const(wxbc.shape), const(wdt.shape), const(cw.shape), const(cb.shape),
                  const(dtb.shape), const(aneg.shape), const(dexp.shape), const(nw.shape)],
        out_specs=pl.BlockSpec((None, rows, SSM_D_INNER), lambda i, j: (i, j, 0)),
        out_shape=jax.ShapeDtypeStruct((b, s, SSM_D_INNER), BF16),
        scratch_shapes=[pltpu.VMEM((SUBLANES + rows, SSM_XBC), F32),
                        pltpu.VMEM((SSM_STATE, SSM_D_INNER), F32)],
        compiler_params=pltpu.CompilerParams(dimension_semantics=("arbitrary", "arbitrary"),
                                             vmem_limit_bytes=VMEM_LIMIT),
        name="ssd_mixer",
    )(x, wz, wxbc, wdt, cw, cb, dtb, aneg, dexp, nw)


def _mlstm_kernel(x_ref, wq_ref, wk_ref, wv_ref, wo_ref, wif_ref, bif_ref, nw_ref,
                  y_ref, c_ref, m_ref, *, rows):
    @pl.when(pl.program_id(1) == 0)
    def _():
        c_ref[...] = jnp.zeros(c_ref.shape, F32)
        m_ref[...] = jnp.full(m_ref.shape, -jnp.inf, F32)

    xb = x_ref[...].astype(BF16)
    q = _dot(xb, wq_ref[...])
    k = _dot(xb, wk_ref[...])
    v = _dot(xb, wv_ref[...])
    o = _dot(xb, wo_ref[...])
    gif = _dot(xb, wif_ref[...]) + bif_ref[...]
    causal, tril = _causal_masks()
    lane = lax.broadcasted_iota(jnp.int32, (CHUNK, LANES), 1)
    lane1 = lax.broadcasted_iota(jnp.int32, (1, LANES), 1)
    row = lax.broadcasted_iota(jnp.int32, (LANES, 1), 0)
    head_lane = lane < LSTM_HEADS
    ones_col = jnp.where(lane == 0, 1.0, 0.0).astype(BF16)

    for ci in range(rows // CHUNK):
        r0 = ci * CHUNK
        log_i = gif[r0:r0 + CHUNK, 0:LANES]
        f_pre = gif[r0:r0 + CHUNK, LANES:2 * LANES]
        log_f = jnp.where(head_lane, jnp.minimum(f_pre, 0.0) - jnp.log(1.0 + jnp.exp(-jnp.abs(f_pre))), 0.0)
        fcum = _cumsum_rows(tril, log_f)
        f_tot = fcum[CHUNK - 1:CHUNK, :]
        a_end = f_tot - fcum + log_i
        m_loc = jnp.max(a_end, axis=0, keepdims=True)
        w_end = jnp.exp(a_end - m_loc)
        fcum_t = fcum.T
        g_t = (log_i - fcum).T
        m_prev = m_ref[...]
        m_new = jnp.maximum(f_tot + m_prev, m_loc)
        s_prev = jnp.exp(f_tot + m_prev - m_new)
        s_loc = jnp.exp(m_loc - m_new)
        m_ref[...] = jnp.where(lane1 < LSTM_HEADS, m_new, -jnp.inf)
        inter_log_all = fcum + m_prev

        for pr in range(LSTM_HEADS // 2):
            q_p = q[r0:r0 + CHUNK, pr * LANES:(pr + 1) * LANES]
            k_p = k[r0:r0 + CHUNK, pr * LANES:(pr + 1) * LANES]
            k_pb = k_p.astype(BF16)
            c_pair = c_ref[pr]
            c_pair_b = c_pair.astype(BF16)
            c_new = c_pair * jnp.where(row < LSTM_QK,
                                       jnp.broadcast_to(s_prev[:, 2 * pr:2 * pr + 1], (LANES, 1)),
                                       jnp.broadcast_to(s_prev[:, 2 * pr + 1:2 * pr + 2], (LANES, 1)))
            for hh in range(2):
                h = 2 * pr + hh
                in_head = (lane < LSTM_QK) if hh == 0 else (lane >= LSTM_QK)
                v_h = v[r0:r0 + CHUNK, h * LSTM_V:(h + 1) * LSTM_V].astype(BF16)
                v_ext = jnp.concatenate([v_h, ones_col], axis=1)
                q_m = jnp.where(in_head, q_p, 0.0)
                colb = jnp.broadcast_to(fcum[:, h:h + 1], (CHUNK, CHUNK))
                rowb = jnp.broadcast_to(g_t[h:h + 1, :], (CHUNK, CHUNK))
                log_d = jnp.where(causal, colb + rowb, -jnp.inf)
                inter_log = inter_log_all[:, h:h + 1]
                m_t = jnp.maximum(inter_log, jnp.max(log_d, axis=-1, keepdims=True))
                scores = _dot_nt(q_m.astype(BF16), k_pb) * jnp.exp(log_d - m_t)
                inter_w = jnp.exp(inter_log - m_t)
                lhs = jnp.concatenate([scores.astype(BF16), (q_m * inter_w).astype(BF16)], axis=1)
                rhs = jnp.concatenate([v_ext, c_pair_b], axis=0)
                res = _dot(lhs, rhs)
                num = res[:, 0:LSTM_V]
                den = res[:, LSTM_V:LSTM_V + 1]
                hv = num / jnp.maximum(jnp.abs(den), jnp.exp(-m_t))
                ms = jnp.mean(hv * hv, axis=-1, keepdims=True)
                o_h = o[r0:r0 + CHUNK, h * LSTM_V:(h + 1) * LSTM_V]
                y_ref[r0:r0 + CHUNK, h * LSTM_V:(h + 1) * LSTM_V] = (
                    _sigmoid(o_h) * (hv * lax.rsqrt(ms + NORM_EPS) * nw_ref[:, h * LSTM_V:(h + 1) * LSTM_V])
                ).astype(BF16)
                kw = jnp.where(in_head, k_p * jnp.broadcast_to(w_end[:, h:h + 1], (CHUNK, LANES)), 0.0)
                c_new = c_new + jnp.broadcast_to(s_loc[:, h:h + 1], (1, 1)) * _dot_tn(kw.astype(BF16), v_ext)
            c_ref[pr] = c_new


def _mlstm_mixer(x, wq, wk, wv, wo, wif, bif, nw):
    b, s, d = x.shape
    rows = MIX_ROWS
    const = lambda shape: pl.BlockSpec(shape, lambda i, j: (0,) * len(shape))
    return pl.pallas_call(
        functools.partial(_mlstm_kernel, rows=rows),
        grid=(b, s // rows),
        in_specs=[pl.BlockSpec((None, rows, d), lambda i, j: (i, j, 0)),
                  const(wq.shape), const(wk.shape), const(wv.shape), const(wo.shape), const(wif.shape),
                  const(bif.shape), const(nw.shape)],
        out_specs=pl.BlockSpec((None, rows, LSTM_D_V), lambda i, j: (i, j, 0)),
        out_shape=jax.ShapeDtypeStruct((b, s, LSTM_D_V), BF16),
        scratch_shapes=[pltpu.VMEM((LSTM_HEADS // 2, 2 * LSTM_QK, 2 * LSTM_V), F32),
                        pltpu.VMEM((1, LANES), F32)],
        compiler_params=pltpu.CompilerParams(dimension_semantics=("arbitrary", "arbitrary"),
                                             vmem_limit_bytes=VMEM_LIMIT),
        name="mlstm_mixer",
    )(x, wq, wk, wv, wo, wif, bif, nw)


def _layer_norm(t, g, b):
    mu = jnp.mean(t, axis=-1, keepdims=True)
    tc = t - mu
    var = jnp.mean(tc * tc, axis=-1, keepdims=True)
    return tc * lax.rsqrt(var + NORM_EPS) * g + b


def _route(logits):
    rows = logits.shape[0]
    lane = lax.broadcasted_iota(jnp.int32, (rows, LANES), 1).astype(F32)
    big = float(LANES)
    neg = -jnp.inf
    gl = jnp.where(lane < MOE_GROUPS, logits, neg)
    gmax = jnp.max(gl, axis=-1, keepdims=True)
    gidx = jnp.min(jnp.where(gl == gmax, lane, big), axis=-1, keepdims=True)
    gsum = jnp.sum(jnp.where(lane < MOE_GROUPS, jnp.exp(logits - gmax), 0.0), axis=-1, keepdims=True)
    grp_p = 1.0 / gsum
    lo = MOE_GROUPS + gidx * MOE_PER_GROUP
    el = jnp.where((lane >= lo) & (lane < lo + MOE_PER_GROUP), logits, neg)
    m1 = jnp.max(el, axis=-1, keepdims=True)
    i1 = jnp.min(jnp.where(el == m1, lane, big), axis=-1, keepdims=True)
    el2 = jnp.where(lane == i1, neg, el)
    m2 = jnp.max(el2, axis=-1, keepdims=True)
    i2 = jnp.min(jnp.where(el2 == m2, lane, big), axis=-1, keepdims=True)
    e21 = jnp.exp(m2 - m1)
    g1 = grp_p / (1.0 + e21)
    g2 = grp_p * e21 / (1.0 + e21)
    e1 = (i1 - MOE_GROUPS).astype(F32)
    e2 = (i2 - MOE_GROUPS).astype(F32)
    return jnp.where(lane == 0, e1, jnp.where(lane == 1, e2, jnp.where(lane == 2, g1, jnp.where(lane == 3, g2, 0.0))))


def _merge_kernel(x_ref, ys_ref, yl_ref, wgs_ref, wgl_ref, wa_ref, wb_ref, wo_ref, g_ref, b_ref,
                  wr_hi_ref, wr_lo_ref, br_ref, x1_ref, route_ref, *, alpha):
    x = x_ref[...]
    xb = x.astype(BF16)
    merged = (_sigmoid(_dot(xb, wgs_ref[...])) * _dot(ys_ref[...], wa_ref[...])
              + _sigmoid(_dot(xb, wgl_ref[...])) * _dot(yl_ref[...], wb_ref[...]))
    x1 = _layer_norm(alpha * x + _dot(merged.astype(BF16), wo_ref[...]), g_ref[...], b_ref[...])
    x1_ref[...] = x1
    x_hi = x1.astype(BF16)
    x_lo = (x1 - x_hi.astype(F32)).astype(BF16)
    logits = (_dot(x_hi, wr_hi_ref[...]) + _dot(x_lo, wr_hi_ref[...]) + _dot(x_hi, wr_lo_ref[...])
              + br_ref[...])
    route_ref[...] = _route(logits)


def _merge(x2d, ys, yl, wgs, wgl, wa, wb, wo, g, b, wr_hi, wr_lo, br, alpha):
    t, d = x2d.shape
    rows = DENSE_ROWS
    const = lambda shape: pl.BlockSpec(shape, lambda i: (0,) * len(shape))
    rowblk = lambda w: pl.BlockSpec((rows, w), lambda i: (i, 0))
    return pl.pallas_call(
        functools.partial(_merge_kernel, alpha=alpha),
        grid=(t // rows,),
        in_specs=[rowblk(d), rowblk(SSM_D_INNER), rowblk(LSTM_D_V),
                  const(wgs.shape), const(wgl.shape), const(wa.shape), const(wb.shape), const(wo.shape),
                  const(g.shape), const(b.shape), const(wr_hi.shape), const(wr_lo.shape), const(br.shape)],
        out_specs=[rowblk(d), rowblk(LANES)],
        out_shape=[jax.ShapeDtypeStruct((t, d), F32), jax.ShapeDtypeStruct((t, LANES), F32)],
        compiler_params=pltpu.CompilerParams(dimension_semantics=("arbitrary",), vmem_limit_bytes=VMEM_LIMIT),
        name="merge_ln_route",
    )(x2d, ys, yl, wgs, wgl, wa, wb, wo, g, b, wr_hi, wr_lo, br)


def _moe_kernel(offs_ref, tok_ref, gate_ref, x_ref, wg_ref, wu_ref, wd_ref, out_ref, xbuf, ybuf):
    i = pl.program_id(0)
    j = pl.program_id(1)

    @pl.when(j == 0)
    def _():
        out_ref[...] = jnp.zeros(out_ref.shape, F32)

    start = offs_ref[i * (MOE_EXPERTS + 1) + j]
    n = offs_ref[i * (MOE_EXPERTS + 1) + j + 1] - start

    def block(bi, carry):
        base = start + bi * MOE_ROWS

        def gather(r, c):
            xbuf[pl.ds(r, 1), :] = x_ref[pl.ds(tok_ref[base + r], 1), :]
            return c

        lax.fori_loop(0, MOE_ROWS, gather, 0, unroll=SUBLANES)
        xb = xbuf[...].astype(BF16)
        hg = _dot(xb, wg_ref[0])
        hid = (hg * _sigmoid(hg) * _dot(xb, wu_ref[0])).astype(BF16)
        ybuf[...] = _dot(hid, wd_ref[0])

        def scatter(r, c):
            t = tok_ref[base + r]
            out_ref[pl.ds(t, 1), :] = out_ref[pl.ds(t, 1), :] + gate_ref[base + r] * ybuf[pl.ds(r, 1), :]
            return c

        lax.fori_loop(0, jnp.minimum(MOE_ROWS, n - bi * MOE_ROWS), scatter, 0)
        return carry

    lax.fori_loop(0, (n + MOE_ROWS - 1) // MOE_ROWS, block, 0)


def _moe_route_tables(route, n_tiles, tile, npad):
    e = route[:, 0:2].astype(jnp.int32).reshape(n_tiles, 2 * tile)
    g = route[:, 2:4].reshape(n_tiles, 2 * tile)
    onehot = (e[:, :, None] == jnp.arange(MOE_EXPERTS, dtype=jnp.int32)).astype(jnp.int32)
    counts = jnp.sum(onehot, axis=1)
    rank = jnp.sum((jnp.cumsum(onehot, axis=1) - onehot) * onehot, axis=2)
    padded = (counts + SUBLANES - 1) // SUBLANES * SUBLANES
    ends = jnp.cumsum(padded, axis=1)
    offs = jnp.concatenate([jnp.zeros((n_tiles, 1), jnp.int32), ends], axis=1)
    dest = jnp.take_along_axis(offs, e, axis=1) + rank
    tile_id = jnp.arange(n_tiles, dtype=jnp.int32)[:, None]
    tok_local = jnp.broadcast_to(jnp.arange(2 * tile, dtype=jnp.int32)[None, :] // 2, dest.shape)
    tok = jnp.zeros((n_tiles, npad), jnp.int32).at[tile_id, dest].set(tok_local)
    gate = jnp.zeros((n_tiles, npad), F32).at[tile_id, dest].set(g)
    return offs.reshape(-1), tok.reshape(-1), gate.reshape(-1)


def _moe(x1, route, wg, wu, wd):
    t, d = x1.shape
    tile = min(MOE_TILE, t)
    n_tiles = t // tile
    npad = -(-(2 * tile + MOE_EXPERTS * SUBLANES + MOE_ROWS) // ROUTE_PAD) * ROUTE_PAD
    offs, tok, gate = _moe_route_tables(route, n_tiles, tile, npad)
    grid_spec = pltpu.PrefetchScalarGridSpec(
        num_scalar_prefetch=1,
        grid=(n_tiles, MOE_EXPERTS),
        in_specs=[pl.BlockSpec((npad,), lambda i, j, o: (i,), memory_space=pltpu.SMEM),
                  pl.BlockSpec((npad,), lambda i, j, o: (i,), memory_space=pltpu.SMEM),
                  pl.BlockSpec((tile, d), lambda i, j, o: (i, 0)),
                  pl.BlockSpec((1, d, MOE_D_FF), lambda i, j, o: (j, 0, 0)),
                  pl.BlockSpec((1, d, MOE_D_FF), lambda i, j, o: (j, 0, 0)),
                  pl.BlockSpec((1, MOE_D_FF, d), lambda i, j, o: (j, 0, 0))],
        out_specs=pl.BlockSpec((tile, d), lambda i, j, o: (i, 0)),
        scratch_shapes=[pltpu.VMEM((MOE_ROWS, d), F32), pltpu.VMEM((MOE_ROWS, d), F32)],
    )
    return pl.pallas_call(
        _moe_kernel,
        grid_spec=grid_spec,
        out_shape=jax.ShapeDtypeStruct((t, d), F32),
        compiler_params=pltpu.CompilerParams(dimension_semantics=("arbitrary", "arbitrary"),
                                             vmem_limit_bytes=VMEM_LIMIT),
        name="moe_experts",
    )(offs, tok, gate, x1, wg, wu, wd)


def _final_kernel(x1_ref, moe_ref, p_ref, g_ref, b_ref, wpg_ref, wpp_ref, out_ref, *, alpha):
    x2 = _layer_norm(alpha * x1_ref[...] + moe_ref[...], g_ref[...], b_ref[...])
    gate = _sigmoid(_dot(x2.astype(BF16), wpg_ref[...]))
    out_ref[...] = x2 + gate * _dot(p_ref[...].astype(BF16), wpp_ref[...])


def _final(x1, moe, p2d, g, b, wpg, wpp, alpha):
    t, d = x1.shape
    rows = DENSE_ROWS
    const = lambda shape: pl.BlockSpec(shape, lambda i: (0,) * len(shape))
    rowblk = lambda w: pl.BlockSpec((rows, w), lambda i: (i, 0))
    return pl.pallas_call(
        functools.partial(_final_kernel, alpha=alpha),
        grid=(t // rows,),
        in_specs=[rowblk(d), rowblk(d), rowblk(p2d.shape[1]), const(g.shape), const(b.shape),
                  const(wpg.shape), const(wpp.shape)],
        out_specs=rowblk(d),
        out_shape=jax.ShapeDtypeStruct((t, d), F32),
        compiler_params=pltpu.CompilerParams(dimension_semantics=("arbitrary",), vmem_limit_bytes=VMEM_LIMIT),
        name="final_ln_ple",
    )(x1, moe, p2d, g, b, wpg, wpp)


def _pad_lanes(w, width=LANES):
    return jnp.pad(w, ((0, 0), (0, width - w.shape[1])))


def _row(v, width=None):
    v = v.astype(F32).reshape(1, -1)
    return v if width is None else _pad_lanes(v, width)


def kernel(x, p, w_in, ssm_conv_w, ssm_conv_b, ssm_dt_bias, ssm_a_log, ssm_d, ssm_norm_w, lstm_i_bias, lstm_f_bias, lstm_norm_w, w_branch_ssm, w_branch_lstm, w_out, ln1_g, ln1_b, moe_w_group, moe_b_group, moe_w_expert, moe_b_expert, moe_w_gate, moe_w_up, moe_w_down, ln2_g, ln2_b, ple_w_proj, ple_w_gate):
    depth = w_in.shape[0]
    bsz, seq, d = x.shape
    t = bsz * seq
    alpha = (2.0 * depth) ** 0.25
    for i in range(depth):
        splits = []
        c0 = 0
        for sz in IN_PROJ_SIZES:
            splits.append(w_in[i][:, c0:c0 + sz])
            c0 += sz
        w_z, w_xbc, w_dt, w_q, w_k, w_v, w_o, w_i, w_f, w_gs, w_gl = splits
        bf = lambda w: w.astype(BF16)
        y_ssm = _ssd_mixer(
            x, bf(w_z), bf(w_xbc), bf(_pad_lanes(w_dt)),
            ssm_conv_w[i].astype(F32), _row(ssm_conv_b[i]), _row(ssm_dt_bias[i], LANES),
            _row(-jnp.exp(ssm_a_log[i].astype(F32)), LANES),
            _row(jnp.repeat(ssm_d[i].astype(F32), SSM_HEAD_DIM)), _row(ssm_norm_w[i]))
        w_if = jnp.concatenate([_pad_lanes(w_i), _pad_lanes(w_f)], axis=1)
        b_if = jnp.concatenate([_row(lstm_i_bias[i], LANES), _row(lstm_f_bias[i], LANES)], axis=1)
        y_lstm = _mlstm_mixer(x, bf(w_q * (LSTM_QK ** -0.5)), bf(w_k), bf(w_v), bf(w_o), bf(w_if), b_if,
                              _row(lstm_norm_w[i]))
        w_r = _pad_lanes(jnp.concatenate([moe_w_group[i], moe_w_expert[i]], axis=1).astype(F32))
        w_r_hi = w_r.astype(BF16)
        w_r_lo = (w_r - w_r_hi.astype(F32)).astype(BF16)
        b_r = _row(jnp.concatenate([moe_b_group[i], moe_b_expert[i]]), LANES)
        x1, route = _merge(x.reshape(t, d), y_ssm.reshape(t, -1), y_lstm.reshape(t, -1),
                           bf(w_gs), bf(w_gl), bf(w_branch_ssm[i]), bf(w_branch_lstm[i]), bf(w_out[i]),
                           _row(ln1_g[i]), _row(ln1_b[i]), w_r_hi, w_r_lo, b_r, alpha)
        moe = _moe(x1, route, bf(moe_w_gate[i]), bf(moe_w_up[i]), bf(moe_w_down[i]))
        x = _final(x1, moe, p[i].reshape(t, -1), _row(ln2_g[i]), _row(ln2_b[i]),
                   bf(ple_w_gate[i]), bf(ple_w_proj[i]), alpha).reshape(bsz, seq, d)
    return x
```

```python
import functools

import jax
import jax.numpy as jnp
from jax import lax
from jax.experimental import pallas as pl
from jax.experimental.pallas import tpu as pltpu

F32 = jnp.float32
BF16 = jnp.bfloat16

D_MODEL = 1024
PLE_DIM = 256
SSM_D_INNER = 1024
SSM_HEAD_DIM = 64
SSM_HEADS = 16
SSM_GROUPS = 4
SSM_STATE = 128
SSM_CONV = 4
SSM_XBC = SSM_D_INNER + 2 * SSM_GROUPS * SSM_STATE
LSTM_HEADS = 8
LSTM_QK = 64
LSTM_V = 128
LSTM_D_QK = LSTM_HEADS * LSTM_QK
LSTM_D_V = LSTM_HEADS * LSTM_V
CHUNK = 128
MOE_GROUPS = 8
MOE_PER_GROUP = 8
MOE_EXPERTS = 64
MOE_D_FF = 512
NORM_EPS = 1e-5
IN_PROJ_SIZES = (SSM_D_INNER, SSM_XBC, SSM_HEADS, LSTM_D_QK, LSTM_D_QK, LSTM_D_V, LSTM_D_V,
                 LSTM_HEADS, LSTM_HEADS, D_MODEL, D_MODEL)

LANES = 128
SUBLANES = 8
VMEM_LIMIT = 56 * 1024 * 1024

MIX_ROWS = 256
DENSE_ROWS = 512
MOE_TILE = 4096
MOE_ROWS = 128


def _dot(a, b):
    return jnp.dot(a, b, preferred_element_type=F32)


def _dot_nt(a, b):
    return lax.dot_general(a, b, (((1,), (1,)), ((), ())), preferred_element_type=F32)


def _dot_tn(a, b):
    return lax.dot_general(a, b, (((0,), (0,)), ((), ())), preferred_element_type=F32)


def _sigmoid(x):
    return 1.0 / (1.0 + jnp.exp(-x))


def _softplus(x):
    return jnp.maximum(x, 0.0) + jnp.log(1.0 + jnp.exp(-jnp.abs(x)))


def _split3(x):
    hi = x.astype(BF16)
    r1 = x - hi.astype(F32)
    mid = r1.astype(BF16)
    lo = (r1 - mid.astype(F32)).astype(BF16)
    return hi, mid, lo


def _cumsum_rows(tril, x):
    hi, mid, lo = _split3(x)
    return _dot(tril, hi) + _dot(tril, mid) + _dot(tril, lo)


def _causal_masks():
    r = lax.broadcasted_iota(jnp.int32, (CHUNK, CHUNK), 0)
    c = lax.broadcasted_iota(jnp.int32, (CHUNK, CHUNK), 1)
    causal = r >= c
    return causal, jnp.where(causal, 1.0, 0.0).astype(BF16)


def _pair_cols(v, ha, hb, rows, lane):
    a = jnp.broadcast_to(v[:, ha:ha + 1], (rows, LANES))
    b = jnp.broadcast_to(v[:, hb:hb + 1], (rows, LANES))
    return jnp.where(lane < SSM_HEAD_DIM, a, b)


def _ssd_kernel(x_ref, wz_ref, wxbc_ref, wdt_ref, cw_ref, cb_ref, dtb_ref, aneg_ref, dexp_ref, nw_ref,
                y_ref, ext_ref, st_ref, *, rows):
    @pl.when(pl.program_id(1) == 0)
    def _():
        ext_ref[0:SUBLANES, :] = jnp.zeros((SUBLANES, SSM_XBC), F32)
        st_ref[...] = jnp.zeros(st_ref.shape, F32)

    xb = x_ref[...].astype(BF16)
    z = _dot(xb, wz_ref[...])
    ext_ref[SUBLANES:SUBLANES + rows, :] = _dot(xb, wxbc_ref[...])
    dtr = _dot(xb, wdt_ref[...])

    conv = cb_ref[...]
    for k in range(SSM_CONV):
        conv = conv + cw_ref[k:k + 1, :] * ext_ref[pl.ds(SUBLANES - SSM_CONV + 1 + k, rows), :]
    ext_ref[0:SUBLANES, :] = ext_ref[rows:rows + SUBLANES, :]
    u = conv * _sigmoid(conv)

    dt = _softplus(dtr + dtb_ref[...])
    da = dt * aneg_ref[...]
    causal, tril = _causal_masks()
    lane = lax.broadcasted_iota(jnp.int32, (CHUNK, LANES), 1)
    lane1 = lax.broadcasted_iota(jnp.int32, (1, LANES), 1)
    gw = SSM_D_INNER // SSM_GROUPS

    for ci in range(rows // CHUNK):
        r0 = ci * CHUNK
        dt_c = dt[r0:r0 + CHUNK]
        a_cs = _cumsum_rows(tril, da[r0:r0 + CHUNK])
        a_cs_t = a_cs.T
        dt_t = dt_c.T
        a_last = a_cs[CHUNK - 1:CHUNK, :]
        ea = jnp.exp(a_cs)
        sw = dt_c * jnp.exp(a_last - a_cs)
        ea_last = jnp.exp(a_last)
        for g in range(SSM_GROUPS):
            bm_g = u[r0:r0 + CHUNK, SSM_D_INNER + g * SSM_STATE:SSM_D_INNER + (g + 1) * SSM_STATE].astype(BF16)
            c0 = SSM_D_INNER + SSM_GROUPS * SSM_STATE
            cm_g = u[r0:r0 + CHUNK, c0 + g * SSM_STATE:c0 + (g + 1) * SSM_STATE].astype(BF16)
            xs_g = u[r0:r0 + CHUNK, g * gw:(g + 1) * gw]
            cb = _dot_nt(cm_g, bm_g)
            st_g = st_ref[:, g * gw:(g + 1) * gw]
            y_off = _dot(cm_g, st_g.astype(BF16))
            yd, ea_x, sw_x, eal_x = [], [], [], []
            for pr in range(2):
                ha = 4 * g + 2 * pr
                hb = ha + 1
                xs_p = xs_g[:, pr * LANES:(pr + 1) * LANES].astype(BF16)
                rhs = jnp.concatenate([jnp.where(lane < SSM_HEAD_DIM, xs_p, 0).astype(BF16),
                                       jnp.where(lane >= SSM_HEAD_DIM, xs_p, 0).astype(BF16)], axis=0)
                gs = []
                for h in (ha, hb):
                    colb = jnp.broadcast_to(a_cs[:, h:h + 1], (CHUNK, CHUNK))
                    rowb = jnp.broadcast_to(a_cs_t[h:h + 1, :], (CHUNK, CHUNK))
                    dec = jnp.exp(jnp.where(causal, colb - rowb, -jnp.inf))
                    gs.append((cb * dec * jnp.broadcast_to(dt_t[h:h + 1, :], (CHUNK, CHUNK))).astype(BF16))
                yd.append(_dot(jnp.concatenate(gs, axis=1), rhs))
                ea_x.append(_pair_cols(ea, ha, hb, CHUNK, lane))
                sw_x.append(_pair_cols(sw, ha, hb, CHUNK, lane))
                eal_x.append(_pair_cols(ea_last, ha, hb, 1, lane1))
            y_g = (jnp.concatenate(yd, axis=1) + y_off * jnp.concatenate(ea_x, axis=1)
                   + dexp_ref[:, g * gw:(g + 1) * gw] * xs_g)
            xw = (xs_g * jnp.concatenate(sw_x, axis=1)).astype(BF16)
            st_ref[:, g * gw:(g + 1) * gw] = st_g * jnp.concatenate(eal_x, axis=1) + _dot_tn(bm_g, xw)
            zz = z[r0:r0 + CHUNK, g * gw:(g + 1) * gw]
            yy = y_g * (zz * _sigmoid(zz))
            ms = jnp.mean(yy * yy, axis=-1, keepdims=True)
            y_ref[r0:r0 + CHUNK, g * gw:(g + 1) * gw] = (
                yy * lax.rsqrt(ms + NORM_EPS) * nw_ref[:, g * gw:(g + 1) * gw]).astype(BF16)


def _ssd_mixer(x, wz, wxbc, wdt, cw, cb, dtb, aneg, dexp, nw):
    b, s, d = x.shape
    rows = MIX_ROWS
    const = lambda shape: pl.BlockSpec(shape, lambda i, j: (0,) * len(shape))
    return pl.pallas_call(
        functools.partial(_ssd_kernel, rows=rows),
        grid=(b, s // rows),
        in_specs=[pl.BlockSpec((None, rows, d), lambda i, j: (i, j, 0)),
                  const(wz.shape), const(wxbc.shape), const(wdt.shape), const(cw.shape), const(cb.shape),
                  const(dtb.shape), const(aneg.shape), const(dexp.shape), const(nw.shape)],
        out_specs=pl.BlockSpec((None, rows, SSM_D_INNER), lambda i, j: (i, j, 0)),
        out_shape=jax.ShapeDtypeStruct((b, s, SSM_D_INNER), BF16),
        scratch_shapes=[pltpu.VMEM((SUBLANES + rows, SSM_XBC), F32),
                        pltpu.VMEM((SSM_STATE, SSM_D_INNER), F32)],
        compiler_params=pltpu.CompilerParams(dimension_semantics=("arbitrary", "arbitrary"),
                                             vmem_limit_bytes=VMEM_LIMIT),
        name="ssd_mixer",
    )(x, wz, wxbc, wdt, cw, cb, dtb, aneg, dexp, nw)


def _mlstm_kernel(x_ref, wq_ref, wk_ref, wv_ref, wo_ref, wif_ref, bif_ref, nw_ref,
                  y_ref, c_ref, m_ref, *, rows):
    @pl.when(pl.program_id(1) == 0)
    def _():
        c_ref[...] = jnp.zeros(c_ref.shape, F32)
        m_ref[...] = jnp.full(m_ref.shape, -jnp.inf, F32)

    xb = x_ref[...].astype(BF16)
    q = _dot(xb, wq_ref[...])
    k = _dot(xb, wk_ref[...])
    v = _dot(xb, wv_ref[...])
    o = _dot(xb, wo_ref[...])
    gif = _dot(xb, wif_ref[...]) + bif_ref[...]
    causal, tril = _causal_masks()
    lane = lax.broadcasted_iota(jnp.int32, (CHUNK, LANES), 1)
    lane1 = lax.broadcasted_iota(jnp.int32, (1, LANES), 1)
    row = lax.broadcasted_iota(jnp.int32, (LANES, 1), 0)
    head_lane = lane < LSTM_HEADS
    ones_col = jnp.where(lane == 0, 1.0, 0.0).astype(BF16)

    for ci in range(rows // CHUNK):
        r0 = ci * CHUNK
        log_i = gif[r0:r0 + CHUNK, 0:LANES]
        f_pre = gif[r0:r0 + CHUNK, LANES:2 * LANES]
        log_f = jnp.where(head_lane, jnp.minimum(f_pre, 0.0) - jnp.log(1.0 + jnp.exp(-jnp.abs(f_pre))), 0.0)
        fcum = _cumsum_rows(tril, log_f)
        f_tot = fcum[CHUNK - 1:CHUNK, :]
        a_end = f_tot - fcum + log_i
        m_loc = jnp.max(a_end, axis=0, keepdims=True)
        w_end = jnp.exp(a_end - m_loc)
        fcum_t = fcum.T
        g_t = (log_i - fcum).T
        m_prev = m_ref[...]
        m_new = jnp.maximum(f_tot + m_prev, m_loc)
        s_prev = jnp.exp(f_tot + m_prev - m_new)
        s_loc = jnp.exp(m_loc - m_new)
        m_ref[...] = jnp.where(lane1 < LSTM_HEADS, m_new, -jnp.inf)
        inter_log_all = fcum + m_prev

        for pr in range(LSTM_HEADS // 2):
            q_p = q[r0:r0 + CHUNK, pr * LANES:(pr + 1) * LANES]
            k_p = k[r0:r0 + CHUNK, pr * LANES:(pr + 1) * LANES]
            k_pb = k_p.astype(BF16)
            c_pair = c_ref[pr]
            c_pair_b = c_pair.astype(BF16)
            c_new = c_pair * jnp.where(row < LSTM_QK,
                                       jnp.broadcast_to(s_prev[:, 2 * pr:2 * pr + 1], (LANES, 1)),
                                       jnp.broadcast_to(s_prev[:, 2 * pr + 1:2 * pr + 2], (LANES, 1)))
            for hh in range(2):
                h = 2 * pr + hh
                in_head = (lane < LSTM_QK) if hh == 0 else (lane >= LSTM_QK)
                v_h = v[r0:r0 + CHUNK, h * LSTM_V:(h + 1) * LSTM_V].astype(BF16)
                v_ext = jnp.concatenate([v_h, ones_col], axis=1)
                q_m = jnp.where(in_head, q_p, 0.0)
                colb = jnp.broadcast_to(fcum[:, h:h + 1], (CHUNK, CHUNK))
                rowb = jnp.broadcast_to(g_t[h:h + 1, :], (CHUNK, CHUNK))
                log_d = jnp.where(causal, colb + rowb, -jnp.inf)
                inter_log = inter_log_all[:, h:h + 1]
                m_t = jnp.maximum(inter_log, jnp.max(log_d, axis=-1, keepdims=True))
                scores = _dot_nt(q_m.astype(BF16), k_pb) * jnp.exp(log_d - m_t)
                inter_w = jnp.exp(inter_log - m_t)
                lhs = jnp.concatenate([scores.astype(BF16), (q_m * inter_w).astype(BF16)], axis=1)
                rhs = jnp.concatenate([v_ext, c_pair_b], axis=0)
                res = _dot(lhs, rhs)
                num = res[:, 0:LSTM_V]
                den = res[:, LSTM_V:LSTM_V + 1]
                hv = num / jnp.maximum(jnp.abs(den), jnp.exp(-m_t))
                ms = jnp.mean(hv * hv, axis=-1, keepdims=True)
                o_h = o[r0:r0 + CHUNK, h * LSTM_V:(h + 1) * LSTM_V]
                y_ref[r0:r0 + CHUNK, h * LSTM_V:(h + 1) * LSTM_V] = (
                    _sigmoid(o_h) * (hv * lax.rsqrt(ms + NORM_EPS) * nw_ref[:, h * LSTM_V:(h + 1) * LSTM_V])
                ).astype(BF16)
                kw = jnp.where(in_head, k_p * jnp.broadcast_to(w_end[:, h:h + 1], (CHUNK, LANES)), 0.0)
                c_new = c_new + jnp.broadcast_to(s_loc[:, h:h + 1], (1, 1)) * _dot_tn(kw.astype(BF16), v_ext)
            c_ref[pr] = c_new


def _mlstm_mixer(x, wq, wk, wv, wo, wif, bif, nw):
    b, s, d = x.shape
    rows = MIX_ROWS
    const = lambda shape: pl.BlockSpec(shape, lambda i, j: (0,) * len(shape))
    return pl.pallas_call(
        functools.partial(_mlstm_kernel, rows=rows),
        grid=(b, s // rows),
        in_specs=[pl.BlockSpec((None, rows, d), lambda i, j: (i, j, 0)),
                  const(wq.shape), const(wk.shape), const(wv.shape), const(wo.shape), const(wif.shape),
                  const(bif.shape), const(nw.shape)],
        out_specs=pl.BlockSpec((None, rows, LSTM_D_V), lambda i, j: (i, j, 0)),
        out_shape=jax.ShapeDtypeStruct((b, s, LSTM_D_V), BF16),
        scratch_shapes=[pltpu.VMEM((LSTM_HEADS // 2, 2 * LSTM_QK, 2 * LSTM_V), F32),
                        pltpu.VMEM((1, LANES), F32)],
        compiler_params=pltpu.CompilerParams(dimension_semantics=("arbitrary", "arbitrary"),
                                             vmem_limit_bytes=VMEM_LIMIT),
        name="mlstm_mixer",
    )(x, wq, wk, wv, wo, wif, bif, nw)


def _layer_norm(t, g, b):
    mu = jnp.mean(t, axis=-1, keepdims=True)
    tc = t - mu
    var = jnp.mean(tc * tc, axis=-1, keepdims=True)
    return tc * lax.rsqrt(var + NORM_EPS) * g + b


def _pack_pair(a, b):
    pa = lax.bitcast_convert_type(a.astype(BF16).astype(F32), jnp.uint32)
    pb = lax.bitcast_convert_type(b.astype(BF16).astype(F32), jnp.uint32)
    return pa | (pb >> 16)


def _unpack_pair(w):
    return (lax.bitcast_convert_type(w & jnp.uint32(0xFFFF0000), F32),
            lax.bitcast_convert_type(w << 16, F32))


def _route(logits):
    rows = logits.shape[0]
    lane = lax.broadcasted_iota(jnp.int32, (rows, LANES), 1).astype(F32)
    big = float(LANES)
    neg = -jnp.inf
    gl = jnp.where(lane < MOE_GROUPS, logits, neg)
    gmax = jnp.max(gl, axis=-1, keepdims=True)
    gidx = jnp.min(jnp.where(gl == gmax, lane, big), axis=-1, keepdims=True)
    gsum = jnp.sum(jnp.where(lane < MOE_GROUPS, jnp.exp(logits - gmax), 0.0), axis=-1, keepdims=True)
    grp_p = 1.0 / gsum
    lo = MOE_GROUPS + gidx * MOE_PER_GROUP
    el = jnp.where(lane >= lo, jnp.where(lane < lo + MOE_PER_GROUP, logits, neg), neg)
    m1 = jnp.max(el, axis=-1, keepdims=True)
    i1 = jnp.min(jnp.where(el == m1, lane, big), axis=-1, keepdims=True)
    el2 = jnp.where(lane == i1, neg, el)
    m2 = jnp.max(el2, axis=-1, keepdims=True)
    i2 = jnp.min(jnp.where(el2 == m2, lane, big), axis=-1, keepdims=True)
    e21 = jnp.exp(m2 - m1)
    g1 = grp_p / (1.0 + e21)
    g2 = grp_p * e21 / (1.0 + e21)
    return i1 - MOE_GROUPS, i2 - MOE_GROUPS, g1, g2


def _merge_kernel(x_ref, ys_ref, yl_ref, wgs_ref, wgl_ref, wa_ref, wb_ref, wo_ref, g_ref, b_ref,
                  wr_hi_ref, wr_lo_ref, br_ref, x1_ref, xp_ref, tab_ref, cnt_ref, carry_ref,
                  *, alpha, steps_per_tile):
    @pl.when(pl.program_id(0) % steps_per_tile == 0)
    def _():
        carry_ref[...] = jnp.zeros(carry_ref.shape, F32)

    x = x_ref[...]
    rows = x.shape[0]
    xb = x.astype(BF16)
    merged = (_sigmoid(_dot(xb, wgs_ref[...])) * _dot(ys_ref[...], wa_ref[...])
              + _sigmoid(_dot(xb, wgl_ref[...])) * _dot(yl_ref[...], wb_ref[...]))
    x1 = _layer_norm(alpha * x + _dot(merged.astype(BF16), wo_ref[...]), g_ref[...], b_ref[...])
    x1_ref[...] = x1
    half = x1.shape[1] // 2
    xp_ref[...] = _pack_pair(x1[:, :half], x1[:, half:])
    x_hi = x1.astype(BF16)
    x_lo = (x1 - x_hi.astype(F32)).astype(BF16)
    logits = (_dot(x_hi, wr_hi_ref[...]) + _dot(x_lo, wr_hi_ref[...]) + _dot(x_hi, wr_lo_ref[...])
              + br_ref[...])
    e1, e2, g1, g2 = _route(logits)
    lane = lax.broadcasted_iota(jnp.int32, (rows, LANES), 1)
    lane_f = lane.astype(F32)
    oh1 = lane_f == e1
    oh2 = lane_f == e2
    ohs = jnp.where(oh1, 1.0, jnp.where(oh2, 1.0, 0.0))
    r_i = lax.broadcasted_iota(jnp.int32, (rows, rows), 0)
    c_i = lax.broadcasted_iota(jnp.int32, (rows, rows), 1)
    before = jnp.where(r_i > c_i, 1.0, 0.0).astype(BF16)
    seen = _dot(before, ohs.astype(BF16)) + carry_ref[...]
    r1 = jnp.sum(jnp.where(oh1, seen, 0.0), axis=-1, keepdims=True)
    r2 = jnp.sum(jnp.where(oh2, seen, 0.0), axis=-1, keepdims=True)
    carry = carry_ref[...] + jnp.sum(ohs, axis=0, keepdims=True)
    carry_ref[...] = carry
    cnt_ref[...] = jnp.broadcast_to(carry, cnt_ref.shape).astype(jnp.int32)
    as_int = lambda v: v.astype(jnp.int32)
    as_bits = lambda v: lax.bitcast_convert_type(v, jnp.int32)
    tab = jnp.where(lane == 0, as_int(e1), jnp.where(lane == 1, as_int(e2), jnp.where(
        lane == 2, as_int(r1), jnp.where(lane == 3, as_int(r2), jnp.where(
            lane == 4, as_bits(g1), jnp.where(lane == 5, as_bits(g2), 0))))))
    tab_ref[...] = tab.T[0:SUBLANES, :]


def _merge(x2d, ys, yl, wgs, wgl, wa, wb, wo, g, b, wr_hi, wr_lo, br, alpha, tile):
    t, d = x2d.shape
    rows = DENSE_ROWS
    spt = tile // rows
    const = lambda shape: pl.BlockSpec(shape, lambda i: (0,) * len(shape))
    rowblk = lambda w: pl.BlockSpec((rows, w), lambda i: (i, 0))
    return pl.pallas_call(
        functools.partial(_merge_kernel, alpha=alpha, steps_per_tile=spt),
        grid=(t // rows,),
        in_specs=[rowblk(d), rowblk(SSM_D_INNER), rowblk(LSTM_D_V),
                  const(wgs.shape), const(wgl.shape), const(wa.shape), const(wb.shape), const(wo.shape),
                  const(g.shape), const(b.shape), const(wr_hi.shape), const(wr_lo.shape), const(br.shape)],
        out_specs=[rowblk(d), rowblk(d // 2),
                   pl.BlockSpec((SUBLANES, rows), lambda i: (0, i)),
                   pl.BlockSpec((SUBLANES, LANES), lambda i: (i // spt, 0))],
        out_shape=[jax.ShapeDtypeStruct((t, d), F32), jax.ShapeDtypeStruct((t, d // 2), jnp.uint32),
                   jax.ShapeDtypeStruct((SUBLANES, t), jnp.int32),
                   jax.ShapeDtypeStruct((t // tile * SUBLANES, LANES), jnp.int32)],
        scratch_shapes=[pltpu.VMEM((1, LANES), F32)],
        compiler_params=pltpu.CompilerParams(dimension_semantics=("arbitrary",), vmem_limit_bytes=VMEM_LIMIT),
        name="merge_ln_route",
    )(x2d, ys, yl, wgs, wgl, wa, wb, wo, g, b, wr_hi, wr_lo, br)


def _moe_kernel(cnt_ref, tab_ref, xp_ref, wg_ref, wu_ref, wd_ref, out_ref, xys, ybuf, off_ref, dst_ref,
                *, tile):
    i = pl.program_id(0)
    j = pl.program_id(1)
    half = xys.shape[1]

    @pl.when(j == 0)
    def _():
        def offsets(e, acc):
            padded = (cnt_ref[i * MOE_EXPERTS + e] + SUBLANES - 1) // SUBLANES * SUBLANES
            off_ref[e] = acc
            end = acc + padded

            @pl.when(padded > 0)
            def _():
                xys[pl.ds(pl.multiple_of(end - SUBLANES, SUBLANES), SUBLANES), :] = jnp.zeros(
                    (SUBLANES, half), jnp.uint32)
            return end

        total = lax.fori_loop(0, MOE_EXPERTS, offsets, 0)
        off_ref[MOE_EXPERTS] = total
        xys[pl.ds(pl.multiple_of(total, SUBLANES), MOE_ROWS), :] = jnp.zeros((MOE_ROWS, half), jnp.uint32)

        def distribute(t, c):
            d1 = off_ref[tab_ref[0, t]] + tab_ref[2, t]
            d2 = off_ref[tab_ref[1, t]] + tab_ref[3, t]
            dst_ref[0, t] = d1
            dst_ref[1, t] = d2
            row = xp_ref[pl.ds(t, 1), :]
            xys[pl.ds(d1, 1), :] = row
            xys[pl.ds(d2, 1), :] = row
            return c

        lax.fori_loop(0, tile, distribute, 0, unroll=SUBLANES)

    start = off_ref[j]
    n = off_ref[j + 1] - start

    def block(bi, carry):
        base = pl.multiple_of(start + bi * MOE_ROWS, SUBLANES)
        x_hi, x_lo = _unpack_pair(xys[pl.ds(base, MOE_ROWS), :])
        x_hi = x_hi.astype(BF16)
        x_lo = x_lo.astype(BF16)
        hg = _dot(x_hi, wg_ref[0, 0:half, :]) + _dot(x_lo, wg_ref[0, half:2 * half, :])
        hu = _dot(x_hi, wu_ref[0, 0:half, :]) + _dot(x_lo, wu_ref[0, half:2 * half, :])
        y = _dot((hg * _sigmoid(hg) * hu).astype(BF16), wd_ref[0])
        ybuf[...] = _pack_pair(y[:, 0:half], y[:, half:2 * half])

        def copy(r8, c):
            r = pl.multiple_of(r8 * SUBLANES, SUBLANES)
            xys[pl.ds(base + r, SUBLANES), :] = ybuf[pl.ds(r, SUBLANES), :]
            return c

        lax.fori_loop(0, jnp.minimum(MOE_ROWS, n - bi * MOE_ROWS) // SUBLANES, copy, 0)
        return carry

    lax.fori_loop(0, (n + MOE_ROWS - 1) // MOE_ROWS, block, 0)

    @pl.when(j == MOE_EXPERTS - 1)
    def _():
        def combine(t, c):
            g1 = lax.bitcast_convert_type(tab_ref[4, t], F32)
            g2 = lax.bitcast_convert_type(tab_ref[5, t], F32)
            a_hi, a_lo = _unpack_pair(xys[pl.ds(dst_ref[0, t], 1), :])
            b_hi, b_lo = _unpack_pair(xys[pl.ds(dst_ref[1, t], 1), :])
            out_ref[pl.ds(t, 1), 0:half] = g1 * a_hi + g2 * b_hi
            out_ref[pl.ds(t, 1), half:2 * half] = g1 * a_lo + g2 * b_lo
            return c

        lax.fori_loop(0, tile, combine, 0, unroll=SUBLANES)


def _moe(xp, tab, cnt, wg, wu, wd, tile):
    t, half = xp.shape
    d = 2 * half
    n_tiles = t // tile
    seg_rows = 2 * tile + MOE_EXPERTS * SUBLANES + MOE_ROWS
    grid_spec = pltpu.PrefetchScalarGridSpec(
        num_scalar_prefetch=1,
        grid=(n_tiles, MOE_EXPERTS),
        in_specs=[pl.BlockSpec((SUBLANES, tile), lambda i, j, c: (0, i), memory_space=pltpu.SMEM),
                  pl.BlockSpec((tile, half), lambda i, j, c: (i, 0), pipeline_mode=pl.Buffered(1)),
                  pl.BlockSpec((1, d, MOE_D_FF), lambda i, j, c: (j, 0, 0)),
                  pl.BlockSpec((1, d, MOE_D_FF), lambda i, j, c: (j, 0, 0)),
                  pl.BlockSpec((1, MOE_D_FF, d), lambda i, j, c: (j, 0, 0))],
        out_specs=pl.BlockSpec((tile, d), lambda i, j, c: (i, 0), pipeline_mode=pl.Buffered(1)),
        scratch_shapes=[pltpu.VMEM((seg_rows, half), jnp.uint32), pltpu.VMEM((MOE_ROWS, half), jnp.uint32),
                        pltpu.SMEM((LANES,), jnp.int32), pltpu.SMEM((SUBLANES, tile), jnp.int32)],
    )
    return pl.pallas_call(
        functools.partial(_moe_kernel, tile=tile),
        grid_spec=grid_spec,
        out_shape=jax.ShapeDtypeStruct((t, d), F32),
        compiler_params=pltpu.CompilerParams(dimension_semantics=("arbitrary", "arbitrary"),
                                             vmem_limit_bytes=VMEM_LIMIT),
        name="moe_experts",
    )(cnt, tab, xp, wg, wu, wd)


def _final_kernel(x1_ref, moe_ref, p_ref, g_ref, b_ref, wpg_ref, wpp_ref, out_ref, *, alpha):
    x2 = _layer_norm(alpha * x1_ref[...] + moe_ref[...], g_ref[...], b_ref[...])
    gate = _sigmoid(_dot(x2.astype(BF16), wpg_ref[...]))
    out_ref[...] = x2 + gate * _dot(p_ref[...].astype(BF16), wpp_ref[...])


def _final(x1, moe, p2d, g, b, wpg, wpp, alpha):
    t, d = x1.shape
    rows = DENSE_ROWS
    const = lambda shape: pl.BlockSpec(shape, lambda i: (0,) * len(shape))
    rowblk = lambda w: pl.BlockSpec((rows, w), lambda i: (i, 0))
    return pl.pallas_call(
        functools.partial(_final_kernel, alpha=alpha),
        grid=(t // rows,),
        in_specs=[rowblk(d), rowblk(d), rowblk(p2d.shape[1]), const(g.shape), const(b.shape),
                  const(wpg.shape), const(wpp.shape)],
        out_specs=rowblk(d),
        out_shape=jax.ShapeDtypeStruct((t, d), F32),
        compiler_params=pltpu.CompilerParams(dimension_semantics=("arbitrary",), vmem_limit_bytes=VMEM_LIMIT),
        name="final_ln_ple",
    )(x1, moe, p2d, g, b, wpg, wpp)


def _pad_lanes(w, width=LANES):
    return jnp.pad(w, ((0, 0), (0, width - w.shape[1])))


def _row(v, width=None):
    v = v.astype(F32).reshape(1, -1)
    return v if width is None else _pad_lanes(v, width)


def kernel(x, p, w_in, ssm_conv_w, ssm_conv_b, ssm_dt_bias, ssm_a_log, ssm_d, ssm_norm_w, lstm_i_bias, lstm_f_bias, lstm_norm_w, w_branch_ssm, w_branch_lstm, w_out, ln1_g, ln1_b, moe_w_group, moe_b_group, moe_w_expert, moe_b_expert, moe_w_gate, moe_w_up, moe_w_down, ln2_g, ln2_b, ple_w_proj, ple_w_gate):
    depth = w_in.shape[0]
    bsz, seq, d = x.shape
    t = bsz * seq
    alpha = (2.0 * depth) ** 0.25
    for i in range(depth):
        splits = []
        c0 = 0
        for sz in IN_PROJ_SIZES:
            splits.append(w_in[i][:, c0:c0 + sz])
            c0 += sz
        w_z, w_xbc, w_dt, w_q, w_k, w_v, w_o, w_i, w_f, w_gs, w_gl = splits
        bf = lambda w: w.astype(BF16)
        y_ssm = _ssd_mixer(
            x, bf(w_z), bf(w_xbc), bf(_pad_lanes(w_dt)),
            ssm_conv_w[i].astype(F32), _row(ssm_conv_b[i]), _row(ssm_dt_bias[i], LANES),
            _row(-jnp.exp(ssm_a_log[i].astype(F32)), LANES),
            _row(jnp.repeat(ssm_d[i].astype(F32), SSM_HEAD_DIM)), _row(ssm_norm_w[i]))
        w_if = jnp.concatenate([_pad_lanes(w_i), _pad_lanes(w_f)], axis=1)
        b_if = jnp.concatenate([_row(lstm_i_bias[i], LANES), _row(lstm_f_bias[i], LANES)], axis=1)
        y_lstm = _mlstm_mixer(x, bf(w_q * (LSTM_QK ** -0.5)), bf(w_k), bf(w_v), bf(w_o), bf(w_if), b_if,
                              _row(lstm_norm_w[i]))
        w_r = _pad_lanes(jnp.concatenate([moe_w_group[i], moe_w_expert[i]], axis=1).astype(F32))
        w_r_hi = w_r.astype(BF16)
        w_r_lo = (w_r - w_r_hi.astype(F32)).astype(BF16)
        b_r = _row(jnp.concatenate([moe_b_group[i], moe_b_expert[i]]), LANES)
        tile = min(MOE_TILE, t)
        x1, xp, tab, cnt = _merge(x.reshape(t, d), y_ssm.reshape(t, -1), y_lstm.reshape(t, -1),
                                  bf(w_gs), bf(w_gl), bf(w_branch_ssm[i]), bf(w_branch_lstm[i]), bf(w_out[i]),
                                  _row(ln1_g[i]), _row(ln1_b[i]), w_r_hi, w_r_lo, b_r, alpha, tile)
        cnt = cnt.reshape(t // tile, SUBLANES, LANES)[:, 0, :MOE_EXPERTS].reshape(-1)
        moe = _moe(xp, tab, cnt, bf(moe_w_gate[i]), bf(moe_w_up[i]), bf(moe_w_down[i]), tile)
        x = _final(x1, moe, p[i].reshape(t, -1), _row(ln2_g[i]), _row(ln2_b[i]),
                   bf(ple_w_gate[i]), bf(ple_w_proj[i]), alpha).reshape(bsz, seq, d)
    return x
```

```python
import functools

import jax
import jax.numpy as jnp
from jax import lax
from jax.experimental import pallas as pl
from jax.experimental.pallas import tpu as pltpu

F32 = jnp.float32
BF16 = jnp.bfloat16

D_MODEL = 1024
PLE_DIM = 256
SSM_D_INNER = 1024
SSM_HEAD_DIM = 64
SSM_HEADS = 16
SSM_GROUPS = 4
SSM_STATE = 128
SSM_CONV = 4
SSM_XBC = SSM_D_INNER + 2 * SSM_GROUPS * SSM_STATE
LSTM_HEADS = 8
LSTM_QK = 64
LSTM_V = 128
LSTM_D_QK = LSTM_HEADS * LSTM_QK
LSTM_D_V = LSTM_HEADS * LSTM_V
CHUNK = 128
MOE_GROUPS = 8
MOE_PER_GROUP = 8
MOE_EXPERTS = 64
MOE_D_FF = 512
NORM_EPS = 1e-5
IN_PROJ_SIZES = (SSM_D_INNER, SSM_XBC, SSM_HEADS, LSTM_D_QK, LSTM_D_QK, LSTM_D_V, LSTM_D_V,
                 LSTM_HEADS, LSTM_HEADS, D_MODEL, D_MODEL)

LANES = 128
SUBLANES = 8
VMEM_LIMIT = 56 * 1024 * 1024

MIX_ROWS = 256
DENSE_ROWS = 512
MOE_TILE = 4096
MOE_BLOCKS = (128, 192, 256)
PACK_SLABS = D_MODEL // 2 // LANES
OUT_SLABS = D_MODEL // LANES


def _dot(a, b):
    return jnp.dot(a, b, preferred_element_type=F32)


def _dot_nt(a, b):
    return lax.dot_general(a, b, (((1,), (1,)), ((), ())), preferred_element_type=F32)


def _dot_tn(a, b):
    return lax.dot_general(a, b, (((0,), (0,)), ((), ())), preferred_element_type=F32)


def _sigmoid(x):
    return 1.0 / (1.0 + jnp.exp(-x))


def _softplus(x):
    return jnp.maximum(x, 0.0) + jnp.log(1.0 + jnp.exp(-jnp.abs(x)))


def _split3(x):
    hi = x.astype(BF16)
    r1 = x - hi.astype(F32)
    mid = r1.astype(BF16)
    lo = (r1 - mid.astype(F32)).astype(BF16)
    return hi, mid, lo


def _cumsum_rows(tril, x):
    hi, mid, lo = _split3(x)
    return _dot(tril, hi) + _dot(tril, mid) + _dot(tril, lo)


def _causal_masks():
    r = lax.broadcasted_iota(jnp.int32, (CHUNK, CHUNK), 0)
    c = lax.broadcasted_iota(jnp.int32, (CHUNK, CHUNK), 1)
    causal = r >= c
    return causal, jnp.where(causal, 1.0, 0.0).astype(BF16)


def _pair_cols(v, ha, hb, rows, lane):
    a = jnp.broadcast_to(v[:, ha:ha + 1], (rows, LANES))
    b = jnp.broadcast_to(v[:, hb:hb + 1], (rows, LANES))
    return jnp.where(lane < SSM_HEAD_DIM, a, b)


def _ssd_kernel(x_ref, wz_ref, wxbc_ref, wdt_ref, cw_ref, cb_ref, dtb_ref, aneg_ref, dexp_ref, nw_ref,
                y_ref, ext_ref, st_ref, *, rows):
    @pl.when(pl.program_id(1) == 0)
    def _():
        ext_ref[0:SUBLANES, :] = jnp.zeros((SUBLANES, SSM_XBC), F32)
        st_ref[...] = jnp.zeros(st_ref.shape, F32)

    xb = x_ref[...].astype(BF16)
    z = _dot(xb, wz_ref[...])
    ext_ref[SUBLANES:SUBLANES + rows, :] = _dot(xb, wxbc_ref[...])
    dtr = _dot(xb, wdt_ref[...])

    conv = cb_ref[...]
    for k in range(SSM_CONV):
        conv = conv + cw_ref[k:k + 1, :] * ext_ref[pl.ds(SUBLANES - SSM_CONV + 1 + k, rows), :]
    ext_ref[0:SUBLANES, :] = ext_ref[rows:rows + SUBLANES, :]
    u = conv * _sigmoid(conv)

    dt = _softplus(dtr + dtb_ref[...])
    da = dt * aneg_ref[...]
    causal, tril = _causal_masks()
    lane = lax.broadcasted_iota(jnp.int32, (CHUNK, LANES), 1)
    lane1 = lax.broadcasted_iota(jnp.int32, (1, LANES), 1)
    gw = SSM_D_INNER // SSM_GROUPS

    for ci in range(rows // CHUNK):
        r0 = ci * CHUNK
        dt_c = dt[r0:r0 + CHUNK]
        a_cs = _cumsum_rows(tril, da[r0:r0 + CHUNK])
        a_cs_t = a_cs.T
        dt_t = dt_c.T
        a_last = a_cs[CHUNK - 1:CHUNK, :]
        ea = jnp.exp(a_cs)
        sw = dt_c * jnp.exp(a_last - a_cs)
        ea_last = jnp.exp(a_last)
        for g in range(SSM_GROUPS):
            bm_g = u[r0:r0 + CHUNK, SSM_D_INNER + g * SSM_STATE:SSM_D_INNER + (g + 1) * SSM_STATE].astype(BF16)
            c0 = SSM_D_INNER + SSM_GROUPS * SSM_STATE
            cm_g = u[r0:r0 + CHUNK, c0 + g * SSM_STATE:c0 + (g + 1) * SSM_STATE].astype(BF16)
            xs_g = u[r0:r0 + CHUNK, g * gw:(g + 1) * gw]
            cb = _dot_nt(cm_g, bm_g)
            st_g = st_ref[:, g * gw:(g + 1) * gw]
            y_off = _dot(cm_g, st_g.astype(BF16))
            yd, ea_x, sw_x, eal_x = [], [], [], []
            for pr in range(2):
                ha = 4 * g + 2 * pr
                hb = ha + 1
                xs_p = xs_g[:, pr * LANES:(pr + 1) * LANES].astype(BF16)
                rhs = jnp.concatenate([jnp.where(lane < SSM_HEAD_DIM, xs_p, 0).astype(BF16),
                                       jnp.where(lane >= SSM_HEAD_DIM, xs_p, 0).astype(BF16)], axis=0)
                gs = []
                for h in (ha, hb):
                    colb = jnp.broadcast_to(a_cs[:, h:h + 1], (CHUNK, CHUNK))
                    rowb = jnp.broadcast_to(a_cs_t[h:h + 1, :], (CHUNK, CHUNK))
                    dec = jnp.exp(jnp.where(causal, colb - rowb, -jnp.inf))
                    gs.append((cb * dec * jnp.broadcast_to(dt_t[h:h + 1, :], (CHUNK, CHUNK))).astype(BF16))
                yd.append(_dot(jnp.concatenate(gs, axis=1), rhs))
                ea_x.append(_pair_cols(ea, ha, hb, CHUNK, lane))
                sw_x.append(_pair_cols(sw, ha, hb, CHUNK, lane))
                eal_x.append(_pair_cols(ea_last, ha, hb, 1, lane1))
            y_g = (jnp.concatenate(yd, axis=1) + y_off * jnp.concatenate(ea_x, axis=1)
                   + dexp_ref[:, g * gw:(g + 1) * gw] * xs_g)
            xw = (xs_g * jnp.concatenate(sw_x, axis=1)).astype(BF16)
            st_ref[:, g * gw:(g + 1) * gw] = st_g * jnp.concatenate(eal_x, axis=1) + _dot_tn(bm_g, xw)
            zz = z[r0:r0 + CHUNK, g * gw:(g + 1) * gw]
            yy = y_g * (zz * _sigmoid(zz))
            ms = jnp.mean(yy * yy, axis=-1, keepdims=True)
            y_ref[r0:r0 + CHUNK, g * gw:(g + 1) * gw] = (
                yy * lax.rsqrt(ms + NORM_EPS) * nw_ref[:, g * gw:(g + 1) * gw]).astype(BF16)


def _ssd_mixer(x, wz, wxbc, wdt, cw, cb, dtb, aneg, dexp, nw):
    b, s, d = x.shape
    rows = MIX_ROWS
    const = lambda shape: pl.BlockSpec(shape, lambda i, j: (0,) * len(shape))
    return pl.pallas_call(
        functools.partial(_ssd_kernel, rows=rows),
        grid=(b, s // rows),
        in_specs=[pl.BlockSpec((None, rows, d), lambda i, j: (i, j, 0)),
                  const(wz.shape), const(wxbc.shape), const(wdt.shape), const(cw.shape), const(cb.shape),
                  const(dtb.shape), const(aneg.shape), const(dexp.shape), const(nw.shape)],
        out_specs=pl.BlockSpec((None, rows, SSM_D_INNER), lambda i, j: (i, j, 0)),
        out_shape=jax.ShapeDtypeStruct((b, s, SSM_D_INNER), BF16),
        scratch_shapes=[pltpu.VMEM((SUBLANES + rows, SSM_XBC), F32),
                        pltpu.VMEM((SSM_STATE, SSM_D_INNER), F32)],
        compiler_params=pltpu.CompilerParams(dimension_semantics=("arbitrary", "arbitrary"),
                                             vmem_limit_bytes=VMEM_LIMIT),
        name="ssd_mixer",
    )(x, wz, wxbc, wdt, cw, cb, dtb, aneg, dexp, nw)


def _mlstm_kernel(x_ref, wq_ref, wk_ref, wv_ref, wo_ref, wif_ref, bif_ref, nw_ref,
                  y_ref, c_ref, m_ref, *, rows):
    @pl.when(pl.program_id(1) == 0)
    def _():
        c_ref[...] = jnp.zeros(c_ref.shape, F32)
        m_ref[...] = jnp.full(m_ref.shape, -jnp.inf, F32)

    xb = x_ref[...].astype(BF16)
    q = _dot(xb, wq_ref[...])
    k = _dot(xb, wk_ref[...])
    v = _dot(xb, wv_ref[...])
    o = _dot(xb, wo_ref[...])
    gif = _dot(xb, wif_ref[...]) + bif_ref[...]
    causal, tril = _causal_masks()
    lane = lax.broadcasted_iota(jnp.int32, (CHUNK, LANES), 1)
    lane1 = lax.broadcasted_iota(jnp.int32, (1, LANES), 1)
    row = lax.broadcasted_iota(jnp.int32, (LANES, 1), 0)
    head_lane = lane < LSTM_HEADS
    ones_col = jnp.where(lane == 0, 1.0, 0.0).astype(BF16)

    for ci in range(rows // CHUNK):
        r0 = ci * CHUNK
        log_i = gif[r0:r0 + CHUNK, 0:LANES]
        f_pre = gif[r0:r0 + CHUNK, LANES:2 * LANES]
        log_f = jnp.where(head_lane, jnp.minimum(f_pre, 0.0) - jnp.log(1.0 + jnp.exp(-jnp.abs(f_pre))), 0.0)
        fcum = _cumsum_rows(tril, log_f)
        f_tot = fcum[CHUNK - 1:CHUNK, :]
        a_end = f_tot - fcum + log_i
        m_loc = jnp.max(a_end, axis=0, keepdims=True)
        w_end = jnp.exp(a_end - m_loc)
        fcum_t = fcum.T
        g_t = (log_i - fcum).T
        m_prev = m_ref[...]
        m_new = jnp.maximum(f_tot + m_prev, m_loc)
        s_prev = jnp.exp(f_tot + m_prev - m_new)
        s_loc = jnp.exp(m_loc - m_new)
        m_ref[...] = jnp.where(lane1 < LSTM_HEADS, m_new, -jnp.inf)
        inter_log_all = fcum + m_prev

        for pr in range(LSTM_HEADS // 2):
            q_p = q[r0:r0 + CHUNK, pr * LANES:(pr + 1) * LANES]
            k_p = k[r0:r0 + CHUNK, pr * LANES:(pr + 1) * LANES]
            k_pb = k_p.astype(BF16)
            c_pair = c_ref[pr]
            c_pair_b = c_pair.astype(BF16)
            c_new = c_pair * jnp.where(row < LSTM_QK,
                                       jnp.broadcast_to(s_prev[:, 2 * pr:2 * pr + 1], (LANES, 1)),
                                       jnp.broadcast_to(s_prev[:, 2 * pr + 1:2 * pr + 2], (LANES, 1)))
            for hh in range(2):
                h = 2 * pr + hh
                in_head = (lane < LSTM_QK) if hh == 0 else (lane >= LSTM_QK)
                v_h = v[r0:r0 + CHUNK, h * LSTM_V:(h + 1) * LSTM_V].astype(BF16)
                v_ext = jnp.concatenate([v_h, ones_col], axis=1)
                q_m = jnp.where(in_head, q_p, 0.0)
                colb = jnp.broadcast_to(fcum[:, h:h + 1], (CHUNK, CHUNK))
                rowb = jnp.broadcast_to(g_t[h:h + 1, :], (CHUNK, CHUNK))
                log_d = jnp.where(causal, colb + rowb, -jnp.inf)
                inter_log = inter_log_all[:, h:h + 1]
                m_t = jnp.maximum(inter_log, jnp.max(log_d, axis=-1, keepdims=True))
                scores = _dot_nt(q_m.astype(BF16), k_pb) * jnp.exp(log_d - m_t)
                inter_w = jnp.exp(inter_log - m_t)
                lhs = jnp.concatenate([scores.astype(BF16), (q_m * inter_w).astype(BF16)], axis=1)
                rhs = jnp.concatenate([v_ext, c_pair_b], axis=0)
                res = _dot(lhs, rhs)
                num = res[:, 0:LSTM_V]
                den = res[:, LSTM_V:LSTM_V + 1]
                hv = num / jnp.maximum(jnp.abs(den), jnp.exp(-m_t))
                ms = jnp.mean(hv * hv, axis=-1, keepdims=True)
                o_h = o[r0:r0 + CHUNK, h * LSTM_V:(h + 1) * LSTM_V]
                y_ref[r0:r0 + CHUNK, h * LSTM_V:(h + 1) * LSTM_V] = (
                    _sigmoid(o_h) * (hv * lax.rsqrt(ms + NORM_EPS) * nw_ref[:, h * LSTM_V:(h + 1) * LSTM_V])
                ).astype(BF16)
                kw = jnp.where(in_head, k_p * jnp.broadcast_to(w_end[:, h:h + 1], (CHUNK, LANES)), 0.0)
                c_new = c_new + jnp.broadcast_to(s_loc[:, h:h + 1], (1, 1)) * _dot_tn(kw.astype(BF16), v_ext)
            c_ref[pr] = c_new


def _mlstm_mixer(x, wq, wk, wv, wo, wif, bif, nw):
    b, s, d = x.shape
    rows = MIX_ROWS
    const = lambda shape: pl.BlockSpec(shape, lambda i, j: (0,) * len(shape))
    return pl.pallas_call(
        functools.partial(_mlstm_kernel, rows=rows),
        grid=(b, s // rows),
        in_specs=[pl.BlockSpec((None, rows, d), lambda i, j: (i, j, 0)),
                  const(wq.shape), const(wk.shape), const(wv.shape), const(wo.shape), const(wif.shape),
                  const(bif.shape), const(nw.shape)],
        out_specs=pl.BlockSpec((None, rows, LSTM_D_V), lambda i, j: (i, j, 0)),
        out_shape=jax.ShapeDtypeStruct((b, s, LSTM_D_V), BF16),
        scratch_shapes=[pltpu.VMEM((LSTM_HEADS // 2, 2 * LSTM_QK, 2 * LSTM_V), F32),
                        pltpu.VMEM((1, LANES), F32)],
        compiler_params=pltpu.CompilerParams(dimension_semantics=("arbitrary", "arbitrary"),
                                             vmem_limit_bytes=VMEM_LIMIT),
        name="mlstm_mixer",
    )(x, wq, wk, wv, wo, wif, bif, nw)


def _layer_norm(t, g, b):
    mu = jnp.mean(t, axis=-1, keepdims=True)
    tc = t - mu
    var = jnp.mean(tc * tc, axis=-1, keepdims=True)
    return tc * lax.rsqrt(var + NORM_EPS) * g + b


def _pack_pair(a, b):
    pa = lax.bitcast_convert_type(a.astype(BF16).astype(F32), jnp.uint32)
    pb = lax.bitcast_convert_type(b.astype(BF16).astype(F32), jnp.uint32)
    return pa | (pb >> 16)


def _unpack_pair(w):
    return (lax.bitcast_convert_type(w & jnp.uint32(0xFFFF0000), F32),
            lax.bitcast_convert_type(w << 16, F32))


def _route(logits):
    rows = logits.shape[0]
    lane = lax.broadcasted_iota(jnp.int32, (rows, LANES), 1).astype(F32)
    big = float(LANES)
    neg = -jnp.inf
    gl = jnp.where(lane < MOE_GROUPS, logits, neg)
    gmax = jnp.max(gl, axis=-1, keepdims=True)
    gidx = jnp.min(jnp.where(gl == gmax, lane, big), axis=-1, keepdims=True)
    gsum = jnp.sum(jnp.where(lane < MOE_GROUPS, jnp.exp(logits - gmax), 0.0), axis=-1, keepdims=True)
    grp_p = 1.0 / gsum
    lo = MOE_GROUPS + gidx * MOE_PER_GROUP
    el = jnp.where(lane >= lo, jnp.where(lane < lo + MOE_PER_GROUP, logits, neg), neg)
    m1 = jnp.max(el, axis=-1, keepdims=True)
    i1 = jnp.min(jnp.where(el == m1, lane, big), axis=-1, keepdims=True)
    el2 = jnp.where(lane == i1, neg, el)
    m2 = jnp.max(el2, axis=-1, keepdims=True)
    i2 = jnp.min(jnp.where(el2 == m2, lane, big), axis=-1, keepdims=True)
    e21 = jnp.exp(m2 - m1)
    g1 = grp_p / (1.0 + e21)
    g2 = grp_p * e21 / (1.0 + e21)
    return i1 - MOE_GROUPS, i2 - MOE_GROUPS, g1, g2


def _merge_kernel(x_ref, ys_ref, yl_ref, wgs_ref, wgl_ref, wa_ref, wb_ref, wo_ref, g_ref, b_ref,
                  wr_hi_ref, wr_lo_ref, br_ref, x1_ref, xp_ref, tab_ref, cnt_ref, carry_ref,
                  *, alpha, steps_per_tile):
    @pl.when(pl.program_id(0) % steps_per_tile == 0)
    def _():
        carry_ref[...] = jnp.zeros(carry_ref.shape, F32)

    x = x_ref[...]
    rows = x.shape[0]
    xb = x.astype(BF16)
    merged = (_sigmoid(_dot(xb, wgs_ref[...])) * _dot(ys_ref[...], wa_ref[...])
              + _sigmoid(_dot(xb, wgl_ref[...])) * _dot(yl_ref[...], wb_ref[...]))
    x1 = _layer_norm(alpha * x + _dot(merged.astype(BF16), wo_ref[...]), g_ref[...], b_ref[...])
    x1_ref[...] = x1
    half = x1.shape[1] // 2
    xp = _pack_pair(x1[:, :half], x1[:, half:])
    for c in range(PACK_SLABS):
        xp_ref[pl.ds(c, rows, stride=PACK_SLABS), :] = xp[:, c * LANES:(c + 1) * LANES]
    x_hi = x1.astype(BF16)
    x_lo = (x1 - x_hi.astype(F32)).astype(BF16)
    logits = (_dot(x_hi, wr_hi_ref[...]) + _dot(x_lo, wr_hi_ref[...]) + _dot(x_hi, wr_lo_ref[...])
              + br_ref[...])
    e1, e2, g1, g2 = _route(logits)
    lane = lax.broadcasted_iota(jnp.int32, (rows, LANES), 1)
    lane_f = lane.astype(F32)
    oh1 = lane_f == e1
    oh2 = lane_f == e2
    ohs = jnp.where(oh1, 1.0, jnp.where(oh2, 1.0, 0.0))
    r_i = lax.broadcasted_iota(jnp.int32, (rows, rows), 0)
    c_i = lax.broadcasted_iota(jnp.int32, (rows, rows), 1)
    before = jnp.where(r_i > c_i, 1.0, 0.0).astype(BF16)
    seen = _dot(before, ohs.astype(BF16)) + carry_ref[...]
    r1 = jnp.sum(jnp.where(oh1, seen, 0.0), axis=-1, keepdims=True)
    r2 = jnp.sum(jnp.where(oh2, seen, 0.0), axis=-1, keepdims=True)
    carry = carry_ref[...] + jnp.sum(ohs, axis=0, keepdims=True)
    carry_ref[...] = carry
    cnt_ref[...] = jnp.broadcast_to(carry, cnt_ref.shape).astype(jnp.int32)
    as_int = lambda v: v.astype(jnp.int32)
    as_bits = lambda v: lax.bitcast_convert_type(v, jnp.int32)
    tab = jnp.where(lane == 0, as_int(e1), jnp.where(lane == 1, as_int(e2), jnp.where(
        lane == 2, as_int(r1) * PACK_SLABS, jnp.where(lane == 3, as_int(r2) * PACK_SLABS, jnp.where(
            lane == 4, as_bits(g1), jnp.where(lane == 5, as_bits(g2), 0))))))
    tab_ref[...] = tab.T[0:SUBLANES, :]


def _merge(x2d, ys, yl, wgs, wgl, wa, wb, wo, g, b, wr_hi, wr_lo, br, alpha, tile):
    t, d = x2d.shape
    rows = DENSE_ROWS
    spt = tile // rows
    const = lambda shape: pl.BlockSpec(shape, lambda i: (0,) * len(shape))
    rowblk = lambda w: pl.BlockSpec((rows, w), lambda i: (i, 0))
    return pl.pallas_call(
        functools.partial(_merge_kernel, alpha=alpha, steps_per_tile=spt),
        grid=(t // rows,),
        in_specs=[rowblk(d), rowblk(SSM_D_INNER), rowblk(LSTM_D_V),
                  const(wgs.shape), const(wgl.shape), const(wa.shape), const(wb.shape), const(wo.shape),
                  const(g.shape), const(b.shape), const(wr_hi.shape), const(wr_lo.shape), const(br.shape)],
        out_specs=[rowblk(d),
                   pl.BlockSpec((rows * PACK_SLABS, LANES), lambda i: (i, 0)),
                   pl.BlockSpec((SUBLANES, rows), lambda i: (i // spt, i % spt)),
                   pl.BlockSpec((SUBLANES, LANES), lambda i: (i // spt, 0))],
        out_shape=[jax.ShapeDtypeStruct((t, d), F32),
                   jax.ShapeDtypeStruct((t * PACK_SLABS, LANES), jnp.uint32),
                   jax.ShapeDtypeStruct((t // tile * SUBLANES, tile), jnp.int32),
                   jax.ShapeDtypeStruct((t // tile * SUBLANES, LANES), jnp.int32)],
        scratch_shapes=[pltpu.VMEM((1, LANES), F32)],
        compiler_params=pltpu.CompilerParams(dimension_semantics=("arbitrary",), vmem_limit_bytes=VMEM_LIMIT),
        name="merge_ln_route",
    )(x2d, ys, yl, wgs, wgl, wa, wb, wo, g, b, wr_hi, wr_lo, br)


def _moe_kernel(cnt_ref, tab_ref, xp_ref, wg_ref, wu_ref, wd_ref, out_ref, xys, ybuf, off_ref, dst_ref,
                *, tile):
    i = pl.program_id(0)
    j = pl.program_id(1)
    group = SUBLANES * PACK_SLABS
    half = PACK_SLABS * LANES

    @pl.when(j == 0)
    def _():
        def offsets(e, acc):
            padded = (cnt_ref[i * MOE_EXPERTS + e] + SUBLANES - 1) // SUBLANES * group
            off_ref[e] = acc
            end = acc + padded

            @pl.when(padded > 0)
            def _():
                xys[pl.ds(pl.multiple_of(end - group, group), group), :] = jnp.zeros((group, LANES), jnp.uint32)
            return end

        total = lax.fori_loop(0, MOE_EXPERTS, offsets, 0)
        off_ref[MOE_EXPERTS] = total
        tail = MOE_BLOCKS[-1] * PACK_SLABS
        xys[pl.ds(pl.multiple_of(total, group), tail), :] = jnp.zeros((tail, LANES), jnp.uint32)

        def distribute(t, c):
            d1 = pl.multiple_of(off_ref[tab_ref[t]] + tab_ref[2 * tile + t], PACK_SLABS)
            d2 = pl.multiple_of(off_ref[tab_ref[tile + t]] + tab_ref[3 * tile + t], PACK_SLABS)
            dst_ref[t] = d1
            dst_ref[tile + t] = d2
            row = xp_ref[pl.ds(pl.multiple_of(t * PACK_SLABS, PACK_SLABS), PACK_SLABS), :]
            xys[pl.ds(d1, PACK_SLABS), :] = row
            xys[pl.ds(d2, PACK_SLABS), :] = row
            return c

        lax.fori_loop(0, tile, distribute, 0, unroll=SUBLANES)

    start = off_ref[j]
    n = off_ref[j + 1] - start

    def expert_block(base, rows, valid):
        words = [xys[pl.ds(base + c, rows, stride=PACK_SLABS), :] for c in range(PACK_SLABS)]
        parts = [_unpack_pair(w) for w in words]
        x_hi = jnp.concatenate([p[0].astype(BF16) for p in parts], axis=1)
        x_lo = jnp.concatenate([p[1].astype(BF16) for p in parts], axis=1)
        hg = _dot(x_hi, wg_ref[0, 0:half, :]) + _dot(x_lo, wg_ref[0, half:2 * half, :])
        hu = _dot(x_hi, wu_ref[0, 0:half, :]) + _dot(x_lo, wu_ref[0, half:2 * half, :])
        y = _dot((hg * _sigmoid(hg) * hu).astype(BF16), wd_ref[0])
        for c in range(PACK_SLABS):
            ybuf[pl.ds(c, rows, stride=PACK_SLABS), :] = _pack_pair(
                y[:, c * LANES:(c + 1) * LANES], y[:, half + c * LANES:half + (c + 1) * LANES])

        def copy(g, c):
            r = pl.multiple_of(g * group, group)
            xys[pl.ds(base + r, group), :] = ybuf[pl.ds(r, group), :]
            return c

        lax.fori_loop(0, valid // group, copy, 0)

    lo = 0
    for rows in MOE_BLOCKS:
        hi = rows * PACK_SLABS

        @pl.when(jnp.logical_and(n > lo, n <= hi))
        def _(rows=rows):
            expert_block(pl.multiple_of(start, group), rows, n)
        lo = hi

    @pl.when(n > lo)
    def _():
        step = MOE_BLOCKS[0] * PACK_SLABS

        def block(bi, carry):
            expert_block(pl.multiple_of(start + bi * step, group), MOE_BLOCKS[0],
                         jnp.minimum(step, n - bi * step))
            return carry

        lax.fori_loop(0, (n + step - 1) // step, block, 0)

    @pl.when(j == MOE_EXPERTS - 1)
    def _():
        def combine(t, c):
            g1 = lax.bitcast_convert_type(tab_ref[4 * tile + t], F32)
            g2 = lax.bitcast_convert_type(tab_ref[5 * tile + t], F32)
            a_hi, a_lo = _unpack_pair(xys[pl.ds(pl.multiple_of(dst_ref[t], PACK_SLABS), PACK_SLABS), :])
            b_hi, b_lo = _unpack_pair(xys[pl.ds(pl.multiple_of(dst_ref[tile + t], PACK_SLABS), PACK_SLABS), :])
            out_ref[pl.ds(pl.multiple_of(t * OUT_SLABS, OUT_SLABS), OUT_SLABS), :] = jnp.concatenate(
                [g1 * a_hi + g2 * b_hi, g1 * a_lo + g2 * b_lo], axis=0)
            return c

        lax.fori_loop(0, tile, combine, 0, unroll=SUBLANES)


def _moe(xp, tab, cnt, wg, wu, wd, tile):
    t = xp.shape[0] // PACK_SLABS
    d = D_MODEL
    n_tiles = t // tile
    seg_rows = 2 * tile + MOE_EXPERTS * SUBLANES + MOE_BLOCKS[-1]
    grid_spec = pltpu.PrefetchScalarGridSpec(
        num_scalar_prefetch=1,
        grid=(n_tiles, MOE_EXPERTS),
        in_specs=[pl.BlockSpec((SUBLANES * tile,), lambda i, j, c: (i,), memory_space=pltpu.SMEM),
                  pl.BlockSpec((tile * PACK_SLABS, LANES), lambda i, j, c: (i, 0), pipeline_mode=pl.Buffered(1)),
                  pl.BlockSpec((1, d, MOE_D_FF), lambda i, j, c: (j, 0, 0)),
                  pl.BlockSpec((1, d, MOE_D_FF), lambda i, j, c: (j, 0, 0)),
                  pl.BlockSpec((1, MOE_D_FF, d), lambda i, j, c: (j, 0, 0))],
        out_specs=pl.BlockSpec((tile * OUT_SLABS, LANES), lambda i, j, c: (i, 0), pipeline_mode=pl.Buffered(1)),
        scratch_shapes=[pltpu.VMEM((seg_rows * PACK_SLABS, LANES), jnp.uint32),
                        pltpu.VMEM((MOE_BLOCKS[-1] * PACK_SLABS, LANES), jnp.uint32),
                        pltpu.SMEM((LANES,), jnp.int32), pltpu.SMEM((2 * tile,), jnp.int32)],
    )
    return pl.pallas_call(
        functools.partial(_moe_kernel, tile=tile),
        grid_spec=grid_spec,
        out_shape=jax.ShapeDtypeStruct((t * OUT_SLABS, LANES), F32),
        compiler_params=pltpu.CompilerParams(dimension_semantics=("arbitrary", "arbitrary"),
                                             vmem_limit_bytes=VMEM_LIMIT),
        name="moe_experts",
    )(cnt, tab, xp, wg, wu, wd)


def _final_kernel(x1_ref, moe_ref, p_ref, g_ref, b_ref, wpg_ref, wpp_ref, out_ref, *, alpha):
    rows = x1_ref.shape[0]
    moe = jnp.concatenate([moe_ref[pl.ds(c, rows, stride=OUT_SLABS), :] for c in range(OUT_SLABS)], axis=1)
    x2 = _layer_norm(alpha * x1_ref[...] + moe, g_ref[...], b_ref[...])
    gate = _sigmoid(_dot(x2.astype(BF16), wpg_ref[...]))
    out_ref[...] = x2 + gate * _dot(p_ref[...].astype(BF16), wpp_ref[...])


def _final(x1, moe, p2d, g, b, wpg, wpp, alpha):
    t, d = x1.shape
    rows = DENSE_ROWS
    const = lambda shape: pl.BlockSpec(shape, lambda i: (0,) * len(shape))
    rowblk = lambda w: pl.BlockSpec((rows, w), lambda i: (i, 0))
    return pl.pallas_call(
        functools.partial(_final_kernel, alpha=alpha),
        grid=(t // rows,),
        in_specs=[rowblk(d), pl.BlockSpec((rows * OUT_SLABS, LANES), lambda i: (i, 0)), rowblk(p2d.shape[1]),
                  const(g.shape), const(b.shape), const(wpg.shape), const(wpp.shape)],
        out_specs=rowblk(d),
        out_shape=jax.ShapeDtypeStruct((t, d), F32),
        compiler_params=pltpu.CompilerParams(dimension_semantics=("arbitrary",), vmem_limit_bytes=VMEM_LIMIT),
        name="final_ln_ple",
    )(x1, moe, p2d, g, b, wpg, wpp)


def _pad_lanes(w, width=LANES):
    return jnp.pad(w, ((0, 0), (0, width - w.shape[1])))


def _row(v, width=None):
    v = v.astype(F32).reshape(1, -1)
    return v if width is None else _pad_lanes(v, width)


def kernel(x, p, w_in, ssm_conv_w, ssm_conv_b, ssm_dt_bias, ssm_a_log, ssm_d, ssm_norm_w, lstm_i_bias, lstm_f_bias, lstm_norm_w, w_branch_ssm, w_branch_lstm, w_out, ln1_g, ln1_b, moe_w_group, moe_b_group, moe_w_expert, moe_b_expert, moe_w_gate, moe_w_up, moe_w_down, ln2_g, ln2_b, ple_w_proj, ple_w_gate):
    depth = w_in.shape[0]
    bsz, seq, d = x.shape
    t = bsz * seq
    alpha = (2.0 * depth) ** 0.25
    for i in range(depth):
        splits = []
        c0 = 0
        for sz in IN_PROJ_SIZES:
            splits.append(w_in[i][:, c0:c0 + sz])
            c0 += sz
        w_z, w_xbc, w_dt, w_q, w_k, w_v, w_o, w_i, w_f, w_gs, w_gl = splits
        bf = lambda w: w.astype(BF16)
        y_ssm = _ssd_mixer(
            x, bf(w_z), bf(w_xbc), bf(_pad_lanes(w_dt)),
            ssm_conv_w[i].astype(F32), _row(ssm_conv_b[i]), _row(ssm_dt_bias[i], LANES),
            _row(-jnp.exp(ssm_a_log[i].astype(F32)), LANES),
            _row(jnp.repeat(ssm_d[i].astype(F32), SSM_HEAD_DIM)), _row(ssm_norm_w[i]))
        w_if = jnp.concatenate([_pad_lanes(w_i), _pad_lanes(w_f)], axis=1)
        b_if = jnp.concatenate([_row(lstm_i_bias[i], LANES), _row(lstm_f_bias[i], LANES)], axis=1)
        y_lstm = _mlstm_mixer(x, bf(w_q * (LSTM_QK ** -0.5)), bf(w_k), bf(w_v), bf(w_o), bf(w_if), b_if,
                              _row(lstm_norm_w[i]))
        w_r = _pad_lanes(jnp.concatenate([moe_w_group[i], moe_w_expert[i]], axis=1).astype(F32))
        w_r_hi = w_r.astype(BF16)
        w_r_lo = (w_r - w_r_hi.astype(F32)).astype(BF16)
        b_r = _row(jnp.concatenate([moe_b_group[i], moe_b_expert[i]]), LANES)
        tile = min(MOE_TILE, t)
        x1, xp, tab, cnt = _merge(x.reshape(t, d), y_ssm.reshape(t, -1), y_lstm.reshape(t, -1),
                                  bf(w_gs), bf(w_gl), bf(w_branch_ssm[i]), bf(w_branch_lstm[i]), bf(w_out[i]),
                                  _row(ln1_g[i]), _row(ln1_b[i]), w_r_hi, w_r_lo, b_r, alpha, tile)
        cnt = cnt.reshape(t // tile, SUBLANES, LANES)[:, 0, :MOE_EXPERTS].reshape(-1)
        moe = _moe(xp, tab.reshape(-1), cnt, bf(moe_w_gate[i]), bf(moe_w_up[i]), bf(moe_w_down[i]), tile)
        x = _final(x1, moe, p[i].reshape(t, -1), _row(ln2_g[i]), _row(ln2_b[i]),
                   bf(ple_w_gate[i]), bf(ple_w_proj[i]), alpha).reshape(bsz, seq, d)
    return x
```

```python
import functools

import jax
import jax.numpy as jnp
from jax import lax
from jax.experimental import pallas as pl
from jax.experimental.pallas import tpu as pltpu

F32 = jnp.float32
BF16 = jnp.bfloat16

D_MODEL = 1024
PLE_DIM = 256
SSM_D_INNER = 1024
SSM_HEAD_DIM = 64
SSM_HEADS = 16
SSM_GROUPS = 4
SSM_STATE = 128
SSM_CONV = 4
SSM_XBC = SSM_D_INNER + 2 * SSM_GROUPS * SSM_STATE
LSTM_HEADS = 8
LSTM_QK = 64
LSTM_V = 128
LSTM_D_QK = LSTM_HEADS * LSTM_QK
LSTM_D_V = LSTM_HEADS * LSTM_V
CHUNK = 128
MOE_GROUPS = 8
MOE_PER_GROUP = 8
MOE_EXPERTS = 64
MOE_D_FF = 512
NORM_EPS = 1e-5
IN_PROJ_SIZES = (SSM_D_INNER, SSM_XBC, SSM_HEADS, LSTM_D_QK, LSTM_D_QK, LSTM_D_V, LSTM_D_V,
                 LSTM_HEADS, LSTM_HEADS, D_MODEL, D_MODEL)

LANES = 128
SUBLANES = 8
VMEM_LIMIT = 56 * 1024 * 1024

MIX_ROWS = 256
MIX_SEQS = 2
DENSE_ROWS = 512
MOE_TILE = 4096
MOE_BLOCKS = (128, 192, 256)
PACK_SLABS = D_MODEL // 2 // LANES
OUT_SLABS = D_MODEL // LANES


def _dot(a, b):
    return jnp.dot(a, b, preferred_element_type=F32)


def _dot_nt(a, b):
    return lax.dot_general(a, b, (((1,), (1,)), ((), ())), preferred_element_type=F32)


def _dot_tn(a, b):
    return lax.dot_general(a, b, (((0,), (0,)), ((), ())), preferred_element_type=F32)


def _sigmoid(x):
    return 1.0 / (1.0 + jnp.exp(-x))


def _softplus(x):
    return jnp.maximum(x, 0.0) + jnp.log(1.0 + jnp.exp(-jnp.abs(x)))


def _split3(x):
    hi = x.astype(BF16)
    r1 = x - hi.astype(F32)
    mid = r1.astype(BF16)
    lo = (r1 - mid.astype(F32)).astype(BF16)
    return hi, mid, lo


def _cumsum_rows(tril, x):
    hi, mid, lo = _split3(x)
    return _dot(tril, hi) + _dot(tril, mid) + _dot(tril, lo)


def _causal_masks():
    r = lax.broadcasted_iota(jnp.int32, (CHUNK, CHUNK), 0)
    c = lax.broadcasted_iota(jnp.int32, (CHUNK, CHUNK), 1)
    causal = r >= c
    return causal, jnp.where(causal, 1.0, 0.0).astype(BF16)


def _pair_cols(v, ha, hb, rows, lane):
    a = jnp.broadcast_to(v[:, ha:ha + 1], (rows, LANES))
    b = jnp.broadcast_to(v[:, hb:hb + 1], (rows, LANES))
    return jnp.where(lane < SSM_HEAD_DIM, a, b)


def _ssd_kernel(x_ref, wz_ref, wxbc_ref, wdt_ref, cw_ref, cb_ref, dtb_ref, aneg_ref, dexp_ref, nw_ref, ex_ref,
                y_ref, ext_ref, u_ref, st_ref, *, rows, nseq):
    n_slab = SSM_XBC // LANES
    tail = SUBLANES - SSM_CONV + 1

    @pl.when(pl.program_id(1) == 0)
    def _():
        ext_ref[:, :, 0:SUBLANES, :] = jnp.zeros((nseq, n_slab, SUBLANES, LANES), F32)
        st_ref[...] = jnp.zeros(st_ref.shape, F32)

    xb = x_ref[...].reshape(nseq * rows, x_ref.shape[-1]).astype(BF16)
    z = _dot(xb, wz_ref[...])
    dtr = _dot(xb, wdt_ref[...])
    for c2 in range(n_slab // 2):
        xbc = _dot(xb, wxbc_ref[:, c2 * 2 * LANES:(c2 + 1) * 2 * LANES])
        for s in range(nseq):
            for cc in range(2):
                ext_ref[s, 2 * c2 + cc, SUBLANES:SUBLANES + rows, :] = (
                    xbc[s * rows:(s + 1) * rows, cc * LANES:(cc + 1) * LANES])
    for s in range(nseq):
        for c in range(n_slab):
            conv = cb_ref[:, c * LANES:(c + 1) * LANES]
            for k in range(SSM_CONV):
                conv = conv + cw_ref[k:k + 1, c * LANES:(c + 1) * LANES] * ext_ref[s, c, pl.ds(tail + k, rows), :]
            u_ref[s, c] = conv * _sigmoid(conv)
            ext_ref[s, c, 0:SUBLANES, :] = ext_ref[s, c, rows:rows + SUBLANES, :]

    dt = _softplus(dtr + dtb_ref[...])
    da = dt * aneg_ref[...]
    causal, tril = _causal_masks()
    lane = lax.broadcasted_iota(jnp.int32, (CHUNK, LANES), 1)
    lane1 = lax.broadcasted_iota(jnp.int32, (1, LANES), 1)
    gw = SSM_D_INNER // SSM_GROUPS
    b_slab = SSM_D_INNER // LANES
    c_slab = b_slab + SSM_GROUPS

    for ci in range(rows // CHUNK):
        for s in range(nseq):
            r0 = ci * CHUNK
            q0 = s * rows + r0
            dt_c = dt[q0:q0 + CHUNK]
            a_cs = _cumsum_rows(tril, da[q0:q0 + CHUNK])
            a_cs_t = a_cs.T
            dt_t = dt_c.T
            a_last = a_cs[CHUNK - 1:CHUNK, :]
            ea_last = jnp.exp(a_last)
            ea_x = _dot(jnp.exp(a_cs).astype(BF16), ex_ref[...])
            sw_x = _dot((dt_c * jnp.exp(a_last - a_cs)).astype(BF16), ex_ref[...])
            for g in range(SSM_GROUPS):
                bm_g = u_ref[s, b_slab + g, r0:r0 + CHUNK, :].astype(BF16)
                cm_g = u_ref[s, c_slab + g, r0:r0 + CHUNK, :].astype(BF16)
                cb = _dot_nt(cm_g, bm_g)
                st_g = st_ref[s, :, g * gw:(g + 1) * gw]
                y_off = _dot(cm_g, st_g.astype(BF16))
                yy, xw, eal_x = [], [], []
                for pr in range(2):
                    ha = 4 * g + 2 * pr
                    hb = ha + 1
                    lo = g * gw + pr * LANES
                    xs_p = u_ref[s, 2 * g + pr, r0:r0 + CHUNK, :]
                    xs_pb = xs_p.astype(BF16)
                    rhs = jnp.concatenate([jnp.where(lane < SSM_HEAD_DIM, xs_pb, 0).astype(BF16),
                                           jnp.where(lane >= SSM_HEAD_DIM, xs_pb, 0).astype(BF16)], axis=0)
                    gs = []
                    for h in (ha, hb):
                        colb = jnp.broadcast_to(a_cs[:, h:h + 1], (CHUNK, CHUNK))
                        rowb = jnp.broadcast_to(a_cs_t[h:h + 1, :], (CHUNK, CHUNK))
                        dec = jnp.exp(jnp.where(causal, colb - rowb, -jnp.inf))
                        gs.append((cb * dec * jnp.broadcast_to(dt_t[h:h + 1, :], (CHUNK, CHUNK))).astype(BF16))
                    y_p = (_dot(jnp.concatenate(gs, axis=1), rhs)
                           + y_off[:, pr * LANES:(pr + 1) * LANES] * ea_x[:, lo:lo + LANES]
                           + dexp_ref[:, lo:lo + LANES] * xs_p)
                    zz = z[q0:q0 + CHUNK, lo:lo + LANES]
                    yy.append(y_p * (zz * _sigmoid(zz)))
                    xw.append((xs_p * sw_x[:, lo:lo + LANES]).astype(BF16))
                    eal_x.append(_pair_cols(ea_last, ha, hb, 1, lane1))
                st_ref[s, :, g * gw:(g + 1) * gw] = (st_g * jnp.concatenate(eal_x, axis=1)
                                                     + _dot_tn(bm_g, jnp.concatenate(xw, axis=1)))
                ms = sum(jnp.sum(t * t, axis=-1, keepdims=True) for t in yy) * (1.0 / gw)
                inv = lax.rsqrt(ms + NORM_EPS)
                for pr in range(2):
                    lo = g * gw + pr * LANES
                    y_ref[s, r0:r0 + CHUNK, lo:lo + LANES] = (yy[pr] * inv * nw_ref[:, lo:lo + LANES]).astype(BF16)


def _ssd_mixer(x, wz, wxbc, wdt, cw, cb, dtb, aneg, dexp, nw, ex):
    b, s, d = x.shape
    rows = MIX_ROWS
    nseq = MIX_SEQS
    const = lambda shape: pl.BlockSpec(shape, lambda i, j: (0,) * len(shape), pipeline_mode=pl.Buffered(1))
    return pl.pallas_call(
        functools.partial(_ssd_kernel, rows=rows, nseq=nseq),
        grid=(b // nseq, s // rows),
        in_specs=[pl.BlockSpec((nseq, rows, d), lambda i, j: (i, j, 0)),
                  const(wz.shape), const(wxbc.shape), const(wdt.shape), const(cw.shape), const(cb.shape),
                  const(dtb.shape), const(aneg.shape), const(dexp.shape), const(nw.shape), const(ex.shape)],
        out_specs=pl.BlockSpec((nseq, rows, SSM_D_INNER), lambda i, j: (i, j, 0)),
        out_shape=jax.ShapeDtypeStruct((b, s, SSM_D_INNER), BF16),
        scratch_shapes=[pltpu.VMEM((nseq, SSM_XBC // LANES, SUBLANES + rows, LANES), F32),
                        pltpu.VMEM((nseq, SSM_XBC // LANES, rows, LANES), F32),
                        pltpu.VMEM((nseq, SSM_STATE, SSM_D_INNER), F32)],
        compiler_params=pltpu.CompilerParams(dimension_semantics=("arbitrary", "arbitrary"),
                                             vmem_limit_bytes=VMEM_LIMIT),
        name="ssd_mixer",
    )(x, wz, wxbc, wdt, cw, cb, dtb, aneg, dexp, nw, ex)


def _mlstm_kernel(x_ref, wq_ref, wk_ref, wv_ref, wo_ref, wif_ref, bif_ref, nw_ref,
                  y_ref, c_ref, m_ref, *, rows, nseq):
    @pl.when(pl.program_id(1) == 0)
    def _():
        c_ref[...] = jnp.zeros(c_ref.shape, F32)
        m_ref[...] = jnp.full(m_ref.shape, -jnp.inf, F32)

    xb = x_ref[...].reshape(nseq * rows, x_ref.shape[-1]).astype(BF16)
    q = _dot(xb, wq_ref[...])
    k = _dot(xb, wk_ref[...])
    v = _dot(xb, wv_ref[...])
    o = _dot(xb, wo_ref[...])
    gif = _dot(xb, wif_ref[...]) + bif_ref[...]
    causal, tril = _causal_masks()
    lane = lax.broadcasted_iota(jnp.int32, (CHUNK, LANES), 1)
    lane1 = lax.broadcasted_iota(jnp.int32, (1, LANES), 1)
    row = lax.broadcasted_iota(jnp.int32, (LANES, 1), 0)
    head_lane = lane < LSTM_HEADS
    ones = jnp.ones((CHUNK, LANES), BF16)

    for ci in range(rows // CHUNK):
        for s in range(nseq):
            r0 = ci * CHUNK
            q0 = s * rows + r0
            log_i = gif[q0:q0 + CHUNK, 0:LANES]
            f_pre = gif[q0:q0 + CHUNK, LANES:2 * LANES]
            log_f = jnp.where(head_lane, jnp.minimum(f_pre, 0.0) - jnp.log(1.0 + jnp.exp(-jnp.abs(f_pre))), 0.0)
            fcum = _cumsum_rows(tril, log_f)
            f_tot = fcum[CHUNK - 1:CHUNK, :]
            gk = log_i - fcum
            m_loc = jnp.max(f_tot + gk, axis=0, keepdims=True)
            g_t = gk.T
            m_prev = m_ref[s]
            m_new = jnp.maximum(f_tot + m_prev, m_loc)
            s_prev = jnp.exp(f_tot + m_prev - m_new)
            s_loc = jnp.exp(m_loc - m_new)
            m_ref[s] = jnp.where(lane1 < LSTM_HEADS, m_new, -jnp.inf)
            w_shift = f_tot - m_loc

            for pr in range(LSTM_HEADS // 2):
                q_p = q[q0:q0 + CHUNK, pr * LANES:(pr + 1) * LANES]
                k_p = k[q0:q0 + CHUNK, pr * LANES:(pr + 1) * LANES]
                k_pb = k_p.astype(BF16)
                c_pair = c_ref[s, pr]
                c_pair_b = c_pair.astype(BF16)
                c_new = c_pair * jnp.where(row < LSTM_QK,
                                           jnp.broadcast_to(s_prev[:, 2 * pr:2 * pr + 1], (LANES, 1)),
                                           jnp.broadcast_to(s_prev[:, 2 * pr + 1:2 * pr + 2], (LANES, 1)))
                for hh in range(2):
                    h = 2 * pr + hh
                    in_head = (lane < LSTM_QK) if hh == 0 else (lane >= LSTM_QK)
                    v_h = v[q0:q0 + CHUNK, h * LSTM_V:(h + 1) * LSTM_V].astype(BF16)
                    v_ext = jnp.concatenate([v_h, ones], axis=1)
                    q_m = jnp.where(in_head, q_p, 0.0)
                    fcol = jnp.broadcast_to(fcum[:, h:h + 1], (CHUNK, CHUNK))
                    gcol = jnp.broadcast_to(gk[:, h:h + 1], (CHUNK, LANES))
                    rowb = jnp.broadcast_to(g_t[h:h + 1, :], (CHUNK, CHUNK))
                    log_d = jnp.where(causal, fcol + rowb, -jnp.inf)
                    inter_log = fcol + m_prev[:, h:h + 1]
                    m_t = jnp.maximum(inter_log, jnp.max(log_d, axis=-1, keepdims=True))
                    scores = _dot_nt(q_m.astype(BF16), k_pb) * jnp.exp(log_d - m_t)
                    inter_w = jnp.exp(inter_log - m_t)
                    lhs = jnp.concatenate([scores.astype(BF16), (q_m * inter_w).astype(BF16)], axis=1)
                    res = _dot(lhs, jnp.concatenate([v_ext, c_pair_b], axis=0))
                    hv = res[:, 0:LSTM_V] / jnp.maximum(jnp.abs(res[:, LSTM_V:2 * LSTM_V]), jnp.exp(-m_t))
                    ms = jnp.mean(hv * hv, axis=-1, keepdims=True)
                    o_h = o[q0:q0 + CHUNK, h * LSTM_V:(h + 1) * LSTM_V]
                    y_ref[s, r0:r0 + CHUNK, h * LSTM_V:(h + 1) * LSTM_V] = (
                        _sigmoid(o_h) * (hv * lax.rsqrt(ms + NORM_EPS) * nw_ref[:, h * LSTM_V:(h + 1) * LSTM_V])
                    ).astype(BF16)
                    kw = jnp.where(in_head, k_p * jnp.exp(gcol + w_shift[:, h:h + 1]), 0.0)
                    c_new = c_new + s_loc[:, h:h + 1] * _dot_tn(kw.astype(BF16), v_ext)
                c_ref[s, pr] = c_new


def _mlstm_mixer(x, wq, wk, wv, wo, wif, bif, nw):
    b, s, d = x.shape
    rows = MIX_ROWS
    nseq = MIX_SEQS
    const = lambda shape: pl.BlockSpec(shape, lambda i, j: (0,) * len(shape), pipeline_mode=pl.Buffered(1))
    return pl.pallas_call(
        functools.partial(_mlstm_kernel, rows=rows, nseq=nseq),
        grid=(b // nseq, s // rows),
        in_specs=[pl.BlockSpec((nseq, rows, d), lambda i, j: (i, j, 0)),
                  const(wq.shape), const(wk.shape), const(wv.shape), const(wo.shape), const(wif.shape),
                  const(bif.shape), const(nw.shape)],
        out_specs=pl.BlockSpec((nseq, rows, LSTM_D_V), lambda i, j: (i, j, 0)),
        out_shape=jax.ShapeDtypeStruct((b, s, LSTM_D_V), BF16),
        scratch_shapes=[pltpu.VMEM((nseq, LSTM_HEADS // 2, 2 * LSTM_QK, 2 * LSTM_V), F32),
                        pltpu.VMEM((nseq, 1, LANES), F32)],
        compiler_params=pltpu.CompilerParams(dimension_semantics=("arbitrary", "arbitrary"),
                                             vmem_limit_bytes=VMEM_LIMIT),
        name="mlstm_mixer",
    )(x, wq, wk, wv, wo, wif, bif, nw)


def _layer_norm(t, g, b):
    mu = jnp.mean(t, axis=-1, keepdims=True)
    tc = t - mu
    var = jnp.mean(tc * tc, axis=-1, keepdims=True)
    return tc * lax.rsqrt(var + NORM_EPS) * g + b


def _pack_pair(a, b):
    pa = lax.bitcast_convert_type(a.astype(BF16).astype(F32), jnp.uint32)
    pb = lax.bitcast_convert_type(b.astype(BF16).astype(F32), jnp.uint32)
    return pa | (pb >> 16)


def _unpack_pair(w):
    return (lax.bitcast_convert_type(w & jnp.uint32(0xFFFF0000), F32),
            lax.bitcast_convert_type(w << 16, F32))


def _route(logits):
    rows = logits.shape[0]
    lane = lax.broadcasted_iota(jnp.int32, (rows, LANES), 1).astype(F32)
    big = float(LANES)
    neg = -jnp.inf
    gl = jnp.where(lane < MOE_GROUPS, logits, neg)
    gmax = jnp.max(gl, axis=-1, keepdims=True)
    gidx = jnp.min(jnp.where(gl == gmax, lane, big), axis=-1, keepdims=True)
    gsum = jnp.sum(jnp.where(lane < MOE_GROUPS, jnp.exp(logits - gmax), 0.0), axis=-1, keepdims=True)
    grp_p = 1.0 / gsum
    lo = MOE_GROUPS + gidx * MOE_PER_GROUP
    el = jnp.where(lane >= lo, jnp.where(lane < lo + MOE_PER_GROUP, logits, neg), neg)
    m1 = jnp.max(el, axis=-1, keepdims=True)
    i1 = jnp.min(jnp.where(el == m1, lane, big), axis=-1, keepdims=True)
    el2 = jnp.where(lane == i1, neg, el)
    m2 = jnp.max(el2, axis=-1, keepdims=True)
    i2 = jnp.min(jnp.where(el2 == m2, lane, big), axis=-1, keepdims=True)
    e21 = jnp.exp(m2 - m1)
    g1 = grp_p / (1.0 + e21)
    g2 = grp_p * e21 / (1.0 + e21)
    return i1 - MOE_GROUPS, i2 - MOE_GROUPS, g1, g2


def _merge_kernel(x_ref, ys_ref, yl_ref, wgs_ref, wgl_ref, wa_ref, wb_ref, wo_ref, g_ref, b_ref,
                  wr_hi_ref, wr_lo_ref, br_ref, x1_ref, xp_ref, tab_ref, cnt_ref, carry_ref,
                  *, alpha, steps_per_tile):
    @pl.when(pl.program_id(0) % steps_per_tile == 0)
    def _():
        carry_ref[...] = jnp.zeros(carry_ref.shape, F32)

    x = x_ref[...]
    rows = x.shape[0]
    xb = x.astype(BF16)
    merged = (_sigmoid(_dot(xb, wgs_ref[...])) * _dot(ys_ref[...], wa_ref[...])
              + _sigmoid(_dot(xb, wgl_ref[...])) * _dot(yl_ref[...], wb_ref[...]))
    x1 = _layer_norm(alpha * x + _dot(merged.astype(BF16), wo_ref[...]), g_ref[...], b_ref[...])
    x1_ref[...] = x1
    half = x1.shape[1] // 2
    xp = _pack_pair(x1[:, :half], x1[:, half:])
    for c in range(PACK_SLABS):
        xp_ref[pl.ds(c, rows, stride=PACK_SLABS), :] = xp[:, c * LANES:(c + 1) * LANES]
    x_hi = x1.astype(BF16)
    x_lo = (x1 - x_hi.astype(F32)).astype(BF16)
    logits = (_dot(x_hi, wr_hi_ref[...]) + _dot(x_lo, wr_hi_ref[...]) + _dot(x_hi, wr_lo_ref[...])
              + br_ref[...])
    e1, e2, g1, g2 = _route(logits)
    lane = lax.broadcasted_iota(jnp.int32, (rows, LANES), 1)
    lane_f = lane.astype(F32)
    oh1 = lane_f == e1
    oh2 = lane_f == e2
    ohs = jnp.where(oh1, 1.0, jnp.where(oh2, 1.0, 0.0))
    r_i = lax.broadcasted_iota(jnp.int32, (rows, rows), 0)
    c_i = lax.broadcasted_iota(jnp.int32, (rows, rows), 1)
    before = jnp.where(r_i > c_i, 1.0, 0.0).astype(BF16)
    seen = _dot(before, ohs.astype(BF16)) + carry_ref[...]
    r1 = jnp.sum(jnp.where(oh1, seen, 0.0), axis=-1, keepdims=True)
    r2 = jnp.sum(jnp.where(oh2, seen, 0.0), axis=-1, keepdims=True)
    carry = carry_ref[...] + jnp.sum(ohs, axis=0, keepdims=True)
    carry_ref[...] = carry
    cnt_ref[...] = jnp.broadcast_to(carry, cnt_ref.shape).astype(jnp.int32)
    as_int = lambda v: v.astype(jnp.int32)
    as_bits = lambda v: lax.bitcast_convert_type(v, jnp.int32)
    tab = jnp.where(lane == 0, as_int(e1), jnp.where(lane == 1, as_int(e2), jnp.where(
        lane == 2, as_int(r1) * PACK_SLABS, jnp.where(lane == 3, as_int(r2) * PACK_SLABS, jnp.where(
            lane == 4, as_bits(g1), jnp.where(lane == 5, as_bits(g2), 0))))))
    tab_ref[...] = tab.T[0:SUBLANES, :]


def _merge(x2d, ys, yl, wgs, wgl, wa, wb, wo, g, b, wr_hi, wr_lo, br, alpha, tile):
    t, d = x2d.shape
    rows = DENSE_ROWS
    spt = tile // rows
    const = lambda shape: pl.BlockSpec(shape, lambda i: (0,) * len(shape))
    rowblk = lambda w: pl.BlockSpec((rows, w), lambda i: (i, 0))
    return pl.pallas_call(
        functools.partial(_merge_kernel, alpha=alpha, steps_per_tile=spt),
        grid=(t // rows,),
        in_specs=[rowblk(d), rowblk(SSM_D_INNER), rowblk(LSTM_D_V),
                  const(wgs.shape), const(wgl.shape), const(wa.shape), const(wb.shape), const(wo.shape),
                  const(g.shape), const(b.shape), const(wr_hi.shape), const(wr_lo.shape), const(br.shape)],
        out_specs=[rowblk(d),
                   pl.BlockSpec((rows * PACK_SLABS, LANES), lambda i: (i, 0)),
                   pl.BlockSpec((SUBLANES, rows), lambda i: (i // spt, i % spt)),
                   pl.BlockSpec((SUBLANES, LANES), lambda i: (i // spt, 0))],
        out_shape=[jax.ShapeDtypeStruct((t, d), F32),
                   jax.ShapeDtypeStruct((t * PACK_SLABS, LANES), jnp.uint32),
                   jax.ShapeDtypeStruct((t // tile * SUBLANES, tile), jnp.int32),
                   jax.ShapeDtypeStruct((t // tile * SUBLANES, LANES), jnp.int32)],
        scratch_shapes=[pltpu.VMEM((1, LANES), F32)],
        compiler_params=pltpu.CompilerParams(dimension_semantics=("arbitrary",), vmem_limit_bytes=VMEM_LIMIT),
        name="merge_ln_route",
    )(x2d, ys, yl, wgs, wgl, wa, wb, wo, g, b, wr_hi, wr_lo, br)


def _moe_kernel(cnt_ref, tab_ref, xp_ref, wg_ref, wu_ref, wd_ref, out_ref, xys, ybuf, off_ref, dst_ref,
                *, tile):
    i = pl.program_id(0)
    j = pl.program_id(1)
    group = SUBLANES * PACK_SLABS
    half = PACK_SLABS * LANES

    @pl.when(j == 0)
    def _():
        def offsets(e, acc):
            padded = (cnt_ref[i * MOE_EXPERTS + e] + SUBLANES - 1) // SUBLANES * group
            off_ref[e] = acc
            end = acc + padded

            @pl.when(padded > 0)
            def _():
                xys[pl.ds(pl.multiple_of(end - group, group), group), :] = jnp.zeros((group, LANES), jnp.uint32)
            return end

        total = lax.fori_loop(0, MOE_EXPERTS, offsets, 0)
        off_ref[MOE_EXPERTS] = total
        tail = MOE_BLOCKS[-1] * PACK_SLABS
        xys[pl.ds(pl.multiple_of(total, group), tail), :] = jnp.zeros((tail, LANES), jnp.uint32)

        def distribute(t, c):
            d1 = pl.multiple_of(off_ref[tab_ref[t]] + tab_ref[2 * tile + t], PACK_SLABS)
            d2 = pl.multiple_of(off_ref[tab_ref[tile + t]] + tab_ref[3 * tile + t], PACK_SLABS)
            dst_ref[t] = d1
            dst_ref[tile + t] = d2
            row = xp_ref[pl.ds(pl.multiple_of(t * PACK_SLABS, PACK_SLABS), PACK_SLABS), :]
            xys[pl.ds(d1, PACK_SLABS), :] = row
            xys[pl.ds(d2, PACK_SLABS), :] = row
            return c

        lax.fori_loop(0, tile, distribute, 0, unroll=SUBLANES)

    start = off_ref[j]
    n = off_ref[j + 1] - start

    def expert_block(base, rows, valid):
        words = [xys[pl.ds(base + c, rows, stride=PACK_SLABS), :] for c in range(PACK_SLABS)]
        parts = [_unpack_pair(w) for w in words]
        x_hi = jnp.concatenate([p[0].astype(BF16) for p in parts], axis=1)
        x_lo = jnp.concatenate([p[1].astype(BF16) for p in parts], axis=1)
        hg = _dot(x_hi, wg_ref[0, 0:half, :]) + _dot(x_lo, wg_ref[0, half:2 * half, :])
        hu = _dot(x_hi, wu_ref[0, 0:half, :]) + _dot(x_lo, wu_ref[0, half:2 * half, :])
        y = _dot((hg * _sigmoid(hg) * hu).astype(BF16), wd_ref[0])
        for c in range(PACK_SLABS):
            ybuf[pl.ds(c, rows, stride=PACK_SLABS), :] = _pack_pair(
                y[:, c * LANES:(c + 1) * LANES], y[:, half + c * LANES:half + (c + 1) * LANES])

        def copy(g, c):
            r = pl.multiple_of(g * group, group)
            xys[pl.ds(base + r, group), :] = ybuf[pl.ds(r, group), :]
            return c

        lax.fori_loop(0, valid // group, copy, 0)

    lo = 0
    for rows in MOE_BLOCKS:
        hi = rows * PACK_SLABS

        @pl.when(jnp.logical_and(n > lo, n <= hi))
        def _(rows=rows):
            expert_block(pl.multiple_of(start, group), rows, n)
        lo = hi

    @pl.when(n > lo)
    def _():
        step = MOE_BLOCKS[0] * PACK_SLABS

        def block(bi, carry):
            expert_block(pl.multiple_of(start + bi * step, group), MOE_BLOCKS[0],
                         jnp.minimum(step, n - bi * step))
            return carry

        lax.fori_loop(0, (n + step - 1) // step, block, 0)

    @pl.when(j == MOE_EXPERTS - 1)
    def _():
        def combine(t, c):
            g1 = lax.bitcast_convert_type(tab_ref[4 * tile + t], F32)
            g2 = lax.bitcast_convert_type(tab_ref[5 * tile + t], F32)
            a_hi, a_lo = _unpack_pair(xys[pl.ds(pl.multiple_of(dst_ref[t], PACK_SLABS), PACK_SLABS), :])
            b_hi, b_lo = _unpack_pair(xys[pl.ds(pl.multiple_of(dst_ref[tile + t], PACK_SLABS), PACK_SLABS), :])
            out_ref[pl.ds(pl.multiple_of(t * OUT_SLABS, OUT_SLABS), OUT_SLABS), :] = jnp.concatenate(
                [g1 * a_hi + g2 * b_hi, g1 * a_lo + g2 * b_lo], axis=0)
            return c

        lax.fori_loop(0, tile, combine, 0, unroll=SUBLANES)


def _moe(xp, tab, cnt, wg, wu, wd, tile):
    t = xp.shape[0] // PACK_SLABS
    d = D_MODEL
    n_tiles = t // tile
    seg_rows = 2 * tile + MOE_EXPERTS * SUBLANES + MOE_BLOCKS[-1]
    grid_spec = pltpu.PrefetchScalarGridSpec(
        num_scalar_prefetch=1,
        grid=(n_tiles, MOE_EXPERTS),
        in_specs=[pl.BlockSpec((SUBLANES * tile,), lambda i, j, c: (i,), memory_space=pltpu.SMEM),
                  pl.BlockSpec((tile * PACK_SLABS, LANES), lambda i, j, c: (i, 0), pipeline_mode=pl.Buffered(1)),
                  pl.BlockSpec((1, d, MOE_D_FF), lambda i, j, c: (j, 0, 0)),
                  pl.BlockSpec((1, d, MOE_D_FF), lambda i, j, c: (j, 0, 0)),
                  pl.BlockSpec((1, MOE_D_FF, d), lambda i, j, c: (j, 0, 0))],
        out_specs=pl.BlockSpec((tile * OUT_SLABS, LANES), lambda i, j, c: (i, 0), pipeline_mode=pl.Buffered(1)),
        scratch_shapes=[pltpu.VMEM((seg_rows * PACK_SLABS, LANES), jnp.uint32),
                        pltpu.VMEM((MOE_BLOCKS[-1] * PACK_SLABS, LANES), jnp.uint32),
                        pltpu.SMEM((LANES,), jnp.int32), pltpu.SMEM((2 * tile,), jnp.int32)],
    )
    return pl.pallas_call(
        functools.partial(_moe_kernel, tile=tile),
        grid_spec=grid_spec,
        out_shape=jax.ShapeDtypeStruct((t * OUT_SLABS, LANES), F32),
        compiler_params=pltpu.CompilerParams(dimension_semantics=("arbitrary", "arbitrary"),
                                             vmem_limit_bytes=VMEM_LIMIT),
        name="moe_experts",
    )(cnt, tab, xp, wg, wu, wd)


def _final_kernel(x1_ref, moe_ref, p_ref, g_ref, b_ref, wpg_ref, wpp_ref, out_ref, *, alpha):
    rows = x1_ref.shape[0]
    moe = jnp.concatenate([moe_ref[pl.ds(c, rows, stride=OUT_SLABS), :] for c in range(OUT_SLABS)], axis=1)
    x2 = _layer_norm(alpha * x1_ref[...] + moe, g_ref[...], b_ref[...])
    gate = _sigmoid(_dot(x2.astype(BF16), wpg_ref[...]))
    out_ref[...] = x2 + gate * _dot(p_ref[...].astype(BF16), wpp_ref[...])


def _final(x1, moe, p2d, g, b, wpg, wpp, alpha):
    t, d = x1.shape
    rows = DENSE_ROWS
    const = lambda shape: pl.BlockSpec(shape, lambda i: (0,) * len(shape))
    rowblk = lambda w: pl.BlockSpec((rows, w), lambda i: (i, 0))
    return pl.pallas_call(
        functools.partial(_final_kernel, alpha=alpha),
        grid=(t // rows,),
        in_specs=[rowblk(d), pl.BlockSpec((rows * OUT_SLABS, LANES), lambda i: (i, 0)), rowblk(p2d.shape[1]),
                  const(g.shape), const(b.shape), const(wpg.shape), const(wpp.shape)],
        out_specs=rowblk(d),
        out_shape=jax.ShapeDtypeStruct((t, d), F32),
        compiler_params=pltpu.CompilerParams(dimension_semantics=("arbitrary",), vmem_limit_bytes=VMEM_LIMIT),
        name="final_ln_ple",
    )(x1, moe, p2d, g, b, wpg, wpp)


def _pad_lanes(w, width=LANES):
    return jnp.pad(w, ((0, 0), (0, width - w.shape[1])))


def _row(v, width=None):
    v = v.astype(F32).reshape(1, -1)
    return v if width is None else _pad_lanes(v, width)


def kernel(x, p, w_in, ssm_conv_w, ssm_conv_b, ssm_dt_bias, ssm_a_log, ssm_d, ssm_norm_w, lstm_i_bias, lstm_f_bias, lstm_norm_w, w_branch_ssm, w_branch_lstm, w_out, ln1_g, ln1_b, moe_w_group, moe_b_group, moe_w_expert, moe_b_expert, moe_w_gate, moe_w_up, moe_w_down, ln2_g, ln2_b, ple_w_proj, ple_w_gate):
    depth = w_in.shape[0]
    bsz, seq, d = x.shape
    t = bsz * seq
    alpha = (2.0 * depth) ** 0.25
    head_expand = (jnp.arange(LANES, dtype=jnp.int32)[:, None]
                   == jnp.arange(SSM_D_INNER, dtype=jnp.int32)[None, :] // SSM_HEAD_DIM).astype(BF16)
    for i in range(depth):
        splits = []
        c0 = 0
        for sz in IN_PROJ_SIZES:
            splits.append(w_in[i][:, c0:c0 + sz])
            c0 += sz
        w_z, w_xbc, w_dt, w_q, w_k, w_v, w_o, w_i, w_f, w_gs, w_gl = splits
        bf = lambda w: w.astype(BF16)
        y_ssm = _ssd_mixer(
            x, bf(w_z), bf(w_xbc), bf(_pad_lanes(w_dt)),
            ssm_conv_w[i].astype(F32), _row(ssm_conv_b[i]), _row(ssm_dt_bias[i], LANES),
            _row(-jnp.exp(ssm_a_log[i].astype(F32)), LANES),
            _row(jnp.repeat(ssm_d[i].astype(F32), SSM_HEAD_DIM)), _row(ssm_norm_w[i]), head_expand)
        w_if = jnp.concatenate([_pad_lanes(w_i), _pad_lanes(w_f)], axis=1)
        b_if = jnp.concatenate([_row(lstm_i_bias[i], LANES), _row(lstm_f_bias[i], LANES)], axis=1)
        y_lstm = _mlstm_mixer(x, bf(w_q * (LSTM_QK ** -0.5)), bf(w_k), bf(w_v), bf(w_o), bf(w_if), b_if,
                              _row(lstm_norm_w[i]))
        w_r = _pad_lanes(jnp.concatenate([moe_w_group[i], moe_w_expert[i]], axis=1).astype(F32))
        w_r_hi = w_r.astype(BF16)
        w_r_lo = (w_r - w_r_hi.astype(F32)).astype(BF16)
        b_r = _row(jnp.concatenate([moe_b_group[i], moe_b_expert[i]]), LANES)
        tile = min(MOE_TILE, t)
        x1, xp, tab, cnt = _merge(x.reshape(t, d), y_ssm.reshape(t, -1), y_lstm.reshape(t, -1),
                                  bf(w_gs), bf(w_gl), bf(w_branch_ssm[i]), bf(w_branch_lstm[i]), bf(w_out[i]),
                                  _row(ln1_g[i]), _row(ln1_b[i]), w_r_hi, w_r_lo, b_r, alpha, tile)
        cnt = cnt.reshape(t // tile, SUBLANES, LANES)[:, 0, :MOE_EXPERTS].reshape(-1)
        moe = _moe(xp, tab.reshape(-1), cnt, bf(moe_w_gate[i]), bf(moe_w_up[i]), bf(moe_w_down[i]), tile)
        x = _final(x1, moe, p[i].reshape(t, -1), _row(ln2_g[i]), _row(ln2_b[i]),
                   bf(ple_w_gate[i]), bf(ple_w_proj[i]), alpha).reshape(bsz, seq, d)
    return x
```

```python
import functools

import jax
import jax.numpy as jnp
from jax import lax
from jax.experimental import pallas as pl
from jax.experimental.pallas import tpu as pltpu

F32 = jnp.float32
BF16 = jnp.bfloat16

D_MODEL = 1024
PLE_DIM = 256
SSM_D_INNER = 1024
SSM_HEAD_DIM = 64
SSM_HEADS = 16
SSM_GROUPS = 4
SSM_STATE = 128
SSM_CONV = 4
SSM_XBC = SSM_D_INNER + 2 * SSM_GROUPS * SSM_STATE
LSTM_HEADS = 8
LSTM_QK = 64
LSTM_V = 128
LSTM_D_QK = LSTM_HEADS * LSTM_QK
LSTM_D_V = LSTM_HEADS * LSTM_V
CHUNK = 128
MOE_GROUPS = 8
MOE_PER_GROUP = 8
MOE_EXPERTS = 64
MOE_D_FF = 512
NORM_EPS = 1e-5
IN_PROJ_SIZES = (SSM_D_INNER, SSM_XBC, SSM_HEADS, LSTM_D_QK, LSTM_D_QK, LSTM_D_V, LSTM_D_V,
                 LSTM_HEADS, LSTM_HEADS, D_MODEL, D_MODEL)

LANES = 128
SUBLANES = 8
VMEM_LIMIT = 56 * 1024 * 1024

MIX_ROWS = 256
MIX_SEQS = 2
DENSE_ROWS = 512
MOE_TILE = 4096
MOE_BLOCKS = (128, 192, 256)
MOE_WEIGHT_BUFFERS = 3
PACK_SLABS = D_MODEL // 2 // LANES
OUT_SLABS = D_MODEL // LANES


def _dot(a, b):
    return jnp.dot(a, b, preferred_element_type=F32)


def _dot_nt(a, b):
    return lax.dot_general(a, b, (((1,), (1,)), ((), ())), preferred_element_type=F32)


def _dot_tn(a, b):
    return lax.dot_general(a, b, (((0,), (0,)), ((), ())), preferred_element_type=F32)


def _sigmoid(x):
    return 1.0 / (1.0 + jnp.exp(-x))


def _softplus(x):
    return jnp.maximum(x, 0.0) + jnp.log(1.0 + jnp.exp(-jnp.abs(x)))


def _split3(x):
    hi = x.astype(BF16)
    r1 = x - hi.astype(F32)
    mid = r1.astype(BF16)
    lo = (r1 - mid.astype(F32)).astype(BF16)
    return hi, mid, lo


def _cumsum_rows(tril, x):
    hi, mid, lo = _split3(x)
    return _dot(tril, hi) + _dot(tril, mid) + _dot(tril, lo)


def _causal_masks():
    r = lax.broadcasted_iota(jnp.int32, (CHUNK, CHUNK), 0)
    c = lax.broadcasted_iota(jnp.int32, (CHUNK, CHUNK), 1)
    causal = r >= c
    return causal, jnp.where(causal, 1.0, 0.0).astype(BF16)


def _pair_cols(v, ha, hb, rows, lane):
    a = jnp.broadcast_to(v[:, ha:ha + 1], (rows, LANES))
    b = jnp.broadcast_to(v[:, hb:hb + 1], (rows, LANES))
    return jnp.where(lane < SSM_HEAD_DIM, a, b)


def _ssd_kernel(x_ref, wz_ref, wxbc_ref, wdt_ref, cw_ref, cb_ref, dtb_ref, aneg_ref, dexp_ref, nw_ref, ex_ref,
                y_ref, ext_ref, u_ref, st_ref, *, rows, nseq):
    n_slab = SSM_XBC // LANES
    tail = SUBLANES - SSM_CONV + 1

    @pl.when(pl.program_id(1) == 0)
    def _():
        ext_ref[:, :, 0:SUBLANES, :] = jnp.zeros((nseq, n_slab, SUBLANES, LANES), F32)
        st_ref[...] = jnp.zeros(st_ref.shape, F32)

    xb = x_ref[...].reshape(nseq * rows, x_ref.shape[-1]).astype(BF16)
    z = _dot(xb, wz_ref[...])
    dtr = _dot(xb, wdt_ref[...])
    for c2 in range(n_slab // 2):
        xbc = _dot(xb, wxbc_ref[:, c2 * 2 * LANES:(c2 + 1) * 2 * LANES])
        for s in range(nseq):
            for cc in range(2):
                ext_ref[s, 2 * c2 + cc, SUBLANES:SUBLANES + rows, :] = (
                    xbc[s * rows:(s + 1) * rows, cc * LANES:(cc + 1) * LANES])
    for s in range(nseq):
        for c in range(n_slab):
            conv = cb_ref[:, c * LANES:(c + 1) * LANES]
            for k in range(SSM_CONV):
                conv = conv + cw_ref[k:k + 1, c * LANES:(c + 1) * LANES] * ext_ref[s, c, pl.ds(tail + k, rows), :]
            u_ref[s, c] = conv * _sigmoid(conv)
            ext_ref[s, c, 0:SUBLANES, :] = ext_ref[s, c, rows:rows + SUBLANES, :]

    dt = _softplus(dtr + dtb_ref[...])
    da = dt * aneg_ref[...]
    causal, tril = _causal_masks()
    lane = lax.broadcasted_iota(jnp.int32, (CHUNK, LANES), 1)
    lane1 = lax.broadcasted_iota(jnp.int32, (1, LANES), 1)
    gw = SSM_D_INNER // SSM_GROUPS
    b_slab = SSM_D_INNER // LANES
    c_slab = b_slab + SSM_GROUPS

    for ci in range(rows // CHUNK):
        for s in range(nseq):
            r0 = ci * CHUNK
            q0 = s * rows + r0
            dt_c = dt[q0:q0 + CHUNK]
            a_cs = _cumsum_rows(tril, da[q0:q0 + CHUNK])
            a_cs_t = a_cs.T
            dt_t = dt_c.T
            a_last = a_cs[CHUNK - 1:CHUNK, :]
            ea_last = jnp.exp(a_last)
            ea_x = _dot(jnp.exp(a_cs).astype(BF16), ex_ref[...])
            sw_x = _dot((dt_c * jnp.exp(a_last - a_cs)).astype(BF16), ex_ref[...])
            for g in range(SSM_GROUPS):
                bm_g = u_ref[s, b_slab + g, r0:r0 + CHUNK, :].astype(BF16)
                cm_g = u_ref[s, c_slab + g, r0:r0 + CHUNK, :].astype(BF16)
                cb = _dot_nt(cm_g, bm_g)
                st_g = st_ref[s, :, g * gw:(g + 1) * gw]
                y_off = _dot(cm_g, st_g.astype(BF16))
                yy, xw, eal_x = [], [], []
                for pr in range(2):
                    ha = 4 * g + 2 * pr
                    hb = ha + 1
                    lo = g * gw + pr * LANES
                    xs_p = u_ref[s, 2 * g + pr, r0:r0 + CHUNK, :]
                    xs_pb = xs_p.astype(BF16)
                    rhs = jnp.concatenate([jnp.where(lane < SSM_HEAD_DIM, xs_pb, 0).astype(BF16),
                                           jnp.where(lane >= SSM_HEAD_DIM, xs_pb, 0).astype(BF16)], axis=0)
                    gs = []
                    for h in (ha, hb):
                        colb = jnp.broadcast_to(a_cs[:, h:h + 1], (CHUNK, CHUNK))
                        rowb = jnp.broadcast_to(a_cs_t[h:h + 1, :], (CHUNK, CHUNK))
                        dec = jnp.exp(jnp.where(causal, colb - rowb, -jnp.inf))
                        gs.append((cb * dec * jnp.broadcast_to(dt_t[h:h + 1, :], (CHUNK, CHUNK))).astype(BF16))
                    y_p = (_dot(jnp.concatenate(gs, axis=1), rhs)
                           + y_off[:, pr * LANES:(pr + 1) * LANES] * ea_x[:, lo:lo + LANES]
                           + dexp_ref[:, lo:lo + LANES] * xs_p)
                    zz = z[q0:q0 + CHUNK, lo:lo + LANES]
                    yy.append(y_p * (zz * _sigmoid(zz)))
                    xw.append((xs_p * sw_x[:, lo:lo + LANES]).astype(BF16))
                    eal_x.append(_pair_cols(ea_last, ha, hb, 1, lane1))
                st_ref[s, :, g * gw:(g + 1) * gw] = (st_g * jnp.concatenate(eal_x, axis=1)
                                                     + _dot_tn(bm_g, jnp.concatenate(xw, axis=1)))
                ms = sum(jnp.sum(t * t, axis=-1, keepdims=True) for t in yy) * (1.0 / gw)
                inv = lax.rsqrt(ms + NORM_EPS)
                for pr in range(2):
                    lo = g * gw + pr * LANES
                    y_ref[s, r0:r0 + CHUNK, lo:lo + LANES] = (yy[pr] * inv * nw_ref[:, lo:lo + LANES]).astype(BF16)


def _ssd_mixer(x, wz, wxbc, wdt, cw, cb, dtb, aneg, dexp, nw, ex):
    b, s, d = x.shape
    rows = MIX_ROWS
    nseq = MIX_SEQS
    const = lambda shape: pl.BlockSpec(shape, lambda i, j: (0,) * len(shape), pipeline_mode=pl.Buffered(1))
    return pl.pallas_call(
        functools.partial(_ssd_kernel, rows=rows, nseq=nseq),
        grid=(b // nseq, s // rows),
        in_specs=[pl.BlockSpec((nseq, rows, d), lambda i, j: (i, j, 0)),
                  const(wz.shape), const(wxbc.shape), const(wdt.shape), const(cw.shape), const(cb.shape),
                  const(dtb.shape), const(aneg.shape), const(dexp.shape), const(nw.shape), const(ex.shape)],
        out_specs=pl.BlockSpec((nseq, rows, SSM_D_INNER), lambda i, j: (i, j, 0)),
        out_shape=jax.ShapeDtypeStruct((b, s, SSM_D_INNER), BF16),
        scratch_shapes=[pltpu.VMEM((nseq, SSM_XBC // LANES, SUBLANES + rows, LANES), F32),
                        pltpu.VMEM((nseq, SSM_XBC // LANES, rows, LANES), F32),
                        pltpu.VMEM((nseq, SSM_STATE, SSM_D_INNER), F32)],
        compiler_params=pltpu.CompilerParams(dimension_semantics=("arbitrary", "arbitrary"),
                                             vmem_limit_bytes=VMEM_LIMIT),
        name="ssd_mixer",
    )(x, wz, wxbc, wdt, cw, cb, dtb, aneg, dexp, nw, ex)


def _mlstm_kernel(x_ref, wq_ref, wk_ref, wv_ref, wo_ref, wif_ref, bif_ref, nw_ref,
                  y_ref, c_ref, m_ref, *, rows, nseq):
    @pl.when(pl.program_id(1) == 0)
    def _():
        c_ref[...] = jnp.zeros(c_ref.shape, F32)
        m_ref[...] = jnp.full(m_ref.shape, -jnp.inf, F32)

    xb = x_ref[...].reshape(nseq * rows, x_ref.shape[-1]).astype(BF16)
    q = _dot(xb, wq_ref[...])
    k = _dot(xb, wk_ref[...])
    v = _dot(xb, wv_ref[...])
    o = _dot(xb, wo_ref[...])
    gif = _dot(xb, wif_ref[...]) + bif_ref[...]
    causal, tril = _causal_masks()
    lane = lax.broadcasted_iota(jnp.int32, (CHUNK, LANES), 1)
    lane1 = lax.broadcasted_iota(jnp.int32, (1, LANES), 1)
    row = lax.broadcasted_iota(jnp.int32, (LANES, 1), 0)
    head_lane = lane < LSTM_HEADS
    ones = jnp.ones((CHUNK, LANES), BF16)

    for ci in range(rows // CHUNK):
        for s in range(nseq):
            r0 = ci * CHUNK
            q0 = s * rows + r0
            log_i = gif[q0:q0 + CHUNK, 0:LANES]
            f_pre = gif[q0:q0 + CHUNK, LANES:2 * LANES]
            log_f = jnp.where(head_lane, jnp.minimum(f_pre, 0.0) - jnp.log(1.0 + jnp.exp(-jnp.abs(f_pre))), 0.0)
            fcum = _cumsum_rows(tril, log_f)
            f_tot = fcum[CHUNK - 1:CHUNK, :]
            gk = log_i - fcum
            m_loc = jnp.max(f_tot + gk, axis=0, keepdims=True)
            g_t = gk.T
            m_prev = m_ref[s]
            m_new = jnp.maximum(f_tot + m_prev, m_loc)
            s_prev = jnp.exp(f_tot + m_prev - m_new)
            s_loc = jnp.exp(m_loc - m_new)
            m_ref[s] = jnp.where(lane1 < LSTM_HEADS, m_new, -jnp.inf)
            w_shift = f_tot - m_loc

            for pr in range(LSTM_HEADS // 2):
                q_p = q[q0:q0 + CHUNK, pr * LANES:(pr + 1) * LANES]
                k_p = k[q0:q0 + CHUNK, pr * LANES:(pr + 1) * LANES]
                k_pb = k_p.astype(BF16)
                c_pair = c_ref[s, pr]
                c_pair_b = c_pair.astype(BF16)
                c_new = c_pair * jnp.where(row < LSTM_QK,
                                           jnp.broadcast_to(s_prev[:, 2 * pr:2 * pr + 1], (LANES, 1)),
                                           jnp.broadcast_to(s_prev[:, 2 * pr + 1:2 * pr + 2], (LANES, 1)))
                for hh in range(2):
                    h = 2 * pr + hh
                    in_head = (lane < LSTM_QK) if hh == 0 else (lane >= LSTM_QK)
                    v_h = v[q0:q0 + CHUNK, h * LSTM_V:(h + 1) * LSTM_V].astype(BF16)
                    v_ext = jnp.concatenate([v_h, ones], axis=1)
                    q_m = jnp.where(in_head, q_p, 0.0)
                    fcol = jnp.broadcast_to(fcum[:, h:h + 1], (CHUNK, CHUNK))
                    gcol = jnp.broadcast_to(gk[:, h:h + 1], (CHUNK, LANES))
                    rowb = jnp.broadcast_to(g_t[h:h + 1, :], (CHUNK, CHUNK))
                    log_d = jnp.where(causal, fcol + rowb, -jnp.inf)
                    inter_log = fcol + m_prev[:, h:h + 1]
                    m_t = jnp.maximum(inter_log, jnp.max(log_d, axis=-1, keepdims=True))
                    scores = _dot_nt(q_m.astype(BF16), k_pb) * jnp.exp(log_d - m_t)
                    inter_w = jnp.exp(inter_log - m_t)
                    lhs = jnp.concatenate([scores.astype(BF16), (q_m * inter_w).astype(BF16)], axis=1)
                    res = _dot(lhs, jnp.concatenate([v_ext, c_pair_b], axis=0))
                    hv = res[:, 0:LSTM_V] / jnp.maximum(jnp.abs(res[:, LSTM_V:2 * LSTM_V]), jnp.exp(-m_t))
                    ms = jnp.mean(hv * hv, axis=-1, keepdims=True)
                    o_h = o[q0:q0 + CHUNK, h * LSTM_V:(h + 1) * LSTM_V]
                    y_ref[s, r0:r0 + CHUNK, h * LSTM_V:(h + 1) * LSTM_V] = (
                        _sigmoid(o_h) * (hv * lax.rsqrt(ms + NORM_EPS) * nw_ref[:, h * LSTM_V:(h + 1) * LSTM_V])
                    ).astype(BF16)
                    kw = jnp.where(in_head, k_p * jnp.exp(gcol + w_shift[:, h:h + 1]), 0.0)
                    c_new = c_new + s_loc[:, h:h + 1] * _dot_tn(kw.astype(BF16), v_ext)
                c_ref[s, pr] = c_new


def _mlstm_mixer(x, wq, wk, wv, wo, wif, bif, nw):
    b, s, d = x.shape
    rows = MIX_ROWS
    nseq = MIX_SEQS
    const = lambda shape: pl.BlockSpec(shape, lambda i, j: (0,) * len(shape), pipeline_mode=pl.Buffered(1))
    return pl.pallas_call(
        functools.partial(_mlstm_kernel, rows=rows, nseq=nseq),
        grid=(b // nseq, s // rows),
        in_specs=[pl.BlockSpec((nseq, rows, d), lambda i, j: (i, j, 0)),
                  const(wq.shape), const(wk.shape), const(wv.shape), const(wo.shape), const(wif.shape),
                  const(bif.shape), const(nw.shape)],
        out_specs=pl.BlockSpec((nseq, rows, LSTM_D_V), lambda i, j: (i, j, 0)),
        out_shape=jax.ShapeDtypeStruct((b, s, LSTM_D_V), BF16),
        scratch_shapes=[pltpu.VMEM((nseq, LSTM_HEADS // 2, 2 * LSTM_QK, 2 * LSTM_V), F32),
                        pltpu.VMEM((nseq, 1, LANES), F32)],
        compiler_params=pltpu.CompilerParams(dimension_semantics=("arbitrary", "arbitrary"),
                                             vmem_limit_bytes=VMEM_LIMIT),
        name="mlstm_mixer",
    )(x, wq, wk, wv, wo, wif, bif, nw)


def _layer_norm(t, g, b):
    mu = jnp.mean(t, axis=-1, keepdims=True)
    tc = t - mu
    var = jnp.mean(tc * tc, axis=-1, keepdims=True)
    return tc * lax.rsqrt(var + NORM_EPS) * g + b


def _pack_pair(a, b):
    pa = lax.bitcast_convert_type(a.astype(BF16).astype(F32), jnp.uint32)
    pb = lax.bitcast_convert_type(b.astype(BF16).astype(F32), jnp.uint32)
    return pa | (pb >> 16)


def _unpack_pair(w):
    return (lax.bitcast_convert_type(w & jnp.uint32(0xFFFF0000), F32),
            lax.bitcast_convert_type(w << 16, F32))


def _route(logits):
    rows = logits.shape[0]
    lane = lax.broadcasted_iota(jnp.int32, (rows, LANES), 1).astype(F32)
    big = float(LANES)
    neg = -jnp.inf
    gl = jnp.where(lane < MOE_GROUPS, logits, neg)
    gmax = jnp.max(gl, axis=-1, keepdims=True)
    gidx = jnp.min(jnp.where(gl == gmax, lane, big), axis=-1, keepdims=True)
    gsum = jnp.sum(jnp.where(lane < MOE_GROUPS, jnp.exp(logits - gmax), 0.0), axis=-1, keepdims=True)
    grp_p = 1.0 / gsum
    lo = MOE_GROUPS + gidx * MOE_PER_GROUP
    el = jnp.where(lane >= lo, jnp.where(lane < lo + MOE_PER_GROUP, logits, neg), neg)
    m1 = jnp.max(el, axis=-1, keepdims=True)
    i1 = jnp.min(jnp.where(el == m1, lane, big), axis=-1, keepdims=True)
    el2 = jnp.where(lane == i1, neg, el)
    m2 = jnp.max(el2, axis=-1, keepdims=True)
    i2 = jnp.min(jnp.where(el2 == m2, lane, big), axis=-1, keepdims=True)
    e21 = jnp.exp(m2 - m1)
    g1 = grp_p / (1.0 + e21)
    g2 = grp_p * e21 / (1.0 + e21)
    return i1 - MOE_GROUPS, i2 - MOE_GROUPS, g1, g2


def _merge_kernel(x_ref, ys_ref, yl_ref, wgs_ref, wgl_ref, wa_ref, wb_ref, wo_ref, g_ref, b_ref,
                  wr_hi_ref, wr_lo_ref, br_ref, x1_ref, xp_ref, tab_ref, cnt_ref, carry_ref,
                  *, alpha, steps_per_tile):
    @pl.when(pl.program_id(0) % steps_per_tile == 0)
    def _():
        carry_ref[...] = jnp.zeros(carry_ref.shape, F32)

    x = x_ref[...]
    rows = x.shape[0]
    xb = x.astype(BF16)
    merged = (_sigmoid(_dot(xb, wgs_ref[...])) * _dot(ys_ref[...], wa_ref[...])
              + _sigmoid(_dot(xb, wgl_ref[...])) * _dot(yl_ref[...], wb_ref[...]))
    x1 = _layer_norm(alpha * x + _dot(merged.astype(BF16), wo_ref[...]), g_ref[...], b_ref[...])
    x1_ref[...] = x1
    half = x1.shape[1] // 2
    xp = _pack_pair(x1[:, :half], x1[:, half:])
    for c in range(PACK_SLABS):
        xp_ref[pl.ds(c, rows, stride=PACK_SLABS), :] = xp[:, c * LANES:(c + 1) * LANES]
    x_hi = x1.astype(BF16)
    x_lo = (x1 - x_hi.astype(F32)).astype(BF16)
    logits = (_dot(x_hi, wr_hi_ref[...]) + _dot(x_lo, wr_hi_ref[...]) + _dot(x_hi, wr_lo_ref[...])
              + br_ref[...])
    e1, e2, g1, g2 = _route(logits)
    lane = lax.broadcasted_iota(jnp.int32, (rows, LANES), 1)
    lane_f = lane.astype(F32)
    oh1 = lane_f == e1
    oh2 = lane_f == e2
    ohs = jnp.where(oh1, 1.0, jnp.where(oh2, 1.0, 0.0))
    r_i = lax.broadcasted_iota(jnp.int32, (rows, rows), 0)
    c_i = lax.broadcasted_iota(jnp.int32, (rows, rows), 1)
    before = jnp.where(r_i > c_i, 1.0, 0.0).astype(BF16)
    seen = _dot(before, ohs.astype(BF16)) + carry_ref[...]
    r1 = jnp.sum(jnp.where(oh1, seen, 0.0), axis=-1, keepdims=True)
    r2 = jnp.sum(jnp.where(oh2, seen, 0.0), axis=-1, keepdims=True)
    carry = carry_ref[...] + jnp.sum(ohs, axis=0, keepdims=True)
    carry_ref[...] = carry
    cnt_ref[...] = jnp.broadcast_to(carry, cnt_ref.shape).astype(jnp.int32)
    as_int = lambda v: v.astype(jnp.int32)
    as_bits = lambda v: lax.bitcast_convert_type(v, jnp.int32)
    tab = jnp.where(lane == 0, as_int(e1), jnp.where(lane == 1, as_int(e2), jnp.where(
        lane == 2, as_int(r1) * PACK_SLABS, jnp.where(lane == 3, as_int(r2) * PACK_SLABS, jnp.where(
            lane == 4, as_bits(g1), jnp.where(lane == 5, as_bits(g2), 0))))))
    tab_ref[...] = tab.T[0:SUBLANES, :]


def _merge(x2d, ys, yl, wgs, wgl, wa, wb, wo, g, b, wr_hi, wr_lo, br, alpha, tile):
    t, d = x2d.shape
    rows = DENSE_ROWS
    spt = tile // rows
    const = lambda shape: pl.BlockSpec(shape, lambda i: (0,) * len(shape))
    rowblk = lambda w: pl.BlockSpec((rows, w), lambda i: (i, 0))
    return pl.pallas_call(
        functools.partial(_merge_kernel, alpha=alpha, steps_per_tile=spt),
        grid=(t // rows,),
        in_specs=[rowblk(d), rowblk(SSM_D_INNER), rowblk(LSTM_D_V),
                  const(wgs.shape), const(wgl.shape), const(wa.shape), const(wb.shape), const(wo.shape),
                  const(g.shape), const(b.shape), const(wr_hi.shape), const(wr_lo.shape), const(br.shape)],
        out_specs=[rowblk(d),
                   pl.BlockSpec((rows * PACK_SLABS, LANES), lambda i: (i, 0)),
                   pl.BlockSpec((SUBLANES, rows), lambda i: (i // spt, i % spt)),
                   pl.BlockSpec((SUBLANES, LANES), lambda i: (i // spt, 0))],
        out_shape=[jax.ShapeDtypeStruct((t, d), F32),
                   jax.ShapeDtypeStruct((t * PACK_SLABS, LANES), jnp.uint32),
                   jax.ShapeDtypeStruct((t // tile * SUBLANES, tile), jnp.int32),
                   jax.ShapeDtypeStruct((t // tile * SUBLANES, LANES), jnp.int32)],
        scratch_shapes=[pltpu.VMEM((1, LANES), F32)],
        compiler_params=pltpu.CompilerParams(dimension_semantics=("arbitrary",), vmem_limit_bytes=VMEM_LIMIT),
        name="merge_ln_route",
    )(x2d, ys, yl, wgs, wgl, wa, wb, wo, g, b, wr_hi, wr_lo, br)


def _moe_kernel(cnt_ref, tab_ref, xp_ref, wg_hbm, wu_hbm, wd_hbm, out_ref, xys, ybuf, off_ref, dst_ref,
                wg_buf, wu_buf, wd_buf, wsem, *, tile, n_steps):
    i = pl.program_id(0)
    j = pl.program_id(1)
    group = SUBLANES * PACK_SLABS
    half = PACK_SLABS * LANES
    step = i * MOE_EXPERTS + j
    slot = step % MOE_WEIGHT_BUFFERS
    weights = ((wg_hbm, wg_buf), (wu_hbm, wu_buf), (wd_hbm, wd_buf))

    def weight_copy(k, expert, into):
        hbm, buf = weights[k]
        return pltpu.make_async_copy(hbm.at[expert], buf.at[into], wsem.at[k, into])

    @pl.when(step == 0)
    def _():
        for ahead in range(MOE_WEIGHT_BUFFERS - 1):
            for k in range(len(weights)):
                weight_copy(k, ahead % MOE_EXPERTS, ahead).start()

    nxt = step + MOE_WEIGHT_BUFFERS - 1

    @pl.when(nxt < n_steps)
    def _():
        for k in range(len(weights)):
            weight_copy(k, nxt % MOE_EXPERTS, nxt % MOE_WEIGHT_BUFFERS).start()

    @pl.when(j == 0)
    def _():
        def offsets(e, acc):
            padded = (cnt_ref[i * MOE_EXPERTS + e] + SUBLANES - 1) // SUBLANES * group
            off_ref[e] = acc
            end = acc + padded

            @pl.when(padded > 0)
            def _():
                xys[pl.ds(pl.multiple_of(end - group, group), group), :] = jnp.zeros((group, LANES), jnp.uint32)
            return end

        total = lax.fori_loop(0, MOE_EXPERTS, offsets, 0)
        off_ref[MOE_EXPERTS] = total
        tail = MOE_BLOCKS[-1] * PACK_SLABS
        xys[pl.ds(pl.multiple_of(total, group), tail), :] = jnp.zeros((tail, LANES), jnp.uint32)

        def distribute(t, c):
            d1 = pl.multiple_of(off_ref[tab_ref[t]] + tab_ref[2 * tile + t], PACK_SLABS)
            d2 = pl.multiple_of(off_ref[tab_ref[tile + t]] + tab_ref[3 * tile + t], PACK_SLABS)
            dst_ref[t] = d1
            dst_ref[tile + t] = d2
            row = xp_ref[pl.ds(pl.multiple_of(t * PACK_SLABS, PACK_SLABS), PACK_SLABS), :]
            xys[pl.ds(d1, PACK_SLABS), :] = row
            xys[pl.ds(d2, PACK_SLABS), :] = row
            return c

        lax.fori_loop(0, tile, distribute, 0, unroll=SUBLANES)

    for k in range(len(weights)):
        weight_copy(k, j, slot).wait()
    wg_ref, wu_ref, wd_ref = wg_buf.at[slot], wu_buf.at[slot], wd_buf.at[slot]
    start = off_ref[j]
    n = off_ref[j + 1] - start

    def expert_block(base, rows, valid):
        words = [xys[pl.ds(base + c, rows, stride=PACK_SLABS), :] for c in range(PACK_SLABS)]
        parts = [_unpack_pair(w) for w in words]
        x_hi = jnp.concatenate([p[0].astype(BF16) for p in parts], axis=1)
        x_lo = jnp.concatenate([p[1].astype(BF16) for p in parts], axis=1)
        hg = _dot(x_hi, wg_ref[0:half, :]) + _dot(x_lo, wg_ref[half:2 * half, :])
        hu = _dot(x_hi, wu_ref[0:half, :]) + _dot(x_lo, wu_ref[half:2 * half, :])
        y = _dot((hg * _sigmoid(hg) * hu).astype(BF16), wd_ref[...])
        for c in range(PACK_SLABS):
            ybuf[pl.ds(c, rows, stride=PACK_SLABS), :] = _pack_pair(
                y[:, c * LANES:(c + 1) * LANES], y[:, half + c * LANES:half + (c + 1) * LANES])

        def copy(g, c):
            r = pl.multiple_of(g * group, group)
            xys[pl.ds(base + r, group), :] = ybuf[pl.ds(r, group), :]
            return c

        lax.fori_loop(0, valid // group, copy, 0)

    lo = 0
    for rows in MOE_BLOCKS:
        hi = rows * PACK_SLABS

        @pl.when(jnp.logical_and(n > lo, n <= hi))
        def _(rows=rows):
            expert_block(pl.multiple_of(start, group), rows, n)
        lo = hi

    @pl.when(n > lo)
    def _():
        step = MOE_BLOCKS[0] * PACK_SLABS

        def block(bi, carry):
            expert_block(pl.multiple_of(start + bi * step, group), MOE_BLOCKS[0],
                         jnp.minimum(step, n - bi * step))
            return carry

        lax.fori_loop(0, (n + step - 1) // step, block, 0)

    @pl.when(j == MOE_EXPERTS - 1)
    def _():
        def combine(t, c):
            g1 = lax.bitcast_convert_type(tab_ref[4 * tile + t], F32)
            g2 = lax.bitcast_convert_type(tab_ref[5 * tile + t], F32)
            a_hi, a_lo = _unpack_pair(xys[pl.ds(pl.multiple_of(dst_ref[t], PACK_SLABS), PACK_SLABS), :])
            b_hi, b_lo = _unpack_pair(xys[pl.ds(pl.multiple_of(dst_ref[tile + t], PACK_SLABS), PACK_SLABS), :])
            out_ref[pl.ds(pl.multiple_of(t * OUT_SLABS, OUT_SLABS), OUT_SLABS), :] = jnp.concatenate(
                [g1 * a_hi + g2 * b_hi, g1 * a_lo + g2 * b_lo], axis=0)
            return c

        lax.fori_loop(0, tile, combine, 0, unroll=SUBLANES)


def _moe(xp, tab, cnt, wg, wu, wd, tile):
    t = xp.shape[0] // PACK_SLABS
    d = D_MODEL
    n_tiles = t // tile
    seg_rows = 2 * tile + MOE_EXPERTS * SUBLANES + MOE_BLOCKS[-1]
    grid_spec = pltpu.PrefetchScalarGridSpec(
        num_scalar_prefetch=1,
        grid=(n_tiles, MOE_EXPERTS),
        in_specs=[pl.BlockSpec((SUBLANES * tile,), lambda i, j, c: (i,), memory_space=pltpu.SMEM),
                  pl.BlockSpec((tile * PACK_SLABS, LANES), lambda i, j, c: (i, 0), pipeline_mode=pl.Buffered(1)),
                  pl.BlockSpec(memory_space=pl.ANY), pl.BlockSpec(memory_space=pl.ANY),
                  pl.BlockSpec(memory_space=pl.ANY)],
        out_specs=pl.BlockSpec((tile * OUT_SLABS, LANES), lambda i, j, c: (i, 0), pipeline_mode=pl.Buffered(1)),
        scratch_shapes=[pltpu.VMEM((seg_rows * PACK_SLABS, LANES), jnp.uint32),
                        pltpu.VMEM((MOE_BLOCKS[-1] * PACK_SLABS, LANES), jnp.uint32),
                        pltpu.SMEM((LANES,), jnp.int32), pltpu.SMEM((2 * tile,), jnp.int32),
                        pltpu.VMEM((MOE_WEIGHT_BUFFERS, d, MOE_D_FF), BF16),
                        pltpu.VMEM((MOE_WEIGHT_BUFFERS, d, MOE_D_FF), BF16),
                        pltpu.VMEM((MOE_WEIGHT_BUFFERS, MOE_D_FF, d), BF16),
                        pltpu.SemaphoreType.DMA((3, MOE_WEIGHT_BUFFERS))],
    )
    return pl.pallas_call(
        functools.partial(_moe_kernel, tile=tile, n_steps=n_tiles * MOE_EXPERTS),
        grid_spec=grid_spec,
        out_shape=jax.ShapeDtypeStruct((t * OUT_SLABS, LANES), F32),
        compiler_params=pltpu.CompilerParams(dimension_semantics=("arbitrary", "arbitrary"),
                                             vmem_limit_bytes=VMEM_LIMIT),
        name="moe_experts",
    )(cnt, tab, xp, wg, wu, wd)


def _final_kernel(x1_ref, moe_ref, p_ref, g_ref, b_ref, wpg_ref, wpp_ref, out_ref, *, alpha):
    rows = x1_ref.shape[0]
    moe = jnp.concatenate([moe_ref[pl.ds(c, rows, stride=OUT_SLABS), :] for c in range(OUT_SLABS)], axis=1)
    x2 = _layer_norm(alpha * x1_ref[...] + moe, g_ref[...], b_ref[...])
    gate = _sigmoid(_dot(x2.astype(BF16), wpg_ref[...]))
    out_ref[...] = x2 + gate * _dot(p_ref[...].astype(BF16), wpp_ref[...])


def _final(x1, moe, p2d, g, b, wpg, wpp, alpha):
    t, d = x1.shape
    rows = DENSE_ROWS
    const = lambda shape: pl.BlockSpec(shape, lambda i: (0,) * len(shape))
    rowblk = lambda w: pl.BlockSpec((rows, w), lambda i: (i, 0))
    return pl.pallas_call(
        functools.partial(_final_kernel, alpha=alpha),
        grid=(t // rows,),
        in_specs=[rowblk(d), pl.BlockSpec((rows * OUT_SLABS, LANES), lambda i: (i, 0)), rowblk(p2d.shape[1]),
                  const(g.shape), const(b.shape), const(wpg.shape), const(wpp.shape)],
        out_specs=rowblk(d),
        out_shape=jax.ShapeDtypeStruct((t, d), F32),
        compiler_params=pltpu.CompilerParams(dimension_semantics=("arbitrary",), vmem_limit_bytes=VMEM_LIMIT),
        name="final_ln_ple",
    )(x1, moe, p2d, g, b, wpg, wpp)


def _pad_lanes(w, width=LANES):
    return jnp.pad(w, ((0, 0), (0, width - w.shape[1])))


def _row(v, width=None):
    v = v.astype(F32).reshape(1, -1)
    return v if width is None else _pad_lanes(v, width)


def kernel(x, p, w_in, ssm_conv_w, ssm_conv_b, ssm_dt_bias, ssm_a_log, ssm_d, ssm_norm_w, lstm_i_bias, lstm_f_bias, lstm_norm_w, w_branch_ssm, w_branch_lstm, w_out, ln1_g, ln1_b, moe_w_group, moe_b_group, moe_w_expert, moe_b_expert, moe_w_gate, moe_w_up, moe_w_down, ln2_g, ln2_b, ple_w_proj, ple_w_gate):
    depth = w_in.shape[0]
    bsz, seq, d = x.shape
    t = bsz * seq
    alpha = (2.0 * depth) ** 0.25
    head_expand = (jnp.arange(LANES, dtype=jnp.int32)[:, None]
                   == jnp.arange(SSM_D_INNER, dtype=jnp.int32)[None, :] // SSM_HEAD_DIM).astype(BF16)
    for i in range(depth):
        splits = []
        c0 = 0
        for sz in IN_PROJ_SIZES:
            splits.append(w_in[i][:, c0:c0 + sz])
            c0 += sz
        w_z, w_xbc, w_dt, w_q, w_k, w_v, w_o, w_i, w_f, w_gs, w_gl = splits
        bf = lambda w: w.astype(BF16)
        y_ssm = _ssd_mixer(
            x, bf(w_z), bf(w_xbc), bf(_pad_lanes(w_dt)),
            ssm_conv_w[i].astype(F32), _row(ssm_conv_b[i]), _row(ssm_dt_bias[i], LANES),
            _row(-jnp.exp(ssm_a_log[i].astype(F32)), LANES),
            _row(jnp.repeat(ssm_d[i].astype(F32), SSM_HEAD_DIM)), _row(ssm_norm_w[i]), head_expand)
        w_if = jnp.concatenate([_pad_lanes(w_i), _pad_lanes(w_f)], axis=1)
        b_if = jnp.concatenate([_row(lstm_i_bias[i], LANES), _row(lstm_f_bias[i], LANES)], axis=1)
        y_lstm = _mlstm_mixer(x, bf(w_q * (LSTM_QK ** -0.5)), bf(w_k), bf(w_v), bf(w_o), bf(w_if), b_if,
                              _row(lstm_norm_w[i]))
        w_r = _pad_lanes(jnp.concatenate([moe_w_group[i], moe_w_expert[i]], axis=1).astype(F32))
        w_r_hi = w_r.astype(BF16)
        w_r_lo = (w_r - w_r_hi.astype(F32)).astype(BF16)
        b_r = _row(jnp.concatenate([moe_b_group[i], moe_b_expert[i]]), LANES)
        tile = min(MOE_TILE, t)
        x1, xp, tab, cnt = _merge(x.reshape(t, d), y_ssm.reshape(t, -1), y_lstm.reshape(t, -1),
                                  bf(w_gs), bf(w_gl), bf(w_branch_ssm[i]), bf(w_branch_lstm[i]), bf(w_out[i]),
                                  _row(ln1_g[i]), _row(ln1_b[i]), w_r_hi, w_r_lo, b_r, alpha, tile)
        cnt = cnt.reshape(t // tile, SUBLANES, LANES)[:, 0, :MOE_EXPERTS].reshape(-1)
        moe = _moe(xp, tab.reshape(-1), cnt, bf(moe_w_gate[i]), bf(moe_w_up[i]), bf(moe_w_down[i]), tile)
        x = _final(x1, moe, p[i].reshape(t, -1), _row(ln2_g[i]), _row(ln2_b[i]),
                   bf(ple_w_gate[i]), bf(ple_w_proj[i]), alpha).reshape(bsz, seq, d)
    return x
```

```python
import functools

import jax
import jax.numpy as jnp
from jax import lax
from jax.experimental import pallas as pl
from jax.experimental.pallas import tpu as pltpu

F32 = jnp.float32
BF16 = jnp.bfloat16

D_MODEL = 1024
PLE_DIM = 256
SSM_D_INNER = 1024
SSM_HEAD_DIM = 64
SSM_HEADS = 16
SSM_GROUPS = 4
SSM_STATE = 128
SSM_CONV = 4
SSM_XBC = SSM_D_INNER + 2 * SSM_GROUPS * SSM_STATE
LSTM_HEADS = 8
LSTM_QK = 64
LSTM_V = 128
LSTM_D_QK = LSTM_HEADS * LSTM_QK
LSTM_D_V = LSTM_HEADS * LSTM_V
CHUNK = 128
MOE_GROUPS = 8
MOE_PER_GROUP = 8
MOE_EXPERTS = 64
MOE_D_FF = 512
NORM_EPS = 1e-5
IN_PROJ_SIZES = (SSM_D_INNER, SSM_XBC, SSM_HEADS, LSTM_D_QK, LSTM_D_QK, LSTM_D_V, LSTM_D_V,
                 LSTM_HEADS, LSTM_HEADS, D_MODEL, D_MODEL)

LANES = 128
SUBLANES = 8
VMEM_LIMIT = 56 * 1024 * 1024
MOE_VMEM_LIMIT = 60 * 1024 * 1024

MIX_ROWS = 256
MIX_SEQS = 2
DENSE_ROWS = 512
MOE_TILE = 8192
MOE_CHUNK = 256
MOE_BLOCKS = (256, 320, 384)
MOE_WEIGHT_BUFFERS = 3
PACK_SLABS = D_MODEL // 2 // LANES
OUT_SLABS = D_MODEL // LANES


def _dot(a, b):
    return jnp.dot(a, b, preferred_element_type=F32)


def _dot_nt(a, b):
    return lax.dot_general(a, b, (((1,), (1,)), ((), ())), preferred_element_type=F32)


def _dot_tn(a, b):
    return lax.dot_general(a, b, (((0,), (0,)), ((), ())), preferred_element_type=F32)


def _sigmoid(x):
    return 1.0 / (1.0 + jnp.exp(-x))


def _softplus(x):
    return jnp.maximum(x, 0.0) + jnp.log(1.0 + jnp.exp(-jnp.abs(x)))


def _split3(x):
    hi = x.astype(BF16)
    r1 = x - hi.astype(F32)
    mid = r1.astype(BF16)
    lo = (r1 - mid.astype(F32)).astype(BF16)
    return hi, mid, lo


def _cumsum_rows(tril, x):
    hi, mid, lo = _split3(x)
    return _dot(tril, hi) + _dot(tril, mid) + _dot(tril, lo)


def _causal_masks():
    r = lax.broadcasted_iota(jnp.int32, (CHUNK, CHUNK), 0)
    c = lax.broadcasted_iota(jnp.int32, (CHUNK, CHUNK), 1)
    causal = r >= c
    return causal, jnp.where(causal, 1.0, 0.0).astype(BF16)


def _pair_cols(v, ha, hb, rows, lane):
    a = jnp.broadcast_to(v[:, ha:ha + 1], (rows, LANES))
    b = jnp.broadcast_to(v[:, hb:hb + 1], (rows, LANES))
    return jnp.where(lane < SSM_HEAD_DIM, a, b)


def _ssd_kernel(x_ref, wz_ref, wxbc_ref, wdt_ref, cw_ref, cb_ref, dtb_ref, aneg_ref, dexp_ref, nw_ref, ex_ref,
                y_ref, ext_ref, u_ref, st_ref, *, rows, nseq):
    n_slab = SSM_XBC // LANES
    tail = SUBLANES - SSM_CONV + 1

    @pl.when(pl.program_id(1) == 0)
    def _():
        ext_ref[:, :, 0:SUBLANES, :] = jnp.zeros((nseq, n_slab, SUBLANES, LANES), F32)
        st_ref[...] = jnp.zeros(st_ref.shape, F32)

    xb = x_ref[...].reshape(nseq * rows, x_ref.shape[-1]).astype(BF16)
    z = _dot(xb, wz_ref[...])
    dtr = _dot(xb, wdt_ref[...])
    for c2 in range(n_slab // 2):
        xbc = _dot(xb, wxbc_ref[:, c2 * 2 * LANES:(c2 + 1) * 2 * LANES])
        for s in range(nseq):
            for cc in range(2):
                ext_ref[s, 2 * c2 + cc, SUBLANES:SUBLANES + rows, :] = (
                    xbc[s * rows:(s + 1) * rows, cc * LANES:(cc + 1) * LANES])
    for s in range(nseq):
        for c in range(n_slab):
            conv = cb_ref[:, c * LANES:(c + 1) * LANES]
            for k in range(SSM_CONV):
                conv = conv + cw_ref[k:k + 1, c * LANES:(c + 1) * LANES] * ext_ref[s, c, pl.ds(tail + k, rows), :]
            u_ref[s, c] = conv * _sigmoid(conv)
            ext_ref[s, c, 0:SUBLANES, :] = ext_ref[s, c, rows:rows + SUBLANES, :]

    dt = _softplus(dtr + dtb_ref[...])
    da = dt * aneg_ref[...]
    causal, tril = _causal_masks()
    lane = lax.broadcasted_iota(jnp.int32, (CHUNK, LANES), 1)
    lane1 = lax.broadcasted_iota(jnp.int32, (1, LANES), 1)
    gw = SSM_D_INNER // SSM_GROUPS
    b_slab = SSM_D_INNER // LANES
    c_slab = b_slab + SSM_GROUPS

    for ci in range(rows // CHUNK):
        for s in range(nseq):
            r0 = ci * CHUNK
            q0 = s * rows + r0
            dt_c = dt[q0:q0 + CHUNK]
            a_cs = _cumsum_rows(tril, da[q0:q0 + CHUNK])
            a_cs_t = a_cs.T
            dt_t = dt_c.T
            a_last = a_cs[CHUNK - 1:CHUNK, :]
            ea_last = jnp.exp(a_last)
            ea_x = _dot(jnp.exp(a_cs).astype(BF16), ex_ref[...])
            sw_x = _dot((dt_c * jnp.exp(a_last - a_cs)).astype(BF16), ex_ref[...])
            for g in range(SSM_GROUPS):
                bm_g = u_ref[s, b_slab + g, r0:r0 + CHUNK, :].astype(BF16)
                cm_g = u_ref[s, c_slab + g, r0:r0 + CHUNK, :].astype(BF16)
                cb = _dot_nt(cm_g, bm_g)
                st_g = st_ref[s, :, g * gw:(g + 1) * gw]
                y_off = _dot(cm_g, st_g.astype(BF16))
                yy, xw, eal_x = [], [], []
                for pr in range(2):
                    ha = 4 * g + 2 * pr
                    hb = ha + 1
                    lo = g * gw + pr * LANES
                    xs_p = u_ref[s, 2 * g + pr, r0:r0 + CHUNK, :]
                    xs_pb = xs_p.astype(BF16)
                    rhs = jnp.concatenate([jnp.where(lane < SSM_HEAD_DIM, xs_pb, 0).astype(BF16),
                                           jnp.where(lane >= SSM_HEAD_DIM, xs_pb, 0).astype(BF16)], axis=0)
                    gs = []
                    for h in (ha, hb):
                        colb = jnp.broadcast_to(a_cs[:, h:h + 1], (CHUNK, CHUNK))
                        rowb = jnp.broadcast_to(a_cs_t[h:h + 1, :], (CHUNK, CHUNK))
                        dec = jnp.exp(jnp.where(causal, colb - rowb, -jnp.inf))
                        gs.append((cb * dec * jnp.broadcast_to(dt_t[h:h + 1, :], (CHUNK, CHUNK))).astype(BF16))
                    y_p = (_dot(jnp.concatenate(gs, axis=1), rhs)
                           + y_off[:, pr * LANES:(pr + 1) * LANES] * ea_x[:, lo:lo + LANES]
                           + dexp_ref[:, lo:lo + LANES] * xs_p)
                    zz = z[q0:q0 + CHUNK, lo:lo + LANES]
                    yy.append(y_p * (zz * _sigmoid(zz)))
                    xw.append((xs_p * sw_x[:, lo:lo + LANES]).astype(BF16))
                    eal_x.append(_pair_cols(ea_last, ha, hb, 1, lane1))
                st_ref[s, :, g * gw:(g + 1) * gw] = (st_g * jnp.concatenate(eal_x, axis=1)
                                                     + _dot_tn(bm_g, jnp.concatenate(xw, axis=1)))
                ms = sum(jnp.sum(t * t, axis=-1, keepdims=True) for t in yy) * (1.0 / gw)
                inv = lax.rsqrt(ms + NORM_EPS)
                for pr in range(2):
                    lo = g * gw + pr * LANES
                    y_ref[s, r0:r0 + CHUNK, lo:lo + LANES] = (yy[pr] * inv * nw_ref[:, lo:lo + LANES]).astype(BF16)


def _ssd_mixer(x, wz, wxbc, wdt, cw, cb, dtb, aneg, dexp, nw, ex):
    b, s, d = x.shape
    rows = MIX_ROWS
    nseq = MIX_SEQS
    const = lambda shape: pl.BlockSpec(shape, lambda i, j: (0,) * len(shape), pipeline_mode=pl.Buffered(1))
    return pl.pallas_call(
        functools.partial(_ssd_kernel, rows=rows, nseq=nseq),
        grid=(b // nseq, s // rows),
        in_specs=[pl.BlockSpec((nseq, rows, d), lambda i, j: (i, j, 0)),
                  const(wz.shape), const(wxbc.shape), const(wdt.shape), const(cw.shape), const(cb.shape),
                  const(dtb.shape), const(aneg.shape), const(dexp.shape), const(nw.shape), const(ex.shape)],
        out_specs=pl.BlockSpec((nseq, rows, SSM_D_INNER), lambda i, j: (i, j, 0)),
        out_shape=jax.ShapeDtypeStruct((b, s, SSM_D_INNER), BF16),
        scratch_shapes=[pltpu.VMEM((nseq, SSM_XBC // LANES, SUBLANES + rows, LANES), F32),
                        pltpu.VMEM((nseq, SSM_XBC // LANES, rows, LANES), F32),
                        pltpu.VMEM((nseq, SSM_STATE, SSM_D_INNER), F32)],
        compiler_params=pltpu.CompilerParams(dimension_semantics=("arbitrary", "arbitrary"),
                                             vmem_limit_bytes=VMEM_LIMIT),
        name="ssd_mixer",
    )(x, wz, wxbc, wdt, cw, cb, dtb, aneg, dexp, nw, ex)


def _mlstm_kernel(x_ref, wq_ref, wk_ref, wv_ref, wo_ref, wif_ref, bif_ref, nw_ref,
                  y_ref, c_ref, m_ref, *, rows, nseq):
    @pl.when(pl.program_id(1) == 0)
    def _():
        c_ref[...] = jnp.zeros(c_ref.shape, F32)
        m_ref[...] = jnp.full(m_ref.shape, -jnp.inf, F32)

    xb = x_ref[...].reshape(nseq * rows, x_ref.shape[-1]).astype(BF16)
    q = _dot(xb, wq_ref[...])
    k = _dot(xb, wk_ref[...])
    v = _dot(xb, wv_ref[...])
    o = _dot(xb, wo_ref[...])
    gif = _dot(xb, wif_ref[...]) + bif_ref[...]
    causal, tril = _causal_masks()
    lane = lax.broadcasted_iota(jnp.int32, (CHUNK, LANES), 1)
    lane1 = lax.broadcasted_iota(jnp.int32, (1, LANES), 1)
    row = lax.broadcasted_iota(jnp.int32, (LANES, 1), 0)
    head_lane = lane < LSTM_HEADS
    ones = jnp.ones((CHUNK, LANES), BF16)

    for ci in range(rows // CHUNK):
        for s in range(nseq):
            r0 = ci * CHUNK
            q0 = s * rows + r0
            log_i = gif[q0:q0 + CHUNK, 0:LANES]
            f_pre = gif[q0:q0 + CHUNK, LANES:2 * LANES]
            log_f = jnp.where(head_lane, jnp.minimum(f_pre, 0.0) - jnp.log(1.0 + jnp.exp(-jnp.abs(f_pre))), 0.0)
            fcum = _cumsum_rows(tril, log_f)
            f_tot = fcum[CHUNK - 1:CHUNK, :]
            gk = log_i - fcum
            m_loc = jnp.max(f_tot + gk, axis=0, keepdims=True)
            g_t = gk.T
            m_prev = m_ref[s]
            m_new = jnp.maximum(f_tot + m_prev, m_loc)
            s_prev = jnp.exp(f_tot + m_prev - m_new)
            s_loc = jnp.exp(m_loc - m_new)
            m_ref[s] = jnp.where(lane1 < LSTM_HEADS, m_new, -jnp.inf)
            w_shift = f_tot - m_loc

            for pr in range(LSTM_HEADS // 2):
                q_p = q[q0:q0 + CHUNK, pr * LANES:(pr + 1) * LANES]
                k_p = k[q0:q0 + CHUNK, pr * LANES:(pr + 1) * LANES]
                k_pb = k_p.astype(BF16)
                c_pair = c_ref[s, pr]
                c_pair_b = c_pair.astype(BF16)
                c_new = c_pair * jnp.where(row < LSTM_QK,
                                           jnp.broadcast_to(s_prev[:, 2 * pr:2 * pr + 1], (LANES, 1)),
                                           jnp.broadcast_to(s_prev[:, 2 * pr + 1:2 * pr + 2], (LANES, 1)))
                for hh in range(2):
                    h = 2 * pr + hh
                    in_head = (lane < LSTM_QK) if hh == 0 else (lane >= LSTM_QK)
                    v_h = v[q0:q0 + CHUNK, h * LSTM_V:(h + 1) * LSTM_V].astype(BF16)
                    v_ext = jnp.concatenate([v_h, ones], axis=1)
                    q_m = jnp.where(in_head, q_p, 0.0)
                    fcol = jnp.broadcast_to(fcum[:, h:h + 1], (CHUNK, CHUNK))
                    gcol = jnp.broadcast_to(gk[:, h:h + 1], (CHUNK, LANES))
                    rowb = jnp.broadcast_to(g_t[h:h + 1, :], (CHUNK, CHUNK))
                    log_d = jnp.where(causal, fcol + rowb, -jnp.inf)
                    inter_log = fcol + m_prev[:, h:h + 1]
                    m_t = jnp.maximum(inter_log, jnp.max(log_d, axis=-1, keepdims=True))
                    scores = _dot_nt(q_m.astype(BF16), k_pb) * jnp.exp(log_d - m_t)
                    inter_w = jnp.exp(inter_log - m_t)
                    lhs = jnp.concatenate([scores.astype(BF16), (q_m * inter_w).astype(BF16)], axis=1)
                    res = _dot(lhs, jnp.concatenate([v_ext, c_pair_b], axis=0))
                    hv = res[:, 0:LSTM_V] / jnp.maximum(jnp.abs(res[:, LSTM_V:2 * LSTM_V]), jnp.exp(-m_t))
                    ms = jnp.mean(hv * hv, axis=-1, keepdims=True)
                    o_h = o[q0:q0 + CHUNK, h * LSTM_V:(h + 1) * LSTM_V]
                    y_ref[s, r0:r0 + CHUNK, h * LSTM_V:(h + 1) * LSTM_V] = (
                        _sigmoid(o_h) * (hv * lax.rsqrt(ms + NORM_EPS) * nw_ref[:, h * LSTM_V:(h + 1) * LSTM_V])
                    ).astype(BF16)
                    kw = jnp.where(in_head, k_p * jnp.exp(gcol + w_shift[:, h:h + 1]), 0.0)
                    c_new = c_new + s_loc[:, h:h + 1] * _dot_tn(kw.astype(BF16), v_ext)
                c_ref[s, pr] = c_new


def _mlstm_mixer(x, wq, wk, wv, wo, wif, bif, nw):
    b, s, d = x.shape
    rows = MIX_ROWS
    nseq = MIX_SEQS
    const = lambda shape: pl.BlockSpec(shape, lambda i, j: (0,) * len(shape), pipeline_mode=pl.Buffered(1))
    return pl.pallas_call(
        functools.partial(_mlstm_kernel, rows=rows, nseq=nseq),
        grid=(b // nseq, s // rows),
        in_specs=[pl.BlockSpec((nseq, rows, d), lambda i, j: (i, j, 0)),
                  const(wq.shape), const(wk.shape), const(wv.shape), const(wo.shape), const(wif.shape),
                  const(bif.shape), const(nw.shape)],
        out_specs=pl.BlockSpec((nseq, rows, LSTM_D_V), lambda i, j: (i, j, 0)),
        out_shape=jax.ShapeDtypeStruct((b, s, LSTM_D_V), BF16),
        scratch_shapes=[pltpu.VMEM((nseq, LSTM_HEADS // 2, 2 * LSTM_QK, 2 * LSTM_V), F32),
                        pltpu.VMEM((nseq, 1, LANES), F32)],
        compiler_params=pltpu.CompilerParams(dimension_semantics=("arbitrary", "arbitrary"),
                                             vmem_limit_bytes=VMEM_LIMIT),
        name="mlstm_mixer",
    )(x, wq, wk, wv, wo, wif, bif, nw)


def _layer_norm(t, g, b):
    mu = jnp.mean(t, axis=-1, keepdims=True)
    tc = t - mu
    var = jnp.mean(tc * tc, axis=-1, keepdims=True)
    return tc * lax.rsqrt(var + NORM_EPS) * g + b


def _pack_pair(a, b):
    pa = lax.bitcast_convert_type(a.astype(BF16).astype(F32), jnp.uint32)
    pb = lax.bitcast_convert_type(b.astype(BF16).astype(F32), jnp.uint32)
    return pa | (pb >> 16)


def _unpack_pair(w):
    return (lax.bitcast_convert_type(w & jnp.uint32(0xFFFF0000), F32),
            lax.bitcast_convert_type(w << 16, F32))


def _route(logits):
    rows = logits.shape[0]
    lane = lax.broadcasted_iota(jnp.int32, (rows, LANES), 1).astype(F32)
    big = float(LANES)
    neg = -jnp.inf
    gl = jnp.where(lane < MOE_GROUPS, logits, neg)
    gmax = jnp.max(gl, axis=-1, keepdims=True)
    gidx = jnp.min(jnp.where(gl == gmax, lane, big), axis=-1, keepdims=True)
    gsum = jnp.sum(jnp.where(lane < MOE_GROUPS, jnp.exp(logits - gmax), 0.0), axis=-1, keepdims=True)
    grp_p = 1.0 / gsum
    lo = MOE_GROUPS + gidx * MOE_PER_GROUP
    el = jnp.where(lane >= lo, jnp.where(lane < lo + MOE_PER_GROUP, logits, neg), neg)
    m1 = jnp.max(el, axis=-1, keepdims=True)
    i1 = jnp.min(jnp.where(el == m1, lane, big), axis=-1, keepdims=True)
    el2 = jnp.where(lane == i1, neg, el)
    m2 = jnp.max(el2, axis=-1, keepdims=True)
    i2 = jnp.min(jnp.where(el2 == m2, lane, big), axis=-1, keepdims=True)
    e21 = jnp.exp(m2 - m1)
    g1 = grp_p / (1.0 + e21)
    g2 = grp_p * e21 / (1.0 + e21)
    return i1 - MOE_GROUPS, i2 - MOE_GROUPS, g1, g2


def _merge_kernel(x_ref, ys_ref, yl_ref, wgs_ref, wgl_ref, wa_ref, wb_ref, wo_ref, g_ref, b_ref,
                  wr_hi_ref, wr_lo_ref, br_ref, x1_ref, xp_ref, tab_ref, cnt_ref, carry_ref,
                  *, alpha, steps_per_tile):
    @pl.when(pl.program_id(0) % steps_per_tile == 0)
    def _():
        carry_ref[...] = jnp.zeros(carry_ref.shape, F32)

    x = x_ref[...]
    rows = x.shape[0]
    xb = x.astype(BF16)
    merged = (_sigmoid(_dot(xb, wgs_ref[...])) * _dot(ys_ref[...], wa_ref[...])
              + _sigmoid(_dot(xb, wgl_ref[...])) * _dot(yl_ref[...], wb_ref[...]))
    x1 = _layer_norm(alpha * x + _dot(merged.astype(BF16), wo_ref[...]), g_ref[...], b_ref[...])
    x1_ref[...] = x1
    half = x1.shape[1] // 2
    xp = _pack_pair(x1[:, :half], x1[:, half:])
    for c in range(PACK_SLABS):
        xp_ref[pl.ds(c, rows, stride=PACK_SLABS), :] = xp[:, c * LANES:(c + 1) * LANES]
    x_hi = x1.astype(BF16)
    x_lo = (x1 - x_hi.astype(F32)).astype(BF16)
    logits = (_dot(x_hi, wr_hi_ref[...]) + _dot(x_lo, wr_hi_ref[...]) + _dot(x_hi, wr_lo_ref[...])
              + br_ref[...])
    e1, e2, g1, g2 = _route(logits)
    lane = lax.broadcasted_iota(jnp.int32, (rows, LANES), 1)
    lane_f = lane.astype(F32)
    oh1 = lane_f == e1
    oh2 = lane_f == e2
    ohs = jnp.where(oh1, 1.0, jnp.where(oh2, 1.0, 0.0))
    r_i = lax.broadcasted_iota(jnp.int32, (rows, rows), 0)
    c_i = lax.broadcasted_iota(jnp.int32, (rows, rows), 1)
    before = jnp.where(r_i > c_i, 1.0, 0.0).astype(BF16)
    seen = _dot(before, ohs.astype(BF16)) + carry_ref[...]
    r1 = jnp.sum(jnp.where(oh1, seen, 0.0), axis=-1, keepdims=True)
    r2 = jnp.sum(jnp.where(oh2, seen, 0.0), axis=-1, keepdims=True)
    carry = carry_ref[...] + jnp.sum(ohs, axis=0, keepdims=True)
    carry_ref[...] = carry
    cnt_ref[...] = jnp.broadcast_to(carry, cnt_ref.shape).astype(jnp.int32)
    as_int = lambda v: v.astype(jnp.int32)
    as_bits = lambda v: lax.bitcast_convert_type(v, jnp.int32)
    tab = jnp.where(lane == 0, as_int(e1), jnp.where(lane == 1, as_int(e2), jnp.where(
        lane == 2, as_int(r1) * PACK_SLABS, jnp.where(lane == 3, as_int(r2) * PACK_SLABS, jnp.where(
            lane == 4, as_bits(g1), jnp.where(lane == 5, as_bits(g2), 0))))))
    tab_ref[...] = tab.T[0:SUBLANES, :]


def _merge(x2d, ys, yl, wgs, wgl, wa, wb, wo, g, b, wr_hi, wr_lo, br, alpha, tile):
    t, d = x2d.shape
    rows = DENSE_ROWS
    spt = tile // rows
    const = lambda shape: pl.BlockSpec(shape, lambda i: (0,) * len(shape))
    rowblk = lambda w: pl.BlockSpec((rows, w), lambda i: (i, 0))
    return pl.pallas_call(
        functools.partial(_merge_kernel, alpha=alpha, steps_per_tile=spt),
        grid=(t // rows,),
        in_specs=[rowblk(d), rowblk(SSM_D_INNER), rowblk(LSTM_D_V),
                  const(wgs.shape), const(wgl.shape), const(wa.shape), const(wb.shape), const(wo.shape),
                  const(g.shape), const(b.shape), const(wr_hi.shape), const(wr_lo.shape), const(br.shape)],
        out_specs=[rowblk(d),
                   pl.BlockSpec((rows * PACK_SLABS, LANES), lambda i: (i, 0)),
                   pl.BlockSpec((SUBLANES, rows), lambda i: (i // spt, i % spt)),
                   pl.BlockSpec((SUBLANES, LANES), lambda i: (i // spt, 0))],
        out_shape=[jax.ShapeDtypeStruct((t, d), F32),
                   jax.ShapeDtypeStruct((t * PACK_SLABS, LANES), jnp.uint32),
                   jax.ShapeDtypeStruct((t // tile * SUBLANES, tile), jnp.int32),
                   jax.ShapeDtypeStruct((t // tile * SUBLANES, LANES), jnp.int32)],
        scratch_shapes=[pltpu.VMEM((1, LANES), F32)],
        compiler_params=pltpu.CompilerParams(dimension_semantics=("arbitrary",), vmem_limit_bytes=VMEM_LIMIT),
        name="merge_ln_route",
    )(x2d, ys, yl, wgs, wgl, wa, wb, wo, g, b, wr_hi, wr_lo, br)


def _moe_kernel(cnt_ref, tab_ref, xp_hbm, wg_hbm, wu_hbm, wd_hbm, out_hbm, xys, ybuf, off_ref, dst_ref,
                wg_buf, wu_buf, wd_buf, xin, xout, wsem, isem, osem, *, tile, chunk, n_steps):
    i = pl.program_id(0)
    j = pl.program_id(1)
    group = SUBLANES * PACK_SLABS
    half = PACK_SLABS * LANES
    step = i * MOE_EXPERTS + j
    slot = step % MOE_WEIGHT_BUFFERS
    weights = ((wg_hbm, wg_buf), (wu_hbm, wu_buf), (wd_hbm, wd_buf))

    def weight_copy(k, expert, into):
        hbm, buf = weights[k]
        return pltpu.make_async_copy(hbm.at[expert], buf.at[into], wsem.at[k, into])

    @pl.when(step == 0)
    def _():
        for ahead in range(MOE_WEIGHT_BUFFERS - 1):
            for k in range(len(weights)):
                weight_copy(k, ahead % MOE_EXPERTS, ahead).start()

    nxt = step + MOE_WEIGHT_BUFFERS - 1

    @pl.when(nxt < n_steps)
    def _():
        for k in range(len(weights)):
            weight_copy(k, nxt % MOE_EXPERTS, nxt % MOE_WEIGHT_BUFFERS).start()

    n_chunks = tile // chunk
    rolled = lambda trips: trips + jnp.minimum(j, 0)

    def in_buf(c):
        return pl.multiple_of((c % 2) * (chunk * PACK_SLABS), group)

    def out_buf(c):
        return pl.multiple_of((c % 2) * (chunk * OUT_SLABS), SUBLANES)

    def in_copy(c):
        row0 = pl.multiple_of((i * tile + c * chunk) * PACK_SLABS, group)
        return pltpu.make_async_copy(xp_hbm.at[pl.ds(row0, chunk * PACK_SLABS)],
                                     xin.at[pl.ds(in_buf(c), chunk * PACK_SLABS)], isem.at[c % 2])

    def out_copy(c):
        row0 = pl.multiple_of((i * tile + c * chunk) * OUT_SLABS, SUBLANES)
        return pltpu.make_async_copy(xout.at[pl.ds(out_buf(c), chunk * OUT_SLABS)],
                                     out_hbm.at[pl.ds(row0, chunk * OUT_SLABS)], osem.at[c % 2])

    @pl.when(j == 0)
    def _():
        in_copy(0).start()

        def offsets(e, acc):
            padded = (cnt_ref[i * MOE_EXPERTS + e] + SUBLANES - 1) // SUBLANES * group
            off_ref[e] = acc
            end = acc + padded

            @pl.when(padded > 0)
            def _():
                xys[pl.ds(pl.multiple_of(end - group, group), group), :] = jnp.zeros((group, LANES), jnp.uint32)
            return end

        total = lax.fori_loop(0, MOE_EXPERTS, offsets, 0)
        off_ref[MOE_EXPERTS] = total
        tail = MOE_BLOCKS[-1] * PACK_SLABS
        xys[pl.ds(pl.multiple_of(total, group), tail), :] = jnp.zeros((tail, LANES), jnp.uint32)

        def distribute_chunk(c, carry):
            @pl.when(c + 1 < n_chunks)
            def _():
                in_copy(c + 1).start()
            in_copy(c).wait()

            g0 = c * (chunk // SUBLANES)
            shift = in_buf(c) - c * (chunk * PACK_SLABS)

            def distribute(tg, carry2):
                for u in range(SUBLANES):
                    t = tg * SUBLANES + u
                    d1 = pl.multiple_of(off_ref[tab_ref[t]] + tab_ref[2 * tile + t], PACK_SLABS)
                    d2 = pl.multiple_of(off_ref[tab_ref[tile + t]] + tab_ref[3 * tile + t], PACK_SLABS)
                    dst_ref[t] = d1
                    dst_ref[tile + t] = d2
                    row = xin[pl.ds(pl.multiple_of(shift + t * PACK_SLABS, PACK_SLABS), PACK_SLABS), :]
                    xys[pl.ds(d1, PACK_SLABS), :] = row
                    xys[pl.ds(d2, PACK_SLABS), :] = row
                return carry2

            lax.fori_loop(g0, g0 + rolled(chunk // SUBLANES), distribute, 0)
            return carry

        lax.fori_loop(0, rolled(n_chunks), distribute_chunk, 0)

    for k in range(len(weights)):
        weight_copy(k, j, slot).wait()
    wg_ref, wu_ref, wd_ref = wg_buf.at[slot], wu_buf.at[slot], wd_buf.at[slot]
    start = off_ref[j]
    n = off_ref[j + 1] - start

    def expert_block(base, rows, valid):
        words = [xys[pl.ds(base + c, rows, stride=PACK_SLABS), :] for c in range(PACK_SLABS)]
        parts = [_unpack_pair(w) for w in words]
        x_hi = jnp.concatenate([p[0].astype(BF16) for p in parts], axis=1)
        x_lo = jnp.concatenate([p[1].astype(BF16) for p in parts], axis=1)
        hg = _dot(x_hi, wg_ref[0:half, :]) + _dot(x_lo, wg_ref[half:2 * half, :])
        hu = _dot(x_hi, wu_ref[0:half, :]) + _dot(x_lo, wu_ref[half:2 * half, :])
        y = _dot((hg * _sigmoid(hg) * hu).astype(BF16), wd_ref[...])
        for c in range(PACK_SLABS):
            ybuf[pl.ds(c, rows, stride=PACK_SLABS), :] = _pack_pair(
                y[:, c * LANES:(c + 1) * LANES], y[:, half + c * LANES:half + (c + 1) * LANES])

        def copy(g, c):
            r = pl.multiple_of(g * group, group)
            xys[pl.ds(base + r, group), :] = ybuf[pl.ds(r, group), :]
            return c

        lax.fori_loop(0, valid // group, copy, 0)

    lo = 0
    for rows in MOE_BLOCKS:
        hi = rows * PACK_SLABS

        @pl.when(jnp.logical_and(n > lo, n <= hi))
        def _(rows=rows):
            expert_block(pl.multiple_of(start, group), rows, n)
        lo = hi

    @pl.when(n > lo)
    def _():
        step = MOE_BLOCKS[0] * PACK_SLABS

        def block(bi, carry):
            expert_block(pl.multiple_of(start + bi * step, group), MOE_BLOCKS[0],
                         jnp.minimum(step, n - bi * step))
            return carry

        lax.fori_loop(0, (n + step - 1) // step, block, 0)

    @pl.when(j == MOE_EXPERTS - 1)
    def _():
        def combine_chunk(c, carry):
            @pl.when(c >= 2)
            def _():
                out_copy(c - 2).wait()

            g0 = c * (chunk // SUBLANES)
            shift = out_buf(c) - c * (chunk * OUT_SLABS)

            def combine(tg, carry2):
                for u in range(SUBLANES):
                    t = tg * SUBLANES + u
                    g1 = lax.bitcast_convert_type(tab_ref[4 * tile + t], F32)
                    g2 = lax.bitcast_convert_type(tab_ref[5 * tile + t], F32)
                    a_hi, a_lo = _unpack_pair(xys[pl.ds(pl.multiple_of(dst_ref[t], PACK_SLABS), PACK_SLABS), :])
                    b_hi, b_lo = _unpack_pair(
                        xys[pl.ds(pl.multiple_of(dst_ref[tile + t], PACK_SLABS), PACK_SLABS), :])
                    xout[pl.ds(pl.multiple_of(shift + t * OUT_SLABS, OUT_SLABS), OUT_SLABS), :] = jnp.concatenate(
                        [g1 * a_hi + g2 * b_hi, g1 * a_lo + g2 * b_lo], axis=0)
                return carry2

            lax.fori_loop(g0, g0 + rolled(chunk // SUBLANES), combine, 0)
            out_copy(c).start()
            return carry

        lax.fori_loop(0, rolled(n_chunks), combine_chunk, 0)
        for c in range(max(0, n_chunks - 2), n_chunks):
            out_copy(c).wait()


def _moe(xp, tab, cnt, wg, wu, wd, tile):
    t = xp.shape[0] // PACK_SLABS
    d = D_MODEL
    n_tiles = t // tile
    chunk = min(MOE_CHUNK, tile)
    seg_rows = 2 * tile + MOE_EXPERTS * SUBLANES + MOE_BLOCKS[-1]
    grid_spec = pltpu.PrefetchScalarGridSpec(
        num_scalar_prefetch=1,
        grid=(n_tiles, MOE_EXPERTS),
        in_specs=[pl.BlockSpec((SUBLANES * tile,), lambda i, j, c: (i,), memory_space=pltpu.SMEM,
                               pipeline_mode=pl.Buffered(1)),
                  pl.BlockSpec(memory_space=pl.ANY), pl.BlockSpec(memory_space=pl.ANY),
                  pl.BlockSpec(memory_space=pl.ANY), pl.BlockSpec(memory_space=pl.ANY)],
        out_specs=pl.BlockSpec(memory_space=pl.ANY),
        scratch_shapes=[pltpu.VMEM((seg_rows * PACK_SLABS, LANES), jnp.uint32),
                        pltpu.VMEM((MOE_BLOCKS[-1] * PACK_SLABS, LANES), jnp.uint32),
                        pltpu.SMEM((LANES,), jnp.int32), pltpu.SMEM((2 * tile,), jnp.int32),
                        pltpu.VMEM((MOE_WEIGHT_BUFFERS, d, MOE_D_FF), BF16),
                        pltpu.VMEM((MOE_WEIGHT_BUFFERS, d, MOE_D_FF), BF16),
                        pltpu.VMEM((MOE_WEIGHT_BUFFERS, MOE_D_FF, d), BF16),
                        pltpu.VMEM((2 * chunk * PACK_SLABS, LANES), jnp.uint32),
                        pltpu.VMEM((2 * chunk * OUT_SLABS, LANES), F32),
                        pltpu.SemaphoreType.DMA((3, MOE_WEIGHT_BUFFERS)),
                        pltpu.SemaphoreType.DMA((2,)), pltpu.SemaphoreType.DMA((2,))],
    )
    return pl.pallas_call(
        functools.partial(_moe_kernel, tile=tile, chunk=chunk, n_steps=n_tiles * MOE_EXPERTS),
        grid_spec=grid_spec,
        out_shape=jax.ShapeDtypeStruct((t * OUT_SLABS, LANES), F32),
        compiler_params=pltpu.CompilerParams(dimension_semantics=("arbitrary", "arbitrary"),
                                             vmem_limit_bytes=MOE_VMEM_LIMIT),
        name="moe_experts",
    )(cnt, tab, xp, wg, wu, wd)


def _final_kernel(x1_ref, moe_ref, p_ref, g_ref, b_ref, wpg_ref, wpp_ref, out_ref, *, alpha):
    rows = x1_ref.shape[0]
    moe = jnp.concatenate([moe_ref[pl.ds(c, rows, stride=OUT_SLABS), :] for c in range(OUT_SLABS)], axis=1)
    x2 = _layer_norm(alpha * x1_ref[...] + moe, g_ref[...], b_ref[...])
    gate = _sigmoid(_dot(x2.astype(BF16), wpg_ref[...]))
    out_ref[...] = x2 + gate * _dot(p_ref[...].astype(BF16), wpp_ref[...])


def _final(x1, moe, p2d, g, b, wpg, wpp, alpha):
    t, d = x1.shape
    rows = DENSE_ROWS
    const = lambda shape: pl.BlockSpec(shape, lambda i: (0,) * len(shape))
    rowblk = lambda w: pl.BlockSpec((rows, w), lambda i: (i, 0))
    return pl.pallas_call(
        functools.partial(_final_kernel, alpha=alpha),
        grid=(t // rows,),
        in_specs=[rowblk(d), pl.BlockSpec((rows * OUT_SLABS, LANES), lambda i: (i, 0)), rowblk(p2d.shape[1]),
                  const(g.shape), const(b.shape), const(wpg.shape), const(wpp.shape)],
        out_specs=rowblk(d),
        out_shape=jax.ShapeDtypeStruct((t, d), F32),
        compiler_params=pltpu.CompilerParams(dimension_semantics=("arbitrary",), vmem_limit_bytes=VMEM_LIMIT),
        name="final_ln_ple",
    )(x1, moe, p2d, g, b, wpg, wpp)


def _pad_lanes(w, width=LANES):
    return jnp.pad(w, ((0, 0), (0, width - w.shape[1])))


def _row(v, width=None):
    v = v.astype(F32).reshape(1, -1)
    return v if width is None else _pad_lanes(v, width)


def kernel(x, p, w_in, ssm_conv_w, ssm_conv_b, ssm_dt_bias, ssm_a_log, ssm_d, ssm_norm_w, lstm_i_bias, lstm_f_bias, lstm_norm_w, w_branch_ssm, w_branch_lstm, w_out, ln1_g, ln1_b, moe_w_group, moe_b_group, moe_w_expert, moe_b_expert, moe_w_gate, moe_w_up, moe_w_down, ln2_g, ln2_b, ple_w_proj, ple_w_gate):
    depth = w_in.shape[0]
    bsz, seq, d = x.shape
    t = bsz * seq
    alpha = (2.0 * depth) ** 0.25
    head_expand = (jnp.arange(LANES, dtype=jnp.int32)[:, None]
                   == jnp.arange(SSM_D_INNER, dtype=jnp.int32)[None, :] // SSM_HEAD_DIM).astype(BF16)
    for i in range(depth):
        splits = []
        c0 = 0
        for sz in IN_PROJ_SIZES:
            splits.append(w_in[i][:, c0:c0 + sz])
            c0 += sz
        w_z, w_xbc, w_dt, w_q, w_k, w_v, w_o, w_i, w_f, w_gs, w_gl = splits
        bf = lambda w: w.astype(BF16)
        y_ssm = _ssd_mixer(
            x, bf(w_z), bf(w_xbc), bf(_pad_lanes(w_dt)),
            ssm_conv_w[i].astype(F32), _row(ssm_conv_b[i]), _row(ssm_dt_bias[i], LANES),
            _row(-jnp.exp(ssm_a_log[i].astype(F32)), LANES),
            _row(jnp.repeat(ssm_d[i].astype(F32), SSM_HEAD_DIM)), _row(ssm_norm_w[i]), head_expand)
        w_if = jnp.concatenate([_pad_lanes(w_i), _pad_lanes(w_f)], axis=1)
        b_if = jnp.concatenate([_row(lstm_i_bias[i], LANES), _row(lstm_f_bias[i], LANES)], axis=1)
        y_lstm = _mlstm_mixer(x, bf(w_q * (LSTM_QK ** -0.5)), bf(w_k), bf(w_v), bf(w_o), bf(w_if), b_if,
                              _row(lstm_norm_w[i]))
        w_r = _pad_lanes(jnp.concatenate([moe_w_group[i], moe_w_expert[i]], axis=1).astype(F32))
        w_r_hi = w_r.astype(BF16)
        w_r_lo = (w_r - w_r_hi.astype(F32)).astype(BF16)
        b_r = _row(jnp.concatenate([moe_b_group[i], moe_b_expert[i]]), LANES)
        tile = min(MOE_TILE, t)
        x1, xp, tab, cnt = _merge(x.reshape(t, d), y_ssm.reshape(t, -1), y_lstm.reshape(t, -1),
                                  bf(w_gs), bf(w_gl), bf(w_branch_ssm[i]), bf(w_branch_lstm[i]), bf(w_out[i]),
                                  _row(ln1_g[i]), _row(ln1_b[i]), w_r_hi, w_r_lo, b_r, alpha, tile)
        cnt = cnt.reshape(t // tile, SUBLANES, LANES)[:, 0, :MOE_EXPERTS].reshape(-1)
        moe = _moe(xp, tab.reshape(-1), cnt, bf(moe_w_gate[i]), bf(moe_w_up[i]), bf(moe_w_down[i]), tile)
        x = _final(x1, moe, p[i].reshape(t, -1), _row(ln2_g[i]), _row(ln2_b[i]),
                   bf(ple_w_gate[i]), bf(ple_w_proj[i]), alpha).reshape(bsz, seq, d)
    return x
```

```python
import functools

import jax
import jax.numpy as jnp
from jax import lax
from jax.experimental import pallas as pl
from jax.experimental.pallas import tpu as pltpu

F32 = jnp.float32
BF16 = jnp.bfloat16

D_MODEL = 1024
PLE_DIM = 256
SSM_D_INNER = 1024
SSM_HEAD_DIM = 64
SSM_HEADS = 16
SSM_GROUPS = 4
SSM_STATE = 128
SSM_CONV = 4
SSM_XBC = SSM_D_INNER + 2 * SSM_GROUPS * SSM_STATE
LSTM_HEADS = 8
LSTM_QK = 64
LSTM_V = 128
LSTM_D_QK = LSTM_HEADS * LSTM_QK
LSTM_D_V = LSTM_HEADS * LSTM_V
CHUNK = 128
MOE_GROUPS = 8
MOE_PER_GROUP = 8
MOE_EXPERTS = 64
MOE_D_FF = 512
NORM_EPS = 1e-5
IN_PROJ_SIZES = (SSM_D_INNER, SSM_XBC, SSM_HEADS, LSTM_D_QK, LSTM_D_QK, LSTM_D_V, LSTM_D_V,
                 LSTM_HEADS, LSTM_HEADS, D_MODEL, D_MODEL)

LANES = 128
SUBLANES = 8
VMEM_LIMIT = 56 * 1024 * 1024
MOE_VMEM_LIMIT = 60 * 1024 * 1024

MIX_ROWS = 256
MIX_SEQS = 2
DENSE_ROWS = 512
MOE_TILE = 8192
MOE_CHUNK = 256
MOE_BLOCKS = (256, 320, 384)
MOE_WEIGHT_BUFFERS = 3
PACK_SLABS = D_MODEL // 2 // LANES
OUT_SLABS = D_MODEL // LANES


def _dot(a, b):
    return jnp.dot(a, b, preferred_element_type=F32)


def _dot_nt(a, b):
    return lax.dot_general(a, b, (((1,), (1,)), ((), ())), preferred_element_type=F32)


def _dot_tn(a, b):
    return lax.dot_general(a, b, (((0,), (0,)), ((), ())), preferred_element_type=F32)


def _sigmoid(x):
    return 1.0 / (1.0 + jnp.exp(-x))


def _softplus(x):
    return jnp.maximum(x, 0.0) + jnp.log(1.0 + jnp.exp(-jnp.abs(x)))


def _split3(x):
    hi = x.astype(BF16)
    r1 = x - hi.astype(F32)
    mid = r1.astype(BF16)
    lo = (r1 - mid.astype(F32)).astype(BF16)
    return hi, mid, lo


def _cumsum_rows(tril, x):
    hi, mid, lo = _split3(x)
    return _dot(tril, hi) + _dot(tril, mid) + _dot(tril, lo)


def _causal_masks():
    r = lax.broadcasted_iota(jnp.int32, (CHUNK, CHUNK), 0)
    c = lax.broadcasted_iota(jnp.int32, (CHUNK, CHUNK), 1)
    causal = r >= c
    return causal, jnp.where(causal, 1.0, 0.0).astype(BF16)


def _pair_cols(v, ha, hb, rows, lane):
    a = jnp.broadcast_to(v[:, ha:ha + 1], (rows, LANES))
    b = jnp.broadcast_to(v[:, hb:hb + 1], (rows, LANES))
    return jnp.where(lane < SSM_HEAD_DIM, a, b)


def _ssd_kernel(x_ref, wz_ref, wxbc_ref, wdt_ref, cw_ref, cb_ref, dtb_ref, aneg_ref, dexp_ref, nw_ref, ex_ref,
                y_ref, ext_ref, u_ref, st_ref, *, rows, nseq):
    n_slab = SSM_XBC // LANES
    tail = SUBLANES - SSM_CONV + 1

    @pl.when(pl.program_id(1) == 0)
    def _():
        ext_ref[:, :, 0:SUBLANES, :] = jnp.zeros((nseq, n_slab, SUBLANES, LANES), F32)
        st_ref[...] = jnp.zeros(st_ref.shape, F32)

    xb = x_ref[...].reshape(nseq * rows, x_ref.shape[-1]).astype(BF16)
    z = _dot(xb, wz_ref[...])
    dtr = _dot(xb, wdt_ref[...])
    for c2 in range(n_slab // 2):
        xbc = _dot(xb, wxbc_ref[:, c2 * 2 * LANES:(c2 + 1) * 2 * LANES])
        for s in range(nseq):
            for cc in range(2):
                ext_ref[s, 2 * c2 + cc, SUBLANES:SUBLANES + rows, :] = (
                    xbc[s * rows:(s + 1) * rows, cc * LANES:(cc + 1) * LANES])
    for s in range(nseq):
        for c in range(n_slab):
            conv = cb_ref[:, c * LANES:(c + 1) * LANES]
            for k in range(SSM_CONV):
                conv = conv + cw_ref[k:k + 1, c * LANES:(c + 1) * LANES] * ext_ref[s, c, pl.ds(tail + k, rows), :]
            u_ref[s, c] = conv * _sigmoid(conv)
            ext_ref[s, c, 0:SUBLANES, :] = ext_ref[s, c, rows:rows + SUBLANES, :]

    dt = _softplus(dtr + dtb_ref[...])
    da = dt * aneg_ref[...]
    causal, tril = _causal_masks()
    lane = lax.broadcasted_iota(jnp.int32, (CHUNK, LANES), 1)
    lane1 = lax.broadcasted_iota(jnp.int32, (1, LANES), 1)
    gw = SSM_D_INNER // SSM_GROUPS
    b_slab = SSM_D_INNER // LANES
    c_slab = b_slab + SSM_GROUPS

    for ci in range(rows // CHUNK):
        for s in range(nseq):
            r0 = ci * CHUNK
            q0 = s * rows + r0
            dt_c = dt[q0:q0 + CHUNK]
            a_cs = _cumsum_rows(tril, da[q0:q0 + CHUNK])
            a_cs_t = a_cs.T
            dt_t = dt_c.T
            a_last = a_cs[CHUNK - 1:CHUNK, :]
            ea_last = jnp.exp(a_last)
            ea_x = _dot(jnp.exp(a_cs).astype(BF16), ex_ref[...])
            sw_x = _dot((dt_c * jnp.exp(a_last - a_cs)).astype(BF16), ex_ref[...])
            for g in range(SSM_GROUPS):
                bm_g = u_ref[s, b_slab + g, r0:r0 + CHUNK, :].astype(BF16)
                cm_g = u_ref[s, c_slab + g, r0:r0 + CHUNK, :].astype(BF16)
                cb = _dot_nt(cm_g, bm_g)
                st_g = st_ref[s, :, g * gw:(g + 1) * gw]
                y_off = _dot(cm_g, st_g.astype(BF16))
                yy, xw, eal_x = [], [], []
                for pr in range(2):
                    ha = 4 * g + 2 * pr
                    hb = ha + 1
                    lo = g * gw + pr * LANES
                    xs_p = u_ref[s, 2 * g + pr, r0:r0 + CHUNK, :]
                    xs_pb = xs_p.astype(BF16)
                    rhs = jnp.concatenate([jnp.where(lane < SSM_HEAD_DIM, xs_pb, 0).astype(BF16),
                                           jnp.where(lane >= SSM_HEAD_DIM, xs_pb, 0).astype(BF16)], axis=0)
                    gs = []
                    for h in (ha, hb):
                        colb = jnp.broadcast_to(a_cs[:, h:h + 1], (CHUNK, CHUNK))
                        rowb = jnp.broadcast_to(a_cs_t[h:h + 1, :], (CHUNK, CHUNK))
                        dec = jnp.exp(jnp.where(causal, colb - rowb, -jnp.inf))
                        gs.append((cb * dec * jnp.broadcast_to(dt_t[h:h + 1, :], (CHUNK, CHUNK))).astype(BF16))
                    y_p = (_dot(jnp.concatenate(gs, axis=1), rhs)
                           + y_off[:, pr * LANES:(pr + 1) * LANES] * ea_x[:, lo:lo + LANES]
                           + dexp_ref[:, lo:lo + LANES] * xs_p)
                    zz = z[q0:q0 + CHUNK, lo:lo + LANES]
                    yy.append(y_p * (zz * _sigmoid(zz)))
                    xw.append((xs_p * sw_x[:, lo:lo + LANES]).astype(BF16))
                    eal_x.append(_pair_cols(ea_last, ha, hb, 1, lane1))
                st_ref[s, :, g * gw:(g + 1) * gw] = (st_g * jnp.concatenate(eal_x, axis=1)
                                                     + _dot_tn(bm_g, jnp.concatenate(xw, axis=1)))
                ms = sum(jnp.sum(t * t, axis=-1, keepdims=True) for t in yy) * (1.0 / gw)
                inv = lax.rsqrt(ms + NORM_EPS)
                for pr in range(2):
                    lo = g * gw + pr * LANES
                    y_ref[s, r0:r0 + CHUNK, lo:lo + LANES] = (yy[pr] * inv * nw_ref[:, lo:lo + LANES]).astype(BF16)


def _ssd_mixer(x, wz, wxbc, wdt, cw, cb, dtb, aneg, dexp, nw, ex):
    b, s, d = x.shape
    rows = MIX_ROWS
    nseq = MIX_SEQS
    const = lambda shape: pl.BlockSpec(shape, lambda i, j: (0,) * len(shape), pipeline_mode=pl.Buffered(1))
    return pl.pallas_call(
        functools.partial(_ssd_kernel, rows=rows, nseq=nseq),
        grid=(b // nseq, s // rows),
        in_specs=[pl.BlockSpec((nseq, rows, d), lambda i, j: (i, j, 0)),
                  const(wz.shape), const(wxbc.shape), const(wdt.shape), const(cw.shape), const(cb.shape),
                  const(dtb.shape), const(aneg.shape), const(dexp.shape), const(nw.shape), const(ex.shape)],
        out_specs=pl.BlockSpec((nseq, rows, SSM_D_INNER), lambda i, j: (i, j, 0)),
        out_shape=jax.ShapeDtypeStruct((b, s, SSM_D_INNER), BF16),
        scratch_shapes=[pltpu.VMEM((nseq, SSM_XBC // LANES, SUBLANES + rows, LANES), F32),
                        pltpu.VMEM((nseq, SSM_XBC // LANES, rows, LANES), F32),
                        pltpu.VMEM((nseq, SSM_STATE, SSM_D_INNER), F32)],
        compiler_params=pltpu.CompilerParams(dimension_semantics=("arbitrary", "arbitrary"),
                                             vmem_limit_bytes=VMEM_LIMIT),
        name="ssd_mixer",
    )(x, wz, wxbc, wdt, cw, cb, dtb, aneg, dexp, nw, ex)


def _mlstm_kernel(x_ref, wq_ref, wk_ref, wv_ref, wo_ref, wif_ref, bif_ref, nw_ref,
                  y_ref, c_ref, m_ref, *, rows, nseq):
    @pl.when(pl.program_id(1) == 0)
    def _():
        c_ref[...] = jnp.zeros(c_ref.shape, F32)
        m_ref[...] = jnp.full(m_ref.shape, -jnp.inf, F32)

    xb = x_ref[...].reshape(nseq * rows, x_ref.shape[-1]).astype(BF16)
    q = _dot(xb, wq_ref[...])
    k = _dot(xb, wk_ref[...])
    v = _dot(xb, wv_ref[...])
    o = _dot(xb, wo_ref[...])
    gif = _dot(xb, wif_ref[...]) + bif_ref[...]
    causal, tril = _causal_masks()
    lane = lax.broadcasted_iota(jnp.int32, (CHUNK, LANES), 1)
    lane1 = lax.broadcasted_iota(jnp.int32, (1, LANES), 1)
    row = lax.broadcasted_iota(jnp.int32, (LANES, 1), 0)
    head_lane = lane < LSTM_HEADS
    ones = jnp.ones((CHUNK, LANES), BF16)

    for ci in range(rows // CHUNK):
        for s in range(nseq):
            r0 = ci * CHUNK
            q0 = s * rows + r0
            log_i = gif[q0:q0 + CHUNK, 0:LANES]
            f_pre = gif[q0:q0 + CHUNK, LANES:2 * LANES]
            log_f = jnp.where(head_lane, jnp.minimum(f_pre, 0.0) - jnp.log(1.0 + jnp.exp(-jnp.abs(f_pre))), 0.0)
            fcum = _cumsum_rows(tril, log_f)
            f_tot = fcum[CHUNK - 1:CHUNK, :]
            gk = log_i - fcum
            m_loc = jnp.max(f_tot + gk, axis=0, keepdims=True)
            g_t = gk.T
            m_prev = m_ref[s]
            m_new = jnp.maximum(f_tot + m_prev, m_loc)
            s_prev = jnp.exp(f_tot + m_prev - m_new)
            s_loc = jnp.exp(m_loc - m_new)
            m_ref[s] = jnp.where(lane1 < LSTM_HEADS, m_new, -jnp.inf)
            w_shift = f_tot - m_loc

            for pr in range(LSTM_HEADS // 2):
                q_p = q[q0:q0 + CHUNK, pr * LANES:(pr + 1) * LANES]
                k_p = k[q0:q0 + CHUNK, pr * LANES:(pr + 1) * LANES]
                k_pb = k_p.astype(BF16)
                c_pair = c_ref[s, pr]
                c_pair_b = c_pair.astype(BF16)
                c_new = c_pair * jnp.where(row < LSTM_QK,
                                           jnp.broadcast_to(s_prev[:, 2 * pr:2 * pr + 1], (LANES, 1)),
                                           jnp.broadcast_to(s_prev[:, 2 * pr + 1:2 * pr + 2], (LANES, 1)))
                for hh in range(2):
                    h = 2 * pr + hh
                    in_head = (lane < LSTM_QK) if hh == 0 else (lane >= LSTM_QK)
                    v_h = v[q0:q0 + CHUNK, h * LSTM_V:(h + 1) * LSTM_V].astype(BF16)
                    v_ext = jnp.concatenate([v_h, ones], axis=1)
                    q_m = jnp.where(in_head, q_p, 0.0)
                    fcol = jnp.broadcast_to(fcum[:, h:h + 1], (CHUNK, CHUNK))
                    gcol = jnp.broadcast_to(gk[:, h:h + 1], (CHUNK, LANES))
                    rowb = jnp.broadcast_to(g_t[h:h + 1, :], (CHUNK, CHUNK))
                    log_d = jnp.where(causal, fcol + rowb, -jnp.inf)
                    inter_log = fcol + m_prev[:, h:h + 1]
                    m_t = jnp.maximum(inter_log, jnp.max(log_d, axis=-1, keepdims=True))
                    scores = _dot_nt(q_m.astype(BF16), k_pb) * jnp.exp(log_d - m_t)
                    inter_w = jnp.exp(inter_log - m_t)
                    lhs = jnp.concatenate([scores.astype(BF16), (q_m * inter_w).astype(BF16)], axis=1)
                    res = _dot(lhs, jnp.concatenate([v_ext, c_pair_b], axis=0))
                    hv = res[:, 0:LSTM_V] / jnp.maximum(jnp.abs(res[:, LSTM_V:2 * LSTM_V]), jnp.exp(-m_t))
                    ms = jnp.mean(hv * hv, axis=-1, keepdims=True)
                    o_h = o[q0:q0 + CHUNK, h * LSTM_V:(h + 1) * LSTM_V]
                    y_ref[s, r0:r0 + CHUNK, h * LSTM_V:(h + 1) * LSTM_V] = (
                        _sigmoid(o_h) * (hv * lax.rsqrt(ms + NORM_EPS) * nw_ref[:, h * LSTM_V:(h + 1) * LSTM_V])
                    ).astype(BF16)
                    kw = jnp.where(in_head, k_p * jnp.exp(gcol + w_shift[:, h:h + 1]), 0.0)
                    c_new = c_new + s_loc[:, h:h + 1] * _dot_tn(kw.astype(BF16), v_ext)
                c_ref[s, pr] = c_new


def _mlstm_mixer(x, wq, wk, wv, wo, wif, bif, nw):
    b, s, d = x.shape
    rows = MIX_ROWS
    nseq = MIX_SEQS
    const = lambda shape: pl.BlockSpec(shape, lambda i, j: (0,) * len(shape), pipeline_mode=pl.Buffered(1))
    return pl.pallas_call(
        functools.partial(_mlstm_kernel, rows=rows, nseq=nseq),
        grid=(b // nseq, s // rows),
        in_specs=[pl.BlockSpec((nseq, rows, d), lambda i, j: (i, j, 0)),
                  const(wq.shape), const(wk.shape), const(wv.shape), const(wo.shape), const(wif.shape),
                  const(bif.shape), const(nw.shape)],
        out_specs=pl.BlockSpec((nseq, rows, LSTM_D_V), lambda i, j: (i, j, 0)),
        out_shape=jax.ShapeDtypeStruct((b, s, LSTM_D_V), BF16),
        scratch_shapes=[pltpu.VMEM((nseq, LSTM_HEADS // 2, 2 * LSTM_QK, 2 * LSTM_V), F32),
                        pltpu.VMEM((nseq, 1, LANES), F32)],
        compiler_params=pltpu.CompilerParams(dimension_semantics=("arbitrary", "arbitrary"),
                                             vmem_limit_bytes=VMEM_LIMIT),
        name="mlstm_mixer",
    )(x, wq, wk, wv, wo, wif, bif, nw)


def _layer_norm(t, g, b):
    mu = jnp.mean(t, axis=-1, keepdims=True)
    tc = t - mu
    var = jnp.mean(tc * tc, axis=-1, keepdims=True)
    return tc * lax.rsqrt(var + NORM_EPS) * g + b


def _pack_pair(a, b):
    pa = lax.bitcast_convert_type(a.astype(BF16).astype(F32), jnp.uint32)
    pb = lax.bitcast_convert_type(b.astype(BF16).astype(F32), jnp.uint32)
    return pa | (pb >> 16)


def _unpack_pair(w):
    return (lax.bitcast_convert_type(w & jnp.uint32(0xFFFF0000), F32),
            lax.bitcast_convert_type(w << 16, F32))


def _route(logits):
    rows = logits.shape[0]
    lane = lax.broadcasted_iota(jnp.int32, (rows, LANES), 1).astype(F32)
    big = float(LANES)
    neg = -jnp.inf
    gl = jnp.where(lane < MOE_GROUPS, logits, neg)
    gmax = jnp.max(gl, axis=-1, keepdims=True)
    gidx = jnp.min(jnp.where(gl == gmax, lane, big), axis=-1, keepdims=True)
    gsum = jnp.sum(jnp.where(lane < MOE_GROUPS, jnp.exp(logits - gmax), 0.0), axis=-1, keepdims=True)
    grp_p = 1.0 / gsum
    lo = MOE_GROUPS + gidx * MOE_PER_GROUP
    el = jnp.where(lane >= lo, jnp.where(lane < lo + MOE_PER_GROUP, logits, neg), neg)
    m1 = jnp.max(el, axis=-1, keepdims=True)
    i1 = jnp.min(jnp.where(el == m1, lane, big), axis=-1, keepdims=True)
    el2 = jnp.where(lane == i1, neg, el)
    m2 = jnp.max(el2, axis=-1, keepdims=True)
    i2 = jnp.min(jnp.where(el2 == m2, lane, big), axis=-1, keepdims=True)
    e21 = jnp.exp(m2 - m1)
    g1 = grp_p / (1.0 + e21)
    g2 = grp_p * e21 / (1.0 + e21)
    return i1 - MOE_GROUPS, i2 - MOE_GROUPS, g1, g2


def _merge_kernel(x_ref, ys_ref, yl_ref, wgs_ref, wgl_ref, wa_ref, wb_ref, wo_ref, g_ref, b_ref,
                  wr_hi_ref, wr_lo_ref, br_ref, eg_ref, eu_ref, ed_ref,
                  x1_ref, xp_ref, tab_ref, cnt_ref, egb_ref, eub_ref, edb_ref, carry_ref,
                  *, alpha, steps_per_tile, cast_steps):
    @pl.when(pl.program_id(0) % steps_per_tile == 0)
    def _():
        carry_ref[...] = jnp.zeros(carry_ref.shape, F32)

    @pl.when(pl.program_id(0) < cast_steps)
    def _():
        egb_ref[...] = eg_ref[...].astype(BF16)
        eub_ref[...] = eu_ref[...].astype(BF16)
        edb_ref[...] = ed_ref[...].astype(BF16)

    x = x_ref[...]
    rows = x.shape[0]
    xb = x.astype(BF16)
    merged = (_sigmoid(_dot(xb, wgs_ref[...])) * _dot(ys_ref[...], wa_ref[...])
              + _sigmoid(_dot(xb, wgl_ref[...])) * _dot(yl_ref[...], wb_ref[...]))
    x1 = _layer_norm(alpha * x + _dot(merged.astype(BF16), wo_ref[...]), g_ref[...], b_ref[...])
    x1_ref[...] = x1
    half = x1.shape[1] // 2
    xp = _pack_pair(x1[:, :half], x1[:, half:])
    for c in range(PACK_SLABS):
        xp_ref[pl.ds(c, rows, stride=PACK_SLABS), :] = xp[:, c * LANES:(c + 1) * LANES]
    x_hi = x1.astype(BF16)
    x_lo = (x1 - x_hi.astype(F32)).astype(BF16)
    logits = (_dot(x_hi, wr_hi_ref[...]) + _dot(x_lo, wr_hi_ref[...]) + _dot(x_hi, wr_lo_ref[...])
              + br_ref[...])
    e1, e2, g1, g2 = _route(logits)
    lane = lax.broadcasted_iota(jnp.int32, (rows, LANES), 1)
    lane_f = lane.astype(F32)
    oh1 = lane_f == e1
    oh2 = lane_f == e2
    ohs = jnp.where(oh1, 1.0, jnp.where(oh2, 1.0, 0.0))
    r_i = lax.broadcasted_iota(jnp.int32, (rows, rows), 0)
    c_i = lax.broadcasted_iota(jnp.int32, (rows, rows), 1)
    before = jnp.where(r_i > c_i, 1.0, 0.0).astype(BF16)
    seen = _dot(before, ohs.astype(BF16)) + carry_ref[...]
    r1 = jnp.sum(jnp.where(oh1, seen, 0.0), axis=-1, keepdims=True)
    r2 = jnp.sum(jnp.where(oh2, seen, 0.0), axis=-1, keepdims=True)
    carry = carry_ref[...] + jnp.sum(ohs, axis=0, keepdims=True)
    carry_ref[...] = carry
    cnt_ref[...] = jnp.broadcast_to(carry, cnt_ref.shape).astype(jnp.int32)
    as_int = lambda v: v.astype(jnp.int32)
    as_bits = lambda v: lax.bitcast_convert_type(v, jnp.int32)
    tab = jnp.where(lane == 0, as_int(e1), jnp.where(lane == 1, as_int(e2), jnp.where(
        lane == 2, as_int(r1) * PACK_SLABS, jnp.where(lane == 3, as_int(r2) * PACK_SLABS, jnp.where(
            lane == 4, as_bits(g1), jnp.where(lane == 5, as_bits(g2), 0))))))
    tab_ref[...] = tab.T[0:SUBLANES, :]


def _merge(x2d, ys, yl, wgs, wgl, wa, wb, wo, g, b, wr_hi, wr_lo, br, eg, eu, ed, alpha, tile):
    t, d = x2d.shape
    rows = DENSE_ROWS
    spt = tile // rows
    n_steps = t // rows
    per_step = -(-MOE_EXPERTS // n_steps)
    cast_steps = MOE_EXPERTS // per_step
    assert per_step * cast_steps == MOE_EXPERTS and cast_steps <= n_steps
    const = lambda shape: pl.BlockSpec(shape, lambda i: (0,) * len(shape), pipeline_mode=pl.Buffered(1))
    rowblk = lambda w: pl.BlockSpec((rows, w), lambda i: (i, 0))
    expert = lambda w: pl.BlockSpec((per_step,) + w.shape[1:], lambda i: (jnp.minimum(i, cast_steps - 1), 0, 0))
    return pl.pallas_call(
        functools.partial(_merge_kernel, alpha=alpha, steps_per_tile=spt, cast_steps=cast_steps),
        grid=(n_steps,),
        in_specs=[rowblk(d), rowblk(SSM_D_INNER), rowblk(LSTM_D_V),
                  const(wgs.shape), const(wgl.shape), const(wa.shape), const(wb.shape), const(wo.shape),
                  const(g.shape), const(b.shape), const(wr_hi.shape), const(wr_lo.shape), const(br.shape),
                  expert(eg), expert(eu), expert(ed)],
        out_specs=[rowblk(d),
                   pl.BlockSpec((rows * PACK_SLABS, LANES), lambda i: (i, 0)),
                   pl.BlockSpec((SUBLANES, rows), lambda i: (i // spt, i % spt)),
                   pl.BlockSpec((SUBLANES, LANES), lambda i: (i // spt, 0)),
                   expert(eg), expert(eu), expert(ed)],
        out_shape=[jax.ShapeDtypeStruct((t, d), F32),
                   jax.ShapeDtypeStruct((t * PACK_SLABS, LANES), jnp.uint32),
                   jax.ShapeDtypeStruct((t // tile * SUBLANES, tile), jnp.int32),
                   jax.ShapeDtypeStruct((t // tile * SUBLANES, LANES), jnp.int32),
                   jax.ShapeDtypeStruct(eg.shape, BF16), jax.ShapeDtypeStruct(eu.shape, BF16),
                   jax.ShapeDtypeStruct(ed.shape, BF16)],
        scratch_shapes=[pltpu.VMEM((1, LANES), F32)],
        compiler_params=pltpu.CompilerParams(dimension_semantics=("arbitrary",), vmem_limit_bytes=VMEM_LIMIT),
        name="merge_ln_route",
    )(x2d, ys, yl, wgs, wgl, wa, wb, wo, g, b, wr_hi, wr_lo, br, eg, eu, ed)


def _moe_kernel(cnt_ref, tab_ref, xp_hbm, wg_hbm, wu_hbm, wd_hbm, out_hbm, xys, ybuf, off_ref, dst_ref,
                wg_buf, wu_buf, wd_buf, xin, xout, wsem, isem, osem, *, tile, chunk, n_steps):
    i = pl.program_id(0)
    j = pl.program_id(1)
    group = SUBLANES * PACK_SLABS
    half = PACK_SLABS * LANES
    step = i * MOE_EXPERTS + j
    slot = step % MOE_WEIGHT_BUFFERS
    weights = ((wg_hbm, wg_buf), (wu_hbm, wu_buf), (wd_hbm, wd_buf))

    def weight_copy(k, expert, into):
        hbm, buf = weights[k]
        return pltpu.make_async_copy(hbm.at[expert], buf.at[into], wsem.at[k, into])

    @pl.when(step == 0)
    def _():
        for ahead in range(MOE_WEIGHT_BUFFERS - 1):
            for k in range(len(weights)):
                weight_copy(k, ahead % MOE_EXPERTS, ahead).start()

    nxt = step + MOE_WEIGHT_BUFFERS - 1

    @pl.when(nxt < n_steps)
    def _():
        for k in range(len(weights)):
            weight_copy(k, nxt % MOE_EXPERTS, nxt % MOE_WEIGHT_BUFFERS).start()

    n_chunks = tile // chunk
    rolled = lambda trips: trips + jnp.minimum(j, 0)

    def in_buf(c):
        return pl.multiple_of((c % 2) * (chunk * PACK_SLABS), group)

    def out_buf(c):
        return pl.multiple_of((c % 2) * (chunk * OUT_SLABS), SUBLANES)

    def in_copy(c):
        row0 = pl.multiple_of((i * tile + c * chunk) * PACK_SLABS, group)
        return pltpu.make_async_copy(xp_hbm.at[pl.ds(row0, chunk * PACK_SLABS)],
                                     xin.at[pl.ds(in_buf(c), chunk * PACK_SLABS)], isem.at[c % 2])

    def out_copy(c):
        row0 = pl.multiple_of((i * tile + c * chunk) * OUT_SLABS, SUBLANES)
        return pltpu.make_async_copy(xout.at[pl.ds(out_buf(c), chunk * OUT_SLABS)],
                                     out_hbm.at[pl.ds(row0, chunk * OUT_SLABS)], osem.at[c % 2])

    @pl.when(j == 0)
    def _():
        in_copy(0).start()

        def offsets(e, acc):
            padded = (cnt_ref[i * MOE_EXPERTS + e] + SUBLANES - 1) // SUBLANES * group
            off_ref[e] = acc
            end = acc + padded

            @pl.when(padded > 0)
            def _():
                xys[pl.ds(pl.multiple_of(end - group, group), group), :] = jnp.zeros((group, LANES), jnp.uint32)
            return end

        total = lax.fori_loop(0, MOE_EXPERTS, offsets, 0)
        off_ref[MOE_EXPERTS] = total
        tail = MOE_BLOCKS[-1] * PACK_SLABS
        xys[pl.ds(pl.multiple_of(total, group), tail), :] = jnp.zeros((tail, LANES), jnp.uint32)

        def distribute_chunk(c, carry):
            @pl.when(c + 1 < n_chunks)
            def _():
                in_copy(c + 1).start()
            in_copy(c).wait()

            g0 = c * (chunk // SUBLANES)
            shift = in_buf(c) - c * (chunk * PACK_SLABS)

            def distribute(tg, carry2):
                for u in range(SUBLANES):
                    t = tg * SUBLANES + u
                    d1 = pl.multiple_of(off_ref[tab_ref[t]] + tab_ref[2 * tile + t], PACK_SLABS)
                    d2 = pl.multiple_of(off_ref[tab_ref[tile + t]] + tab_ref[3 * tile + t], PACK_SLABS)
                    dst_ref[t] = d1
                    dst_ref[tile + t] = d2
                    row = xin[pl.ds(pl.multiple_of(shift + t * PACK_SLABS, PACK_SLABS), PACK_SLABS), :]
                    xys[pl.ds(d1, PACK_SLABS), :] = row
                    xys[pl.ds(d2, PACK_SLABS), :] = row
                return carry2

            lax.fori_loop(g0, g0 + rolled(chunk // SUBLANES), distribute, 0)
            return carry

        lax.fori_loop(0, rolled(n_chunks), distribute_chunk, 0)

    for k in range(len(weights)):
        weight_copy(k, j, slot).wait()
    wg_ref, wu_ref, wd_ref = wg_buf.at[slot], wu_buf.at[slot], wd_buf.at[slot]
    start = off_ref[j]
    n = off_ref[j + 1] - start

    def expert_block(base, rows, valid):
        words = [xys[pl.ds(base + c, rows, stride=PACK_SLABS), :] for c in range(PACK_SLABS)]
        parts = [_unpack_pair(w) for w in words]
        x_hi = jnp.concatenate([p[0].astype(BF16) for p in parts], axis=1)
        x_lo = jnp.concatenate([p[1].astype(BF16) for p in parts], axis=1)
        hg = _dot(x_hi, wg_ref[0:half, :]) + _dot(x_lo, wg_ref[half:2 * half, :])
        hu = _dot(x_hi, wu_ref[0:half, :]) + _dot(x_lo, wu_ref[half:2 * half, :])
        y = _dot((hg * _sigmoid(hg) * hu).astype(BF16), wd_ref[...])
        for c in range(PACK_SLABS):
            ybuf[pl.ds(c, rows, stride=PACK_SLABS), :] = _pack_pair(
                y[:, c * LANES:(c + 1) * LANES], y[:, half + c * LANES:half + (c + 1) * LANES])

        def copy(g, c):
            r = pl.multiple_of(g * group, group)
            xys[pl.ds(base + r, group), :] = ybuf[pl.ds(r, group), :]
            return c

        lax.fori_loop(0, valid // group, copy, 0)

    lo = 0
    for rows in MOE_BLOCKS:
        hi = rows * PACK_SLABS

        @pl.when(jnp.logical_and(n > lo, n <= hi))
        def _(rows=rows):
            expert_block(pl.multiple_of(start, group), rows, n)
        lo = hi

    @pl.when(n > lo)
    def _():
        step = MOE_BLOCKS[0] * PACK_SLABS

        def block(bi, carry):
            expert_block(pl.multiple_of(start + bi * step, group), MOE_BLOCKS[0],
                         jnp.minimum(step, n - bi * step))
            return carry

        lax.fori_loop(0, (n + step - 1) // step, block, 0)

    @pl.when(j == MOE_EXPERTS - 1)
    def _():
        def combine_chunk(c, carry):
            @pl.when(c >= 2)
            def _():
                out_copy(c - 2).wait()

            g0 = c * (chunk // SUBLANES)
            shift = out_buf(c) - c * (chunk * OUT_SLABS)

            def combine(tg, carry2):
                for u in range(SUBLANES):
                    t = tg * SUBLANES + u
                    g1 = lax.bitcast_convert_type(tab_ref[4 * tile + t], F32)
                    g2 = lax.bitcast_convert_type(tab_ref[5 * tile + t], F32)
                    a_hi, a_lo = _unpack_pair(xys[pl.ds(pl.multiple_of(dst_ref[t], PACK_SLABS), PACK_SLABS), :])
                    b_hi, b_lo = _unpack_pair(
                        xys[pl.ds(pl.multiple_of(dst_ref[tile + t], PACK_SLABS), PACK_SLABS), :])
                    xout[pl.ds(pl.multiple_of(shift + t * OUT_SLABS, OUT_SLABS), OUT_SLABS), :] = jnp.concatenate(
                        [g1 * a_hi + g2 * b_hi, g1 * a_lo + g2 * b_lo], axis=0)
                return carry2

            lax.fori_loop(g0, g0 + rolled(chunk // SUBLANES), combine, 0)
            out_copy(c).start()
            return carry

        lax.fori_loop(0, rolled(n_chunks), combine_chunk, 0)
        for c in range(max(0, n_chunks - 2), n_chunks):
            out_copy(c).wait()


def _moe(xp, tab, cnt, wg, wu, wd, tile):
    t = xp.shape[0] // PACK_SLABS
    d = D_MODEL
    n_tiles = t // tile
    chunk = min(MOE_CHUNK, tile)
    seg_rows = 2 * tile + MOE_EXPERTS * SUBLANES + MOE_BLOCKS[-1]
    grid_spec = pltpu.PrefetchScalarGridSpec(
        num_scalar_prefetch=1,
        grid=(n_tiles, MOE_EXPERTS),
        in_specs=[pl.BlockSpec((SUBLANES * tile,), lambda i, j, c: (i,), memory_space=pltpu.SMEM,
                               pipeline_mode=pl.Buffered(1)),
                  pl.BlockSpec(memory_space=pl.ANY), pl.BlockSpec(memory_space=pl.ANY),
                  pl.BlockSpec(memory_space=pl.ANY), pl.BlockSpec(memory_space=pl.ANY)],
        out_specs=pl.BlockSpec(memory_space=pl.ANY),
        scratch_shapes=[pltpu.VMEM((seg_rows * PACK_SLABS, LANES), jnp.uint32),
                        pltpu.VMEM((MOE_BLOCKS[-1] * PACK_SLABS, LANES), jnp.uint32),
                        pltpu.SMEM((LANES,), jnp.int32), pltpu.SMEM((2 * tile,), jnp.int32),
                        pltpu.VMEM((MOE_WEIGHT_BUFFERS, d, MOE_D_FF), BF16),
                        pltpu.VMEM((MOE_WEIGHT_BUFFERS, d, MOE_D_FF), BF16),
                        pltpu.VMEM((MOE_WEIGHT_BUFFERS, MOE_D_FF, d), BF16),
                        pltpu.VMEM((2 * chunk * PACK_SLABS, LANES), jnp.uint32),
                        pltpu.VMEM((2 * chunk * OUT_SLABS, LANES), F32),
                        pltpu.SemaphoreType.DMA((3, MOE_WEIGHT_BUFFERS)),
                        pltpu.SemaphoreType.DMA((2,)), pltpu.SemaphoreType.DMA((2,))],
    )
    return pl.pallas_call(
        functools.partial(_moe_kernel, tile=tile, chunk=chunk, n_steps=n_tiles * MOE_EXPERTS),
        grid_spec=grid_spec,
        out_shape=jax.ShapeDtypeStruct((t * OUT_SLABS, LANES), F32),
        compiler_params=pltpu.CompilerParams(dimension_semantics=("arbitrary", "arbitrary"),
                                             vmem_limit_bytes=MOE_VMEM_LIMIT),
        name="moe_experts",
    )(cnt, tab, xp, wg, wu, wd)


def _final_kernel(x1_ref, moe_ref, p_ref, g_ref, b_ref, wpg_ref, wpp_ref, out_ref, *, alpha):
    rows = x1_ref.shape[0]
    moe = jnp.concatenate([moe_ref[pl.ds(c, rows, stride=OUT_SLABS), :] for c in range(OUT_SLABS)], axis=1)
    x2 = _layer_norm(alpha * x1_ref[...] + moe, g_ref[...], b_ref[...])
    gate = _sigmoid(_dot(x2.astype(BF16), wpg_ref[...]))
    out_ref[...] = x2 + gate * _dot(p_ref[...].astype(BF16), wpp_ref[...])


def _final(x1, moe, p2d, g, b, wpg, wpp, alpha):
    t, d = x1.shape
    rows = DENSE_ROWS
    const = lambda shape: pl.BlockSpec(shape, lambda i: (0,) * len(shape))
    rowblk = lambda w: pl.BlockSpec((rows, w), lambda i: (i, 0))
    return pl.pallas_call(
        functools.partial(_final_kernel, alpha=alpha),
        grid=(t // rows,),
        in_specs=[rowblk(d), pl.BlockSpec((rows * OUT_SLABS, LANES), lambda i: (i, 0)), rowblk(p2d.shape[1]),
                  const(g.shape), const(b.shape), const(wpg.shape), const(wpp.shape)],
        out_specs=rowblk(d),
        out_shape=jax.ShapeDtypeStruct((t, d), F32),
        compiler_params=pltpu.CompilerParams(dimension_semantics=("arbitrary",), vmem_limit_bytes=VMEM_LIMIT),
        name="final_ln_ple",
    )(x1, moe, p2d, g, b, wpg, wpp)


def _pad_lanes(w, width=LANES):
    return jnp.pad(w, ((0, 0), (0, width - w.shape[1])))


def _row(v, width=None):
    v = v.astype(F32).reshape(1, -1)
    return v if width is None else _pad_lanes(v, width)


def kernel(x, p, w_in, ssm_conv_w, ssm_conv_b, ssm_dt_bias, ssm_a_log, ssm_d, ssm_norm_w, lstm_i_bias, lstm_f_bias, lstm_norm_w, w_branch_ssm, w_branch_lstm, w_out, ln1_g, ln1_b, moe_w_group, moe_b_group, moe_w_expert, moe_b_expert, moe_w_gate, moe_w_up, moe_w_down, ln2_g, ln2_b, ple_w_proj, ple_w_gate):
    depth = w_in.shape[0]
    bsz, seq, d = x.shape
    t = bsz * seq
    alpha = (2.0 * depth) ** 0.25
    head_expand = (jnp.arange(LANES, dtype=jnp.int32)[:, None]
                   == jnp.arange(SSM_D_INNER, dtype=jnp.int32)[None, :] // SSM_HEAD_DIM).astype(BF16)
    for i in range(depth):
        splits = []
        c0 = 0
        for sz in IN_PROJ_SIZES:
            splits.append(w_in[i][:, c0:c0 + sz])
            c0 += sz
        w_z, w_xbc, w_dt, w_q, w_k, w_v, w_o, w_i, w_f, w_gs, w_gl = splits
        bf = lambda w: w.astype(BF16)
        y_ssm = _ssd_mixer(
            x, bf(w_z), bf(w_xbc), bf(_pad_lanes(w_dt)),
            ssm_conv_w[i].astype(F32), _row(ssm_conv_b[i]), _row(ssm_dt_bias[i], LANES),
            _row(-jnp.exp(ssm_a_log[i].astype(F32)), LANES),
            _row(jnp.repeat(ssm_d[i].astype(F32), SSM_HEAD_DIM)), _row(ssm_norm_w[i]), head_expand)
        w_if = jnp.concatenate([_pad_lanes(w_i), _pad_lanes(w_f)], axis=1)
        b_if = jnp.concatenate([_row(lstm_i_bias[i], LANES), _row(lstm_f_bias[i], LANES)], axis=1)
        y_lstm = _mlstm_mixer(x, bf(w_q * (LSTM_QK ** -0.5)), bf(w_k), bf(w_v), bf(w_o), bf(w_if), b_if,
                              _row(lstm_norm_w[i]))
        w_r = _pad_lanes(jnp.concatenate([moe_w_group[i], moe_w_expert[i]], axis=1).astype(F32))
        w_r_hi = w_r.astype(BF16)
        w_r_lo = (w_r - w_r_hi.astype(F32)).astype(BF16)
        b_r = _row(jnp.concatenate([moe_b_group[i], moe_b_expert[i]]), LANES)
        tile = min(MOE_TILE, t)
        x1, xp, tab, cnt, e_gate, e_up, e_down = _merge(
            x.reshape(t, d), y_ssm.reshape(t, -1), y_lstm.reshape(t, -1),
            bf(w_gs), bf(w_gl), bf(w_branch_ssm[i]), bf(w_branch_lstm[i]), bf(w_out[i]),
            _row(ln1_g[i]), _row(ln1_b[i]), w_r_hi, w_r_lo, b_r,
            moe_w_gate[i].astype(F32), moe_w_up[i].astype(F32), moe_w_down[i].astype(F32), alpha, tile)
        cnt = cnt.reshape(t // tile, SUBLANES, LANES)[:, 0, :MOE_EXPERTS].reshape(-1)
        moe = _moe(xp, tab.reshape(-1), cnt, e_gate, e_up, e_down, tile)
        x = _final(x1, moe, p[i].reshape(t, -1), _row(ln2_g[i]), _row(ln2_b[i]),
                   bf(ple_w_gate[i]), bf(ple_w_proj[i]), alpha).reshape(bsz, seq, d)
    return x
```

```python
import functools

import jax
import jax.numpy as jnp
from jax import lax
from jax.experimental import pallas as pl
from jax.experimental.pallas import tpu as pltpu

F32 = jnp.float32
BF16 = jnp.bfloat16

D_MODEL = 1024
PLE_DIM = 256
SSM_D_INNER = 1024
SSM_HEAD_DIM = 64
SSM_HEADS = 16
SSM_GROUPS = 4
SSM_STATE = 128
SSM_CONV = 4
SSM_XBC = SSM_D_INNER + 2 * SSM_GROUPS * SSM_STATE
LSTM_HEADS = 8
LSTM_QK = 64
LSTM_V = 128
LSTM_D_QK = LSTM_HEADS * LSTM_QK
LSTM_D_V = LSTM_HEADS * LSTM_V
CHUNK = 128
MOE_GROUPS = 8
MOE_PER_GROUP = 8
MOE_EXPERTS = 64
MOE_D_FF = 512
NORM_EPS = 1e-5
IN_PROJ_SIZES = (SSM_D_INNER, SSM_XBC, SSM_HEADS, LSTM_D_QK, LSTM_D_QK, LSTM_D_V, LSTM_D_V,
                 LSTM_HEADS, LSTM_HEADS, D_MODEL, D_MODEL)

LANES = 128
SUBLANES = 8
VMEM_LIMIT = 56 * 1024 * 1024
MOE_VMEM_LIMIT = 60 * 1024 * 1024

MIX_ROWS = 256
MIX_SEQS = 2
DENSE_ROWS = 512
MOE_TILE = 8192
MOE_CHUNK = 256
MOE_BLOCKS = (256, 320, 384)
MOE_WEIGHT_BUFFERS = 4
MOE_WEIGHT_SPLIT = 4
PACK_SLABS = D_MODEL // 2 // LANES
OUT_SLABS = D_MODEL // LANES


def _dot(a, b):
    return jnp.dot(a, b, preferred_element_type=F32)


def _dot_nt(a, b):
    return lax.dot_general(a, b, (((1,), (1,)), ((), ())), preferred_element_type=F32)


def _dot_tn(a, b):
    return lax.dot_general(a, b, (((0,), (0,)), ((), ())), preferred_element_type=F32)


def _sigmoid(x):
    return 1.0 / (1.0 + jnp.exp(-x))


def _softplus(x):
    return jnp.maximum(x, 0.0) + jnp.log(1.0 + jnp.exp(-jnp.abs(x)))


def _split3(x):
    hi = x.astype(BF16)
    r1 = x - hi.astype(F32)
    mid = r1.astype(BF16)
    lo = (r1 - mid.astype(F32)).astype(BF16)
    return hi, mid, lo


def _cumsum_rows(tril, x):
    hi, mid, lo = _split3(x)
    return _dot(tril, hi) + _dot(tril, mid) + _dot(tril, lo)


def _causal_masks():
    r = lax.broadcasted_iota(jnp.int32, (CHUNK, CHUNK), 0)
    c = lax.broadcasted_iota(jnp.int32, (CHUNK, CHUNK), 1)
    causal = r >= c
    return causal, jnp.where(causal, 1.0, 0.0).astype(BF16)


def _pair_cols(v, ha, hb, rows, lane):
    a = jnp.broadcast_to(v[:, ha:ha + 1], (rows, LANES))
    b = jnp.broadcast_to(v[:, hb:hb + 1], (rows, LANES))
    return jnp.where(lane < SSM_HEAD_DIM, a, b)


def _ssd_kernel(x_ref, wz_ref, wxbc_ref, wdt_ref, cw_ref, cb_ref, dtb_ref, aneg_ref, dexp_ref, nw_ref, ex_ref,
                y_ref, ext_ref, u_ref, st_ref, *, rows, nseq):
    n_slab = SSM_XBC // LANES
    tail = SUBLANES - SSM_CONV + 1

    @pl.when(pl.program_id(1) == 0)
    def _():
        ext_ref[:, :, 0:SUBLANES, :] = jnp.zeros((nseq, n_slab, SUBLANES, LANES), F32)
        st_ref[...] = jnp.zeros(st_ref.shape, F32)

    xb = x_ref[...].reshape(nseq * rows, x_ref.shape[-1]).astype(BF16)
    z = _dot(xb, wz_ref[...])
    dtr = _dot(xb, wdt_ref[...])
    for c2 in range(n_slab // 2):
        xbc = _dot(xb, wxbc_ref[:, c2 * 2 * LANES:(c2 + 1) * 2 * LANES])
        for s in range(nseq):
            for cc in range(2):
                ext_ref[s, 2 * c2 + cc, SUBLANES:SUBLANES + rows, :] = (
                    xbc[s * rows:(s + 1) * rows, cc * LANES:(cc + 1) * LANES])
    for s in range(nseq):
        for c in range(n_slab):
            conv = cb_ref[:, c * LANES:(c + 1) * LANES]
            for k in range(SSM_CONV):
                conv = conv + cw_ref[k:k + 1, c * LANES:(c + 1) * LANES] * ext_ref[s, c, pl.ds(tail + k, rows), :]
            u_ref[s, c] = conv * _sigmoid(conv)
            ext_ref[s, c, 0:SUBLANES, :] = ext_ref[s, c, rows:rows + SUBLANES, :]

    dt = _softplus(dtr + dtb_ref[...])
    da = dt * aneg_ref[...]
    causal, tril = _causal_masks()
    lane = lax.broadcasted_iota(jnp.int32, (CHUNK, LANES), 1)
    lane1 = lax.broadcasted_iota(jnp.int32, (1, LANES), 1)
    gw = SSM_D_INNER // SSM_GROUPS
    b_slab = SSM_D_INNER // LANES
    c_slab = b_slab + SSM_GROUPS

    for ci in range(rows // CHUNK):
        for s in range(nseq):
            r0 = ci * CHUNK
            q0 = s * rows + r0
            dt_c = dt[q0:q0 + CHUNK]
            a_cs = _cumsum_rows(tril, da[q0:q0 + CHUNK])
            a_cs_t = a_cs.T
            dt_t = dt_c.T
            a_last = a_cs[CHUNK - 1:CHUNK, :]
            ea_last = jnp.exp(a_last)
            ea_x = _dot(jnp.exp(a_cs).astype(BF16), ex_ref[...])
            sw_x = _dot((dt_c * jnp.exp(a_last - a_cs)).astype(BF16), ex_ref[...])
            for g in range(SSM_GROUPS):
                bm_g = u_ref[s, b_slab + g, r0:r0 + CHUNK, :].astype(BF16)
                cm_g = u_ref[s, c_slab + g, r0:r0 + CHUNK, :].astype(BF16)
                cb = _dot_nt(cm_g, bm_g)
                st_g = st_ref[s, :, g * gw:(g + 1) * gw]
                y_off = _dot(cm_g, st_g.astype(BF16))
                yy, xw, eal_x = [], [], []
                for pr in range(2):
                    ha = 4 * g + 2 * pr
                    hb = ha + 1
                    lo = g * gw + pr * LANES
                    xs_p = u_ref[s, 2 * g + pr, r0:r0 + CHUNK, :]
                    xs_pb = xs_p.astype(BF16)
                    rhs = jnp.concatenate([jnp.where(lane < SSM_HEAD_DIM, xs_pb, 0).astype(BF16),
                                           jnp.where(lane >= SSM_HEAD_DIM, xs_pb, 0).astype(BF16)], axis=0)
                    gs = []
                    for h in (ha, hb):
                        colb = jnp.broadcast_to(a_cs[:, h:h + 1], (CHUNK, CHUNK))
                        rowb = jnp.broadcast_to(a_cs_t[h:h + 1, :], (CHUNK, CHUNK))
                        dec = jnp.exp(jnp.where(causal, colb - rowb, -jnp.inf))
                        gs.append((cb * dec * jnp.broadcast_to(dt_t[h:h + 1, :], (CHUNK, CHUNK))).astype(BF16))
                    y_p = (_dot(jnp.concatenate(gs, axis=1), rhs)
                           + y_off[:, pr * LANES:(pr + 1) * LANES] * ea_x[:, lo:lo + LANES]
                           + dexp_ref[:, lo:lo + LANES] * xs_p)
                    zz = z[q0:q0 + CHUNK, lo:lo + LANES]
                    yy.append(y_p * (zz * _sigmoid(zz)))
                    xw.append((xs_p * sw_x[:, lo:lo + LANES]).astype(BF16))
                    eal_x.append(_pair_cols(ea_last, ha, hb, 1, lane1))
                st_ref[s, :, g * gw:(g + 1) * gw] = (st_g * jnp.concatenate(eal_x, axis=1)
                                                     + _dot_tn(bm_g, jnp.concatenate(xw, axis=1)))
                ms = sum(jnp.sum(t * t, axis=-1, keepdims=True) for t in yy) * (1.0 / gw)
                inv = lax.rsqrt(ms + NORM_EPS)
                for pr in range(2):
                    lo = g * gw + pr * LANES
                    y_ref[s, r0:r0 + CHUNK, lo:lo + LANES] = (yy[pr] * inv * nw_ref[:, lo:lo + LANES]).astype(BF16)


def _ssd_mixer(x, wz, wxbc, wdt, cw, cb, dtb, aneg, dexp, nw, ex):
    b, s, d = x.shape
    rows = MIX_ROWS
    nseq = MIX_SEQS
    const = lambda shape: pl.BlockSpec(shape, lambda i, j: (0,) * len(shape), pipeline_mode=pl.Buffered(1))
    return pl.pallas_call(
        functools.partial(_ssd_kernel, rows=rows, nseq=nseq),
        grid=(b // nseq, s // rows),
        in_specs=[pl.BlockSpec((nseq, rows, d), lambda i, j: (i, j, 0)),
                  const(wz.shape), const(wxbc.shape), const(wdt.shape), const(cw.shape), const(cb.shape),
                  const(dtb.shape), const(aneg.shape), const(dexp.shape), const(nw.shape), const(ex.shape)],
        out_specs=pl.BlockSpec((nseq, rows, SSM_D_INNER), lambda i, j: (i, j, 0)),
        out_shape=jax.ShapeDtypeStruct((b, s, SSM_D_INNER), BF16),
        scratch_shapes=[pltpu.VMEM((nseq, SSM_XBC // LANES, SUBLANES + rows, LANES), F32),
                        pltpu.VMEM((nseq, SSM_XBC // LANES, rows, LANES), F32),
                        pltpu.VMEM((nseq, SSM_STATE, SSM_D_INNER), F32)],
        compiler_params=pltpu.CompilerParams(dimension_semantics=("arbitrary", "arbitrary"),
                                             vmem_limit_bytes=VMEM_LIMIT),
        name="ssd_mixer",
    )(x, wz, wxbc, wdt, cw, cb, dtb, aneg, dexp, nw, ex)


def _mlstm_kernel(x_ref, wq_ref, wk_ref, wv_ref, wo_ref, wif_ref, bif_ref, nw_ref,
                  y_ref, c_ref, m_ref, *, rows, nseq):
    @pl.when(pl.program_id(1) == 0)
    def _():
        c_ref[...] = jnp.zeros(c_ref.shape, F32)
        m_ref[...] = jnp.full(m_ref.shape, -jnp.inf, F32)

    xb = x_ref[...].reshape(nseq * rows, x_ref.shape[-1]).astype(BF16)
    q = _dot(xb, wq_ref[...])
    k = _dot(xb, wk_ref[...])
    v = _dot(xb, wv_ref[...])
    o = _dot(xb, wo_ref[...])
    gif = _dot(xb, wif_ref[...]) + bif_ref[...]
    causal, tril = _causal_masks()
    lane = lax.broadcasted_iota(jnp.int32, (CHUNK, LANES), 1)
    lane1 = lax.broadcasted_iota(jnp.int32, (1, LANES), 1)
    row = lax.broadcasted_iota(jnp.int32, (LANES, 1), 0)
    head_lane = lane < LSTM_HEADS
    ones = jnp.ones((CHUNK, LANES), BF16)

    for ci in range(rows // CHUNK):
        for s in range(nseq):
            r0 = ci * CHUNK
            q0 = s * rows + r0
            log_i = gif[q0:q0 + CHUNK, 0:LANES]
            f_pre = gif[q0:q0 + CHUNK, LANES:2 * LANES]
            log_f = jnp.where(head_lane, jnp.minimum(f_pre, 0.0) - jnp.log(1.0 + jnp.exp(-jnp.abs(f_pre))), 0.0)
            fcum = _cumsum_rows(tril, log_f)
            f_tot = fcum[CHUNK - 1:CHUNK, :]
            gk = log_i - fcum
            m_loc = jnp.max(f_tot + gk, axis=0, keepdims=True)
            g_t = gk.T
            m_prev = m_ref[s]
            m_new = jnp.maximum(f_tot + m_prev, m_loc)
            s_prev = jnp.exp(f_tot + m_prev - m_new)
            s_loc = jnp.exp(m_loc - m_new)
            m_ref[s] = jnp.where(lane1 < LSTM_HEADS, m_new, -jnp.inf)
            w_shift = f_tot - m_loc

            for pr in range(LSTM_HEADS // 2):
                q_p = q[q0:q0 + CHUNK, pr * LANES:(pr + 1) * LANES]
                k_p = k[q0:q0 + CHUNK, pr * LANES:(pr + 1) * LANES]
                k_pb = k_p.astype(BF16)
                c_pair = c_ref[s, pr]
                c_pair_b = c_pair.astype(BF16)
                c_new = c_pair * jnp.where(row < LSTM_QK,
                                           jnp.broadcast_to(s_prev[:, 2 * pr:2 * pr + 1], (LANES, 1)),
                                           jnp.broadcast_to(s_prev[:, 2 * pr + 1:2 * pr + 2], (LANES, 1)))
                for hh in range(2):
                    h = 2 * pr + hh
                    in_head = (lane < LSTM_QK) if hh == 0 else (lane >= LSTM_QK)
                    v_h = v[q0:q0 + CHUNK, h * LSTM_V:(h + 1) * LSTM_V].astype(BF16)
                    v_ext = jnp.concatenate([v_h, ones], axis=1)
                    q_m = jnp.where(in_head, q_p, 0.0)
                    fcol = jnp.broadcast_to(fcum[:, h:h + 1], (CHUNK, CHUNK))
                    gcol = jnp.broadcast_to(gk[:, h:h + 1], (CHUNK, LANES))
                    rowb = jnp.broadcast_to(g_t[h:h + 1, :], (CHUNK, CHUNK))
                    log_d = jnp.where(causal, fcol + rowb, -jnp.inf)
                    inter_log = fcol + m_prev[:, h:h + 1]
                    m_t = jnp.maximum(inter_log, jnp.max(log_d, axis=-1, keepdims=True))
                    scores = _dot_nt(q_m.astype(BF16), k_pb) * jnp.exp(log_d - m_t)
                    inter_w = jnp.exp(inter_log - m_t)
                    lhs = jnp.concatenate([scores.astype(BF16), (q_m * inter_w).astype(BF16)], axis=1)
                    res = _dot(lhs, jnp.concatenate([v_ext, c_pair_b], axis=0))
                    hv = res[:, 0:LSTM_V] / jnp.maximum(jnp.abs(res[:, LSTM_V:2 * LSTM_V]), jnp.exp(-m_t))
                    ms = jnp.mean(hv * hv, axis=-1, keepdims=True)
                    o_h = o[q0:q0 + CHUNK, h * LSTM_V:(h + 1) * LSTM_V]
                    y_ref[s, r0:r0 + CHUNK, h * LSTM_V:(h + 1) * LSTM_V] = (
                        _sigmoid(o_h) * (hv * lax.rsqrt(ms + NORM_EPS) * nw_ref[:, h * LSTM_V:(h + 1) * LSTM_V])
                    ).astype(BF16)
                    kw = jnp.where(in_head, k_p * jnp.exp(gcol + w_shift[:, h:h + 1]), 0.0)
                    c_new = c_new + s_loc[:, h:h + 1] * _dot_tn(kw.astype(BF16), v_ext)
                c_ref[s, pr] = c_new


def _mlstm_mixer(x, wq, wk, wv, wo, wif, bif, nw):
    b, s, d = x.shape
    rows = MIX_ROWS
    nseq = MIX_SEQS
    const = lambda shape: pl.BlockSpec(shape, lambda i, j: (0,) * len(shape), pipeline_mode=pl.Buffered(1))
    return pl.pallas_call(
        functools.partial(_mlstm_kernel, rows=rows, nseq=nseq),
        grid=(b // nseq, s // rows),
        in_specs=[pl.BlockSpec((nseq, rows, d), lambda i, j: (i, j, 0)),
                  const(wq.shape), const(wk.shape), const(wv.shape), const(wo.shape), const(wif.shape),
                  const(bif.shape), const(nw.shape)],
        out_specs=pl.BlockSpec((nseq, rows, LSTM_D_V), lambda i, j: (i, j, 0)),
        out_shape=jax.ShapeDtypeStruct((b, s, LSTM_D_V), BF16),
        scratch_shapes=[pltpu.VMEM((nseq, LSTM_HEADS // 2, 2 * LSTM_QK, 2 * LSTM_V), F32),
                        pltpu.VMEM((nseq, 1, LANES), F32)],
        compiler_params=pltpu.CompilerParams(dimension_semantics=("arbitrary", "arbitrary"),
                                             vmem_limit_bytes=VMEM_LIMIT),
        name="mlstm_mixer",
    )(x, wq, wk, wv, wo, wif, bif, nw)


def _layer_norm(t, g, b):
    mu = jnp.mean(t, axis=-1, keepdims=True)
    tc = t - mu
    var = jnp.mean(tc * tc, axis=-1, keepdims=True)
    return tc * lax.rsqrt(var + NORM_EPS) * g + b


def _pack_pair(a, b):
    pa = lax.bitcast_convert_type(a.astype(BF16).astype(F32), jnp.uint32)
    pb = lax.bitcast_convert_type(b.astype(BF16).astype(F32), jnp.uint32)
    return pa | (pb >> 16)


def _unpack_pair(w):
    return (lax.bitcast_convert_type(w & jnp.uint32(0xFFFF0000), F32),
            lax.bitcast_convert_type(w << 16, F32))


def _route(logits):
    rows = logits.shape[0]
    lane = lax.broadcasted_iota(jnp.int32, (rows, LANES), 1).astype(F32)
    big = float(LANES)
    neg = -jnp.inf
    gl = jnp.where(lane < MOE_GROUPS, logits, neg)
    gmax = jnp.max(gl, axis=-1, keepdims=True)
    gidx = jnp.min(jnp.where(gl == gmax, lane, big), axis=-1, keepdims=True)
    gsum = jnp.sum(jnp.where(lane < MOE_GROUPS, jnp.exp(logits - gmax), 0.0), axis=-1, keepdims=True)
    grp_p = 1.0 / gsum
    lo = MOE_GROUPS + gidx * MOE_PER_GROUP
    el = jnp.where(lane >= lo, jnp.where(lane < lo + MOE_PER_GROUP, logits, neg), neg)
    m1 = jnp.max(el, axis=-1, keepdims=True)
    i1 = jnp.min(jnp.where(el == m1, lane, big), axis=-1, keepdims=True)
    el2 = jnp.where(lane == i1, neg, el)
    m2 = jnp.max(el2, axis=-1, keepdims=True)
    i2 = jnp.min(jnp.where(el2 == m2, lane, big), axis=-1, keepdims=True)
    e21 = jnp.exp(m2 - m1)
    g1 = grp_p / (1.0 + e21)
    g2 = grp_p * e21 / (1.0 + e21)
    return i1 - MOE_GROUPS, i2 - MOE_GROUPS, g1, g2


def _merge_kernel(x_ref, ys_ref, yl_ref, wgs_ref, wgl_ref, wa_ref, wb_ref, wo_ref, g_ref, b_ref,
                  wr_hi_ref, wr_lo_ref, br_ref, eg_ref, eu_ref, ed_ref,
                  x1_ref, xp_ref, tab_ref, cnt_ref, egb_ref, eub_ref, edb_ref, carry_ref,
                  *, alpha, steps_per_tile, cast_steps):
    @pl.when(pl.program_id(0) % steps_per_tile == 0)
    def _():
        carry_ref[...] = jnp.zeros(carry_ref.shape, F32)

    @pl.when(pl.program_id(0) < cast_steps)
    def _():
        egb_ref[...] = eg_ref[...].astype(BF16)
        eub_ref[...] = eu_ref[...].astype(BF16)
        edb_ref[...] = ed_ref[...].astype(BF16)

    x = x_ref[...]
    rows = x.shape[0]
    xb = x.astype(BF16)
    merged = (_sigmoid(_dot(xb, wgs_ref[...])) * _dot(ys_ref[...], wa_ref[...])
              + _sigmoid(_dot(xb, wgl_ref[...])) * _dot(yl_ref[...], wb_ref[...]))
    x1 = _layer_norm(alpha * x + _dot(merged.astype(BF16), wo_ref[...]), g_ref[...], b_ref[...])
    x1_ref[...] = x1
    half = x1.shape[1] // 2
    xp = _pack_pair(x1[:, :half], x1[:, half:])
    for c in range(PACK_SLABS):
        xp_ref[pl.ds(c, rows, stride=PACK_SLABS), :] = xp[:, c * LANES:(c + 1) * LANES]
    x_hi = x1.astype(BF16)
    x_lo = (x1 - x_hi.astype(F32)).astype(BF16)
    logits = (_dot(x_hi, wr_hi_ref[...]) + _dot(x_lo, wr_hi_ref[...]) + _dot(x_hi, wr_lo_ref[...])
              + br_ref[...])
    e1, e2, g1, g2 = _route(logits)
    lane = lax.broadcasted_iota(jnp.int32, (rows, LANES), 1)
    lane_f = lane.astype(F32)
    oh1 = lane_f == e1
    oh2 = lane_f == e2
    ohs = jnp.where(oh1, 1.0, jnp.where(oh2, 1.0, 0.0))
    r_i = lax.broadcasted_iota(jnp.int32, (rows, rows), 0)
    c_i = lax.broadcasted_iota(jnp.int32, (rows, rows), 1)
    before = jnp.where(r_i > c_i, 1.0, 0.0).astype(BF16)
    seen = _dot(before, ohs.astype(BF16)) + carry_ref[...]
    r1 = jnp.sum(jnp.where(oh1, seen, 0.0), axis=-1, keepdims=True)
    r2 = jnp.sum(jnp.where(oh2, seen, 0.0), axis=-1, keepdims=True)
    carry = carry_ref[...] + jnp.sum(ohs, axis=0, keepdims=True)
    carry_ref[...] = carry
    cnt_ref[...] = jnp.broadcast_to(carry, cnt_ref.shape).astype(jnp.int32)
    as_int = lambda v: v.astype(jnp.int32)
    as_bits = lambda v: lax.bitcast_convert_type(v, jnp.int32)
    tab = jnp.where(lane == 0, as_int(e1), jnp.where(lane == 1, as_int(e2), jnp.where(
        lane == 2, as_int(r1) * PACK_SLABS, jnp.where(lane == 3, as_int(r2) * PACK_SLABS, jnp.where(
            lane == 4, as_bits(g1), jnp.where(lane == 5, as_bits(g2), 0))))))
    tab_ref[...] = tab.T[0:SUBLANES, :]


def _merge(x2d, ys, yl, wgs, wgl, wa, wb, wo, g, b, wr_hi, wr_lo, br, eg, eu, ed, alpha, tile):
    t, d = x2d.shape
    rows = DENSE_ROWS
    spt = tile // rows
    n_steps = t // rows
    per_step = -(-MOE_EXPERTS // n_steps)
    cast_steps = MOE_EXPERTS // per_step
    assert per_step * cast_steps == MOE_EXPERTS and cast_steps <= n_steps
    const = lambda shape: pl.BlockSpec(shape, lambda i: (0,) * len(shape), pipeline_mode=pl.Buffered(1))
    rowblk = lambda w: pl.BlockSpec((rows, w), lambda i: (i, 0))
    expert = lambda w: pl.BlockSpec((per_step,) + w.shape[1:], lambda i: (jnp.minimum(i, cast_steps - 1), 0, 0))
    return pl.pallas_call(
        functools.partial(_merge_kernel, alpha=alpha, steps_per_tile=spt, cast_steps=cast_steps),
        grid=(n_steps,),
        in_specs=[rowblk(d), rowblk(SSM_D_INNER), rowblk(LSTM_D_V),
                  const(wgs.shape), const(wgl.shape), const(wa.shape), const(wb.shape), const(wo.shape),
                  const(g.shape), const(b.shape), const(wr_hi.shape), const(wr_lo.shape), const(br.shape),
                  expert(eg), expert(eu), expert(ed)],
        out_specs=[rowblk(d),
                   pl.BlockSpec((rows * PACK_SLABS, LANES), lambda i: (i, 0)),
                   pl.BlockSpec((SUBLANES, rows), lambda i: (i // spt, i % spt)),
                   pl.BlockSpec((SUBLANES, LANES), lambda i: (i // spt, 0)),
                   expert(eg), expert(eu), expert(ed)],
        out_shape=[jax.ShapeDtypeStruct((t, d), F32),
                   jax.ShapeDtypeStruct((t * PACK_SLABS, LANES), jnp.uint32),
                   jax.ShapeDtypeStruct((t // tile * SUBLANES, tile), jnp.int32),
                   jax.ShapeDtypeStruct((t // tile * SUBLANES, LANES), jnp.int32),
                   jax.ShapeDtypeStruct(eg.shape, BF16), jax.ShapeDtypeStruct(eu.shape, BF16),
                   jax.ShapeDtypeStruct(ed.shape, BF16)],
        scratch_shapes=[pltpu.VMEM((1, LANES), F32)],
        compiler_params=pltpu.CompilerParams(dimension_semantics=("arbitrary",), vmem_limit_bytes=VMEM_LIMIT),
        name="merge_ln_route",
    )(x2d, ys, yl, wgs, wgl, wa, wb, wo, g, b, wr_hi, wr_lo, br, eg, eu, ed)


def _segment_rows(count):
    return lax.shift_right_logical(count + (SUBLANES - 1), 3) * (SUBLANES * PACK_SLABS)


def _route_rows_kernel(cnt_ref, tab_ref, out_ref):
    i = pl.program_id(0)
    tab = tab_ref[...]
    starts = jnp.zeros(tab.shape, jnp.int32)
    off = jnp.int32(0)
    for e in range(MOE_EXPERTS):
        starts = jnp.where(tab == e, off, starts)
        off = off + _segment_rows(cnt_ref[i * MOE_EXPERTS + e])
    row = lax.broadcasted_iota(jnp.int32, tab.shape, 0)
    out_ref[...] = jnp.where(lax.shift_right_logical(row, 1) == 1, tab + pltpu.roll(starts, 2, axis=0), tab)


def _route_rows(tab, cnt, tile):
    n_tiles = tab.shape[0] // SUBLANES
    blk = pl.BlockSpec((SUBLANES, tile), lambda i, c: (i, 0))
    return pl.pallas_call(
        _route_rows_kernel,
        grid_spec=pltpu.PrefetchScalarGridSpec(num_scalar_prefetch=1, grid=(n_tiles,), in_specs=[blk],
                                               out_specs=blk),
        out_shape=jax.ShapeDtypeStruct(tab.shape, jnp.int32),
        compiler_params=pltpu.CompilerParams(dimension_semantics=("arbitrary",), vmem_limit_bytes=VMEM_LIMIT),
        name="route_rows",
    )(cnt, tab)


def _moe_kernel(cnt_ref, tab_ref, xp_hbm, wg_hbm, wu_hbm, wd_hbm, out_hbm, xys, ybuf, off_ref,
                wg_buf, wu_buf, wd_buf, xin, xout, wsem, isem, osem, *, tile, chunk, n_steps):
    i = pl.program_id(0)
    j = pl.program_id(1)
    group = SUBLANES * PACK_SLABS
    half = PACK_SLABS * LANES
    step = i * MOE_EXPERTS + j
    slot = step % MOE_WEIGHT_BUFFERS
    weights = ((wg_hbm, wg_buf), (wu_hbm, wu_buf), (wd_hbm, wd_buf))

    def weight_copies(expert, into):
        copies = []
        for k, (hbm, buf) in enumerate(weights):
            part = buf.shape[1] // MOE_WEIGHT_SPLIT
            for q in range(MOE_WEIGHT_SPLIT):
                copies.append(pltpu.make_async_copy(hbm.at[expert, pl.ds(q * part, part)],
                                                    buf.at[into, pl.ds(q * part, part)], wsem.at[k, into]))
        return copies

    @pl.when(step == 0)
    def _():
        for ahead in range(MOE_WEIGHT_BUFFERS - 1):
            for cp in weight_copies(ahead % MOE_EXPERTS, ahead):
                cp.start()

    nxt = step + MOE_WEIGHT_BUFFERS - 1

    @pl.when(nxt < n_steps)
    def _():
        for cp in weight_copies(nxt % MOE_EXPERTS, nxt % MOE_WEIGHT_BUFFERS):
            cp.start()

    n_chunks = tile // chunk
    rolled = lambda trips: trips + jnp.minimum(j, 0)

    def in_buf(c):
        return pl.multiple_of((c % 2) * (chunk * PACK_SLABS), group)

    def out_buf(c):
        return pl.multiple_of((c % 2) * (chunk * OUT_SLABS), SUBLANES)

    def in_copy(c):
        row0 = pl.multiple_of((i * tile + c * chunk) * PACK_SLABS, group)
        return pltpu.make_async_copy(xp_hbm.at[pl.ds(row0, chunk * PACK_SLABS)],
                                     xin.at[pl.ds(in_buf(c), chunk * PACK_SLABS)], isem.at[c % 2])

    def out_copy(c):
        row0 = pl.multiple_of((i * tile + c * chunk) * OUT_SLABS, SUBLANES)
        return pltpu.make_async_copy(xout.at[pl.ds(out_buf(c), chunk * OUT_SLABS)],
                                     out_hbm.at[pl.ds(row0, chunk * OUT_SLABS)], osem.at[c % 2])

    @pl.when(j == 0)
    def _():
        in_copy(0).start()

        def offsets(e, acc):
            padded = _segment_rows(cnt_ref[i * MOE_EXPERTS + e])
            off_ref[e] = acc
            end = acc + padded

            @pl.when(padded > 0)
            def _():
                xys[pl.ds(pl.multiple_of(end - group, group), group), :] = jnp.zeros((group, LANES), jnp.uint32)
            return end

        total = lax.fori_loop(0, MOE_EXPERTS, offsets, 0)
        off_ref[MOE_EXPERTS] = total
        tail = MOE_BLOCKS[-1] * PACK_SLABS
        xys[pl.ds(pl.multiple_of(total, group), tail), :] = jnp.zeros((tail, LANES), jnp.uint32)

        def distribute_chunk(c, carry):
            @pl.when(c + 1 < n_chunks)
            def _():
                in_copy(c + 1).start()
            in_copy(c).wait()

            g0 = c * (chunk // SUBLANES)
            shift = in_buf(c) - c * (chunk * PACK_SLABS)

            def distribute(tg, carry2):
                for u in range(SUBLANES):
                    t = tg * SUBLANES + u
                    d1 = pl.multiple_of(tab_ref[2 * tile + t], PACK_SLABS)
                    d2 = pl.multiple_of(tab_ref[3 * tile + t], PACK_SLABS)
                    row = xin[pl.ds(pl.multiple_of(shift + t * PACK_SLABS, PACK_SLABS), PACK_SLABS), :]
                    xys[pl.ds(d1, PACK_SLABS), :] = row
                    xys[pl.ds(d2, PACK_SLABS), :] = row
                return carry2

            lax.fori_loop(g0, g0 + rolled(chunk // SUBLANES), distribute, 0)
            return carry

        lax.fori_loop(0, rolled(n_chunks), distribute_chunk, 0)

    for cp in weight_copies(j, slot):
        cp.wait()
    wg_ref, wu_ref, wd_ref = wg_buf.at[slot], wu_buf.at[slot], wd_buf.at[slot]
    start = off_ref[j]
    n = off_ref[j + 1] - start

    def expert_block(base, rows, valid):
        words = [xys[pl.ds(base + c, rows, stride=PACK_SLABS), :] for c in range(PACK_SLABS)]
        parts = [_unpack_pair(w) for w in words]
        x_hi = jnp.concatenate([p[0].astype(BF16) for p in parts], axis=1)
        x_lo = jnp.concatenate([p[1].astype(BF16) for p in parts], axis=1)
        hg = _dot(x_hi, wg_ref[0:half, :]) + _dot(x_lo, wg_ref[half:2 * half, :])
        hu = _dot(x_hi, wu_ref[0:half, :]) + _dot(x_lo, wu_ref[half:2 * half, :])
        y = _dot((hg * _sigmoid(hg) * hu).astype(BF16), wd_ref[...])
        for c in range(PACK_SLABS):
            ybuf[pl.ds(c, rows, stride=PACK_SLABS), :] = _pack_pair(
                y[:, c * LANES:(c + 1) * LANES], y[:, half + c * LANES:half + (c + 1) * LANES])

        def copy(g, c):
            r = pl.multiple_of(g * group, group)
            xys[pl.ds(base + r, group), :] = ybuf[pl.ds(r, group), :]
            return c

        lax.fori_loop(0, valid // group, copy, 0)

    lo = 0
    for rows in MOE_BLOCKS:
        hi = rows * PACK_SLABS

        @pl.when(jnp.logical_and(n > lo, n <= hi))
        def _(rows=rows):
            expert_block(pl.multiple_of(start, group), rows, n)
        lo = hi

    @pl.when(n > lo)
    def _():
        step = MOE_BLOCKS[0] * PACK_SLABS

        def block(bi, carry):
            expert_block(pl.multiple_of(start + bi * step, group), MOE_BLOCKS[0],
                         jnp.minimum(step, n - bi * step))
            return carry

        lax.fori_loop(0, (n + step - 1) // step, block, 0)

    @pl.when(j == MOE_EXPERTS - 1)
    def _():
        def combine_chunk(c, carry):
            @pl.when(c >= 2)
            def _():
                out_copy(c - 2).wait()

            g0 = c * (chunk // SUBLANES)
            shift = out_buf(c) - c * (chunk * OUT_SLABS)

            def combine(tg, carry2):
                for u in range(SUBLANES):
                    t = tg * SUBLANES + u
                    g1 = lax.bitcast_convert_type(tab_ref[4 * tile + t], F32)
                    g2 = lax.bitcast_convert_type(tab_ref[5 * tile + t], F32)
                    a_hi, a_lo = _unpack_pair(
                        xys[pl.ds(pl.multiple_of(tab_ref[2 * tile + t], PACK_SLABS), PACK_SLABS), :])
                    b_hi, b_lo = _unpack_pair(
                        xys[pl.ds(pl.multiple_of(tab_ref[3 * tile + t], PACK_SLABS), PACK_SLABS), :])
                    xout[pl.ds(pl.multiple_of(shift + t * OUT_SLABS, OUT_SLABS), OUT_SLABS), :] = jnp.concatenate(
                        [g1 * a_hi + g2 * b_hi, g1 * a_lo + g2 * b_lo], axis=0)
                return carry2

            lax.fori_loop(g0, g0 + rolled(chunk // SUBLANES), combine, 0)
            out_copy(c).start()
            return carry

        lax.fori_loop(0, rolled(n_chunks), combine_chunk, 0)
        for c in range(max(0, n_chunks - 2), n_chunks):
            out_copy(c).wait()


def _moe(xp, tab, cnt, wg, wu, wd, tile):
    t = xp.shape[0] // PACK_SLABS
    d = D_MODEL
    n_tiles = t // tile
    chunk = min(MOE_CHUNK, tile)
    seg_rows = 2 * tile + MOE_EXPERTS * SUBLANES + MOE_BLOCKS[-1]
    grid_spec = pltpu.PrefetchScalarGridSpec(
        num_scalar_prefetch=1,
        grid=(n_tiles, MOE_EXPERTS),
        in_specs=[pl.BlockSpec((SUBLANES * tile,), lambda i, j, c: (i,), memory_space=pltpu.SMEM,
                               pipeline_mode=pl.Buffered(1)),
                  pl.BlockSpec(memory_space=pl.ANY), pl.BlockSpec(memory_space=pl.ANY),
                  pl.BlockSpec(memory_space=pl.ANY), pl.BlockSpec(memory_space=pl.ANY)],
        out_specs=pl.BlockSpec(memory_space=pl.ANY),
        scratch_shapes=[pltpu.VMEM((seg_rows * PACK_SLABS, LANES), jnp.uint32),
                        pltpu.VMEM((MOE_BLOCKS[-1] * PACK_SLABS, LANES), jnp.uint32),
                        pltpu.SMEM((LANES,), jnp.int32),
                        pltpu.VMEM((MOE_WEIGHT_BUFFERS, d, MOE_D_FF), BF16),
                        pltpu.VMEM((MOE_WEIGHT_BUFFERS, d, MOE_D_FF), BF16),
                        pltpu.VMEM((MOE_WEIGHT_BUFFERS, MOE_D_FF, d), BF16),
                        pltpu.VMEM((2 * chunk * PACK_SLABS, LANES), jnp.uint32),
                        pltpu.VMEM((2 * chunk * OUT_SLABS, LANES), F32),
                        pltpu.SemaphoreType.DMA((3, MOE_WEIGHT_BUFFERS)),
                        pltpu.SemaphoreType.DMA((2,)), pltpu.SemaphoreType.DMA((2,))],
    )
    return pl.pallas_call(
        functools.partial(_moe_kernel, tile=tile, chunk=chunk, n_steps=n_tiles * MOE_EXPERTS),
        grid_spec=grid_spec,
        out_shape=jax.ShapeDtypeStruct((t * OUT_SLABS, LANES), F32),
        compiler_params=pltpu.CompilerParams(dimension_semantics=("arbitrary", "arbitrary"),
                                             vmem_limit_bytes=MOE_VMEM_LIMIT),
        name="moe_experts",
    )(cnt, tab, xp, wg, wu, wd)


def _final_kernel(x1_ref, moe_ref, p_ref, g_ref, b_ref, wpg_ref, wpp_ref, out_ref, *, alpha):
    rows = x1_ref.shape[0]
    moe = jnp.concatenate([moe_ref[pl.ds(c, rows, stride=OUT_SLABS), :] for c in range(OUT_SLABS)], axis=1)
    x2 = _layer_norm(alpha * x1_ref[...] + moe, g_ref[...], b_ref[...])
    gate = _sigmoid(_dot(x2.astype(BF16), wpg_ref[...]))
    out_ref[...] = x2 + gate * _dot(p_ref[...].astype(BF16), wpp_ref[...])


def _final(x1, moe, p2d, g, b, wpg, wpp, alpha):
    t, d = x1.shape
    rows = DENSE_ROWS
    const = lambda shape: pl.BlockSpec(shape, lambda i: (0,) * len(shape))
    rowblk = lambda w: pl.BlockSpec((rows, w), lambda i: (i, 0))
    return pl.pallas_call(
        functools.partial(_final_kernel, alpha=alpha),
        grid=(t // rows,),
        in_specs=[rowblk(d), pl.BlockSpec((rows * OUT_SLABS, LANES), lambda i: (i, 0)), rowblk(p2d.shape[1]),
                  const(g.shape), const(b.shape), const(wpg.shape), const(wpp.shape)],
        out_specs=rowblk(d),
        out_shape=jax.ShapeDtypeStruct((t, d), F32),
        compiler_params=pltpu.CompilerParams(dimension_semantics=("arbitrary",), vmem_limit_bytes=VMEM_LIMIT),
        name="final_ln_ple",
    )(x1, moe, p2d, g, b, wpg, wpp)


def _pad_lanes(w, width=LANES):
    return jnp.pad(w, ((0, 0), (0, width - w.shape[1])))


def _row(v, width=None):
    v = v.astype(F32).reshape(1, -1)
    return v if width is None else _pad_lanes(v, width)


def kernel(x, p, w_in, ssm_conv_w, ssm_conv_b, ssm_dt_bias, ssm_a_log, ssm_d, ssm_norm_w, lstm_i_bias, lstm_f_bias, lstm_norm_w, w_branch_ssm, w_branch_lstm, w_out, ln1_g, ln1_b, moe_w_group, moe_b_group, moe_w_expert, moe_b_expert, moe_w_gate, moe_w_up, moe_w_down, ln2_g, ln2_b, ple_w_proj, ple_w_gate):
    depth = w_in.shape[0]
    bsz, seq, d = x.shape
    t = bsz * seq
    alpha = (2.0 * depth) ** 0.25
    head_expand = (jnp.arange(LANES, dtype=jnp.int32)[:, None]
                   == jnp.arange(SSM_D_INNER, dtype=jnp.int32)[None, :] // SSM_HEAD_DIM).astype(BF16)
    for i in range(depth):
        splits = []
        c0 = 0
        for sz in IN_PROJ_SIZES:
            splits.append(w_in[i][:, c0:c0 + sz])
            c0 += sz
        w_z, w_xbc, w_dt, w_q, w_k, w_v, w_o, w_i, w_f, w_gs, w_gl = splits
        bf = lambda w: w.astype(BF16)
        y_ssm = _ssd_mixer(
            x, bf(w_z), bf(w_xbc), bf(_pad_lanes(w_dt)),
            ssm_conv_w[i].astype(F32), _row(ssm_conv_b[i]), _row(ssm_dt_bias[i], LANES),
            _row(-jnp.exp(ssm_a_log[i].astype(F32)), LANES),
            _row(jnp.repeat(ssm_d[i].astype(F32), SSM_HEAD_DIM)), _row(ssm_norm_w[i]), head_expand)
        w_if = jnp.concatenate([_pad_lanes(w_i), _pad_lanes(w_f)], axis=1)
        b_if = jnp.concatenate([_row(lstm_i_bias[i], LANES), _row(lstm_f_bias[i], LANES)], axis=1)
        y_lstm = _mlstm_mixer(x, bf(w_q * (LSTM_QK ** -0.5)), bf(w_k), bf(w_v), bf(w_o), bf(w_if), b_if,
                              _row(lstm_norm_w[i]))
        w_r = _pad_lanes(jnp.concatenate([moe_w_group[i], moe_w_expert[i]], axis=1).astype(F32))
        w_r_hi = w_r.astype(BF16)
        w_r_lo = (w_r - w_r_hi.astype(F32)).astype(BF16)
        b_r = _row(jnp.concatenate([moe_b_group[i], moe_b_expert[i]]), LANES)
        tile = min(MOE_TILE, t)
        x1, xp, tab, cnt, e_gate, e_up, e_down = _merge(
            x.reshape(t, d), y_ssm.reshape(t, -1), y_lstm.reshape(t, -1),
            bf(w_gs), bf(w_gl), bf(w_branch_ssm[i]), bf(w_branch_lstm[i]), bf(w_out[i]),
            _row(ln1_g[i]), _row(ln1_b[i]), w_r_hi, w_r_lo, b_r,
            moe_w_gate[i].astype(F32), moe_w_up[i].astype(F32), moe_w_down[i].astype(F32), alpha, tile)
        cnt = cnt.reshape(t // tile, SUBLANES, LANES)[:, 0, :MOE_EXPERTS].reshape(-1)
        moe = _moe(xp, _route_rows(tab, cnt, tile).reshape(-1), cnt, e_gate, e_up, e_down, tile)
        x = _final(x1, moe, p[i].reshape(t, -1), _row(ln2_g[i]), _row(ln2_b[i]),
                   bf(ple_w_gate[i]), bf(ple_w_proj[i]), alpha).reshape(bsz, seq, d)
    return x
```

```python
import functools

import jax
import jax.numpy as jnp
from jax import lax
from jax.experimental import pallas as pl
from jax.experimental.pallas import tpu as pltpu

F32 = jnp.float32
BF16 = jnp.bfloat16

D_MODEL = 1024
PLE_DIM = 256
SSM_D_INNER = 1024
SSM_HEAD_DIM = 64
SSM_HEADS = 16
SSM_GROUPS = 4
SSM_STATE = 128
SSM_CONV = 4
SSM_XBC = SSM_D_INNER + 2 * SSM_GROUPS * SSM_STATE
LSTM_HEADS = 8
LSTM_QK = 64
LSTM_V = 128
LSTM_D_QK = LSTM_HEADS * LSTM_QK
LSTM_D_V = LSTM_HEADS * LSTM_V
CHUNK = 128
MOE_GROUPS = 8
MOE_PER_GROUP = 8
MOE_EXPERTS = 64
MOE_D_FF = 512
NORM_EPS = 1e-5
IN_PROJ_SIZES = (SSM_D_INNER, SSM_XBC, SSM_HEADS, LSTM_D_QK, LSTM_D_QK, LSTM_D_V, LSTM_D_V,
                 LSTM_HEADS, LSTM_HEADS, D_MODEL, D_MODEL)

LANES = 128
SUBLANES = 8
VMEM_LIMIT = 56 * 1024 * 1024
MOE_VMEM_LIMIT = 60 * 1024 * 1024

MIX_ROWS = 256
MIX_SEQS = 2
DENSE_ROWS = 512
MOE_TILE = 8192
MOE_CHUNK = 256
MOE_BLOCKS = (256, 320, 384)
MOE_WEIGHT_BUFFERS = 4
MOE_WEIGHT_SPLIT = 4
PACK_SLABS = D_MODEL // 2 // LANES
OUT_SLABS = D_MODEL // LANES


def _dot(a, b):
    return jnp.dot(a, b, preferred_element_type=F32)


def _dot_nt(a, b):
    return lax.dot_general(a, b, (((1,), (1,)), ((), ())), preferred_element_type=F32)


def _dot_tn(a, b):
    return lax.dot_general(a, b, (((0,), (0,)), ((), ())), preferred_element_type=F32)


def _sigmoid(x):
    return 1.0 / (1.0 + jnp.exp(-x))


def _softplus(x):
    return jnp.maximum(x, 0.0) + jnp.log(1.0 + jnp.exp(-jnp.abs(x)))


def _split3(x):
    hi = x.astype(BF16)
    r1 = x - hi.astype(F32)
    mid = r1.astype(BF16)
    lo = (r1 - mid.astype(F32)).astype(BF16)
    return hi, mid, lo


def _cumsum_rows(tril, x):
    hi, mid, lo = _split3(x)
    return _dot(tril, hi) + _dot(tril, mid) + _dot(tril, lo)


def _causal_masks():
    r = lax.broadcasted_iota(jnp.int32, (CHUNK, CHUNK), 0)
    c = lax.broadcasted_iota(jnp.int32, (CHUNK, CHUNK), 1)
    causal = r >= c
    return causal, jnp.where(causal, 1.0, 0.0).astype(BF16)


def _pair_cols(v, ha, hb, rows, lane):
    a = jnp.broadcast_to(v[:, ha:ha + 1], (rows, LANES))
    b = jnp.broadcast_to(v[:, hb:hb + 1], (rows, LANES))
    return jnp.where(lane < SSM_HEAD_DIM, a, b)


def _ssd_kernel(x_ref, wz_ref, wxbc_ref, wdt_ref, cw_ref, cb_ref, dtb_ref, aneg_ref, dexp_ref, nw_ref, ex_ref,
                y_ref, ext_ref, u_ref, st_ref, *, rows, nseq):
    n_slab = SSM_XBC // LANES
    tail = SUBLANES - SSM_CONV + 1

    @pl.when(pl.program_id(1) == 0)
    def _():
        ext_ref[:, :, 0:SUBLANES, :] = jnp.zeros((nseq, n_slab, SUBLANES, LANES), F32)
        st_ref[...] = jnp.zeros(st_ref.shape, F32)

    xb = x_ref[...].reshape(nseq * rows, x_ref.shape[-1]).astype(BF16)
    z = _dot(xb, wz_ref[...])
    dtr = _dot(xb, wdt_ref[...])
    for c2 in range(n_slab // 2):
        xbc = _dot(xb, wxbc_ref[:, c2 * 2 * LANES:(c2 + 1) * 2 * LANES])
        for s in range(nseq):
            for cc in range(2):
                ext_ref[s, 2 * c2 + cc, SUBLANES:SUBLANES + rows, :] = (
                    xbc[s * rows:(s + 1) * rows, cc * LANES:(cc + 1) * LANES])
    for s in range(nseq):
        for c in range(n_slab):
            conv = cb_ref[:, c * LANES:(c + 1) * LANES]
            for k in range(SSM_CONV):
                conv = conv + cw_ref[k:k + 1, c * LANES:(c + 1) * LANES] * ext_ref[s, c, pl.ds(tail + k, rows), :]
            u_ref[s, c] = conv * _sigmoid(conv)
            ext_ref[s, c, 0:SUBLANES, :] = ext_ref[s, c, rows:rows + SUBLANES, :]

    dt = _softplus(dtr + dtb_ref[...])
    da = dt * aneg_ref[...]
    causal, tril = _causal_masks()
    lane = lax.broadcasted_iota(jnp.int32, (CHUNK, LANES), 1)
    lane1 = lax.broadcasted_iota(jnp.int32, (1, LANES), 1)
    gw = SSM_D_INNER // SSM_GROUPS
    b_slab = SSM_D_INNER // LANES
    c_slab = b_slab + SSM_GROUPS

    for ci in range(rows // CHUNK):
        for s in range(nseq):
            r0 = ci * CHUNK
            q0 = s * rows + r0
            dt_c = dt[q0:q0 + CHUNK]
            a_cs = _cumsum_rows(tril, da[q0:q0 + CHUNK])
            a_cs_t = a_cs.T
            dt_t = dt_c.T
            a_last = a_cs[CHUNK - 1:CHUNK, :]
            ea_last = jnp.exp(a_last)
            ea_x = _dot(jnp.exp(a_cs).astype(BF16), ex_ref[...])
            sw_x = _dot((dt_c * jnp.exp(a_last - a_cs)).astype(BF16), ex_ref[...])
            for g in range(SSM_GROUPS):
                bm_g = u_ref[s, b_slab + g, r0:r0 + CHUNK, :].astype(BF16)
                cm_g = u_ref[s, c_slab + g, r0:r0 + CHUNK, :].astype(BF16)
                cb = _dot_nt(cm_g, bm_g)
                st_g = st_ref[s, :, g * gw:(g + 1) * gw]
                y_off = _dot(cm_g, st_g.astype(BF16))
                yy, xw, eal_x = [], [], []
                for pr in range(2):
                    ha = 4 * g + 2 * pr
                    hb = ha + 1
                    lo = g * gw + pr * LANES
                    xs_p = u_ref[s, 2 * g + pr, r0:r0 + CHUNK, :]
                    xs_pb = xs_p.astype(BF16)
                    rhs = jnp.concatenate([jnp.where(lane < SSM_HEAD_DIM, xs_pb, 0).astype(BF16),
                                           jnp.where(lane >= SSM_HEAD_DIM, xs_pb, 0).astype(BF16)], axis=0)
                    gs = []
                    for h in (ha, hb):
                        colb = jnp.broadcast_to(a_cs[:, h:h + 1], (CHUNK, CHUNK))
                        rowb = jnp.broadcast_to(a_cs_t[h:h + 1, :], (CHUNK, CHUNK))
                        dec = jnp.exp(jnp.where(causal, colb - rowb, -jnp.inf))
                        gs.append((cb * dec * jnp.broadcast_to(dt_t[h:h + 1, :], (CHUNK, CHUNK))).astype(BF16))
                    y_p = (_dot(jnp.concatenate(gs, axis=1), rhs)
                           + y_off[:, pr * LANES:(pr + 1) * LANES] * ea_x[:, lo:lo + LANES]
                           + dexp_ref[:, lo:lo + LANES] * xs_p)
                    zz = z[q0:q0 + CHUNK, lo:lo + LANES]
                    yy.append(y_p * (zz * _sigmoid(zz)))
                    xw.append((xs_p * sw_x[:, lo:lo + LANES]).astype(BF16))
                    eal_x.append(_pair_cols(ea_last, ha, hb, 1, lane1))
                st_ref[s, :, g * gw:(g + 1) * gw] = (st_g * jnp.concatenate(eal_x, axis=1)
                                                     + _dot_tn(bm_g, jnp.concatenate(xw, axis=1)))
                ms = sum(jnp.sum(t * t, axis=-1, keepdims=True) for t in yy) * (1.0 / gw)
                inv = lax.rsqrt(ms + NORM_EPS)
                for pr in range(2):
                    lo = g * gw + pr * LANES
                    y_ref[s, r0:r0 + CHUNK, lo:lo + LANES] = (yy[pr] * inv * nw_ref[:, lo:lo + LANES]).astype(BF16)


def _ssd_mixer(x, wz, wxbc, wdt, cw, cb, dtb, aneg, dexp, nw, ex):
    b, s, d = x.shape
    rows = MIX_ROWS
    nseq = MIX_SEQS
    const = lambda shape: pl.BlockSpec(shape, lambda i, j: (0,) * len(shape), pipeline_mode=pl.Buffered(1))
    return pl.pallas_call(
        functools.partial(_ssd_kernel, rows=rows, nseq=nseq),
        grid=(b // nseq, s // rows),
        in_specs=[pl.BlockSpec((nseq, rows, d), lambda i, j: (i, j, 0)),
                  const(wz.shape), const(wxbc.shape), const(wdt.shape), const(cw.shape), const(cb.shape),
                  const(dtb.shape), const(aneg.shape), const(dexp.shape), const(nw.shape), const(ex.shape)],
        out_specs=pl.BlockSpec((nseq, rows, SSM_D_INNER), lambda i, j: (i, j, 0)),
        out_shape=jax.ShapeDtypeStruct((b, s, SSM_D_INNER), BF16),
        scratch_shapes=[pltpu.VMEM((nseq, SSM_XBC // LANES, SUBLANES + rows, LANES), F32),
                        pltpu.VMEM((nseq, SSM_XBC // LANES, rows, LANES), F32),
                        pltpu.VMEM((nseq, SSM_STATE, SSM_D_INNER), F32)],
        compiler_params=pltpu.CompilerParams(dimension_semantics=("arbitrary", "arbitrary"),
                                             vmem_limit_bytes=VMEM_LIMIT),
        name="ssd_mixer",
    )(x, wz, wxbc, wdt, cw, cb, dtb, aneg, dexp, nw, ex)


def _mlstm_kernel(x_ref, wq_ref, wk_ref, wv_ref, wo_ref, wif_ref, bif_ref, nw_ref,
                  y_ref, c_ref, m_ref, *, rows, nseq):
    @pl.when(pl.program_id(1) == 0)
    def _():
        c_ref[...] = jnp.zeros(c_ref.shape, F32)
        m_ref[...] = jnp.full(m_ref.shape, -jnp.inf, F32)

    xb = x_ref[...].reshape(nseq * rows, x_ref.shape[-1]).astype(BF16)
    q = _dot(xb, wq_ref[...])
    k = _dot(xb, wk_ref[...])
    v = _dot(xb, wv_ref[...])
    o = _dot(xb, wo_ref[...])
    gif = _dot(xb, wif_ref[...]) + bif_ref[...]
    causal, tril = _causal_masks()
    lane = lax.broadcasted_iota(jnp.int32, (CHUNK, LANES), 1)
    lane1 = lax.broadcasted_iota(jnp.int32, (1, LANES), 1)
    row = lax.broadcasted_iota(jnp.int32, (LANES, 1), 0)
    head_lane = lane < LSTM_HEADS
    ones = jnp.ones((CHUNK, LANES), BF16)

    for ci in range(rows // CHUNK):
        for s in range(nseq):
            r0 = ci * CHUNK
            q0 = s * rows + r0
            log_i = gif[q0:q0 + CHUNK, 0:LANES]
            f_pre = gif[q0:q0 + CHUNK, LANES:2 * LANES]
            log_f = jnp.where(head_lane, jnp.minimum(f_pre, 0.0) - jnp.log(1.0 + jnp.exp(-jnp.abs(f_pre))), 0.0)
            fcum = _cumsum_rows(tril, log_f)
            f_tot = fcum[CHUNK - 1:CHUNK, :]
            gk = log_i - fcum
            m_loc = jnp.max(f_tot + gk, axis=0, keepdims=True)
            g_t = gk.T
            m_prev = m_ref[s]
            m_new = jnp.maximum(f_tot + m_prev, m_loc)
            s_prev = jnp.exp(f_tot + m_prev - m_new)
            s_loc = jnp.exp(m_loc - m_new)
            m_ref[s] = jnp.where(lane1 < LSTM_HEADS, m_new, -jnp.inf)
            w_shift = f_tot - m_loc

            for pr in range(LSTM_HEADS // 2):
                q_p = q[q0:q0 + CHUNK, pr * LANES:(pr + 1) * LANES]
                k_p = k[q0:q0 + CHUNK, pr * LANES:(pr + 1) * LANES]
                k_pb = k_p.astype(BF16)
                c_pair = c_ref[s, pr]
                c_pair_b = c_pair.astype(BF16)
                c_new = c_pair * jnp.where(row < LSTM_QK,
                                           jnp.broadcast_to(s_prev[:, 2 * pr:2 * pr + 1], (LANES, 1)),
                                           jnp.broadcast_to(s_prev[:, 2 * pr + 1:2 * pr + 2], (LANES, 1)))
                for hh in range(2):
                    h = 2 * pr + hh
                    in_head = (lane < LSTM_QK) if hh == 0 else (lane >= LSTM_QK)
                    v_h = v[q0:q0 + CHUNK, h * LSTM_V:(h + 1) * LSTM_V].astype(BF16)
                    v_ext = jnp.concatenate([v_h, ones], axis=1)
                    q_m = jnp.where(in_head, q_p, 0.0)
                    fcol = jnp.broadcast_to(fcum[:, h:h + 1], (CHUNK, CHUNK))
                    gcol = jnp.broadcast_to(gk[:, h:h + 1], (CHUNK, LANES))
                    rowb = jnp.broadcast_to(g_t[h:h + 1, :], (CHUNK, CHUNK))
                    log_d = jnp.where(causal, fcol + rowb, -jnp.inf)
                    inter_log = fcol + m_prev[:, h:h + 1]
                    m_t = jnp.maximum(inter_log, jnp.max(log_d, axis=-1, keepdims=True))
                    scores = _dot_nt(q_m.astype(BF16), k_pb) * jnp.exp(log_d - m_t)
                    inter_w = jnp.exp(inter_log - m_t)
                    lhs = jnp.concatenate([scores.astype(BF16), (q_m * inter_w).astype(BF16)], axis=1)
                    res = _dot(lhs, jnp.concatenate([v_ext, c_pair_b], axis=0))
                    hv = res[:, 0:LSTM_V] / jnp.maximum(jnp.abs(res[:, LSTM_V:2 * LSTM_V]), jnp.exp(-m_t))
                    ms = jnp.mean(hv * hv, axis=-1, keepdims=True)
                    o_h = o[q0:q0 + CHUNK, h * LSTM_V:(h + 1) * LSTM_V]
                    y_ref[s, r0:r0 + CHUNK, h * LSTM_V:(h + 1) * LSTM_V] = (
                        _sigmoid(o_h) * (hv * lax.rsqrt(ms + NORM_EPS) * nw_ref[:, h * LSTM_V:(h + 1) * LSTM_V])
                    ).astype(BF16)
                    kw = jnp.where(in_head, k_p * jnp.exp(gcol + w_shift[:, h:h + 1]), 0.0)
                    c_new = c_new + s_loc[:, h:h + 1] * _dot_tn(kw.astype(BF16), v_ext)
                c_ref[s, pr] = c_new


def _mlstm_mixer(x, wq, wk, wv, wo, wif, bif, nw):
    b, s, d = x.shape
    rows = MIX_ROWS
    nseq = MIX_SEQS
    const = lambda shape: pl.BlockSpec(shape, lambda i, j: (0,) * len(shape), pipeline_mode=pl.Buffered(1))
    return pl.pallas_call(
        functools.partial(_mlstm_kernel, rows=rows, nseq=nseq),
        grid=(b // nseq, s // rows),
        in_specs=[pl.BlockSpec((nseq, rows, d), lambda i, j: (i, j, 0)),
                  const(wq.shape), const(wk.shape), const(wv.shape), const(wo.shape), const(wif.shape),
                  const(bif.shape), const(nw.shape)],
        out_specs=pl.BlockSpec((nseq, rows, LSTM_D_V), lambda i, j: (i, j, 0)),
        out_shape=jax.ShapeDtypeStruct((b, s, LSTM_D_V), BF16),
        scratch_shapes=[pltpu.VMEM((nseq, LSTM_HEADS // 2, 2 * LSTM_QK, 2 * LSTM_V), F32),
                        pltpu.VMEM((nseq, 1, LANES), F32)],
        compiler_params=pltpu.CompilerParams(dimension_semantics=("arbitrary", "arbitrary"),
                                             vmem_limit_bytes=VMEM_LIMIT),
        name="mlstm_mixer",
    )(x, wq, wk, wv, wo, wif, bif, nw)


def _layer_norm(t, g, b):
    mu = jnp.mean(t, axis=-1, keepdims=True)
    tc = t - mu
    var = jnp.mean(tc * tc, axis=-1, keepdims=True)
    return tc * lax.rsqrt(var + NORM_EPS) * g + b


def _pack_pair(a, b):
    pa = lax.bitcast_convert_type(a.astype(BF16).astype(F32), jnp.uint32)
    pb = lax.bitcast_convert_type(b.astype(BF16).astype(F32), jnp.uint32)
    return pa | (pb >> 16)


def _unpack_pair(w):
    return (lax.bitcast_convert_type(w & jnp.uint32(0xFFFF0000), F32),
            lax.bitcast_convert_type(w << 16, F32))


def _route(logits):
    rows = logits.shape[0]
    lane = lax.broadcasted_iota(jnp.int32, (rows, LANES), 1).astype(F32)
    big = float(LANES)
    neg = -jnp.inf
    gl = jnp.where(lane < MOE_GROUPS, logits, neg)
    gmax = jnp.max(gl, axis=-1, keepdims=True)
    gidx = jnp.min(jnp.where(gl == gmax, lane, big), axis=-1, keepdims=True)
    gsum = jnp.sum(jnp.where(lane < MOE_GROUPS, jnp.exp(logits - gmax), 0.0), axis=-1, keepdims=True)
    grp_p = 1.0 / gsum
    lo = MOE_GROUPS + gidx * MOE_PER_GROUP
    el = jnp.where(lane >= lo, jnp.where(lane < lo + MOE_PER_GROUP, logits, neg), neg)
    m1 = jnp.max(el, axis=-1, keepdims=True)
    i1 = jnp.min(jnp.where(el == m1, lane, big), axis=-1, keepdims=True)
    el2 = jnp.where(lane == i1, neg, el)
    m2 = jnp.max(el2, axis=-1, keepdims=True)
    i2 = jnp.min(jnp.where(el2 == m2, lane, big), axis=-1, keepdims=True)
    e21 = jnp.exp(m2 - m1)
    g1 = grp_p / (1.0 + e21)
    g2 = grp_p * e21 / (1.0 + e21)
    return i1 - MOE_GROUPS, i2 - MOE_GROUPS, g1, g2


def _merge_kernel(x_ref, ys_ref, yl_ref, wgs_ref, wgl_ref, wa_ref, wb_ref, wo_ref, g_ref, b_ref,
                  wr_hi_ref, wr_lo_ref, br_ref, eg_ref, eu_ref, ed_ref,
                  x1_ref, xp_ref, tab_ref, cnt_ref, egb_ref, eub_ref, edb_ref, carry_ref,
                  *, alpha, steps_per_tile, cast_steps):
    @pl.when(pl.program_id(0) % steps_per_tile == 0)
    def _():
        carry_ref[...] = jnp.zeros(carry_ref.shape, F32)

    @pl.when(pl.program_id(0) < cast_steps)
    def _():
        egb_ref[...] = eg_ref[...].astype(BF16)
        eub_ref[...] = eu_ref[...].astype(BF16)
        edb_ref[...] = ed_ref[...].astype(BF16)

    x = x_ref[...]
    rows = x.shape[0]
    xb = x.astype(BF16)
    merged = (_sigmoid(_dot(xb, wgs_ref[...])) * _dot(ys_ref[...], wa_ref[...])
              + _sigmoid(_dot(xb, wgl_ref[...])) * _dot(yl_ref[...], wb_ref[...]))
    x1 = _layer_norm(alpha * x + _dot(merged.astype(BF16), wo_ref[...]), g_ref[...], b_ref[...])
    x1_ref[...] = x1
    half = x1.shape[1] // 2
    xp = _pack_pair(x1[:, :half], x1[:, half:])
    for c in range(PACK_SLABS):
        xp_ref[pl.ds(c, rows, stride=PACK_SLABS), :] = xp[:, c * LANES:(c + 1) * LANES]
    x_hi = x1.astype(BF16)
    x_lo = (x1 - x_hi.astype(F32)).astype(BF16)
    logits = (_dot(x_hi, wr_hi_ref[...]) + _dot(x_lo, wr_hi_ref[...]) + _dot(x_hi, wr_lo_ref[...])
              + br_ref[...])
    e1, e2, g1, g2 = _route(logits)
    lane = lax.broadcasted_iota(jnp.int32, (rows, LANES), 1)
    lane_f = lane.astype(F32)
    oh1 = lane_f == e1
    oh2 = lane_f == e2
    ohs = jnp.where(oh1, 1.0, jnp.where(oh2, 1.0, 0.0))
    r_i = lax.broadcasted_iota(jnp.int32, (rows, rows), 0)
    c_i = lax.broadcasted_iota(jnp.int32, (rows, rows), 1)
    before = jnp.where(r_i > c_i, 1.0, 0.0).astype(BF16)
    seen = _dot(before, ohs.astype(BF16)) + carry_ref[...]
    r1 = jnp.sum(jnp.where(oh1, seen, 0.0), axis=-1, keepdims=True)
    r2 = jnp.sum(jnp.where(oh2, seen, 0.0), axis=-1, keepdims=True)
    carry = carry_ref[...] + jnp.sum(ohs, axis=0, keepdims=True)
    carry_ref[...] = carry
    cnt_ref[...] = jnp.broadcast_to(carry, cnt_ref.shape).astype(jnp.int32)
    as_int = lambda v: v.astype(jnp.int32)
    as_bits = lambda v: lax.bitcast_convert_type(v, jnp.int32)
    tab = jnp.where(lane == 0, as_int(e1), jnp.where(lane == 1, as_int(e2), jnp.where(
        lane == 2, as_int(r1) * PACK_SLABS, jnp.where(lane == 3, as_int(r2) * PACK_SLABS, jnp.where(
            lane == 4, as_bits(g1), jnp.where(lane == 5, as_bits(g2), 0))))))
    tab_ref[...] = tab.T[0:SUBLANES, :]


def _merge(x2d, ys, yl, wgs, wgl, wa, wb, wo, g, b, wr_hi, wr_lo, br, eg, eu, ed, alpha, tile):
    t, d = x2d.shape
    rows = DENSE_ROWS
    spt = tile // rows
    n_steps = t // rows
    per_step = -(-MOE_EXPERTS // n_steps)
    cast_steps = MOE_EXPERTS // per_step
    assert per_step * cast_steps == MOE_EXPERTS and cast_steps <= n_steps
    const = lambda shape: pl.BlockSpec(shape, lambda i: (0,) * len(shape), pipeline_mode=pl.Buffered(1))
    rowblk = lambda w: pl.BlockSpec((rows, w), lambda i: (i, 0))
    expert = lambda w: pl.BlockSpec((per_step,) + w.shape[1:], lambda i: (jnp.minimum(i, cast_steps - 1), 0, 0))
    return pl.pallas_call(
        functools.partial(_merge_kernel, alpha=alpha, steps_per_tile=spt, cast_steps=cast_steps),
        grid=(n_steps,),
        in_specs=[rowblk(d), rowblk(SSM_D_INNER), rowblk(LSTM_D_V),
                  const(wgs.shape), const(wgl.shape), const(wa.shape), const(wb.shape), const(wo.shape),
                  const(g.shape), const(b.shape), const(wr_hi.shape), const(wr_lo.shape), const(br.shape),
                  expert(eg), expert(eu), expert(ed)],
        out_specs=[rowblk(d),
                   pl.BlockSpec((rows * PACK_SLABS, LANES), lambda i: (i, 0)),
                   pl.BlockSpec((SUBLANES, rows), lambda i: (i // spt, i % spt)),
                   pl.BlockSpec((SUBLANES, LANES), lambda i: (i // spt, 0)),
                   expert(eg), expert(eu), expert(ed)],
        out_shape=[jax.ShapeDtypeStruct((t, d), F32),
                   jax.ShapeDtypeStruct((t * PACK_SLABS, LANES), jnp.uint32),
                   jax.ShapeDtypeStruct((t // tile * SUBLANES, tile), jnp.int32),
                   jax.ShapeDtypeStruct((t // tile * SUBLANES, LANES), jnp.int32),
                   jax.ShapeDtypeStruct(eg.shape, BF16), jax.ShapeDtypeStruct(eu.shape, BF16),
                   jax.ShapeDtypeStruct(ed.shape, BF16)],
        scratch_shapes=[pltpu.VMEM((1, LANES), F32)],
        compiler_params=pltpu.CompilerParams(dimension_semantics=("arbitrary",), vmem_limit_bytes=VMEM_LIMIT),
        name="merge_ln_route",
    )(x2d, ys, yl, wgs, wgl, wa, wb, wo, g, b, wr_hi, wr_lo, br, eg, eu, ed)


def _segment_rows(count):
    return lax.shift_right_logical(count + (SUBLANES - 1), 3) * (SUBLANES * PACK_SLABS)


def _route_rows_kernel(cnt_ref, tab_ref, out_ref):
    i = pl.program_id(0)
    tab = tab_ref[...]
    starts = jnp.zeros(tab.shape, jnp.int32)
    off = jnp.int32(0)
    for e in range(MOE_EXPERTS):
        starts = jnp.where(tab == e, off, starts)
        off = off + _segment_rows(cnt_ref[i * MOE_EXPERTS + e])
    row = lax.broadcasted_iota(jnp.int32, tab.shape, 0)
    out_ref[...] = jnp.where(lax.shift_right_logical(row, 1) == 1, tab + pltpu.roll(starts, 2, axis=0), tab)


def _route_rows(tab, cnt, tile):
    n_tiles = tab.shape[0] // SUBLANES
    blk = pl.BlockSpec((SUBLANES, tile), lambda i, c: (i, 0))
    return pl.pallas_call(
        _route_rows_kernel,
        grid_spec=pltpu.PrefetchScalarGridSpec(num_scalar_prefetch=1, grid=(n_tiles,), in_specs=[blk],
                                               out_specs=blk),
        out_shape=jax.ShapeDtypeStruct(tab.shape, jnp.int32),
        compiler_params=pltpu.CompilerParams(dimension_semantics=("arbitrary",), vmem_limit_bytes=VMEM_LIMIT),
        name="route_rows",
    )(cnt, tab)


def _moe_kernel(cnt_ref, tab_ref, xp_hbm, wg_hbm, wu_hbm, wd_hbm, out_hbm, xys, ybuf, off_ref,
                wg_buf, wu_buf, wd_buf, xin, xout, wsem, isem, osem, *, tile, chunk, n_steps):
    i = pl.program_id(0)
    group = SUBLANES * PACK_SLABS
    half = PACK_SLABS * LANES
    weights = ((wg_hbm, wg_buf), (wu_hbm, wu_buf), (wd_hbm, wd_buf))

    def weight_copies(expert, into):
        copies = []
        for k, (hbm, buf) in enumerate(weights):
            part = buf.shape[1] // MOE_WEIGHT_SPLIT
            for q in range(MOE_WEIGHT_SPLIT):
                copies.append(pltpu.make_async_copy(hbm.at[expert, pl.ds(q * part, part)],
                                                    buf.at[into, pl.ds(q * part, part)], wsem.at[k, into]))
        return copies

    @pl.when(i == 0)
    def _():
        for ahead in range(MOE_WEIGHT_BUFFERS - 1):
            for cp in weight_copies(ahead % MOE_EXPERTS, ahead):
                cp.start()

    n_chunks = tile // chunk
    rolled = lambda trips: trips + jnp.minimum(i, 0)

    def in_buf(c):
        return pl.multiple_of((c % 2) * (chunk * PACK_SLABS), group)

    def out_buf(c):
        return pl.multiple_of((c % 2) * (chunk * OUT_SLABS), SUBLANES)

    def in_copy(c):
        row0 = pl.multiple_of((i * tile + c * chunk) * PACK_SLABS, group)
        return pltpu.make_async_copy(xp_hbm.at[pl.ds(row0, chunk * PACK_SLABS)],
                                     xin.at[pl.ds(in_buf(c), chunk * PACK_SLABS)], isem.at[c % 2])

    def out_copy(c):
        row0 = pl.multiple_of((i * tile + c * chunk) * OUT_SLABS, SUBLANES)
        return pltpu.make_async_copy(xout.at[pl.ds(out_buf(c), chunk * OUT_SLABS)],
                                     out_hbm.at[pl.ds(row0, chunk * OUT_SLABS)], osem.at[c % 2])

    def sort_rows():
        in_copy(0).start()

        def offsets(e, acc):
            padded = _segment_rows(cnt_ref[i * MOE_EXPERTS + e])
            off_ref[e] = acc
            end = acc + padded

            @pl.when(padded > 0)
            def _():
                xys[pl.ds(pl.multiple_of(end - group, group), group), :] = jnp.zeros((group, LANES), jnp.uint32)
            return end

        total = lax.fori_loop(0, MOE_EXPERTS, offsets, 0)
        off_ref[MOE_EXPERTS] = total
        tail = MOE_BLOCKS[-1] * PACK_SLABS
        xys[pl.ds(pl.multiple_of(total, group), tail), :] = jnp.zeros((tail, LANES), jnp.uint32)

        def distribute_chunk(c, carry):
            @pl.when(c + 1 < n_chunks)
            def _():
                in_copy(c + 1).start()
            in_copy(c).wait()

            g0 = c * (chunk // SUBLANES)
            shift = in_buf(c) - c * (chunk * PACK_SLABS)

            def distribute(tg, carry2):
                for u in range(SUBLANES):
                    t = tg * SUBLANES + u
                    d1 = pl.multiple_of(tab_ref[2 * tile + t], PACK_SLABS)
                    d2 = pl.multiple_of(tab_ref[3 * tile + t], PACK_SLABS)
                    row = xin[pl.ds(pl.multiple_of(shift + t * PACK_SLABS, PACK_SLABS), PACK_SLABS), :]
                    xys[pl.ds(d1, PACK_SLABS), :] = row
                    xys[pl.ds(d2, PACK_SLABS), :] = row
                return carry2

            lax.fori_loop(g0, g0 + rolled(chunk // SUBLANES), distribute, 0)
            return carry

        lax.fori_loop(0, rolled(n_chunks), distribute_chunk, 0)

    def expert_block(refs, base, rows, valid):
        wg_ref, wu_ref, wd_ref = refs
        words = [xys[pl.ds(base + c, rows, stride=PACK_SLABS), :] for c in range(PACK_SLABS)]
        parts = [_unpack_pair(w) for w in words]
        x_hi = jnp.concatenate([p[0].astype(BF16) for p in parts], axis=1)
        x_lo = jnp.concatenate([p[1].astype(BF16) for p in parts], axis=1)
        hg = _dot(x_hi, wg_ref[0:half, :]) + _dot(x_lo, wg_ref[half:2 * half, :])
        hu = _dot(x_hi, wu_ref[0:half, :]) + _dot(x_lo, wu_ref[half:2 * half, :])
        y = _dot((hg * _sigmoid(hg) * hu).astype(BF16), wd_ref[...])
        for c in range(PACK_SLABS):
            ybuf[pl.ds(c, rows, stride=PACK_SLABS), :] = _pack_pair(
                y[:, c * LANES:(c + 1) * LANES], y[:, half + c * LANES:half + (c + 1) * LANES])

        def copy(g, c):
            r = pl.multiple_of(g * group, group)
            xys[pl.ds(base + r, group), :] = ybuf[pl.ds(r, group), :]
            return c

        lax.fori_loop(0, valid // group, copy, 0)

    def expert_step(j, carry):
        step = i * MOE_EXPERTS + j
        slot = step % MOE_WEIGHT_BUFFERS
        nxt = step + MOE_WEIGHT_BUFFERS - 1

        @pl.when(nxt < n_steps)
        def _():
            for cp in weight_copies(nxt % MOE_EXPERTS, nxt % MOE_WEIGHT_BUFFERS):
                cp.start()

        for cp in weight_copies(j, slot):
            cp.wait()
        refs = (wg_buf.at[slot], wu_buf.at[slot], wd_buf.at[slot])
        start = off_ref[j]
        n = off_ref[j + 1] - start
        lo = 0
        for rows in MOE_BLOCKS:
            hi = rows * PACK_SLABS

            @pl.when(jnp.logical_and(n > lo, n <= hi))
            def _(rows=rows):
                expert_block(refs, pl.multiple_of(start, group), rows, n)
            lo = hi

        @pl.when(n > lo)
        def _():
            span = MOE_BLOCKS[0] * PACK_SLABS

            def block(bi, carry2):
                expert_block(refs, pl.multiple_of(start + bi * span, group), MOE_BLOCKS[0],
                             jnp.minimum(span, n - bi * span))
                return carry2

            lax.fori_loop(0, (n + span - 1) // span, block, 0)
        return carry

    def combine_rows():
        def combine_chunk(c, carry):
            @pl.when(c >= 2)
            def _():
                out_copy(c - 2).wait()

            g0 = c * (chunk // SUBLANES)
            shift = out_buf(c) - c * (chunk * OUT_SLABS)

            def combine(tg, carry2):
                for u in range(SUBLANES):
                    t = tg * SUBLANES + u
                    g1 = lax.bitcast_convert_type(tab_ref[4 * tile + t], F32)
                    g2 = lax.bitcast_convert_type(tab_ref[5 * tile + t], F32)
                    a_hi, a_lo = _unpack_pair(
                        xys[pl.ds(pl.multiple_of(tab_ref[2 * tile + t], PACK_SLABS), PACK_SLABS), :])
                    b_hi, b_lo = _unpack_pair(
                        xys[pl.ds(pl.multiple_of(tab_ref[3 * tile + t], PACK_SLABS), PACK_SLABS), :])
                    xout[pl.ds(pl.multiple_of(shift + t * OUT_SLABS, OUT_SLABS), OUT_SLABS), :] = jnp.concatenate(
                        [g1 * a_hi + g2 * b_hi, g1 * a_lo + g2 * b_lo], axis=0)
                return carry2

            lax.fori_loop(g0, g0 + rolled(chunk // SUBLANES), combine, 0)
            out_copy(c).start()
            return carry

        lax.fori_loop(0, rolled(n_chunks), combine_chunk, 0)
        for c in range(max(0, n_chunks - 2), n_chunks):
            out_copy(c).wait()

    sort_rows()
    lax.fori_loop(0, rolled(MOE_EXPERTS), expert_step, 0)
    combine_rows()


def _moe(xp, tab, cnt, wg, wu, wd, tile):
    t = xp.shape[0] // PACK_SLABS
    d = D_MODEL
    n_tiles = t // tile
    chunk = min(MOE_CHUNK, tile)
    seg_rows = 2 * tile + MOE_EXPERTS * SUBLANES + MOE_BLOCKS[-1]
    grid_spec = pltpu.PrefetchScalarGridSpec(
        num_scalar_prefetch=1,
        grid=(n_tiles,),
        in_specs=[pl.BlockSpec((SUBLANES * tile,), lambda i, c: (i,), memory_space=pltpu.SMEM,
                               pipeline_mode=pl.Buffered(1)),
                  pl.BlockSpec(memory_space=pl.ANY), pl.BlockSpec(memory_space=pl.ANY),
                  pl.BlockSpec(memory_space=pl.ANY), pl.BlockSpec(memory_space=pl.ANY)],
        out_specs=pl.BlockSpec(memory_space=pl.ANY),
        scratch_shapes=[pltpu.VMEM((seg_rows * PACK_SLABS, LANES), jnp.uint32),
                        pltpu.VMEM((MOE_BLOCKS[-1] * PACK_SLABS, LANES), jnp.uint32),
                        pltpu.SMEM((LANES,), jnp.int32),
                        pltpu.VMEM((MOE_WEIGHT_BUFFERS, d, MOE_D_FF), BF16),
                        pltpu.VMEM((MOE_WEIGHT_BUFFERS, d, MOE_D_FF), BF16),
                        pltpu.VMEM((MOE_WEIGHT_BUFFERS, MOE_D_FF, d), BF16),
                        pltpu.VMEM((2 * chunk * PACK_SLABS, LANES), jnp.uint32),
                        pltpu.VMEM((2 * chunk * OUT_SLABS, LANES), F32),
                        pltpu.SemaphoreType.DMA((3, MOE_WEIGHT_BUFFERS)),
                        pltpu.SemaphoreType.DMA((2,)), pltpu.SemaphoreType.DMA((2,))],
    )
    return pl.pallas_call(
        functools.partial(_moe_kernel, tile=tile, chunk=chunk, n_steps=n_tiles * MOE_EXPERTS),
        grid_spec=grid_spec,
        out_shape=jax.ShapeDtypeStruct((t * OUT_SLABS, LANES), F32),
        compiler_params=pltpu.CompilerParams(dimension_semantics=("arbitrary",),
                                             vmem_limit_bytes=MOE_VMEM_LIMIT),
        name="moe_experts",
    )(cnt, tab, xp, wg, wu, wd)


def _final_kernel(x1_ref, moe_ref, p_ref, g_ref, b_ref, wpg_ref, wpp_ref, out_ref, *, alpha):
    rows = x1_ref.shape[0]
    moe = jnp.concatenate([moe_ref[pl.ds(c, rows, stride=OUT_SLABS), :] for c in range(OUT_SLABS)], axis=1)
    x2 = _layer_norm(alpha * x1_ref[...] + moe, g_ref[...], b_ref[...])
    gate = _sigmoid(_dot(x2.astype(BF16), wpg_ref[...]))
    out_ref[...] = x2 + gate * _dot(p_ref[...].astype(BF16), wpp_ref[...])


def _final(x1, moe, p2d, g, b, wpg, wpp, alpha):
    t, d = x1.shape
    rows = DENSE_ROWS
    const = lambda shape: pl.BlockSpec(shape, lambda i: (0,) * len(shape))
    rowblk = lambda w: pl.BlockSpec((rows, w), lambda i: (i, 0))
    return pl.pallas_call(
        functools.partial(_final_kernel, alpha=alpha),
        grid=(t // rows,),
        in_specs=[rowblk(d), pl.BlockSpec((rows * OUT_SLABS, LANES), lambda i: (i, 0)), rowblk(p2d.shape[1]),
                  const(g.shape), const(b.shape), const(wpg.shape), const(wpp.shape)],
        out_specs=rowblk(d),
        out_shape=jax.ShapeDtypeStruct((t, d), F32),
        compiler_params=pltpu.CompilerParams(dimension_semantics=("arbitrary",), vmem_limit_bytes=VMEM_LIMIT),
        name="final_ln_ple",
    )(x1, moe, p2d, g, b, wpg, wpp)


def _pad_lanes(w, width=LANES):
    return jnp.pad(w, ((0, 0), (0, width - w.shape[1])))


def _row(v, width=None):
    v = v.astype(F32).reshape(1, -1)
    return v if width is None else _pad_lanes(v, width)


def kernel(x, p, w_in, ssm_conv_w, ssm_conv_b, ssm_dt_bias, ssm_a_log, ssm_d, ssm_norm_w, lstm_i_bias, lstm_f_bias, lstm_norm_w, w_branch_ssm, w_branch_lstm, w_out, ln1_g, ln1_b, moe_w_group, moe_b_group, moe_w_expert, moe_b_expert, moe_w_gate, moe_w_up, moe_w_down, ln2_g, ln2_b, ple_w_proj, ple_w_gate):
    depth = w_in.shape[0]
    bsz, seq, d = x.shape
    t = bsz * seq
    alpha = (2.0 * depth) ** 0.25
    head_expand = (jnp.arange(LANES, dtype=jnp.int32)[:, None]
                   == jnp.arange(SSM_D_INNER, dtype=jnp.int32)[None, :] // SSM_HEAD_DIM).astype(BF16)
    for i in range(depth):
        splits = []
        c0 = 0
        for sz in IN_PROJ_SIZES:
            splits.append(w_in[i][:, c0:c0 + sz])
            c0 += sz
        w_z, w_xbc, w_dt, w_q, w_k, w_v, w_o, w_i, w_f, w_gs, w_gl = splits
        bf = lambda w: w.astype(BF16)
        y_ssm = _ssd_mixer(
            x, bf(w_z), bf(w_xbc), bf(_pad_lanes(w_dt)),
            ssm_conv_w[i].astype(F32), _row(ssm_conv_b[i]), _row(ssm_dt_bias[i], LANES),
            _row(-jnp.exp(ssm_a_log[i].astype(F32)), LANES),
            _row(jnp.repeat(ssm_d[i].astype(F32), SSM_HEAD_DIM)), _row(ssm_norm_w[i]), head_expand)
        w_if = jnp.concatenate([_pad_lanes(w_i), _pad_lanes(w_f)], axis=1)
        b_if = jnp.concatenate([_row(lstm_i_bias[i], LANES), _row(lstm_f_bias[i], LANES)], axis=1)
        y_lstm = _mlstm_mixer(x, bf(w_q * (LSTM_QK ** -0.5)), bf(w_k), bf(w_v), bf(w_o), bf(w_if), b_if,
                              _row(lstm_norm_w[i]))
        w_r = _pad_lanes(jnp.concatenate([moe_w_group[i], moe_w_expert[i]], axis=1).astype(F32))
        w_r_hi = w_r.astype(BF16)
        w_r_lo = (w_r - w_r_hi.astype(F32)).astype(BF16)
        b_r = _row(jnp.concatenate([moe_b_group[i], moe_b_expert[i]]), LANES)
        tile = min(MOE_TILE, t)
        x1, xp, tab, cnt, e_gate, e_up, e_down = _merge(
            x.reshape(t, d), y_ssm.reshape(t, -1), y_lstm.reshape(t, -1),
            bf(w_gs), bf(w_gl), bf(w_branch_ssm[i]), bf(w_branch_lstm[i]), bf(w_out[i]),
            _row(ln1_g[i]), _row(ln1_b[i]), w_r_hi, w_r_lo, b_r,
            moe_w_gate[i].astype(F32), moe_w_up[i].astype(F32), moe_w_down[i].astype(F32), alpha, tile)
        cnt = cnt.reshape(t // tile, SUBLANES, LANES)[:, 0, :MOE_EXPERTS].reshape(-1)
        moe = _moe(xp, _route_rows(tab, cnt, tile).reshape(-1), cnt, e_gate, e_up, e_down, tile)
        x = _final(x1, moe, p[i].reshape(t, -1), _row(ln2_g[i]), _row(ln2_b[i]),
                   bf(ple_w_gate[i]), bf(ple_w_proj[i]), alpha).reshape(bsz, seq, d)
    return x
```

```python
import functools

import jax
import jax.numpy as jnp
from jax import lax
from jax.experimental import pallas as pl
from jax.experimental.pallas import tpu as pltpu

F32 = jnp.float32
BF16 = jnp.bfloat16

D_MODEL = 1024
PLE_DIM = 256
SSM_D_INNER = 1024
SSM_HEAD_DIM = 64
SSM_HEADS = 16
SSM_GROUPS = 4
SSM_STATE = 128
SSM_CONV = 4
SSM_XBC = SSM_D_INNER + 2 * SSM_GROUPS * SSM_STATE
LSTM_HEADS = 8
LSTM_QK = 64
LSTM_V = 128
LSTM_D_QK = LSTM_HEADS * LSTM_QK
LSTM_D_V = LSTM_HEADS * LSTM_V
CHUNK = 128
MOE_GROUPS = 8
MOE_PER_GROUP = 8
MOE_EXPERTS = 64
MOE_D_FF = 512
NORM_EPS = 1e-5
IN_PROJ_SIZES = (SSM_D_INNER, SSM_XBC, SSM_HEADS, LSTM_D_QK, LSTM_D_QK, LSTM_D_V, LSTM_D_V,
                 LSTM_HEADS, LSTM_HEADS, D_MODEL, D_MODEL)

LANES = 128
SUBLANES = 8
VMEM_LIMIT = 56 * 1024 * 1024
MOE_VMEM_LIMIT = 60 * 1024 * 1024

MIX_ROWS = 256
MIX_SEQS = 2
DENSE_ROWS = 512
MOE_TILE = 8192
MOE_CHUNK = 256
MOE_BLOCKS = (256, 320, 384)
MOE_WEIGHT_BUFFERS = 4
MOE_WEIGHT_SPLIT = 4
PACK_SLABS = D_MODEL // 2 // LANES
OUT_SLABS = D_MODEL // LANES


def _dot(a, b):
    return jnp.dot(a, b, preferred_element_type=F32)


def _dot_nt(a, b):
    return lax.dot_general(a, b, (((1,), (1,)), ((), ())), preferred_element_type=F32)


def _dot_tn(a, b):
    return lax.dot_general(a, b, (((0,), (0,)), ((), ())), preferred_element_type=F32)


def _sigmoid(x):
    return 1.0 / (1.0 + jnp.exp(-x))


def _softplus(x):
    return jnp.maximum(x, 0.0) + jnp.log(1.0 + jnp.exp(-jnp.abs(x)))


def _split3(x):
    hi = x.astype(BF16)
    r1 = x - hi.astype(F32)
    mid = r1.astype(BF16)
    lo = (r1 - mid.astype(F32)).astype(BF16)
    return hi, mid, lo


def _cumsum_rows(tril, x):
    hi, mid, lo = _split3(x)
    return _dot(tril, hi) + _dot(tril, mid) + _dot(tril, lo)


def _causal_masks():
    r = lax.broadcasted_iota(jnp.int32, (CHUNK, CHUNK), 0)
    c = lax.broadcasted_iota(jnp.int32, (CHUNK, CHUNK), 1)
    causal = r >= c
    return causal, jnp.where(causal, 1.0, 0.0).astype(BF16)


def _pair_cols(v, ha, hb, rows, lane):
    a = jnp.broadcast_to(v[:, ha:ha + 1], (rows, LANES))
    b = jnp.broadcast_to(v[:, hb:hb + 1], (rows, LANES))
    return jnp.where(lane < SSM_HEAD_DIM, a, b)


def _ssd_kernel(x_ref, w_ref, cw_ref, cb_ref, dtb_ref, aneg_ref, dexp_ref, nw_ref, ex_ref,
                y_ref, ext_ref, u_ref, st_ref, *, rows, nseq):
    n_slab = SSM_XBC // LANES
    tail = SUBLANES - SSM_CONV + 1

    @pl.when(pl.program_id(1) == 0)
    def _():
        ext_ref[:, :, 0:SUBLANES, :] = jnp.zeros((nseq, n_slab, SUBLANES, LANES), F32)
        st_ref[...] = jnp.zeros(st_ref.shape, F32)

    xb = x_ref[...].reshape(nseq * rows, x_ref.shape[-1]).astype(BF16)
    z = _dot(xb, w_ref[:, 0:SSM_D_INNER])
    dt_col = SSM_D_INNER + SSM_XBC
    dtr = _dot(xb, w_ref[:, dt_col:dt_col + LANES])
    dtr = jnp.where(lax.broadcasted_iota(jnp.int32, dtr.shape, 1) < SSM_HEADS, dtr, 0.0)
    for c2 in range(n_slab // 2):
        xbc = _dot(xb, w_ref[:, SSM_D_INNER + c2 * 2 * LANES:SSM_D_INNER + (c2 + 1) * 2 * LANES])
        for s in range(nseq):
            for cc in range(2):
                ext_ref[s, 2 * c2 + cc, SUBLANES:SUBLANES + rows, :] = (
                    xbc[s * rows:(s + 1) * rows, cc * LANES:(cc + 1) * LANES])
    for s in range(nseq):
        for c in range(n_slab):
            conv = cb_ref[:, c * LANES:(c + 1) * LANES]
            for k in range(SSM_CONV):
                conv = conv + cw_ref[k:k + 1, c * LANES:(c + 1) * LANES] * ext_ref[s, c, pl.ds(tail + k, rows), :]
            u_ref[s, c] = conv * _sigmoid(conv)
            ext_ref[s, c, 0:SUBLANES, :] = ext_ref[s, c, rows:rows + SUBLANES, :]

    dt = _softplus(dtr + dtb_ref[...])
    da = dt * aneg_ref[...]
    causal, tril = _causal_masks()
    lane = lax.broadcasted_iota(jnp.int32, (CHUNK, LANES), 1)
    lane1 = lax.broadcasted_iota(jnp.int32, (1, LANES), 1)
    gw = SSM_D_INNER // SSM_GROUPS
    b_slab = SSM_D_INNER // LANES
    c_slab = b_slab + SSM_GROUPS

    for ci in range(rows // CHUNK):
        for s in range(nseq):
            r0 = ci * CHUNK
            q0 = s * rows + r0
            dt_c = dt[q0:q0 + CHUNK]
            a_cs = _cumsum_rows(tril, da[q0:q0 + CHUNK])
            a_cs_t = a_cs.T
            dt_t = dt_c.T
            a_last = a_cs[CHUNK - 1:CHUNK, :]
            ea_last = jnp.exp(a_last)
            ea_x = _dot(jnp.exp(a_cs).astype(BF16), ex_ref[...])
            sw_x = _dot((dt_c * jnp.exp(a_last - a_cs)).astype(BF16), ex_ref[...])
            for g in range(SSM_GROUPS):
                bm_g = u_ref[s, b_slab + g, r0:r0 + CHUNK, :].astype(BF16)
                cm_g = u_ref[s, c_slab + g, r0:r0 + CHUNK, :].astype(BF16)
                cb = _dot_nt(cm_g, bm_g)
                st_g = st_ref[s, :, g * gw:(g + 1) * gw]
                y_off = _dot(cm_g, st_g.astype(BF16))
                yy, xw, eal_x = [], [], []
                for pr in range(2):
                    ha = 4 * g + 2 * pr
                    hb = ha + 1
                    lo = g * gw + pr * LANES
                    xs_p = u_ref[s, 2 * g + pr, r0:r0 + CHUNK, :]
                    xs_pb = xs_p.astype(BF16)
                    rhs = jnp.concatenate([jnp.where(lane < SSM_HEAD_DIM, xs_pb, 0).astype(BF16),
                                           jnp.where(lane >= SSM_HEAD_DIM, xs_pb, 0).astype(BF16)], axis=0)
                    gs = []
                    for h in (ha, hb):
                        colb = jnp.broadcast_to(a_cs[:, h:h + 1], (CHUNK, CHUNK))
                        rowb = jnp.broadcast_to(a_cs_t[h:h + 1, :], (CHUNK, CHUNK))
                        dec = jnp.exp(jnp.where(causal, colb - rowb, -jnp.inf))
                        gs.append((cb * dec * jnp.broadcast_to(dt_t[h:h + 1, :], (CHUNK, CHUNK))).astype(BF16))
                    y_p = (_dot(jnp.concatenate(gs, axis=1), rhs)
                           + y_off[:, pr * LANES:(pr + 1) * LANES] * ea_x[:, lo:lo + LANES]
                           + dexp_ref[:, lo:lo + LANES] * xs_p)
                    zz = z[q0:q0 + CHUNK, lo:lo + LANES]
                    yy.append(y_p * (zz * _sigmoid(zz)))
                    xw.append((xs_p * sw_x[:, lo:lo + LANES]).astype(BF16))
                    eal_x.append(_pair_cols(ea_last, ha, hb, 1, lane1))
                st_ref[s, :, g * gw:(g + 1) * gw] = (st_g * jnp.concatenate(eal_x, axis=1)
                                                     + _dot_tn(bm_g, jnp.concatenate(xw, axis=1)))
                ms = sum(jnp.sum(t * t, axis=-1, keepdims=True) for t in yy) * (1.0 / gw)
                inv = lax.rsqrt(ms + NORM_EPS)
                for pr in range(2):
                    lo = g * gw + pr * LANES
                    y_ref[s, r0:r0 + CHUNK, lo:lo + LANES] = (yy[pr] * inv * nw_ref[:, lo:lo + LANES]).astype(BF16)


def _ssd_mixer(x, w, cw, cb, dtb, aneg, dexp, nw, ex):
    b, s, d = x.shape
    rows = MIX_ROWS
    nseq = MIX_SEQS
    const = lambda shape: pl.BlockSpec(shape, lambda i, j: (0,) * len(shape), pipeline_mode=pl.Buffered(1))
    return pl.pallas_call(
        functools.partial(_ssd_kernel, rows=rows, nseq=nseq),
        grid=(b // nseq, s // rows),
        in_specs=[pl.BlockSpec((nseq, rows, d), lambda i, j: (i, j, 0)),
                  const(w.shape), const(cw.shape), const(cb.shape),
                  const(dtb.shape), const(aneg.shape), const(dexp.shape), const(nw.shape), const(ex.shape)],
        out_specs=pl.BlockSpec((nseq, rows, SSM_D_INNER), lambda i, j: (i, j, 0)),
        out_shape=jax.ShapeDtypeStruct((b, s, SSM_D_INNER), BF16),
        scratch_shapes=[pltpu.VMEM((nseq, SSM_XBC // LANES, SUBLANES + rows, LANES), F32),
                        pltpu.VMEM((nseq, SSM_XBC // LANES, rows, LANES), F32),
                        pltpu.VMEM((nseq, SSM_STATE, SSM_D_INNER), F32)],
        compiler_params=pltpu.CompilerParams(dimension_semantics=("arbitrary", "arbitrary"),
                                             vmem_limit_bytes=VMEM_LIMIT),
        name="ssd_mixer",
    )(x, w, cw, cb, dtb, aneg, dexp, nw, ex)


def _mlstm_kernel(x_ref, w_ref, bif_ref, nw_ref,
                  y_ref, c_ref, m_ref, *, rows, nseq):
    @pl.when(pl.program_id(1) == 0)
    def _():
        c_ref[...] = jnp.zeros(c_ref.shape, F32)
        m_ref[...] = jnp.full(m_ref.shape, -jnp.inf, F32)

    xb = x_ref[...].reshape(nseq * rows, x_ref.shape[-1]).astype(BF16)
    c_k, c_v, c_o, c_g = LSTM_D_QK, 2 * LSTM_D_QK, 2 * LSTM_D_QK + LSTM_D_V, 2 * LSTM_D_QK + 2 * LSTM_D_V
    q = _dot(xb, w_ref[:, 0:c_k])
    k = _dot(xb, w_ref[:, c_k:c_v])
    v = _dot(xb, w_ref[:, c_v:c_o])
    o = _dot(xb, w_ref[:, c_o:c_g])
    gif = _dot(xb, w_ref[:, c_g:c_g + 2 * LANES]) + bif_ref[...]
    causal, tril = _causal_masks()
    lane = lax.broadcasted_iota(jnp.int32, (CHUNK, LANES), 1)
    lane1 = lax.broadcasted_iota(jnp.int32, (1, LANES), 1)
    row = lax.broadcasted_iota(jnp.int32, (LANES, 1), 0)
    head_lane = lane < LSTM_HEADS
    ones = jnp.ones((CHUNK, LANES), BF16)

    for ci in range(rows // CHUNK):
        r0 = ci * CHUNK
        pre = {}
        for s in range(nseq):
            q0 = s * rows + r0
            log_i = gif[q0:q0 + CHUNK, 0:LANES]
            f_pre = gif[q0:q0 + CHUNK, LANES:2 * LANES]
            log_f = jnp.where(head_lane, jnp.minimum(f_pre, 0.0) - jnp.log(1.0 + jnp.exp(-jnp.abs(f_pre))), 0.0)
            fcum = _cumsum_rows(tril, log_f)
            f_tot = fcum[CHUNK - 1:CHUNK, :]
            gk = log_i - fcum
            m_loc = jnp.max(f_tot + gk, axis=0, keepdims=True)
            g_t = gk.T
            m_prev = m_ref[s]
            m_new = jnp.maximum(f_tot + m_prev, m_loc)
            s_prev = jnp.exp(f_tot + m_prev - m_new)
            s_loc = jnp.exp(m_loc - m_new)
            m_ref[s] = jnp.where(lane1 < LSTM_HEADS, m_new, -jnp.inf)
            w_shift = f_tot - m_loc
            pre[s] = (q0, fcum, gk, g_t, m_prev, s_prev, s_loc, w_shift)

        for pr in range(LSTM_HEADS // 2):
            st = {}
            for s in range(nseq):
                q0, fcum, gk, g_t, m_prev, s_prev, s_loc, w_shift = pre[s]
                q_p = q[q0:q0 + CHUNK, pr * LANES:(pr + 1) * LANES]
                k_p = k[q0:q0 + CHUNK, pr * LANES:(pr + 1) * LANES]
                k_pb = k_p.astype(BF16)
                c_pair = c_ref[s, pr]
                c_pair_b = c_pair.astype(BF16)
                c_new = c_pair * jnp.where(row < LSTM_QK,
                                           jnp.broadcast_to(s_prev[:, 2 * pr:2 * pr + 1], (LANES, 1)),
                                           jnp.broadcast_to(s_prev[:, 2 * pr + 1:2 * pr + 2], (LANES, 1)))
                st[s] = [q_p, k_p, k_pb, c_pair_b, c_new]
            for hh in range(2):
                for s in range(nseq):
                    q0, fcum, gk, g_t, m_prev, s_prev, s_loc, w_shift = pre[s]
                    q_p, k_p, k_pb, c_pair_b, c_new = st[s]
                    h = 2 * pr + hh
                    in_head = (lane < LSTM_QK) if hh == 0 else (lane >= LSTM_QK)
                    v_h = v[q0:q0 + CHUNK, h * LSTM_V:(h + 1) * LSTM_V].astype(BF16)
                    v_ext = jnp.concatenate([v_h, ones], axis=1)
                    q_m = jnp.where(in_head, q_p, 0.0)
                    fcol = jnp.broadcast_to(fcum[:, h:h + 1], (CHUNK, CHUNK))
                    gcol = jnp.broadcast_to(gk[:, h:h + 1], (CHUNK, LANES))
                    rowb = jnp.broadcast_to(g_t[h:h + 1, :], (CHUNK, CHUNK))
                    log_d = jnp.where(causal, fcol + rowb, -jnp.inf)
                    inter_log = fcol + m_prev[:, h:h + 1]
                    m_t = jnp.maximum(inter_log, jnp.max(log_d, axis=-1, keepdims=True))
                    scores = _dot_nt(q_m.astype(BF16), k_pb) * jnp.exp(log_d - m_t)
                    inter_w = jnp.exp(inter_log - m_t)
                    lhs = jnp.concatenate([scores.astype(BF16), (q_m * inter_w).astype(BF16)], axis=1)
                    res = _dot(lhs, jnp.concatenate([v_ext, c_pair_b], axis=0))
                    hv = res[:, 0:LSTM_V] / jnp.maximum(jnp.abs(res[:, LSTM_V:2 * LSTM_V]), jnp.exp(-m_t))
                    ms = jnp.mean(hv * hv, axis=-1, keepdims=True)
                    o_h = o[q0:q0 + CHUNK, h * LSTM_V:(h + 1) * LSTM_V]
                    y_ref[s, r0:r0 + CHUNK, h * LSTM_V:(h + 1) * LSTM_V] = (
                        _sigmoid(o_h) * (hv * lax.rsqrt(ms + NORM_EPS) * nw_ref[:, h * LSTM_V:(h + 1) * LSTM_V])
                    ).astype(BF16)
                    kw = jnp.where(in_head, k_p * jnp.exp(gcol + w_shift[:, h:h + 1]), 0.0)
                    st[s][4] = c_new + s_loc[:, h:h + 1] * _dot_tn(kw.astype(BF16), v_ext)
            for s in range(nseq):
                c_ref[s, pr] = st[s][4]


def _mlstm_mixer(x, w, bif, nw):
    b, s, d = x.shape
    rows = MIX_ROWS
    nseq = MIX_SEQS
    const = lambda shape: pl.BlockSpec(shape, lambda i, j: (0,) * len(shape), pipeline_mode=pl.Buffered(1))
    return pl.pallas_call(
        functools.partial(_mlstm_kernel, rows=rows, nseq=nseq),
        grid=(b // nseq, s // rows),
        in_specs=[pl.BlockSpec((nseq, rows, d), lambda i, j: (i, j, 0)),
                  const(w.shape), const(bif.shape), const(nw.shape)],
        out_specs=pl.BlockSpec((nseq, rows, LSTM_D_V), lambda i, j: (i, j, 0)),
        out_shape=jax.ShapeDtypeStruct((b, s, LSTM_D_V), BF16),
        scratch_shapes=[pltpu.VMEM((nseq, LSTM_HEADS // 2, 2 * LSTM_QK, 2 * LSTM_V), F32),
                        pltpu.VMEM((nseq, 1, LANES), F32)],
        compiler_params=pltpu.CompilerParams(dimension_semantics=("arbitrary", "arbitrary"),
                                             vmem_limit_bytes=VMEM_LIMIT),
        name="mlstm_mixer",
    )(x, w, bif, nw)


def _layer_norm(t, g, b):
    mu = jnp.mean(t, axis=-1, keepdims=True)
    tc = t - mu
    var = jnp.mean(tc * tc, axis=-1, keepdims=True)
    return tc * lax.rsqrt(var + NORM_EPS) * g + b


def _pack_pair(a, b):
    pa = lax.bitcast_convert_type(a.astype(BF16).astype(F32), jnp.uint32)
    pb = lax.bitcast_convert_type(b.astype(BF16).astype(F32), jnp.uint32)
    return pa | (pb >> 16)


def _unpack_pair(w):
    return (lax.bitcast_convert_type(w & jnp.uint32(0xFFFF0000), F32),
            lax.bitcast_convert_type(w << 16, F32))


def _route(logits):
    rows = logits.shape[0]
    lane = lax.broadcasted_iota(jnp.int32, (rows, LANES), 1).astype(F32)
    big = float(LANES)
    neg = -jnp.inf
    gl = jnp.where(lane < MOE_GROUPS, logits, neg)
    gmax = jnp.max(gl, axis=-1, keepdims=True)
    gidx = jnp.min(jnp.where(gl == gmax, lane, big), axis=-1, keepdims=True)
    gsum = jnp.sum(jnp.where(lane < MOE_GROUPS, jnp.exp(logits - gmax), 0.0), axis=-1, keepdims=True)
    grp_p = 1.0 / gsum
    lo = MOE_GROUPS + gidx * MOE_PER_GROUP
    el = jnp.where(lane >= lo, jnp.where(lane < lo + MOE_PER_GROUP, logits, neg), neg)
    m1 = jnp.max(el, axis=-1, keepdims=True)
    i1 = jnp.min(jnp.where(el == m1, lane, big), axis=-1, keepdims=True)
    el2 = jnp.where(lane == i1, neg, el)
    m2 = jnp.max(el2, axis=-1, keepdims=True)
    i2 = jnp.min(jnp.where(el2 == m2, lane, big), axis=-1, keepdims=True)
    e21 = jnp.exp(m2 - m1)
    g1 = grp_p / (1.0 + e21)
    g2 = grp_p * e21 / (1.0 + e21)
    return i1 - MOE_GROUPS, i2 - MOE_GROUPS, g1, g2


def _merge_kernel(x_ref, ys_ref, yl_ref, wg_ref, wa_ref, wb_ref, wo_ref, g_ref, b_ref,
                  wr_hi_ref, wr_lo_ref, br_ref, eg_ref, eu_ref, ed_ref,
                  x1_ref, xp_ref, tab_ref, cnt_ref, egb_ref, eub_ref, edb_ref, carry_ref,
                  *, alpha, steps_per_tile, cast_steps):
    @pl.when(pl.program_id(0) % steps_per_tile == 0)
    def _():
        carry_ref[...] = jnp.zeros(carry_ref.shape, F32)

    @pl.when(pl.program_id(0) < cast_steps)
    def _():
        egb_ref[...] = eg_ref[...].astype(BF16)
        eub_ref[...] = eu_ref[...].astype(BF16)
        edb_ref[...] = ed_ref[...].astype(BF16)

    x = x_ref[...]
    rows = x.shape[0]
    xb = x.astype(BF16)
    d = x.shape[1]
    merged = (_sigmoid(_dot(xb, wg_ref[:, 0:d])) * _dot(ys_ref[...], wa_ref[...])
              + _sigmoid(_dot(xb, wg_ref[:, d:2 * d])) * _dot(yl_ref[...], wb_ref[...]))
    x1 = _layer_norm(alpha * x + _dot(merged.astype(BF16), wo_ref[...]), g_ref[...], b_ref[...])
    x1_ref[...] = x1
    half = x1.shape[1] // 2
    xp = _pack_pair(x1[:, :half], x1[:, half:])
    for c in range(PACK_SLABS):
        xp_ref[pl.ds(c, rows, stride=PACK_SLABS), :] = xp[:, c * LANES:(c + 1) * LANES]
    x_hi = x1.astype(BF16)
    x_lo = (x1 - x_hi.astype(F32)).astype(BF16)
    logits = (_dot(x_hi, wr_hi_ref[...]) + _dot(x_lo, wr_hi_ref[...]) + _dot(x_hi, wr_lo_ref[...])
              + br_ref[...])
    e1, e2, g1, g2 = _route(logits)
    lane = lax.broadcasted_iota(jnp.int32, (rows, LANES), 1)
    lane_f = lane.astype(F32)
    oh1 = lane_f == e1
    oh2 = lane_f == e2
    ohs = jnp.where(oh1, 1.0, jnp.where(oh2, 1.0, 0.0))
    r_i = lax.broadcasted_iota(jnp.int32, (rows, rows), 0)
    c_i = lax.broadcasted_iota(jnp.int32, (rows, rows), 1)
    before = jnp.where(r_i > c_i, 1.0, 0.0).astype(BF16)
    seen = _dot(before, ohs.astype(BF16)) + carry_ref[...]
    r1 = jnp.sum(jnp.where(oh1, seen, 0.0), axis=-1, keepdims=True)
    r2 = jnp.sum(jnp.where(oh2, seen, 0.0), axis=-1, keepdims=True)
    carry = carry_ref[...] + jnp.sum(ohs, axis=0, keepdims=True)
    carry_ref[...] = carry
    cnt_ref[...] = jnp.broadcast_to(carry, cnt_ref.shape).astype(jnp.int32)
    as_int = lambda v: v.astype(jnp.int32)
    as_bits = lambda v: lax.bitcast_convert_type(v, jnp.int32)
    tab = jnp.where(lane == 0, as_int(e1), jnp.where(lane == 1, as_int(e2), jnp.where(
        lane == 2, as_int(r1) * PACK_SLABS, jnp.where(lane == 3, as_int(r2) * PACK_SLABS, jnp.where(
            lane == 4, as_bits(g1), jnp.where(lane == 5, as_bits(g2), 0))))))
    tab_ref[...] = tab.T[0:SUBLANES, :]


def _merge(x2d, ys, yl, wg, wa, wb, wo, g, b, wr_hi, wr_lo, br, eg, eu, ed, alpha, tile):
    t, d = x2d.shape
    rows = DENSE_ROWS
    spt = tile // rows
    n_steps = t // rows
    per_step = -(-MOE_EXPERTS // n_steps)
    cast_steps = MOE_EXPERTS // per_step
    assert per_step * cast_steps == MOE_EXPERTS and cast_steps <= n_steps
    const = lambda shape: pl.BlockSpec(shape, lambda i: (0,) * len(shape), pipeline_mode=pl.Buffered(1))
    rowblk = lambda w: pl.BlockSpec((rows, w), lambda i: (i, 0))
    expert = lambda w: pl.BlockSpec((per_step,) + w.shape[1:], lambda i: (jnp.minimum(i, cast_steps - 1), 0, 0))
    return pl.pallas_call(
        functools.partial(_merge_kernel, alpha=alpha, steps_per_tile=spt, cast_steps=cast_steps),
        grid=(n_steps,),
        in_specs=[rowblk(d), rowblk(SSM_D_INNER), rowblk(LSTM_D_V),
                  const(wg.shape), const(wa.shape), const(wb.shape), const(wo.shape),
                  const(g.shape), const(b.shape), const(wr_hi.shape), const(wr_lo.shape), const(br.shape),
                  expert(eg), expert(eu), expert(ed)],
        out_specs=[rowblk(d),
                   pl.BlockSpec((rows * PACK_SLABS, LANES), lambda i: (i, 0)),
                   pl.BlockSpec((SUBLANES, rows), lambda i: (i // spt, i % spt)),
                   pl.BlockSpec((SUBLANES, LANES), lambda i: (i // spt, 0)),
                   expert(eg), expert(eu), expert(ed)],
        out_shape=[jax.ShapeDtypeStruct((t, d), F32),
                   jax.ShapeDtypeStruct((t * PACK_SLABS, LANES), jnp.uint32),
                   jax.ShapeDtypeStruct((t // tile * SUBLANES, tile), jnp.int32),
                   jax.ShapeDtypeStruct((t // tile * SUBLANES, LANES), jnp.int32),
                   jax.ShapeDtypeStruct(eg.shape, BF16), jax.ShapeDtypeStruct(eu.shape, BF16),
                   jax.ShapeDtypeStruct(ed.shape, BF16)],
        scratch_shapes=[pltpu.VMEM((1, LANES), F32)],
        compiler_params=pltpu.CompilerParams(dimension_semantics=("arbitrary",), vmem_limit_bytes=VMEM_LIMIT),
        name="merge_ln_route",
    )(x2d, ys, yl, wg, wa, wb, wo, g, b, wr_hi, wr_lo, br, eg, eu, ed)


def _segment_rows(count):
    return lax.shift_right_logical(count + (SUBLANES - 1), 3) * (SUBLANES * PACK_SLABS)


def _route_rows_kernel(cnt_ref, tab_ref, out_ref):
    i = pl.program_id(0)
    tab = tab_ref[...]
    starts = jnp.zeros(tab.shape, jnp.int32)
    off = jnp.int32(0)
    for e in range(MOE_EXPERTS):
        starts = jnp.where(tab == e, off, starts)
        off = off + _segment_rows(cnt_ref[i * MOE_EXPERTS + e])
    row = lax.broadcasted_iota(jnp.int32, tab.shape, 0)
    out_ref[...] = jnp.where(lax.shift_right_logical(row, 1) == 1, tab + pltpu.roll(starts, 2, axis=0), tab)


def _route_rows(tab, cnt, tile):
    n_tiles = tab.shape[0] // SUBLANES
    blk = pl.BlockSpec((SUBLANES, tile), lambda i, c: (i, 0))
    return pl.pallas_call(
        _route_rows_kernel,
        grid_spec=pltpu.PrefetchScalarGridSpec(num_scalar_prefetch=1, grid=(n_tiles,), in_specs=[blk],
                                               out_specs=blk),
        out_shape=jax.ShapeDtypeStruct(tab.shape, jnp.int32),
        compiler_params=pltpu.CompilerParams(dimension_semantics=("arbitrary",), vmem_limit_bytes=VMEM_LIMIT),
        name="route_rows",
    )(cnt, tab)


def _moe_kernel(cnt_ref, tab_ref, xp_hbm, wg_hbm, wu_hbm, wd_hbm, out_hbm, xys, ybuf, off_ref,
                wg_buf, wu_buf, wd_buf, xin, xout, wsem, isem, osem, *, tile, chunk, n_steps):
    i = pl.program_id(0)
    group = SUBLANES * PACK_SLABS
    half = PACK_SLABS * LANES
    weights = ((wg_hbm, wg_buf), (wu_hbm, wu_buf), (wd_hbm, wd_buf))

    def weight_copies(expert, into):
        copies = []
        for k, (hbm, buf) in enumerate(weights):
            part = buf.shape[1] // MOE_WEIGHT_SPLIT
            for q in range(MOE_WEIGHT_SPLIT):
                copies.append(pltpu.make_async_copy(hbm.at[expert, pl.ds(q * part, part)],
                                                    buf.at[into, pl.ds(q * part, part)], wsem.at[k, into]))
        return copies

    @pl.when(i == 0)
    def _():
        for ahead in range(MOE_WEIGHT_BUFFERS - 1):
            for cp in weight_copies(ahead % MOE_EXPERTS, ahead):
                cp.start()

    n_chunks = tile // chunk
    rolled = lambda trips: trips + jnp.minimum(i, 0)

    def in_buf(c):
        return pl.multiple_of((c % 2) * (chunk * PACK_SLABS), group)

    def out_buf(c):
        return pl.multiple_of((c % 2) * (chunk * OUT_SLABS), SUBLANES)

    def in_copy(c):
        row0 = pl.multiple_of((i * tile + c * chunk) * PACK_SLABS, group)
        return pltpu.make_async_copy(xp_hbm.at[pl.ds(row0, chunk * PACK_SLABS)],
                                     xin.at[pl.ds(in_buf(c), chunk * PACK_SLABS)], isem.at[c % 2])

    def out_copy(c):
        row0 = pl.multiple_of((i * tile + c * chunk) * OUT_SLABS, SUBLANES)
        return pltpu.make_async_copy(xout.at[pl.ds(out_buf(c), chunk * OUT_SLABS)],
                                     out_hbm.at[pl.ds(row0, chunk * OUT_SLABS)], osem.at[c % 2])

    def sort_rows():
        in_copy(0).start()

        def offsets(e, acc):
            padded = _segment_rows(cnt_ref[i * MOE_EXPERTS + e])
            off_ref[e] = acc
            end = acc + padded

            @pl.when(padded > 0)
            def _():
                xys[pl.ds(pl.multiple_of(end - group, group), group), :] = jnp.zeros((group, LANES), jnp.uint32)
            return end

        total = lax.fori_loop(0, MOE_EXPERTS, offsets, 0)
        off_ref[MOE_EXPERTS] = total
        tail = MOE_BLOCKS[-1] * PACK_SLABS
        xys[pl.ds(pl.multiple_of(total, group), tail), :] = jnp.zeros((tail, LANES), jnp.uint32)

        def distribute_chunk(c, carry):
            @pl.when(c + 1 < n_chunks)
            def _():
                in_copy(c + 1).start()
            in_copy(c).wait()

            g0 = c * (chunk // SUBLANES)
            shift = in_buf(c) - c * (chunk * PACK_SLABS)

            def distribute(tg, carry2):
                for u in range(SUBLANES):
                    t = tg * SUBLANES + u
                    d1 = pl.multiple_of(tab_ref[2 * tile + t], PACK_SLABS)
                    d2 = pl.multiple_of(tab_ref[3 * tile + t], PACK_SLABS)
                    row = xin[pl.ds(pl.multiple_of(shift + t * PACK_SLABS, PACK_SLABS), PACK_SLABS), :]
                    xys[pl.ds(d1, PACK_SLABS), :] = row
                    xys[pl.ds(d2, PACK_SLABS), :] = row
                return carry2

            lax.fori_loop(g0, g0 + rolled(chunk // SUBLANES), distribute, 0)
            return carry

        lax.fori_loop(0, rolled(n_chunks), distribute_chunk, 0)

    def expert_block(refs, base, rows, valid):
        wg_ref, wu_ref, wd_ref = refs
        words = [xys[pl.ds(base + c, rows, stride=PACK_SLABS), :] for c in range(PACK_SLABS)]
        parts = [_unpack_pair(w) for w in words]
        x_hi = jnp.concatenate([p[0].astype(BF16) for p in parts], axis=1)
        x_lo = jnp.concatenate([p[1].astype(BF16) for p in parts], axis=1)
        hg = _dot(x_hi, wg_ref[0:half, :]) + _dot(x_lo, wg_ref[half:2 * half, :])
        hu = _dot(x_hi, wu_ref[0:half, :]) + _dot(x_lo, wu_ref[half:2 * half, :])
        y = _dot((hg * _sigmoid(hg) * hu).astype(BF16), wd_ref[...])
        for c in range(PACK_SLABS):
            ybuf[pl.ds(c, rows, stride=PACK_SLABS), :] = _pack_pair(
                y[:, c * LANES:(c + 1) * LANES], y[:, half + c * LANES:half + (c + 1) * LANES])

        def copy(g, c):
            r = pl.multiple_of(g * group, group)
            xys[pl.ds(base + r, group), :] = ybuf[pl.ds(r, group), :]
            return c

        lax.fori_loop(0, valid // group, copy, 0)

    def expert_step(j, carry):
        step = i * MOE_EXPERTS + j
        slot = step % MOE_WEIGHT_BUFFERS
        nxt = step + MOE_WEIGHT_BUFFERS - 1

        @pl.when(nxt < n_steps)
        def _():
            for cp in weight_copies(nxt % MOE_EXPERTS, nxt % MOE_WEIGHT_BUFFERS):
                cp.start()

        for cp in weight_copies(j, slot):
            cp.wait()
        refs = (wg_buf.at[slot], wu_buf.at[slot], wd_buf.at[slot])
        start = off_ref[j]
        n = off_ref[j + 1] - start
        lo = 0
        for rows in MOE_BLOCKS:
            hi = rows * PACK_SLABS

            @pl.when(jnp.logical_and(n > lo, n <= hi))
            def _(rows=rows):
                expert_block(refs, pl.multiple_of(start, group), rows, n)
            lo = hi

        @pl.when(n > lo)
        def _():
            span = MOE_BLOCKS[0] * PACK_SLABS

            def block(bi, carry2):
                expert_block(refs, pl.multiple_of(start + bi * span, group), MOE_BLOCKS[0],
                             jnp.minimum(span, n - bi * span))
                return carry2

            lax.fori_loop(0, (n + span - 1) // span, block, 0)
        return carry

    def combine_rows():
        def combine_chunk(c, carry):
            @pl.when(c >= 2)
            def _():
                out_copy(c - 2).wait()

            g0 = c * (chunk // SUBLANES)
            shift = out_buf(c) - c * (chunk * OUT_SLABS)

            def combine(tg, carry2):
                for u in range(SUBLANES):
                    t = tg * SUBLANES + u
                    g1 = lax.bitcast_convert_type(tab_ref[4 * tile + t], F32)
                    g2 = lax.bitcast_convert_type(tab_ref[5 * tile + t], F32)
                    a_hi, a_lo = _unpack_pair(
                        xys[pl.ds(pl.multiple_of(tab_ref[2 * tile + t], PACK_SLABS), PACK_SLABS), :])
                    b_hi, b_lo = _unpack_pair(
                        xys[pl.ds(pl.multiple_of(tab_ref[3 * tile + t], PACK_SLABS), PACK_SLABS), :])
                    xout[pl.ds(pl.multiple_of(shift + t * OUT_SLABS, OUT_SLABS), OUT_SLABS), :] = jnp.concatenate(
                        [g1 * a_hi + g2 * b_hi, g1 * a_lo + g2 * b_lo], axis=0)
                return carry2

            lax.fori_loop(g0, g0 + rolled(chunk // SUBLANES), combine, 0)
            out_copy(c).start()
            return carry

        lax.fori_loop(0, rolled(n_chunks), combine_chunk, 0)
        for c in range(max(0, n_chunks - 2), n_chunks):
            out_copy(c).wait()

    sort_rows()
    lax.fori_loop(0, rolled(MOE_EXPERTS), expert_step, 0)
    combine_rows()


def _moe(xp, tab, cnt, wg, wu, wd, tile):
    t = xp.shape[0] // PACK_SLABS
    d = D_MODEL
    n_tiles = t // tile
    chunk = min(MOE_CHUNK, tile)
    seg_rows = 2 * tile + MOE_EXPERTS * SUBLANES + MOE_BLOCKS[-1]
    grid_spec = pltpu.PrefetchScalarGridSpec(
        num_scalar_prefetch=1,
        grid=(n_tiles,),
        in_specs=[pl.BlockSpec((SUBLANES * tile,), lambda i, c: (i,), memory_space=pltpu.SMEM,
                               pipeline_mode=pl.Buffered(1)),
                  pl.BlockSpec(memory_space=pl.ANY), pl.BlockSpec(memory_space=pl.ANY),
                  pl.BlockSpec(memory_space=pl.ANY), pl.BlockSpec(memory_space=pl.ANY)],
        out_specs=pl.BlockSpec(memory_space=pl.ANY),
        scratch_shapes=[pltpu.VMEM((seg_rows * PACK_SLABS, LANES), jnp.uint32),
                        pltpu.VMEM((MOE_BLOCKS[-1] * PACK_SLABS, LANES), jnp.uint32),
                        pltpu.SMEM((LANES,), jnp.int32),
                        pltpu.VMEM((MOE_WEIGHT_BUFFERS, d, MOE_D_FF), BF16),
                        pltpu.VMEM((MOE_WEIGHT_BUFFERS, d, MOE_D_FF), BF16),
                        pltpu.VMEM((MOE_WEIGHT_BUFFERS, MOE_D_FF, d), BF16),
                        pltpu.VMEM((2 * chunk * PACK_SLABS, LANES), jnp.uint32),
                        pltpu.VMEM((2 * chunk * OUT_SLABS, LANES), F32),
                        pltpu.SemaphoreType.DMA((3, MOE_WEIGHT_BUFFERS)),
                        pltpu.SemaphoreType.DMA((2,)), pltpu.SemaphoreType.DMA((2,))],
    )
    return pl.pallas_call(
        functools.partial(_moe_kernel, tile=tile, chunk=chunk, n_steps=n_tiles * MOE_EXPERTS),
        grid_spec=grid_spec,
        out_shape=jax.ShapeDtypeStruct((t * OUT_SLABS, LANES), F32),
        compiler_params=pltpu.CompilerParams(dimension_semantics=("arbitrary",),
                                             vmem_limit_bytes=MOE_VMEM_LIMIT),
        name="moe_experts",
    )(cnt, tab, xp, wg, wu, wd)


def _final_kernel(x1_ref, moe_ref, p_ref, g_ref, b_ref, wpg_ref, wpp_ref, out_ref, *, alpha):
    rows = x1_ref.shape[0]
    moe = jnp.concatenate([moe_ref[pl.ds(c, rows, stride=OUT_SLABS), :] for c in range(OUT_SLABS)], axis=1)
    x2 = _layer_norm(alpha * x1_ref[...] + moe, g_ref[...], b_ref[...])
    gate = _sigmoid(_dot(x2.astype(BF16), wpg_ref[...]))
    out_ref[...] = x2 + gate * _dot(p_ref[...].astype(BF16), wpp_ref[...])


def _final(x1, moe, p2d, g, b, wpg, wpp, alpha):
    t, d = x1.shape
    rows = DENSE_ROWS
    const = lambda shape: pl.BlockSpec(shape, lambda i: (0,) * len(shape))
    rowblk = lambda w: pl.BlockSpec((rows, w), lambda i: (i, 0))
    return pl.pallas_call(
        functools.partial(_final_kernel, alpha=alpha),
        grid=(t // rows,),
        in_specs=[rowblk(d), pl.BlockSpec((rows * OUT_SLABS, LANES), lambda i: (i, 0)), rowblk(p2d.shape[1]),
                  const(g.shape), const(b.shape), const(wpg.shape), const(wpp.shape)],
        out_specs=rowblk(d),
        out_shape=jax.ShapeDtypeStruct((t, d), F32),
        compiler_params=pltpu.CompilerParams(dimension_semantics=("arbitrary",), vmem_limit_bytes=VMEM_LIMIT),
        name="final_ln_ple",
    )(x1, moe, p2d, g, b, wpg, wpp)


def _pad_lanes(w, width=LANES):
    return jnp.pad(w, ((0, 0), (0, width - w.shape[1])))


def _row(v, width=None):
    v = v.astype(F32).reshape(1, -1)
    return v if width is None else _pad_lanes(v, width)


def kernel(x, p, w_in, ssm_conv_w, ssm_conv_b, ssm_dt_bias, ssm_a_log, ssm_d, ssm_norm_w, lstm_i_bias, lstm_f_bias, lstm_norm_w, w_branch_ssm, w_branch_lstm, w_out, ln1_g, ln1_b, moe_w_group, moe_b_group, moe_w_expert, moe_b_expert, moe_w_gate, moe_w_up, moe_w_down, ln2_g, ln2_b, ple_w_proj, ple_w_gate):
    depth = w_in.shape[0]
    bsz, seq, d = x.shape
    t = bsz * seq
    alpha = (2.0 * depth) ** 0.25
    head_expand = (jnp.arange(LANES, dtype=jnp.int32)[:, None]
                   == jnp.arange(SSM_D_INNER, dtype=jnp.int32)[None, :] // SSM_HEAD_DIM).astype(BF16)
    for i in range(depth):
        cols = [0]
        for sz in IN_PROJ_SIZES:
            cols.append(cols[-1] + sz)
        part = lambda k: w_in[i][:, cols[k]:cols[k + 1]]
        bf = lambda w: w.astype(BF16)
        y_ssm = _ssd_mixer(
            x, bf(w_in[i][:, 0:cols[2] + LANES]),
            ssm_conv_w[i].astype(F32), _row(ssm_conv_b[i]), _row(ssm_dt_bias[i], LANES),
            _row(-jnp.exp(ssm_a_log[i].astype(F32)), LANES),
            _row(jnp.repeat(ssm_d[i].astype(F32), SSM_HEAD_DIM)), _row(ssm_norm_w[i]), head_expand)
        w_lstm = jnp.concatenate([part(3) * (LSTM_QK ** -0.5), part(4), part(5), part(6),
                                  _pad_lanes(part(7)), _pad_lanes(part(8))], axis=1)
        b_if = jnp.concatenate([_row(lstm_i_bias[i], LANES), _row(lstm_f_bias[i], LANES)], axis=1)
        y_lstm = _mlstm_mixer(x, bf(w_lstm), b_if, _row(lstm_norm_w[i]))
        w_r = _pad_lanes(jnp.concatenate([moe_w_group[i], moe_w_expert[i]], axis=1).astype(F32))
        w_r_hi = w_r.astype(BF16)
        w_r_lo = (w_r - w_r_hi.astype(F32)).astype(BF16)
        b_r = _row(jnp.concatenate([moe_b_group[i], moe_b_expert[i]]), LANES)
        tile = min(MOE_TILE, t)
        x1, xp, tab, cnt, e_gate, e_up, e_down = _merge(
            x.reshape(t, d), y_ssm.reshape(t, -1), y_lstm.reshape(t, -1),
            bf(w_in[i][:, cols[9]:cols[11]]), bf(w_branch_ssm[i]), bf(w_branch_lstm[i]), bf(w_out[i]),
            _row(ln1_g[i]), _row(ln1_b[i]), w_r_hi, w_r_lo, b_r,
            moe_w_gate[i].astype(F32), moe_w_up[i].astype(F32), moe_w_down[i].astype(F32), alpha, tile)
        cnt = cnt.reshape(t // tile, SUBLANES, LANES)[:, 0, :MOE_EXPERTS].reshape(-1)
        moe = _moe(xp, _route_rows(tab, cnt, tile).reshape(-1), cnt, e_gate, e_up, e_down, tile)
        x = _final(x1, moe, p[i].reshape(t, -1), _row(ln2_g[i]), _row(ln2_b[i]),
                   bf(ple_w_gate[i]), bf(ple_w_proj[i]), alpha).reshape(bsz, seq, d)
    return x
```

```python
import functools

import jax
import jax.numpy as jnp
from jax import lax
from jax.experimental import pallas as pl
from jax.experimental.pallas import tpu as pltpu

F32 = jnp.float32
BF16 = jnp.bfloat16

D_MODEL = 1024
PLE_DIM = 256
SSM_D_INNER = 1024
SSM_HEAD_DIM = 64
SSM_HEADS = 16
SSM_GROUPS = 4
SSM_STATE = 128
SSM_CONV = 4
SSM_XBC = SSM_D_INNER + 2 * SSM_GROUPS * SSM_STATE
LSTM_HEADS = 8
LSTM_QK = 64
LSTM_V = 128
LSTM_D_QK = LSTM_HEADS * LSTM_QK
LSTM_D_V = LSTM_HEADS * LSTM_V
CHUNK = 128
MOE_GROUPS = 8
MOE_PER_GROUP = 8
MOE_EXPERTS = 64
MOE_D_FF = 512
NORM_EPS = 1e-5
IN_PROJ_SIZES = (SSM_D_INNER, SSM_XBC, SSM_HEADS, LSTM_D_QK, LSTM_D_QK, LSTM_D_V, LSTM_D_V,
                 LSTM_HEADS, LSTM_HEADS, D_MODEL, D_MODEL)

LANES = 128
SUBLANES = 8
VMEM_LIMIT = 56 * 1024 * 1024
MOE_VMEM_LIMIT = 60 * 1024 * 1024

MIX_ROWS = 256
MIX_SEQS = 2
DENSE_ROWS = 512
MOE_TILE = 8192
MOE_CHUNK = 256
MOE_BLOCKS = (256, 320, 384)
MOE_WEIGHT_BUFFERS = 4
MOE_WEIGHT_SPLIT = 4
PACK_SLABS = D_MODEL // 2 // LANES
OUT_SLABS = D_MODEL // LANES


def _dot(a, b):
    return jnp.dot(a, b, preferred_element_type=F32)


def _dot_nt(a, b):
    return lax.dot_general(a, b, (((1,), (1,)), ((), ())), preferred_element_type=F32)


def _dot_tn(a, b):
    return lax.dot_general(a, b, (((0,), (0,)), ((), ())), preferred_element_type=F32)


def _sigmoid(x):
    return 1.0 / (1.0 + jnp.exp(-x))


def _softplus(x):
    return jnp.maximum(x, 0.0) + jnp.log(1.0 + jnp.exp(-jnp.abs(x)))


def _split3(x):
    hi = x.astype(BF16)
    r1 = x - hi.astype(F32)
    mid = r1.astype(BF16)
    lo = (r1 - mid.astype(F32)).astype(BF16)
    return hi, mid, lo


def _cumsum_rows(tril, x):
    hi, mid, lo = _split3(x)
    return _dot(tril, hi) + _dot(tril, mid) + _dot(tril, lo)


def _causal_masks():
    r = lax.broadcasted_iota(jnp.int32, (CHUNK, CHUNK), 0)
    c = lax.broadcasted_iota(jnp.int32, (CHUNK, CHUNK), 1)
    causal = r >= c
    return causal, jnp.where(causal, 1.0, 0.0).astype(BF16)


def _pair_cols(v, ha, hb, rows, lane):
    a = jnp.broadcast_to(v[:, ha:ha + 1], (rows, LANES))
    b = jnp.broadcast_to(v[:, hb:hb + 1], (rows, LANES))
    return jnp.where(lane < SSM_HEAD_DIM, a, b)


def _ssd_kernel(x_ref, w_ref, cw_ref, cb_ref, dtb_ref, aneg_ref, dexp_ref, nw_ref, ex_ref,
                y_ref, ext_ref, u_ref, st_ref, *, rows, nseq):
    n_slab = SSM_XBC // LANES
    tail = SUBLANES - SSM_CONV + 1

    @pl.when(pl.program_id(1) == 0)
    def _():
        ext_ref[:, :, 0:SUBLANES, :] = jnp.zeros((nseq, n_slab, SUBLANES, LANES), F32)
        st_ref[...] = jnp.zeros(st_ref.shape, F32)

    xb = x_ref[...].reshape(nseq * rows, x_ref.shape[-1]).astype(BF16)
    z = _dot(xb, w_ref[:, 0:SSM_D_INNER])
    dt_col = SSM_D_INNER + SSM_XBC
    dtr = _dot(xb, w_ref[:, dt_col:dt_col + LANES])
    dtr = jnp.where(lax.broadcasted_iota(jnp.int32, dtr.shape, 1) < SSM_HEADS, dtr, 0.0)
    for c2 in range(n_slab // 2):
        xbc = _dot(xb, w_ref[:, SSM_D_INNER + c2 * 2 * LANES:SSM_D_INNER + (c2 + 1) * 2 * LANES])
        for s in range(nseq):
            for cc in range(2):
                ext_ref[s, 2 * c2 + cc, SUBLANES:SUBLANES + rows, :] = (
                    xbc[s * rows:(s + 1) * rows, cc * LANES:(cc + 1) * LANES])
    for s in range(nseq):
        for c in range(n_slab):
            conv = cb_ref[:, c * LANES:(c + 1) * LANES]
            for k in range(SSM_CONV):
                conv = conv + cw_ref[k:k + 1, c * LANES:(c + 1) * LANES] * ext_ref[s, c, pl.ds(tail + k, rows), :]
            u_ref[s, c] = conv * _sigmoid(conv)
            ext_ref[s, c, 0:SUBLANES, :] = ext_ref[s, c, rows:rows + SUBLANES, :]

    dt = _softplus(dtr + dtb_ref[...])
    da = dt * aneg_ref[...]
    causal, tril = _causal_masks()
    lane = lax.broadcasted_iota(jnp.int32, (CHUNK, LANES), 1)
    lane1 = lax.broadcasted_iota(jnp.int32, (1, LANES), 1)
    gw = SSM_D_INNER // SSM_GROUPS
    b_slab = SSM_D_INNER // LANES
    c_slab = b_slab + SSM_GROUPS

    for ci in range(rows // CHUNK):
        for s in range(nseq):
            r0 = ci * CHUNK
            q0 = s * rows + r0
            dt_c = dt[q0:q0 + CHUNK]
            a_cs = _cumsum_rows(tril, da[q0:q0 + CHUNK])
            a_cs_t = a_cs.T
            dt_t = dt_c.T
            a_last = a_cs[CHUNK - 1:CHUNK, :]
            ea_last = jnp.exp(a_last)
            ea_x = _dot(jnp.exp(a_cs).astype(BF16), ex_ref[...])
            sw_x = _dot((dt_c * jnp.exp(a_last - a_cs)).astype(BF16), ex_ref[...])
            for g in range(SSM_GROUPS):
                bm_g = u_ref[s, b_slab + g, r0:r0 + CHUNK, :].astype(BF16)
                cm_g = u_ref[s, c_slab + g, r0:r0 + CHUNK, :].astype(BF16)
                cb = _dot_nt(cm_g, bm_g)
                st_g = st_ref[s, :, g * gw:(g + 1) * gw]
                y_off = _dot(cm_g, st_g.astype(BF16))
                yy, xw, eal_x = [], [], []
                for pr in range(2):
                    ha = 4 * g + 2 * pr
                    hb = ha + 1
                    lo = g * gw + pr * LANES
                    xs_p = u_ref[s, 2 * g + pr, r0:r0 + CHUNK, :]
                    xs_pb = xs_p.astype(BF16)
                    rhs = jnp.concatenate([jnp.where(lane < SSM_HEAD_DIM, xs_pb, 0).astype(BF16),
                                           jnp.where(lane >= SSM_HEAD_DIM, xs_pb, 0).astype(BF16)], axis=0)
                    gs = []
                    for h in (ha, hb):
                        colb = jnp.broadcast_to(a_cs[:, h:h + 1], (CHUNK, CHUNK))
                        rowb = jnp.broadcast_to(a_cs_t[h:h + 1, :], (CHUNK, CHUNK))
                        dec = jnp.exp(jnp.where(causal, colb - rowb, -jnp.inf))
                        gs.append((cb * dec * jnp.broadcast_to(dt_t[h:h + 1, :], (CHUNK, CHUNK))).astype(BF16))
                    y_p = (_dot(jnp.concatenate(gs, axis=1), rhs)
                           + y_off[:, pr * LANES:(pr + 1) * LANES] * ea_x[:, lo:lo + LANES]
                           + dexp_ref[:, lo:lo + LANES] * xs_p)
                    zz = z[q0:q0 + CHUNK, lo:lo + LANES]
                    yy.append(y_p * (zz * _sigmoid(zz)))
                    xw.append((xs_p * sw_x[:, lo:lo + LANES]).astype(BF16))
                    eal_x.append(_pair_cols(ea_last, ha, hb, 1, lane1))
                st_ref[s, :, g * gw:(g + 1) * gw] = (st_g * jnp.concatenate(eal_x, axis=1)
                                                     + _dot_tn(bm_g, jnp.concatenate(xw, axis=1)))
                ms = sum(jnp.sum(t * t, axis=-1, keepdims=True) for t in yy) * (1.0 / gw)
                inv = lax.rsqrt(ms + NORM_EPS)
                for pr in range(2):
                    lo = g * gw + pr * LANES
                    y_ref[s, r0:r0 + CHUNK, lo:lo + LANES] = (yy[pr] * inv * nw_ref[:, lo:lo + LANES]).astype(BF16)


def _ssd_mixer(x, w, cw, cb, dtb, aneg, dexp, nw, ex):
    b, s, d = x.shape
    rows = MIX_ROWS
    nseq = MIX_SEQS
    const = lambda shape: pl.BlockSpec(shape, lambda i, j: (0,) * len(shape), pipeline_mode=pl.Buffered(1))
    return pl.pallas_call(
        functools.partial(_ssd_kernel, rows=rows, nseq=nseq),
        grid=(b // nseq, s // rows),
        in_specs=[pl.BlockSpec((nseq, rows, d), lambda i, j: (i, j, 0)),
                  const(w.shape), const(cw.shape), const(cb.shape),
                  const(dtb.shape), const(aneg.shape), const(dexp.shape), const(nw.shape), const(ex.shape)],
        out_specs=pl.BlockSpec((nseq, rows, SSM_D_INNER), lambda i, j: (i, j, 0)),
        out_shape=jax.ShapeDtypeStruct((b, s, SSM_D_INNER), BF16),
        scratch_shapes=[pltpu.VMEM((nseq, SSM_XBC // LANES, SUBLANES + rows, LANES), F32),
                        pltpu.VMEM((nseq, SSM_XBC // LANES, rows, LANES), F32),
                        pltpu.VMEM((nseq, SSM_STATE, SSM_D_INNER), F32)],
        compiler_params=pltpu.CompilerParams(dimension_semantics=("arbitrary", "arbitrary"),
                                             vmem_limit_bytes=VMEM_LIMIT),
        name="ssd_mixer",
    )(x, w, cw, cb, dtb, aneg, dexp, nw, ex)


def _mlstm_kernel(x_ref, w_ref, bif_ref, nw_ref,
                  y_ref, c_ref, m_ref, *, rows, nseq):
    @pl.when(pl.program_id(1) == 0)
    def _():
        c_ref[...] = jnp.zeros(c_ref.shape, F32)
        m_ref[...] = jnp.full(m_ref.shape, -jnp.inf, F32)

    xb = x_ref[...].reshape(nseq * rows, x_ref.shape[-1]).astype(BF16)
    c_k, c_v, c_o, c_g = LSTM_D_QK, 2 * LSTM_D_QK, 2 * LSTM_D_QK + LSTM_D_V, 2 * LSTM_D_QK + 2 * LSTM_D_V
    q = _dot(xb, w_ref[:, 0:c_k])
    k = _dot(xb, w_ref[:, c_k:c_v])
    v = _dot(xb, w_ref[:, c_v:c_o])
    o = _dot(xb, w_ref[:, c_o:c_g])
    gif = _dot(xb, w_ref[:, c_g:c_g + 2 * LANES]) + bif_ref[...]
    causal, tril = _causal_masks()
    lane = lax.broadcasted_iota(jnp.int32, (CHUNK, LANES), 1)
    lane1 = lax.broadcasted_iota(jnp.int32, (1, LANES), 1)
    row = lax.broadcasted_iota(jnp.int32, (LANES, 1), 0)
    head_lane = lane < LSTM_HEADS
    ones = jnp.ones((CHUNK, LANES), BF16)

    for ci in range(rows // CHUNK):
        r0 = ci * CHUNK
        pre = {}
        for s in range(nseq):
            q0 = s * rows + r0
            log_i = gif[q0:q0 + CHUNK, 0:LANES]
            f_pre = gif[q0:q0 + CHUNK, LANES:2 * LANES]
            log_f = jnp.where(head_lane, jnp.minimum(f_pre, 0.0) - jnp.log(1.0 + jnp.exp(-jnp.abs(f_pre))), 0.0)
            fcum = _cumsum_rows(tril, log_f)
            f_tot = fcum[CHUNK - 1:CHUNK, :]
            gk = log_i - fcum
            m_loc = jnp.max(f_tot + gk, axis=0, keepdims=True)
            g_t = gk.T
            m_prev = m_ref[s]
            m_new = jnp.maximum(f_tot + m_prev, m_loc)
            s_prev = jnp.exp(f_tot + m_prev - m_new)
            s_loc = jnp.exp(m_loc - m_new)
            m_ref[s] = jnp.where(lane1 < LSTM_HEADS, m_new, -jnp.inf)
            w_shift = f_tot - m_loc
            pre[s] = (q0, fcum, gk, g_t, m_prev, s_prev, s_loc, w_shift)

        for pr in range(LSTM_HEADS // 2):
            st = {}
            for s in range(nseq):
                q0, fcum, gk, g_t, m_prev, s_prev, s_loc, w_shift = pre[s]
                q_p = q[q0:q0 + CHUNK, pr * LANES:(pr + 1) * LANES]
                k_p = k[q0:q0 + CHUNK, pr * LANES:(pr + 1) * LANES]
                k_pb = k_p.astype(BF16)
                c_pair = c_ref[s, pr]
                c_pair_b = c_pair.astype(BF16)
                c_new = c_pair * jnp.where(row < LSTM_QK,
                                           jnp.broadcast_to(s_prev[:, 2 * pr:2 * pr + 1], (LANES, 1)),
                                           jnp.broadcast_to(s_prev[:, 2 * pr + 1:2 * pr + 2], (LANES, 1)))
                st[s] = [q_p, k_p, k_pb, c_pair_b, c_new]
            for hh in range(2):
                for s in range(nseq):
                    q0, fcum, gk, g_t, m_prev, s_prev, s_loc, w_shift = pre[s]
                    q_p, k_p, k_pb, c_pair_b, c_new = st[s]
                    h = 2 * pr + hh
                    in_head = (lane < LSTM_QK) if hh == 0 else (lane >= LSTM_QK)
                    v_h = v[q0:q0 + CHUNK, h * LSTM_V:(h + 1) * LSTM_V].astype(BF16)
                    v_ext = jnp.concatenate([v_h, ones], axis=1)
                    q_m = jnp.where(in_head, q_p, 0.0)
                    fcol = jnp.broadcast_to(fcum[:, h:h + 1], (CHUNK, CHUNK))
                    gcol = jnp.broadcast_to(gk[:, h:h + 1], (CHUNK, LANES))
                    rowb = jnp.broadcast_to(g_t[h:h + 1, :], (CHUNK, CHUNK))
                    log_d = jnp.where(causal, fcol + rowb, -jnp.inf)
                    inter_log = fcol + m_prev[:, h:h + 1]
                    m_t = jnp.maximum(inter_log, jnp.max(log_d, axis=-1, keepdims=True))
                    scores = _dot_nt(q_m.astype(BF16), k_pb) * jnp.exp(log_d - m_t)
                    inter_w = jnp.exp(inter_log - m_t)
                    lhs = jnp.concatenate([scores.astype(BF16), (q_m * inter_w).astype(BF16)], axis=1)
                    res = _dot(lhs, jnp.concatenate([v_ext, c_pair_b], axis=0))
                    hv = res[:, 0:LSTM_V] / jnp.maximum(jnp.abs(res[:, LSTM_V:2 * LSTM_V]), jnp.exp(-m_t))
                    ms = jnp.mean(hv * hv, axis=-1, keepdims=True)
                    o_h = o[q0:q0 + CHUNK, h * LSTM_V:(h + 1) * LSTM_V]
                    y_ref[s, r0:r0 + CHUNK, h * LSTM_V:(h + 1) * LSTM_V] = (
                        _sigmoid(o_h) * (hv * lax.rsqrt(ms + NORM_EPS) * nw_ref[:, h * LSTM_V:(h + 1) * LSTM_V])
                    ).astype(BF16)
                    kw = jnp.where(in_head, k_p * jnp.exp(gcol + w_shift[:, h:h + 1]), 0.0)
                    st[s][4] = c_new + s_loc[:, h:h + 1] * _dot_tn(kw.astype(BF16), v_ext)
            for s in range(nseq):
                c_ref[s, pr] = st[s][4]


def _mlstm_mixer(x, w, bif, nw):
    b, s, d = x.shape
    rows = MIX_ROWS
    nseq = MIX_SEQS
    const = lambda shape: pl.BlockSpec(shape, lambda i, j: (0,) * len(shape), pipeline_mode=pl.Buffered(1))
    return pl.pallas_call(
        functools.partial(_mlstm_kernel, rows=rows, nseq=nseq),
        grid=(b // nseq, s // rows),
        in_specs=[pl.BlockSpec((nseq, rows, d), lambda i, j: (i, j, 0)),
                  const(w.shape), const(bif.shape), const(nw.shape)],
        out_specs=pl.BlockSpec((nseq, rows, LSTM_D_V), lambda i, j: (i, j, 0)),
        out_shape=jax.ShapeDtypeStruct((b, s, LSTM_D_V), BF16),
        scratch_shapes=[pltpu.VMEM((nseq, LSTM_HEADS // 2, 2 * LSTM_QK, 2 * LSTM_V), F32),
                        pltpu.VMEM((nseq, 1, LANES), F32)],
        compiler_params=pltpu.CompilerParams(dimension_semantics=("arbitrary", "arbitrary"),
                                             vmem_limit_bytes=VMEM_LIMIT),
        name="mlstm_mixer",
    )(x, w, bif, nw)


def _layer_norm(t, g, b):
    mu = jnp.mean(t, axis=-1, keepdims=True)
    tc = t - mu
    var = jnp.mean(tc * tc, axis=-1, keepdims=True)
    return tc * lax.rsqrt(var + NORM_EPS) * g + b


def _pack_pair(a, b):
    pa = lax.bitcast_convert_type(a.astype(BF16).astype(F32), jnp.uint32)
    pb = lax.bitcast_convert_type(b.astype(BF16).astype(F32), jnp.uint32)
    return pa | (pb >> 16)


def _unpack_pair(w):
    return (lax.bitcast_convert_type(w & jnp.uint32(0xFFFF0000), F32),
            lax.bitcast_convert_type(w << 16, F32))


def _route(lt):
    n = lt.shape[1]
    sub = lax.broadcasted_iota(jnp.int32, (MOE_PER_GROUP, n), 0).astype(F32)
    big = float(LANES)
    neg = -jnp.inf
    gl = lt[0:MOE_GROUPS, :]
    gmax = jnp.max(gl, axis=0, keepdims=True)
    gidx = jnp.min(jnp.where(gl == gmax, sub, big), axis=0, keepdims=True)
    grp_p = 1.0 / jnp.sum(jnp.exp(gl - gmax), axis=0, keepdims=True)
    el = lt[MOE_GROUPS:MOE_GROUPS + MOE_PER_GROUP, :]
    for g in range(1, MOE_GROUPS):
        lo = MOE_GROUPS + g * MOE_PER_GROUP
        el = jnp.where(gidx == g, lt[lo:lo + MOE_PER_GROUP, :], el)
    m1 = jnp.max(el, axis=0, keepdims=True)
    i1 = jnp.min(jnp.where(el == m1, sub, big), axis=0, keepdims=True)
    el2 = jnp.where(sub == i1, neg, el)
    m2 = jnp.max(el2, axis=0, keepdims=True)
    i2 = jnp.min(jnp.where(el2 == m2, sub, big), axis=0, keepdims=True)
    e21 = jnp.exp(m2 - m1)
    g1 = grp_p / (1.0 + e21)
    g2 = grp_p * e21 / (1.0 + e21)
    return gidx * MOE_PER_GROUP + i1, gidx * MOE_PER_GROUP + i2, g1, g2


def _merge_kernel(x_ref, ys_ref, yl_ref, wg_ref, wa_ref, wb_ref, wo_ref, g_ref, b_ref,
                  wr_ref, br_ref, eg_ref, eu_ref, ed_ref,
                  x1_ref, xp_ref, tab_ref, cnt_ref, egb_ref, eub_ref, edb_ref, carry_ref,
                  *, alpha, steps_per_tile, cast_steps):
    @pl.when(pl.program_id(0) % steps_per_tile == 0)
    def _():
        carry_ref[...] = jnp.zeros(carry_ref.shape, F32)

    @pl.when(pl.program_id(0) < cast_steps)
    def _():
        egb_ref[...] = eg_ref[...].astype(BF16)
        eub_ref[...] = eu_ref[...].astype(BF16)
        edb_ref[...] = ed_ref[...].astype(BF16)

    x = x_ref[...]
    rows = x.shape[0]
    xb = x.astype(BF16)
    d = x.shape[1]
    merged = (_sigmoid(_dot(xb, wg_ref[:, 0:d])) * _dot(ys_ref[...], wa_ref[...])
              + _sigmoid(_dot(xb, wg_ref[:, d:2 * d])) * _dot(yl_ref[...], wb_ref[...]))
    x1 = _layer_norm(alpha * x + _dot(merged.astype(BF16), wo_ref[...]), g_ref[...], b_ref[...])
    x1_ref[...] = x1
    half = x1.shape[1] // 2
    xp = _pack_pair(x1[:, :half], x1[:, half:])
    for c in range(PACK_SLABS):
        xp_ref[pl.ds(c, rows, stride=PACK_SLABS), :] = xp[:, c * LANES:(c + 1) * LANES]
    x_hi = x1.astype(BF16)
    x_lo = (x1 - x_hi.astype(F32)).astype(BF16)
    by_hi = _dot_nt(wr_ref[...], x_hi)
    lt = by_hi[0:LANES] + by_hi[LANES:2 * LANES] + _dot_nt(wr_ref[0:LANES, :], x_lo) + br_ref[...]
    e1, e2, g1, g2 = _route(lt)
    expert = lax.broadcasted_iota(jnp.int32, (LANES, rows), 0).astype(F32)
    oh1 = expert == e1
    oh2 = expert == e2
    ohs = jnp.where(oh1, 1.0, jnp.where(oh2, 1.0, 0.0))
    r_i = lax.broadcasted_iota(jnp.int32, (rows, rows), 0)
    c_i = lax.broadcasted_iota(jnp.int32, (rows, rows), 1)
    before = jnp.where(r_i < c_i, 1.0, 0.0).astype(BF16)
    carry = carry_ref[...]
    seen = _dot(ohs.astype(BF16), before) + jnp.concatenate([carry] * (rows // LANES), axis=1)
    r1 = jnp.sum(jnp.where(oh1, seen, 0.0), axis=0, keepdims=True)
    r2 = jnp.sum(jnp.where(oh2, seen, 0.0), axis=0, keepdims=True)
    carry = carry + jnp.sum(ohs, axis=1, keepdims=True)
    carry_ref[...] = carry
    cnt_ref[...] = carry.astype(jnp.int32)
    as_int = lambda v: v.astype(jnp.int32)
    as_bits = lambda v: lax.bitcast_convert_type(v, jnp.int32)
    tab_ref[...] = jnp.concatenate(
        [as_int(e1), as_int(e2), as_int(r1) * PACK_SLABS, as_int(r2) * PACK_SLABS, as_bits(g1), as_bits(g2),
         jnp.zeros((2, rows), jnp.int32)], axis=0)


def _merge(x2d, ys, yl, wg, wa, wb, wo, g, b, wr, br, eg, eu, ed, alpha, tile):
    t, d = x2d.shape
    rows = DENSE_ROWS
    spt = tile // rows
    n_steps = t // rows
    per_step = -(-MOE_EXPERTS // n_steps)
    cast_steps = MOE_EXPERTS // per_step
    assert per_step * cast_steps == MOE_EXPERTS and cast_steps <= n_steps
    const = lambda shape: pl.BlockSpec(shape, lambda i: (0,) * len(shape), pipeline_mode=pl.Buffered(1))
    rowblk = lambda w: pl.BlockSpec((rows, w), lambda i: (i, 0))
    expert = lambda w: pl.BlockSpec((per_step,) + w.shape[1:], lambda i: (jnp.minimum(i, cast_steps - 1), 0, 0))
    return pl.pallas_call(
        functools.partial(_merge_kernel, alpha=alpha, steps_per_tile=spt, cast_steps=cast_steps),
        grid=(n_steps,),
        in_specs=[rowblk(d), rowblk(SSM_D_INNER), rowblk(LSTM_D_V),
                  const(wg.shape), const(wa.shape), const(wb.shape), const(wo.shape),
                  const(g.shape), const(b.shape), const(wr.shape), const(br.shape),
                  expert(eg), expert(eu), expert(ed)],
        out_specs=[rowblk(d),
                   pl.BlockSpec((rows * PACK_SLABS, LANES), lambda i: (i, 0)),
                   pl.BlockSpec((SUBLANES, rows), lambda i: (i // spt, i % spt)),
                   pl.BlockSpec((LANES, LANES), lambda i: (i // spt, 0)),
                   expert(eg), expert(eu), expert(ed)],
        out_shape=[jax.ShapeDtypeStruct((t, d), F32),
                   jax.ShapeDtypeStruct((t * PACK_SLABS, LANES), jnp.uint32),
                   jax.ShapeDtypeStruct((t // tile * SUBLANES, tile), jnp.int32),
                   jax.ShapeDtypeStruct((t // tile * LANES, LANES), jnp.int32),
                   jax.ShapeDtypeStruct(eg.shape, BF16), jax.ShapeDtypeStruct(eu.shape, BF16),
                   jax.ShapeDtypeStruct(ed.shape, BF16)],
        scratch_shapes=[pltpu.VMEM((LANES, LANES), F32)],
        compiler_params=pltpu.CompilerParams(dimension_semantics=("arbitrary",), vmem_limit_bytes=VMEM_LIMIT),
        name="merge_ln_route",
    )(x2d, ys, yl, wg, wa, wb, wo, g, b, wr, br, eg, eu, ed)


def _segment_rows(count):
    return lax.shift_right_logical(count + (SUBLANES - 1), 3) * (SUBLANES * PACK_SLABS)


def _route_rows_kernel(cnt_ref, tab_ref, out_ref):
    i = pl.program_id(0)
    tab = tab_ref[...]
    starts = jnp.zeros(tab.shape, jnp.int32)
    off = jnp.int32(0)
    for e in range(MOE_EXPERTS):
        starts = jnp.where(tab == e, off, starts)
        off = off + _segment_rows(cnt_ref[i * MOE_EXPERTS + e])
    row = lax.broadcasted_iota(jnp.int32, tab.shape, 0)
    res = jnp.where(lax.shift_right_logical(row, 1) == 1, tab + pltpu.roll(starts, 2, axis=0), tab)
    blocks = tab.shape[1] // LANES
    for c in range(blocks):
        out_ref[pl.ds(c, SUBLANES, stride=blocks), :] = res[:, c * LANES:(c + 1) * LANES]


def _route_rows(tab, cnt, tile):
    n_tiles = tab.shape[0] // SUBLANES
    rows = SUBLANES * tile // LANES
    return pl.pallas_call(
        _route_rows_kernel,
        grid_spec=pltpu.PrefetchScalarGridSpec(
            num_scalar_prefetch=1, grid=(n_tiles,),
            in_specs=[pl.BlockSpec((SUBLANES, tile), lambda i, c: (i, 0))],
            out_specs=pl.BlockSpec((rows, LANES), lambda i, c: (i, 0))),
        out_shape=jax.ShapeDtypeStruct((n_tiles * rows, LANES), jnp.int32),
        compiler_params=pltpu.CompilerParams(dimension_semantics=("arbitrary",), vmem_limit_bytes=VMEM_LIMIT),
        name="route_rows",
    )(cnt, tab)


def _moe_kernel(cnt_ref, tab_ref, xp_hbm, wg_hbm, wu_hbm, wd_hbm, out_hbm, xys, ybuf, off_ref,
                wg_buf, wu_buf, wd_buf, xin, xout, wsem, isem, osem, *, tile, chunk, n_steps):
    i = pl.program_id(0)
    group = SUBLANES * PACK_SLABS
    half = PACK_SLABS * LANES
    weights = ((wg_hbm, wg_buf), (wu_hbm, wu_buf), (wd_hbm, wd_buf))

    def weight_copies(expert, into):
        copies = []
        for k, (hbm, buf) in enumerate(weights):
            part = buf.shape[1] // MOE_WEIGHT_SPLIT
            for q in range(MOE_WEIGHT_SPLIT):
                copies.append(pltpu.make_async_copy(hbm.at[expert, pl.ds(q * part, part)],
                                                    buf.at[into, pl.ds(q * part, part)], wsem.at[k, into]))
        return copies

    @pl.when(i == 0)
    def _():
        for ahead in range(MOE_WEIGHT_BUFFERS - 1):
            for cp in weight_copies(ahead % MOE_EXPERTS, ahead):
                cp.start()

    n_chunks = tile // chunk
    rolled = lambda trips: trips + jnp.minimum(i, 0)

    def in_buf(c):
        return pl.multiple_of((c % 2) * (chunk * PACK_SLABS), group)

    def out_buf(c):
        return pl.multiple_of((c % 2) * (chunk * OUT_SLABS), SUBLANES)

    def in_copy(c):
        row0 = pl.multiple_of((i * tile + c * chunk) * PACK_SLABS, group)
        return pltpu.make_async_copy(xp_hbm.at[pl.ds(row0, chunk * PACK_SLABS)],
                                     xin.at[pl.ds(in_buf(c), chunk * PACK_SLABS)], isem.at[c % 2])

    def out_copy(c):
        row0 = pl.multiple_of((i * tile + c * chunk) * OUT_SLABS, SUBLANES)
        return pltpu.make_async_copy(xout.at[pl.ds(out_buf(c), chunk * OUT_SLABS)],
                                     out_hbm.at[pl.ds(row0, chunk * OUT_SLABS)], osem.at[c % 2])

    def sort_rows():
        in_copy(0).start()

        def offsets(e, acc):
            padded = _segment_rows(cnt_ref[i * MOE_EXPERTS + e])
            off_ref[e] = acc
            end = acc + padded

            @pl.when(padded > 0)
            def _():
                xys[pl.ds(pl.multiple_of(end - group, group), group), :] = jnp.zeros((group, LANES), jnp.uint32)
            return end

        total = lax.fori_loop(0, MOE_EXPERTS, offsets, 0)
        off_ref[MOE_EXPERTS] = total
        tail = MOE_BLOCKS[-1] * PACK_SLABS
        xys[pl.ds(pl.multiple_of(total, group), tail), :] = jnp.zeros((tail, LANES), jnp.uint32)

        def distribute_chunk(c, carry):
            @pl.when(c + 1 < n_chunks)
            def _():
                in_copy(c + 1).start()
            in_copy(c).wait()

            g0 = c * (chunk // SUBLANES)
            shift = in_buf(c) - c * (chunk * PACK_SLABS)

            def distribute(tg, carry2):
                for u in range(SUBLANES):
                    t = tg * SUBLANES + u
                    d1 = pl.multiple_of(tab_ref[2 * tile + t], PACK_SLABS)
                    d2 = pl.multiple_of(tab_ref[3 * tile + t], PACK_SLABS)
                    row = xin[pl.ds(pl.multiple_of(shift + t * PACK_SLABS, PACK_SLABS), PACK_SLABS), :]
                    xys[pl.ds(d1, PACK_SLABS), :] = row
                    xys[pl.ds(d2, PACK_SLABS), :] = row
                return carry2

            lax.fori_loop(g0, g0 + rolled(chunk // SUBLANES), distribute, 0)
            return carry

        lax.fori_loop(0, rolled(n_chunks), distribute_chunk, 0)

    def expert_block(refs, base, rows, valid):
        wg_ref, wu_ref, wd_ref = refs
        words = [xys[pl.ds(base + c, rows, stride=PACK_SLABS), :] for c in range(PACK_SLABS)]
        parts = [_unpack_pair(w) for w in words]
        x_hi = jnp.concatenate([p[0].astype(BF16) for p in parts], axis=1)
        x_lo = jnp.concatenate([p[1].astype(BF16) for p in parts], axis=1)
        hg = _dot(x_hi, wg_ref[0:half, :]) + _dot(x_lo, wg_ref[half:2 * half, :])
        hu = _dot(x_hi, wu_ref[0:half, :]) + _dot(x_lo, wu_ref[half:2 * half, :])
        y = _dot((hg * _sigmoid(hg) * hu).astype(BF16), wd_ref[...])
        for c in range(PACK_SLABS):
            ybuf[pl.ds(c, rows, stride=PACK_SLABS), :] = _pack_pair(
                y[:, c * LANES:(c + 1) * LANES], y[:, half + c * LANES:half + (c + 1) * LANES])

        def copy(g, c):
            r = pl.multiple_of(g * group, group)
            xys[pl.ds(base + r, group), :] = ybuf[pl.ds(r, group), :]
            return c

        lax.fori_loop(0, valid // group, copy, 0)

    def expert_step(j, carry):
        step = i * MOE_EXPERTS + j
        slot = step % MOE_WEIGHT_BUFFERS
        nxt = step + MOE_WEIGHT_BUFFERS - 1

        @pl.when(nxt < n_steps)
        def _():
            for cp in weight_copies(nxt % MOE_EXPERTS, nxt % MOE_WEIGHT_BUFFERS):
                cp.start()

        for cp in weight_copies(j, slot):
            cp.wait()
        refs = (wg_buf.at[slot], wu_buf.at[slot], wd_buf.at[slot])
        start = off_ref[j]
        n = off_ref[j + 1] - start
        lo = 0
        for rows in MOE_BLOCKS:
            hi = rows * PACK_SLABS

            @pl.when(jnp.logical_and(n > lo, n <= hi))
            def _(rows=rows):
                expert_block(refs, pl.multiple_of(start, group), rows, n)
            lo = hi

        @pl.when(n > lo)
        def _():
            span = MOE_BLOCKS[0] * PACK_SLABS

            def block(bi, carry2):
                expert_block(refs, pl.multiple_of(start + bi * span, group), MOE_BLOCKS[0],
                             jnp.minimum(span, n - bi * span))
                return carry2

            lax.fori_loop(0, (n + span - 1) // span, block, 0)
        return carry

    def combine_rows():
        def combine_chunk(c, carry):
            @pl.when(c >= 2)
            def _():
                out_copy(c - 2).wait()

            g0 = c * (chunk // SUBLANES)
            shift = out_buf(c) - c * (chunk * OUT_SLABS)

            def combine(tg, carry2):
                for u in range(SUBLANES):
                    t = tg * SUBLANES + u
                    g1 = lax.bitcast_convert_type(tab_ref[4 * tile + t], F32)
                    g2 = lax.bitcast_convert_type(tab_ref[5 * tile + t], F32)
                    a_hi, a_lo = _unpack_pair(
                        xys[pl.ds(pl.multiple_of(tab_ref[2 * tile + t], PACK_SLABS), PACK_SLABS), :])
                    b_hi, b_lo = _unpack_pair(
                        xys[pl.ds(pl.multiple_of(tab_ref[3 * tile + t], PACK_SLABS), PACK_SLABS), :])
                    xout[pl.ds(pl.multiple_of(shift + t * OUT_SLABS, OUT_SLABS), OUT_SLABS), :] = jnp.concatenate(
                        [g1 * a_hi + g2 * b_hi, g1 * a_lo + g2 * b_lo], axis=0)
                return carry2

            lax.fori_loop(g0, g0 + rolled(chunk // SUBLANES), combine, 0)
            out_copy(c).start()
            return carry

        lax.fori_loop(0, rolled(n_chunks), combine_chunk, 0)
        for c in range(max(0, n_chunks - 2), n_chunks):
            out_copy(c).wait()

    sort_rows()
    lax.fori_loop(0, rolled(MOE_EXPERTS), expert_step, 0)
    combine_rows()


def _moe(xp, tab, cnt, wg, wu, wd, tile):
    t = xp.shape[0] // PACK_SLABS
    d = D_MODEL
    n_tiles = t // tile
    chunk = min(MOE_CHUNK, tile)
    seg_rows = 2 * tile + MOE_EXPERTS * SUBLANES + MOE_BLOCKS[-1]
    grid_spec = pltpu.PrefetchScalarGridSpec(
        num_scalar_prefetch=1,
        grid=(n_tiles,),
        in_specs=[pl.BlockSpec((SUBLANES * tile,), lambda i, c: (i,), memory_space=pltpu.SMEM,
                               pipeline_mode=pl.Buffered(1)),
                  pl.BlockSpec(memory_space=pl.ANY), pl.BlockSpec(memory_space=pl.ANY),
                  pl.BlockSpec(memory_space=pl.ANY), pl.BlockSpec(memory_space=pl.ANY)],
        out_specs=pl.BlockSpec(memory_space=pl.ANY),
        scratch_shapes=[pltpu.VMEM((seg_rows * PACK_SLABS, LANES), jnp.uint32),
                        pltpu.VMEM((MOE_BLOCKS[-1] * PACK_SLABS, LANES), jnp.uint32),
                        pltpu.SMEM((LANES,), jnp.int32),
                        pltpu.VMEM((MOE_WEIGHT_BUFFERS, d, MOE_D_FF), BF16),
                        pltpu.VMEM((MOE_WEIGHT_BUFFERS, d, MOE_D_FF), BF16),
                        pltpu.VMEM((MOE_WEIGHT_BUFFERS, MOE_D_FF, d), BF16),
                        pltpu.VMEM((2 * chunk * PACK_SLABS, LANES), jnp.uint32),
                        pltpu.VMEM((2 * chunk * OUT_SLABS, LANES), F32),
                        pltpu.SemaphoreType.DMA((3, MOE_WEIGHT_BUFFERS)),
                        pltpu.SemaphoreType.DMA((2,)), pltpu.SemaphoreType.DMA((2,))],
    )
    return pl.pallas_call(
        functools.partial(_moe_kernel, tile=tile, chunk=chunk, n_steps=n_tiles * MOE_EXPERTS),
        grid_spec=grid_spec,
        out_shape=jax.ShapeDtypeStruct((t * OUT_SLABS, LANES), F32),
        compiler_params=pltpu.CompilerParams(dimension_semantics=("arbitrary",),
                                             vmem_limit_bytes=MOE_VMEM_LIMIT),
        name="moe_experts",
    )(cnt, tab, xp, wg, wu, wd)


def _final_kernel(x1_ref, moe_ref, p_ref, g_ref, b_ref, wpg_ref, wpp_ref, out_ref, *, alpha):
    rows = x1_ref.shape[0]
    moe = jnp.concatenate([moe_ref[pl.ds(c, rows, stride=OUT_SLABS), :] for c in range(OUT_SLABS)], axis=1)
    x2 = _layer_norm(alpha * x1_ref[...] + moe, g_ref[...], b_ref[...])
    gate = _sigmoid(_dot(x2.astype(BF16), wpg_ref[...]))
    out_ref[...] = x2 + gate * _dot(p_ref[...].astype(BF16), wpp_ref[...])


def _final(x1, moe, p2d, g, b, wpg, wpp, alpha):
    t, d = x1.shape
    rows = DENSE_ROWS
    const = lambda shape: pl.BlockSpec(shape, lambda i: (0,) * len(shape))
    rowblk = lambda w: pl.BlockSpec((rows, w), lambda i: (i, 0))
    return pl.pallas_call(
        functools.partial(_final_kernel, alpha=alpha),
        grid=(t // rows,),
        in_specs=[rowblk(d), pl.BlockSpec((rows * OUT_SLABS, LANES), lambda i: (i, 0)), rowblk(p2d.shape[1]),
                  const(g.shape), const(b.shape), const(wpg.shape), const(wpp.shape)],
        out_specs=rowblk(d),
        out_shape=jax.ShapeDtypeStruct((t, d), F32),
        compiler_params=pltpu.CompilerParams(dimension_semantics=("arbitrary",), vmem_limit_bytes=VMEM_LIMIT),
        name="final_ln_ple",
    )(x1, moe, p2d, g, b, wpg, wpp)


def _pad_lanes(w, width=LANES):
    return jnp.pad(w, ((0, 0), (0, width - w.shape[1])))


def _row(v, width=None):
    v = v.astype(F32).reshape(1, -1)
    return v if width is None else _pad_lanes(v, width)


def kernel(x, p, w_in, ssm_conv_w, ssm_conv_b, ssm_dt_bias, ssm_a_log, ssm_d, ssm_norm_w, lstm_i_bias, lstm_f_bias, lstm_norm_w, w_branch_ssm, w_branch_lstm, w_out, ln1_g, ln1_b, moe_w_group, moe_b_group, moe_w_expert, moe_b_expert, moe_w_gate, moe_w_up, moe_w_down, ln2_g, ln2_b, ple_w_proj, ple_w_gate):
    depth = w_in.shape[0]
    bsz, seq, d = x.shape
    t = bsz * seq
    alpha = (2.0 * depth) ** 0.25
    head_expand = (jnp.arange(LANES, dtype=jnp.int32)[:, None]
                   == jnp.arange(SSM_D_INNER, dtype=jnp.int32)[None, :] // SSM_HEAD_DIM).astype(BF16)
    for i in range(depth):
        cols = [0]
        for sz in IN_PROJ_SIZES:
            cols.append(cols[-1] + sz)
        part = lambda k: w_in[i][:, cols[k]:cols[k + 1]]
        bf = lambda w: w.astype(BF16)
        y_ssm = _ssd_mixer(
            x, bf(w_in[i][:, 0:cols[2] + LANES]),
            ssm_conv_w[i].astype(F32), _row(ssm_conv_b[i]), _row(ssm_dt_bias[i], LANES),
            _row(-jnp.exp(ssm_a_log[i].astype(F32)), LANES),
            _row(jnp.repeat(ssm_d[i].astype(F32), SSM_HEAD_DIM)), _row(ssm_norm_w[i]), head_expand)
        w_lstm = jnp.concatenate([part(3) * (LSTM_QK ** -0.5), part(4), part(5), part(6),
                                  _pad_lanes(part(7)), _pad_lanes(part(8))], axis=1)
        b_if = jnp.concatenate([_row(lstm_i_bias[i], LANES), _row(lstm_f_bias[i], LANES)], axis=1)
        y_lstm = _mlstm_mixer(x, bf(w_lstm), b_if, _row(lstm_norm_w[i]))
        w_r = _pad_lanes(jnp.concatenate([moe_w_group[i], moe_w_expert[i]], axis=1).astype(F32)).T
        w_r_hi = w_r.astype(BF16)
        w_r_cat = jnp.concatenate([w_r_hi, (w_r - w_r_hi.astype(F32)).astype(BF16)], axis=0)
        b_r = jnp.broadcast_to(_row(jnp.concatenate([moe_b_group[i], moe_b_expert[i]]), LANES).T, (LANES, DENSE_ROWS))
        tile = min(MOE_TILE, t)
        x1, xp, tab, cnt, e_gate, e_up, e_down = _merge(
            x.reshape(t, d), y_ssm.reshape(t, -1), y_lstm.reshape(t, -1),
            bf(w_in[i][:, cols[9]:cols[11]]), bf(w_branch_ssm[i]), bf(w_branch_lstm[i]), bf(w_out[i]),
            _row(ln1_g[i]), _row(ln1_b[i]), w_r_cat, b_r,
            moe_w_gate[i].astype(F32), moe_w_up[i].astype(F32), moe_w_down[i].astype(F32), alpha, tile)
        cnt = cnt.reshape(t // tile, LANES, LANES)[:, :MOE_EXPERTS, 0].reshape(-1)
        moe = _moe(xp, _route_rows(tab, cnt, tile).reshape(-1), cnt, e_gate, e_up, e_down, tile)
        x = _final(x1, moe, p[i].reshape(t, -1), _row(ln2_g[i]), _row(ln2_b[i]),
                   bf(ple_w_gate[i]), bf(ple_w_proj[i]), alpha).reshape(bsz, seq, d)
    return x
```

```python
import functools

import jax
import jax.numpy as jnp
from jax import lax
from jax.experimental import pallas as pl
from jax.experimental.pallas import tpu as pltpu

F32 = jnp.float32
BF16 = jnp.bfloat16

D_MODEL = 1024
PLE_DIM = 256
SSM_D_INNER = 1024
SSM_HEAD_DIM = 64
SSM_HEADS = 16
SSM_GROUPS = 4
SSM_STATE = 128
SSM_CONV = 4
SSM_XBC = SSM_D_INNER + 2 * SSM_GROUPS * SSM_STATE
LSTM_HEADS = 8
LSTM_QK = 64
LSTM_V = 128
LSTM_D_QK = LSTM_HEADS * LSTM_QK
LSTM_D_V = LSTM_HEADS * LSTM_V
CHUNK = 128
MOE_GROUPS = 8
MOE_PER_GROUP = 8
MOE_EXPERTS = 64
MOE_D_FF = 512
NORM_EPS = 1e-5
IN_PROJ_SIZES = (SSM_D_INNER, SSM_XBC, SSM_HEADS, LSTM_D_QK, LSTM_D_QK, LSTM_D_V, LSTM_D_V,
                 LSTM_HEADS, LSTM_HEADS, D_MODEL, D_MODEL)

LANES = 128
SUBLANES = 8
VMEM_LIMIT = 56 * 1024 * 1024
MOE_VMEM_LIMIT = 60 * 1024 * 1024

MIX_ROWS = 256
MIX_SEQS = 2
DENSE_ROWS = 512
MOE_TILE = 8192
MOE_CHUNK = 256
MOE_BLOCKS = (256, 320, 384)
MOE_WEIGHT_BUFFERS = 4
MOE_WEIGHT_SPLIT = 4
PACK_SLABS = D_MODEL // 2 // LANES
OUT_SLABS = D_MODEL // LANES


def _dot(a, b):
    return jnp.dot(a, b, preferred_element_type=F32)


def _dot_nt(a, b):
    return lax.dot_general(a, b, (((1,), (1,)), ((), ())), preferred_element_type=F32)


def _dot_tn(a, b):
    return lax.dot_general(a, b, (((0,), (0,)), ((), ())), preferred_element_type=F32)


def _sigmoid(x):
    return 1.0 / (1.0 + jnp.exp(-x))


def _softplus(x):
    return jnp.maximum(x, 0.0) + jnp.log(1.0 + jnp.exp(-jnp.abs(x)))


def _split3(x):
    hi = x.astype(BF16)
    r1 = x - hi.astype(F32)
    mid = r1.astype(BF16)
    lo = (r1 - mid.astype(F32)).astype(BF16)
    return hi, mid, lo


def _cumsum_rows(tril, x):
    hi, mid, lo = _split3(x)
    return _dot(tril, hi) + _dot(tril, mid) + _dot(tril, lo)


def _causal_masks():
    r = lax.broadcasted_iota(jnp.int32, (CHUNK, CHUNK), 0)
    c = lax.broadcasted_iota(jnp.int32, (CHUNK, CHUNK), 1)
    causal = r >= c
    return causal, jnp.where(causal, 1.0, 0.0).astype(BF16)


def _pair_cols(v, ha, hb, rows, lane):
    a = jnp.broadcast_to(v[:, ha:ha + 1], (rows, LANES))
    b = jnp.broadcast_to(v[:, hb:hb + 1], (rows, LANES))
    return jnp.where(lane < SSM_HEAD_DIM, a, b)


def _ssd_kernel(x_ref, w_ref, cw_ref, cb_ref, dtb_ref, aneg_ref, dexp_ref, nw_ref, ex_ref,
                y_ref, ext_ref, u_ref, st_ref, *, rows, nseq):
    n_slab = SSM_XBC // LANES
    tail = SUBLANES - SSM_CONV + 1

    @pl.when(pl.program_id(1) == 0)
    def _():
        ext_ref[:, :, 0:SUBLANES, :] = jnp.zeros((nseq, n_slab, SUBLANES, LANES), F32)
        st_ref[...] = jnp.zeros(st_ref.shape, F32)

    xb = x_ref[...].reshape(nseq * rows, x_ref.shape[-1]).astype(BF16)
    z = _dot(xb, w_ref[:, 0:SSM_D_INNER])
    dt_col = SSM_D_INNER + SSM_XBC
    dtr = _dot(xb, w_ref[:, dt_col:dt_col + LANES])
    dtr = jnp.where(lax.broadcasted_iota(jnp.int32, dtr.shape, 1) < SSM_HEADS, dtr, 0.0)
    for c2 in range(n_slab // 2):
        xbc = _dot(xb, w_ref[:, SSM_D_INNER + c2 * 2 * LANES:SSM_D_INNER + (c2 + 1) * 2 * LANES])
        for s in range(nseq):
            for cc in range(2):
                ext_ref[s, 2 * c2 + cc, SUBLANES:SUBLANES + rows, :] = (
                    xbc[s * rows:(s + 1) * rows, cc * LANES:(cc + 1) * LANES])
    for s in range(nseq):
        for c in range(n_slab):
            conv = cb_ref[:, c * LANES:(c + 1) * LANES]
            for k in range(SSM_CONV):
                conv = conv + cw_ref[k:k + 1, c * LANES:(c + 1) * LANES] * ext_ref[s, c, pl.ds(tail + k, rows), :]
            u_ref[s, c] = conv * _sigmoid(conv)
            ext_ref[s, c, 0:SUBLANES, :] = ext_ref[s, c, rows:rows + SUBLANES, :]

    dt = _softplus(dtr + dtb_ref[...])
    da = dt * aneg_ref[...]
    causal, tril = _causal_masks()
    lane = lax.broadcasted_iota(jnp.int32, (CHUNK, LANES), 1)
    lane1 = lax.broadcasted_iota(jnp.int32, (1, LANES), 1)
    gw = SSM_D_INNER // SSM_GROUPS
    b_slab = SSM_D_INNER // LANES
    c_slab = b_slab + SSM_GROUPS

    for ci in range(rows // CHUNK):
        for s in range(nseq):
            r0 = ci * CHUNK
            q0 = s * rows + r0
            dt_c = dt[q0:q0 + CHUNK]
            a_cs = _cumsum_rows(tril, da[q0:q0 + CHUNK])
            a_cs_t = a_cs.T
            dt_t = dt_c.T
            a_last = a_cs[CHUNK - 1:CHUNK, :]
            ea_last = jnp.exp(a_last)
            ea_x = _dot(jnp.exp(a_cs).astype(BF16), ex_ref[...])
            sw_x = _dot((dt_c * jnp.exp(a_last - a_cs)).astype(BF16), ex_ref[...])
            for g in range(SSM_GROUPS):
                bm_g = u_ref[s, b_slab + g, r0:r0 + CHUNK, :].astype(BF16)
                cm_g = u_ref[s, c_slab + g, r0:r0 + CHUNK, :].astype(BF16)
                cb = _dot_nt(cm_g, bm_g)
                st_g = st_ref[s, :, g * gw:(g + 1) * gw]
                y_off = _dot(cm_g, st_g.astype(BF16))
                yy, xw, eal_x = [], [], []
                for pr in range(2):
                    ha = 4 * g + 2 * pr
                    hb = ha + 1
                    lo = g * gw + pr * LANES
                    xs_p = u_ref[s, 2 * g + pr, r0:r0 + CHUNK, :]
                    xs_pb = xs_p.astype(BF16)
                    rhs = jnp.concatenate([jnp.where(lane < SSM_HEAD_DIM, xs_pb, 0).astype(BF16),
                                           jnp.where(lane >= SSM_HEAD_DIM, xs_pb, 0).astype(BF16)], axis=0)
                    gs = []
                    for h in (ha, hb):
                        colb = jnp.broadcast_to(a_cs[:, h:h + 1], (CHUNK, CHUNK))
                        rowb = jnp.broadcast_to(a_cs_t[h:h + 1, :], (CHUNK, CHUNK))
                        dec = jnp.exp(jnp.where(causal, colb - rowb, -jnp.inf))
                        gs.append((cb * dec * jnp.broadcast_to(dt_t[h:h + 1, :], (CHUNK, CHUNK))).astype(BF16))
                    y_p = (_dot(jnp.concatenate(gs, axis=1), rhs)
                           + y_off[:, pr * LANES:(pr + 1) * LANES] * ea_x[:, lo:lo + LANES]
                           + dexp_ref[:, lo:lo + LANES] * xs_p)
                    zz = z[q0:q0 + CHUNK, lo:lo + LANES]
                    yy.append(y_p * (zz * _sigmoid(zz)))
                    xw.append((xs_p * sw_x[:, lo:lo + LANES]).astype(BF16))
                    eal_x.append(_pair_cols(ea_last, ha, hb, 1, lane1))
                st_ref[s, :, g * gw:(g + 1) * gw] = (st_g * jnp.concatenate(eal_x, axis=1)
                                                     + _dot_tn(bm_g, jnp.concatenate(xw, axis=1)))
                ms = sum(jnp.sum(t * t, axis=-1, keepdims=True) for t in yy) * (1.0 / gw)
                inv = lax.rsqrt(ms + NORM_EPS)
                for pr in range(2):
                    lo = g * gw + pr * LANES
                    y_ref[s, r0:r0 + CHUNK, lo:lo + LANES] = (yy[pr] * inv * nw_ref[:, lo:lo + LANES]).astype(BF16)


def _ssd_mixer(x, w, cw, cb, dtb, aneg, dexp, nw, ex):
    b, s, d = x.shape
    rows = MIX_ROWS
    nseq = MIX_SEQS
    const = lambda shape: pl.BlockSpec(shape, lambda i, j: (0,) * len(shape), pipeline_mode=pl.Buffered(1))
    return pl.pallas_call(
        functools.partial(_ssd_kernel, rows=rows, nseq=nseq),
        grid=(b // nseq, s // rows),
        in_specs=[pl.BlockSpec((nseq, rows, d), lambda i, j: (i, j, 0)),
                  const(w.shape), const(cw.shape), const(cb.shape),
                  const(dtb.shape), const(aneg.shape), const(dexp.shape), const(nw.shape), const(ex.shape)],
        out_specs=pl.BlockSpec((nseq, rows, SSM_D_INNER), lambda i, j: (i, j, 0)),
        out_shape=jax.ShapeDtypeStruct((b, s, SSM_D_INNER), BF16),
        scratch_shapes=[pltpu.VMEM((nseq, SSM_XBC // LANES, SUBLANES + rows, LANES), F32),
                        pltpu.VMEM((nseq, SSM_XBC // LANES, rows, LANES), F32),
                        pltpu.VMEM((nseq, SSM_STATE, SSM_D_INNER), F32)],
        compiler_params=pltpu.CompilerParams(dimension_semantics=("arbitrary", "arbitrary"),
                                             vmem_limit_bytes=VMEM_LIMIT),
        name="ssd_mixer",
    )(x, w, cw, cb, dtb, aneg, dexp, nw, ex)


def _mlstm_kernel(x_ref, w_ref, bif_ref, nw_ref,
                  y_ref, c_ref, m_ref, *, rows, nseq):
    @pl.when(pl.program_id(1) == 0)
    def _():
        c_ref[...] = jnp.zeros(c_ref.shape, F32)
        m_ref[...] = jnp.full(m_ref.shape, -jnp.inf, F32)

    xb = x_ref[...].reshape(nseq * rows, x_ref.shape[-1]).astype(BF16)
    c_k, c_v, c_o, c_g = LSTM_D_QK, 2 * LSTM_D_QK, 2 * LSTM_D_QK + LSTM_D_V, 2 * LSTM_D_QK + 2 * LSTM_D_V
    q = _dot(xb, w_ref[:, 0:c_k])
    k = _dot(xb, w_ref[:, c_k:c_v])
    v = _dot(xb, w_ref[:, c_v:c_o])
    o = _dot(xb, w_ref[:, c_o:c_g])
    gif = _dot(xb, w_ref[:, c_g:c_g + 2 * LANES]) + bif_ref[...]
    causal, tril = _causal_masks()
    lane = lax.broadcasted_iota(jnp.int32, (CHUNK, LANES), 1)
    lane1 = lax.broadcasted_iota(jnp.int32, (1, LANES), 1)
    row = lax.broadcasted_iota(jnp.int32, (LANES, 1), 0)
    head_lane = lane < LSTM_HEADS
    ones = jnp.ones((CHUNK, LANES), BF16)

    for ci in range(rows // CHUNK):
        r0 = ci * CHUNK
        pre = {}
        for s in range(nseq):
            q0 = s * rows + r0
            log_i = gif[q0:q0 + CHUNK, 0:LANES]
            f_pre = gif[q0:q0 + CHUNK, LANES:2 * LANES]
            log_f = jnp.where(head_lane, jnp.minimum(f_pre, 0.0) - jnp.log(1.0 + jnp.exp(-jnp.abs(f_pre))), 0.0)
            fcum = _cumsum_rows(tril, log_f)
            f_tot = fcum[CHUNK - 1:CHUNK, :]
            gk = log_i - fcum
            m_loc = jnp.max(f_tot + gk, axis=0, keepdims=True)
            g_t = gk.T
            m_prev = m_ref[s]
            m_new = jnp.maximum(f_tot + m_prev, m_loc)
            s_prev = jnp.exp(f_tot + m_prev - m_new)
            s_loc = jnp.exp(m_loc - m_new)
            m_ref[s] = jnp.where(lane1 < LSTM_HEADS, m_new, -jnp.inf)
            w_shift = f_tot - m_loc
            pre[s] = (q0, fcum, gk, g_t, m_prev, s_prev, s_loc, w_shift)

        for pr in range(LSTM_HEADS // 2):
            st = {}
            for s in range(nseq):
                q0, fcum, gk, g_t, m_prev, s_prev, s_loc, w_shift = pre[s]
                q_p = q[q0:q0 + CHUNK, pr * LANES:(pr + 1) * LANES]
                k_p = k[q0:q0 + CHUNK, pr * LANES:(pr + 1) * LANES]
                k_pb = k_p.astype(BF16)
                c_pair = c_ref[s, pr]
                c_pair_b = c_pair.astype(BF16)
                c_new = c_pair * jnp.where(row < LSTM_QK,
                                           jnp.broadcast_to(s_prev[:, 2 * pr:2 * pr + 1], (LANES, 1)),
                                           jnp.broadcast_to(s_prev[:, 2 * pr + 1:2 * pr + 2], (LANES, 1)))
                st[s] = [q_p, k_p, k_pb, c_pair_b, c_new]
            for hh in range(2):
                for s in range(nseq):
                    q0, fcum, gk, g_t, m_prev, s_prev, s_loc, w_shift = pre[s]
                    q_p, k_p, k_pb, c_pair_b, c_new = st[s]
                    h = 2 * pr + hh
                    in_head = (lane < LSTM_QK) if hh == 0 else (lane >= LSTM_QK)
                    v_h = v[q0:q0 + CHUNK, h * LSTM_V:(h + 1) * LSTM_V].astype(BF16)
                    v_ext = jnp.concatenate([v_h, ones], axis=1)
                    q_m = jnp.where(in_head, q_p, 0.0)
                    fcol = jnp.broadcast_to(fcum[:, h:h + 1], (CHUNK, CHUNK))
                    gcol = jnp.broadcast_to(gk[:, h:h + 1], (CHUNK, LANES))
                    rowb = jnp.broadcast_to(g_t[h:h + 1, :], (CHUNK, CHUNK))
                    log_d = jnp.where(causal, fcol + rowb, -jnp.inf)
                    inter_log = fcol + m_prev[:, h:h + 1]
                    m_t = jnp.maximum(inter_log, jnp.max(log_d, axis=-1, keepdims=True))
                    scores = _dot_nt(q_m.astype(BF16), k_pb) * jnp.exp(log_d - m_t)
                    inter_w = jnp.exp(inter_log - m_t)
                    lhs = jnp.concatenate([scores.astype(BF16), (q_m * inter_w).astype(BF16)], axis=1)
                    res = _dot(lhs, jnp.concatenate([v_ext, c_pair_b], axis=0))
                    hv = res[:, 0:LSTM_V] / jnp.maximum(jnp.abs(res[:, LSTM_V:2 * LSTM_V]), jnp.exp(-m_t))
                    ms = jnp.mean(hv * hv, axis=-1, keepdims=True)
                    o_h = o[q0:q0 + CHUNK, h * LSTM_V:(h + 1) * LSTM_V]
                    y_ref[s, r0:r0 + CHUNK, h * LSTM_V:(h + 1) * LSTM_V] = (
                        _sigmoid(o_h) * (hv * lax.rsqrt(ms + NORM_EPS) * nw_ref[:, h * LSTM_V:(h + 1) * LSTM_V])
                    ).astype(BF16)
                    kw = jnp.where(in_head, k_p * jnp.exp(gcol + w_shift[:, h:h + 1]), 0.0)
                    st[s][4] = c_new + s_loc[:, h:h + 1] * _dot_tn(kw.astype(BF16), v_ext)
            for s in range(nseq):
                c_ref[s, pr] = st[s][4]


def _mlstm_mixer(x, w, bif, nw):
    b, s, d = x.shape
    rows = MIX_ROWS
    nseq = MIX_SEQS
    const = lambda shape: pl.BlockSpec(shape, lambda i, j: (0,) * len(shape), pipeline_mode=pl.Buffered(1))
    return pl.pallas_call(
        functools.partial(_mlstm_kernel, rows=rows, nseq=nseq),
        grid=(b // nseq, s // rows),
        in_specs=[pl.BlockSpec((nseq, rows, d), lambda i, j: (i, j, 0)),
                  const(w.shape), const(bif.shape), const(nw.shape)],
        out_specs=pl.BlockSpec((nseq, rows, LSTM_D_V), lambda i, j: (i, j, 0)),
        out_shape=jax.ShapeDtypeStruct((b, s, LSTM_D_V), BF16),
        scratch_shapes=[pltpu.VMEM((nseq, LSTM_HEADS // 2, 2 * LSTM_QK, 2 * LSTM_V), F32),
                        pltpu.VMEM((nseq, 1, LANES), F32)],
        compiler_params=pltpu.CompilerParams(dimension_semantics=("arbitrary", "arbitrary"),
                                             vmem_limit_bytes=VMEM_LIMIT),
        name="mlstm_mixer",
    )(x, w, bif, nw)


def _layer_norm(t, g, b):
    mu = jnp.mean(t, axis=-1, keepdims=True)
    tc = t - mu
    var = jnp.mean(tc * tc, axis=-1, keepdims=True)
    return tc * lax.rsqrt(var + NORM_EPS) * g + b


def _pack_pair(a, b):
    pa = lax.bitcast_convert_type(a.astype(BF16).astype(F32), jnp.uint32)
    pb = lax.bitcast_convert_type(b.astype(BF16).astype(F32), jnp.uint32)
    return pa | (pb >> 16)


def _unpack_pair(w):
    return (lax.bitcast_convert_type(w & jnp.uint32(0xFFFF0000), F32),
            lax.bitcast_convert_type(w << 16, F32))


def _route(lt):
    n = lt.shape[1]
    sub = lax.broadcasted_iota(jnp.int32, (MOE_PER_GROUP, n), 0).astype(F32)
    big = float(LANES)
    neg = -jnp.inf
    gl = lt[0:MOE_GROUPS, :]
    gmax = jnp.max(gl, axis=0, keepdims=True)
    gidx = jnp.min(jnp.where(gl == gmax, sub, big), axis=0, keepdims=True)
    grp_p = 1.0 / jnp.sum(jnp.exp(gl - gmax), axis=0, keepdims=True)
    el = lt[MOE_GROUPS:MOE_GROUPS + MOE_PER_GROUP, :]
    for g in range(1, MOE_GROUPS):
        lo = MOE_GROUPS + g * MOE_PER_GROUP
        el = jnp.where(gidx == g, lt[lo:lo + MOE_PER_GROUP, :], el)
    m1 = jnp.max(el, axis=0, keepdims=True)
    i1 = jnp.min(jnp.where(el == m1, sub, big), axis=0, keepdims=True)
    el2 = jnp.where(sub == i1, neg, el)
    m2 = jnp.max(el2, axis=0, keepdims=True)
    i2 = jnp.min(jnp.where(el2 == m2, sub, big), axis=0, keepdims=True)
    e21 = jnp.exp(m2 - m1)
    g1 = grp_p / (1.0 + e21)
    g2 = grp_p * e21 / (1.0 + e21)
    return gidx * MOE_PER_GROUP + i1, gidx * MOE_PER_GROUP + i2, g1, g2


def _merge_kernel(x_ref, ys_ref, yl_ref, wg_ref, wa_ref, wb_ref, wo_ref, g_ref, b_ref,
                  wr_ref, br_ref, eg_ref, eu_ref, ed_ref,
                  x1_ref, xp_ref, tab_ref, cnt_ref, egb_ref, eub_ref, edb_ref, carry_ref,
                  *, alpha, steps_per_tile, cast_steps):
    @pl.when(pl.program_id(0) % steps_per_tile == 0)
    def _():
        carry_ref[...] = jnp.zeros(carry_ref.shape, F32)

    @pl.when(pl.program_id(0) < cast_steps)
    def _():
        egb_ref[...] = eg_ref[...].astype(BF16)
        eub_ref[...] = eu_ref[...].astype(BF16)
        edb_ref[...] = ed_ref[...].astype(BF16)

    x = x_ref[...]
    rows = x.shape[0]
    xb = x.astype(BF16)
    d = x.shape[1]
    merged = (_sigmoid(_dot(xb, wg_ref[:, 0:d])) * _dot(ys_ref[...], wa_ref[...])
              + _sigmoid(_dot(xb, wg_ref[:, d:2 * d])) * _dot(yl_ref[...], wb_ref[...]))
    x1 = _layer_norm(alpha * x + _dot(merged.astype(BF16), wo_ref[...]), g_ref[...], b_ref[...])
    x1_ref[...] = x1
    half = x1.shape[1] // 2
    xp = _pack_pair(x1[:, :half], x1[:, half:])
    for c in range(PACK_SLABS):
        xp_ref[pl.ds(c, rows, stride=PACK_SLABS), :] = xp[:, c * LANES:(c + 1) * LANES]
    x_hi = x1.astype(BF16)
    x_lo = (x1 - x_hi.astype(F32)).astype(BF16)
    by_hi = _dot_nt(wr_ref[...], x_hi)
    lt = by_hi[0:LANES] + by_hi[LANES:2 * LANES] + _dot_nt(wr_ref[0:LANES, :], x_lo) + br_ref[...]
    e1, e2, g1, g2 = _route(lt)
    expert = lax.broadcasted_iota(jnp.int32, (LANES, rows), 0).astype(F32)
    oh1 = expert == e1
    oh2 = expert == e2
    ohs = jnp.where(oh1, 1.0, jnp.where(oh2, 1.0, 0.0))
    r_i = lax.broadcasted_iota(jnp.int32, (rows, rows), 0)
    c_i = lax.broadcasted_iota(jnp.int32, (rows, rows), 1)
    before = jnp.where(r_i < c_i, 1.0, 0.0).astype(BF16)
    carry = carry_ref[...]
    seen = _dot(ohs.astype(BF16), before) + jnp.concatenate([carry] * (rows // LANES), axis=1)
    r1 = jnp.sum(jnp.where(oh1, seen, 0.0), axis=0, keepdims=True)
    r2 = jnp.sum(jnp.where(oh2, seen, 0.0), axis=0, keepdims=True)
    carry = carry + jnp.sum(ohs, axis=1, keepdims=True)
    carry_ref[...] = carry
    cnt_ref[...] = carry.astype(jnp.int32)
    as_int = lambda v: v.astype(jnp.int32)
    as_bits = lambda v: lax.bitcast_convert_type(v, jnp.int32)
    tab_ref[...] = jnp.concatenate(
        [as_int(e1), as_int(e2), as_int(r1) * PACK_SLABS, as_int(r2) * PACK_SLABS, as_bits(g1), as_bits(g2),
         jnp.zeros((2, rows), jnp.int32)], axis=0)


def _merge(x2d, ys, yl, wg, wa, wb, wo, g, b, wr, br, eg, eu, ed, alpha, tile):
    t, d = x2d.shape
    rows = DENSE_ROWS
    spt = tile // rows
    n_steps = t // rows
    per_step = -(-MOE_EXPERTS // n_steps)
    cast_steps = MOE_EXPERTS // per_step
    assert per_step * cast_steps == MOE_EXPERTS and cast_steps <= n_steps
    const = lambda shape: pl.BlockSpec(shape, lambda i: (0,) * len(shape), pipeline_mode=pl.Buffered(1))
    rowblk = lambda w: pl.BlockSpec((rows, w), lambda i: (i, 0))
    expert = lambda w: pl.BlockSpec((per_step,) + w.shape[1:], lambda i: (jnp.minimum(i, cast_steps - 1), 0, 0))
    return pl.pallas_call(
        functools.partial(_merge_kernel, alpha=alpha, steps_per_tile=spt, cast_steps=cast_steps),
        grid=(n_steps,),
        in_specs=[rowblk(d), rowblk(SSM_D_INNER), rowblk(LSTM_D_V),
                  const(wg.shape), const(wa.shape), const(wb.shape), const(wo.shape),
                  const(g.shape), const(b.shape), const(wr.shape), const(br.shape),
                  expert(eg), expert(eu), expert(ed)],
        out_specs=[rowblk(d),
                   pl.BlockSpec((rows * PACK_SLABS, LANES), lambda i: (i, 0)),
                   pl.BlockSpec((SUBLANES, rows), lambda i: (i // spt, i % spt)),
                   pl.BlockSpec((LANES, LANES), lambda i: (i // spt, 0)),
                   expert(eg), expert(eu), expert(ed)],
        out_shape=[jax.ShapeDtypeStruct((t, d), F32),
                   jax.ShapeDtypeStruct((t * PACK_SLABS, LANES), jnp.uint32),
                   jax.ShapeDtypeStruct((t // tile * SUBLANES, tile), jnp.int32),
                   jax.ShapeDtypeStruct((t // tile * LANES, LANES), jnp.int32),
                   jax.ShapeDtypeStruct(eg.shape, BF16), jax.ShapeDtypeStruct(eu.shape, BF16),
                   jax.ShapeDtypeStruct(ed.shape, BF16)],
        scratch_shapes=[pltpu.VMEM((LANES, LANES), F32)],
        compiler_params=pltpu.CompilerParams(dimension_semantics=("arbitrary",), vmem_limit_bytes=VMEM_LIMIT),
        name="merge_ln_route",
    )(x2d, ys, yl, wg, wa, wb, wo, g, b, wr, br, eg, eu, ed)


def _segment_rows(count):
    return lax.shift_right_logical(count + (SUBLANES - 1), 3) * (SUBLANES * PACK_SLABS)


def _route_rows_kernel(cnt_ref, tab_ref, out_ref):
    i = pl.program_id(0)
    tab = tab_ref[...]
    starts = jnp.zeros(tab.shape, jnp.int32)
    off = jnp.int32(0)
    for e in range(MOE_EXPERTS):
        starts = jnp.where(tab == e, off, starts)
        off = off + _segment_rows(cnt_ref[i * MOE_EXPERTS + e])
    row = lax.broadcasted_iota(jnp.int32, tab.shape, 0)
    res = jnp.where(lax.shift_right_logical(row, 1) == 1, tab + pltpu.roll(starts, 2, axis=0), tab)
    blocks = tab.shape[1] // LANES
    for c in range(blocks):
        out_ref[pl.ds(c, SUBLANES, stride=blocks), :] = res[:, c * LANES:(c + 1) * LANES]


def _route_rows(tab, cnt, tile):
    n_tiles = tab.shape[0] // SUBLANES
    rows = SUBLANES * tile // LANES
    return pl.pallas_call(
        _route_rows_kernel,
        grid_spec=pltpu.PrefetchScalarGridSpec(
            num_scalar_prefetch=1, grid=(n_tiles,),
            in_specs=[pl.BlockSpec((SUBLANES, tile), lambda i, c: (i, 0))],
            out_specs=pl.BlockSpec((rows, LANES), lambda i, c: (i, 0))),
        out_shape=jax.ShapeDtypeStruct((n_tiles * rows, LANES), jnp.int32),
        compiler_params=pltpu.CompilerParams(dimension_semantics=("arbitrary",), vmem_limit_bytes=VMEM_LIMIT),
        name="route_rows",
    )(cnt, tab)


def _moe_kernel(cnt_ref, tab_ref, xp_hbm, wg_hbm, wu_hbm, wd_hbm, out_hbm, xys, ybuf, off_ref,
                wg_buf, wu_buf, wd_buf, xin, xout, wsem, isem, osem, *, tile, chunk, n_steps):
    i = pl.program_id(0)
    group = SUBLANES * PACK_SLABS
    half = PACK_SLABS * LANES
    weights = ((wg_hbm, wg_buf), (wu_hbm, wu_buf), (wd_hbm, wd_buf))

    def weight_copies(expert, into):
        copies = []
        for k, (hbm, buf) in enumerate(weights):
            part = buf.shape[1] // MOE_WEIGHT_SPLIT
            for q in range(MOE_WEIGHT_SPLIT):
                copies.append(pltpu.make_async_copy(hbm.at[expert, pl.ds(q * part, part)],
                                                    buf.at[into, pl.ds(q * part, part)], wsem.at[k, into]))
        return copies

    @pl.when(i == 0)
    def _():
        for ahead in range(MOE_WEIGHT_BUFFERS - 1):
            for cp in weight_copies(ahead % MOE_EXPERTS, ahead):
                cp.start()

    n_chunks = tile // chunk
    rolled = lambda trips: trips + jnp.minimum(i, 0)

    def in_buf(c):
        return pl.multiple_of((c % 2) * (chunk * PACK_SLABS), group)

    def out_buf(c):
        return pl.multiple_of((c % 2) * (chunk * OUT_SLABS), SUBLANES)

    def in_copy(c):
        row0 = pl.multiple_of((i * tile + c * chunk) * PACK_SLABS, group)
        return pltpu.make_async_copy(xp_hbm.at[pl.ds(row0, chunk * PACK_SLABS)],
                                     xin.at[pl.ds(in_buf(c), chunk * PACK_SLABS)], isem.at[c % 2])

    def out_copy(c):
        row0 = pl.multiple_of((i * tile + c * chunk) * OUT_SLABS, SUBLANES)
        return pltpu.make_async_copy(xout.at[pl.ds(out_buf(c), chunk * OUT_SLABS)],
                                     out_hbm.at[pl.ds(row0, chunk * OUT_SLABS)], osem.at[c % 2])

    def sort_rows():
        in_copy(0).start()

        def offsets(e, acc):
            padded = _segment_rows(cnt_ref[i * MOE_EXPERTS + e])
            off_ref[e] = acc
            end = acc + padded

            @pl.when(padded > 0)
            def _():
                xys[pl.ds(pl.multiple_of(end - group, group), group), :] = jnp.zeros((group, LANES), jnp.uint32)
            return end

        total = lax.fori_loop(0, MOE_EXPERTS, offsets, 0)
        off_ref[MOE_EXPERTS] = total
        tail = MOE_BLOCKS[-1] * PACK_SLABS
        xys[pl.ds(pl.multiple_of(total, group), tail), :] = jnp.zeros((tail, LANES), jnp.uint32)

        def distribute_chunk(c, carry):
            @pl.when(c + 1 < n_chunks)
            def _():
                in_copy(c + 1).start()
            in_copy(c).wait()

            g0 = c * (chunk // SUBLANES)
            shift = in_buf(c) - c * (chunk * PACK_SLABS)

            def distribute(tg, carry2):
                for u in range(SUBLANES):
                    t = tg * SUBLANES + u
                    d1 = pl.multiple_of(tab_ref[2 * tile + t], PACK_SLABS)
                    d2 = pl.multiple_of(tab_ref[3 * tile + t], PACK_SLABS)
                    row = xin[pl.ds(pl.multiple_of(shift + t * PACK_SLABS, PACK_SLABS), PACK_SLABS), :]
                    xys[pl.ds(d1, PACK_SLABS), :] = row
                    xys[pl.ds(d2, PACK_SLABS), :] = row
                return carry2

            lax.fori_loop(g0, g0 + rolled(chunk // SUBLANES), distribute, 0)
            return carry

        lax.fori_loop(0, rolled(n_chunks), distribute_chunk, 0)

    def expert_block(refs, base, rows, valid):
        wg_ref, wu_ref, wd_ref = refs
        words = [xys[pl.ds(base + c, rows, stride=PACK_SLABS), :] for c in range(PACK_SLABS)]
        parts = [_unpack_pair(w) for w in words]
        x_hi = jnp.concatenate([p[0].astype(BF16) for p in parts], axis=1)
        x_lo = jnp.concatenate([p[1].astype(BF16) for p in parts], axis=1)
        hg = _dot(x_hi, wg_ref[0:half, :]) + _dot(x_lo, wg_ref[half:2 * half, :])
        hu = _dot(x_hi, wu_ref[0:half, :]) + _dot(x_lo, wu_ref[half:2 * half, :])
        y = _dot((hg * _sigmoid(hg) * hu).astype(BF16), wd_ref[...])
        for c in range(PACK_SLABS):
            ybuf[pl.ds(c, rows, stride=PACK_SLABS), :] = _pack_pair(
                y[:, c * LANES:(c + 1) * LANES], y[:, half + c * LANES:half + (c + 1) * LANES])

        def copy(g, c):
            r = pl.multiple_of(g * group, group)
            xys[pl.ds(base + r, group), :] = ybuf[pl.ds(r, group), :]
            return c

        lax.fori_loop(0, valid // group, copy, 0)

    def expert_step(j, carry):
        step = i * MOE_EXPERTS + j
        slot = step % MOE_WEIGHT_BUFFERS
        nxt = step + MOE_WEIGHT_BUFFERS - 1

        @pl.when(nxt < n_steps)
        def _():
            for cp in weight_copies(nxt % MOE_EXPERTS, nxt % MOE_WEIGHT_BUFFERS):
                cp.start()

        for cp in weight_copies(j, slot):
            cp.wait()
        refs = (wg_buf.at[slot], wu_buf.at[slot], wd_buf.at[slot])
        start = off_ref[j]
        n = off_ref[j + 1] - start
        lo = 0
        for rows in MOE_BLOCKS:
            hi = rows * PACK_SLABS

            @pl.when(jnp.logical_and(n > lo, n <= hi))
            def _(rows=rows):
                expert_block(refs, pl.multiple_of(start, group), rows, n)
            lo = hi

        @pl.when(n > lo)
        def _():
            span = MOE_BLOCKS[0] * PACK_SLABS

            def block(bi, carry2):
                expert_block(refs, pl.multiple_of(start + bi * span, group), MOE_BLOCKS[0],
                             jnp.minimum(span, n - bi * span))
                return carry2

            lax.fori_loop(0, (n + span - 1) // span, block, 0)
        return carry

    def combine_rows():
        def combine_chunk(c, carry):
            @pl.when(c >= 2)
            def _():
                out_copy(c - 2).wait()

            g0 = c * (chunk // SUBLANES)
            shift = out_buf(c) - c * (chunk * OUT_SLABS)

            def combine(tg, carry2):
                for u in range(SUBLANES):
                    t = tg * SUBLANES + u
                    g1 = lax.bitcast_convert_type(tab_ref[4 * tile + t], F32)
                    g2 = lax.bitcast_convert_type(tab_ref[5 * tile + t], F32)
                    a_hi, a_lo = _unpack_pair(
                        xys[pl.ds(pl.multiple_of(tab_ref[2 * tile + t], PACK_SLABS), PACK_SLABS), :])
                    b_hi, b_lo = _unpack_pair(
                        xys[pl.ds(pl.multiple_of(tab_ref[3 * tile + t], PACK_SLABS), PACK_SLABS), :])
                    xout[pl.ds(pl.multiple_of(shift + t * OUT_SLABS, OUT_SLABS), OUT_SLABS), :] = jnp.concatenate(
                        [g1 * a_hi + g2 * b_hi, g1 * a_lo + g2 * b_lo], axis=0)
                return carry2

            lax.fori_loop(g0, g0 + rolled(chunk // SUBLANES), combine, 0)
            out_copy(c).start()
            return carry

        lax.fori_loop(0, rolled(n_chunks), combine_chunk, 0)
        for c in range(max(0, n_chunks - 2), n_chunks):
            out_copy(c).wait()

    sort_rows()
    lax.fori_loop(0, rolled(MOE_EXPERTS), expert_step, 0)
    combine_rows()


def _moe(xp, tab, cnt, wg, wu, wd, tile):
    t = xp.shape[0] // PACK_SLABS
    d = D_MODEL
    n_tiles = t // tile
    chunk = min(MOE_CHUNK, tile)
    seg_rows = 2 * tile + MOE_EXPERTS * SUBLANES + MOE_BLOCKS[-1]
    grid_spec = pltpu.PrefetchScalarGridSpec(
        num_scalar_prefetch=1,
        grid=(n_tiles,),
        in_specs=[pl.BlockSpec((SUBLANES * tile,), lambda i, c: (i,), memory_space=pltpu.SMEM,
                               pipeline_mode=pl.Buffered(1)),
                  pl.BlockSpec(memory_space=pl.ANY), pl.BlockSpec(memory_space=pl.ANY),
                  pl.BlockSpec(memory_space=pl.ANY), pl.BlockSpec(memory_space=pl.ANY)],
        out_specs=pl.BlockSpec(memory_space=pl.ANY),
        scratch_shapes=[pltpu.VMEM((seg_rows * PACK_SLABS, LANES), jnp.uint32),
                        pltpu.VMEM((MOE_BLOCKS[-1] * PACK_SLABS, LANES), jnp.uint32),
                        pltpu.SMEM((LANES,), jnp.int32),
                        pltpu.VMEM((MOE_WEIGHT_BUFFERS, d, MOE_D_FF), BF16),
                        pltpu.VMEM((MOE_WEIGHT_BUFFERS, d, MOE_D_FF), BF16),
                        pltpu.VMEM((MOE_WEIGHT_BUFFERS, MOE_D_FF, d), BF16),
                        pltpu.VMEM((2 * chunk * PACK_SLABS, LANES), jnp.uint32),
                        pltpu.VMEM((2 * chunk * OUT_SLABS, LANES), F32),
                        pltpu.SemaphoreType.DMA((3, MOE_WEIGHT_BUFFERS)),
                        pltpu.SemaphoreType.DMA((2,)), pltpu.SemaphoreType.DMA((2,))],
    )
    return pl.pallas_call(
        functools.partial(_moe_kernel, tile=tile, chunk=chunk, n_steps=n_tiles * MOE_EXPERTS),
        grid_spec=grid_spec,
        out_shape=jax.ShapeDtypeStruct((t * OUT_SLABS, LANES), F32),
        compiler_params=pltpu.CompilerParams(dimension_semantics=("arbitrary",),
                                             vmem_limit_bytes=MOE_VMEM_LIMIT),
        name="moe_experts",
    )(cnt, tab, xp, wg, wu, wd)


def _final_kernel(x1_ref, moe_ref, p_ref, g_ref, b_ref, wpg_ref, wpp_ref, out_ref, *, alpha):
    rows = x1_ref.shape[0]
    moe = jnp.concatenate([moe_ref[pl.ds(c, rows, stride=OUT_SLABS), :] for c in range(OUT_SLABS)], axis=1)
    x2 = _layer_norm(alpha * x1_ref[...] + moe, g_ref[...], b_ref[...])
    gate = _sigmoid(_dot(x2.astype(BF16), wpg_ref[...]))
    out_ref[...] = x2 + gate * _dot(p_ref[...].astype(BF16), wpp_ref[...])


def _final(x1, moe, p2d, g, b, wpg, wpp, alpha):
    t, d = x1.shape
    rows = DENSE_ROWS
    const = lambda shape: pl.BlockSpec(shape, lambda i: (0,) * len(shape))
    rowblk = lambda w: pl.BlockSpec((rows, w), lambda i: (i, 0))
    return pl.pallas_call(
        functools.partial(_final_kernel, alpha=alpha),
        grid=(t // rows,),
        in_specs=[rowblk(d), pl.BlockSpec((rows * OUT_SLABS, LANES), lambda i: (i, 0)), rowblk(p2d.shape[1]),
                  const(g.shape), const(b.shape), const(wpg.shape), const(wpp.shape)],
        out_specs=rowblk(d),
        out_shape=jax.ShapeDtypeStruct((t, d), F32),
        compiler_params=pltpu.CompilerParams(dimension_semantics=("arbitrary",), vmem_limit_bytes=VMEM_LIMIT),
        name="final_ln_ple",
    )(x1, moe, p2d, g, b, wpg, wpp)


def _proj_weights_kernel(w_ref, ssd_ref, lstm_ref, gate_ref):
    w = w_ref[...]
    cols = [0]
    for sz in IN_PROJ_SIZES:
        cols.append(cols[-1] + sz)
    lane = lax.broadcasted_iota(jnp.int32, (w.shape[0], LANES), 1)
    gate_block = lambda c: jnp.where(lane < LSTM_HEADS, w[:, c:c + LANES], 0.0)
    ssd_ref[...] = w[:, 0:cols[2] + LANES].astype(BF16)
    lstm_ref[...] = jnp.concatenate(
        [w[:, cols[3]:cols[4]] * (LSTM_QK ** -0.5), w[:, cols[4]:cols[7]], gate_block(cols[7]), gate_block(cols[8])],
        axis=1).astype(BF16)
    gate_ref[...] = w[:, cols[9]:cols[11]].astype(BF16)


def _proj_weights(w_all, layer):
    _, d, n = w_all.shape
    rows = LANES
    widths = (IN_PROJ_SIZES[0] + IN_PROJ_SIZES[1] + LANES, 2 * LSTM_D_QK + 2 * LSTM_D_V + 2 * LANES, 2 * D_MODEL)
    return pl.pallas_call(
        _proj_weights_kernel,
        grid=(d // rows,),
        in_specs=[pl.BlockSpec((None, rows, n), lambda i: (layer, i, 0))],
        out_specs=[pl.BlockSpec((rows, wd), lambda i: (i, 0)) for wd in widths],
        out_shape=[jax.ShapeDtypeStruct((d, wd), BF16) for wd in widths],
        compiler_params=pltpu.CompilerParams(dimension_semantics=("arbitrary",), vmem_limit_bytes=VMEM_LIMIT),
        name="proj_weights",
    )(w_all.astype(F32))


def _pad_lanes(w, width=LANES):
    return jnp.pad(w, ((0, 0), (0, width - w.shape[1])))


def _row(v, width=None):
    v = v.astype(F32).reshape(1, -1)
    return v if width is None else _pad_lanes(v, width)


def kernel(x, p, w_in, ssm_conv_w, ssm_conv_b, ssm_dt_bias, ssm_a_log, ssm_d, ssm_norm_w, lstm_i_bias, lstm_f_bias, lstm_norm_w, w_branch_ssm, w_branch_lstm, w_out, ln1_g, ln1_b, moe_w_group, moe_b_group, moe_w_expert, moe_b_expert, moe_w_gate, moe_w_up, moe_w_down, ln2_g, ln2_b, ple_w_proj, ple_w_gate):
    depth = w_in.shape[0]
    bsz, seq, d = x.shape
    t = bsz * seq
    alpha = (2.0 * depth) ** 0.25
    head_expand = (jnp.arange(LANES, dtype=jnp.int32)[:, None]
                   == jnp.arange(SSM_D_INNER, dtype=jnp.int32)[None, :] // SSM_HEAD_DIM).astype(BF16)
    for i in range(depth):
        bf = lambda w: w.astype(BF16)
        w_ssd, w_lstm, w_gates = _proj_weights(w_in, i)
        y_ssm = _ssd_mixer(
            x, w_ssd,
            ssm_conv_w[i].astype(F32), _row(ssm_conv_b[i]), _row(ssm_dt_bias[i], LANES),
            _row(-jnp.exp(ssm_a_log[i].astype(F32)), LANES),
            _row(jnp.repeat(ssm_d[i].astype(F32), SSM_HEAD_DIM)), _row(ssm_norm_w[i]), head_expand)
        b_if = jnp.concatenate([_row(lstm_i_bias[i], LANES), _row(lstm_f_bias[i], LANES)], axis=1)
        y_lstm = _mlstm_mixer(x, w_lstm, b_if, _row(lstm_norm_w[i]))
        w_r = _pad_lanes(jnp.concatenate([moe_w_group[i], moe_w_expert[i]], axis=1).astype(F32)).T
        w_r_hi = w_r.astype(BF16)
        w_r_cat = jnp.concatenate([w_r_hi, (w_r - w_r_hi.astype(F32)).astype(BF16)], axis=0)
        b_r = jnp.broadcast_to(_row(jnp.concatenate([moe_b_group[i], moe_b_expert[i]]), LANES).T, (LANES, DENSE_ROWS))
        tile = min(MOE_TILE, t)
        x1, xp, tab, cnt, e_gate, e_up, e_down = _merge(
            x.reshape(t, d), y_ssm.reshape(t, -1), y_lstm.reshape(t, -1),
            w_gates, bf(w_branch_ssm[i]), bf(w_branch_lstm[i]), bf(w_out[i]),
            _row(ln1_g[i]), _row(ln1_b[i]), w_r_cat, b_r,
            moe_w_gate[i].astype(F32), moe_w_up[i].astype(F32), moe_w_down[i].astype(F32), alpha, tile)
        cnt = cnt.reshape(t // tile, LANES, LANES)[:, :MOE_EXPERTS, 0].reshape(-1)
        moe = _moe(xp, _route_rows(tab, cnt, tile).reshape(-1), cnt, e_gate, e_up, e_down, tile)
        x = _final(x1, moe, p[i].reshape(t, -1), _row(ln2_g[i]), _row(ln2_b[i]),
                   bf(ple_w_gate[i]), bf(ple_w_proj[i]), alpha).reshape(bsz, seq, d)
    return x
```

```python
import functools

import jax
import jax.numpy as jnp
from jax import lax
from jax.experimental import pallas as pl
from jax.experimental.pallas import tpu as pltpu

F32 = jnp.float32
BF16 = jnp.bfloat16

D_MODEL = 1024
PLE_DIM = 256
SSM_D_INNER = 1024
SSM_HEAD_DIM = 64
SSM_HEADS = 16
SSM_GROUPS = 4
SSM_STATE = 128
SSM_CONV = 4
SSM_XBC = SSM_D_INNER + 2 * SSM_GROUPS * SSM_STATE
LSTM_HEADS = 8
LSTM_QK = 64
LSTM_V = 128
LSTM_D_QK = LSTM_HEADS * LSTM_QK
LSTM_D_V = LSTM_HEADS * LSTM_V
CHUNK = 128
MOE_GROUPS = 8
MOE_PER_GROUP = 8
MOE_EXPERTS = 64
MOE_D_FF = 512
NORM_EPS = 1e-5
IN_PROJ_SIZES = (SSM_D_INNER, SSM_XBC, SSM_HEADS, LSTM_D_QK, LSTM_D_QK, LSTM_D_V, LSTM_D_V,
                 LSTM_HEADS, LSTM_HEADS, D_MODEL, D_MODEL)

LANES = 128
SUBLANES = 8
VMEM_LIMIT = 56 * 1024 * 1024
MOE_VMEM_LIMIT = 60 * 1024 * 1024

MIX_ROWS = 256
MIX_SEQS = 2
DENSE_ROWS = 512
MOE_TILE = 8192
MOE_CHUNK = 256
MOE_BLOCKS = (192, 256, 320, 384, 448)
MOE_WEIGHT_BUFFERS = 4
MOE_WEIGHT_SPLIT = 4
PACK_SLABS = D_MODEL // 2 // LANES
OUT_SLABS = D_MODEL // LANES


def _dot(a, b):
    return jnp.dot(a, b, preferred_element_type=F32)


def _dot_nt(a, b):
    return lax.dot_general(a, b, (((1,), (1,)), ((), ())), preferred_element_type=F32)


def _dot_tn(a, b):
    return lax.dot_general(a, b, (((0,), (0,)), ((), ())), preferred_element_type=F32)


def _sigmoid(x):
    return 1.0 / (1.0 + jnp.exp(-x))


def _softplus(x):
    return jnp.maximum(x, 0.0) + jnp.log(1.0 + jnp.exp(-jnp.abs(x)))


def _split3(x):
    hi = x.astype(BF16)
    r1 = x - hi.astype(F32)
    mid = r1.astype(BF16)
    lo = (r1 - mid.astype(F32)).astype(BF16)
    return hi, mid, lo


def _cumsum_rows(tril, x):
    hi, mid, lo = _split3(x)
    return _dot(tril, hi) + _dot(tril, mid) + _dot(tril, lo)


def _causal_masks():
    r = lax.broadcasted_iota(jnp.int32, (CHUNK, CHUNK), 0)
    c = lax.broadcasted_iota(jnp.int32, (CHUNK, CHUNK), 1)
    causal = r >= c
    return causal, jnp.where(causal, 1.0, 0.0).astype(BF16)


def _pair_cols(v, ha, hb, rows, lane):
    a = jnp.broadcast_to(v[:, ha:ha + 1], (rows, LANES))
    b = jnp.broadcast_to(v[:, hb:hb + 1], (rows, LANES))
    return jnp.where(lane < SSM_HEAD_DIM, a, b)


def _ssd_kernel(x_ref, w_ref, cw_ref, cb_ref, dtb_ref, aneg_ref, dexp_ref, nw_ref, ex_ref,
                y_ref, ext_ref, u_ref, st_ref, *, rows, nseq):
    n_slab = SSM_XBC // LANES
    tail = SUBLANES - SSM_CONV + 1

    @pl.when(pl.program_id(1) == 0)
    def _():
        ext_ref[:, :, 0:SUBLANES, :] = jnp.zeros((nseq, n_slab, SUBLANES, LANES), F32)
        st_ref[...] = jnp.zeros(st_ref.shape, F32)

    xb = x_ref[...].reshape(nseq * rows, x_ref.shape[-1]).astype(BF16)
    z = _dot(xb, w_ref[:, 0:SSM_D_INNER])
    dt_col = SSM_D_INNER + SSM_XBC
    dtr = _dot(xb, w_ref[:, dt_col:dt_col + LANES])
    dtr = jnp.where(lax.broadcasted_iota(jnp.int32, dtr.shape, 1) < SSM_HEADS, dtr, 0.0)
    for c2 in range(n_slab // 2):
        xbc = _dot(xb, w_ref[:, SSM_D_INNER + c2 * 2 * LANES:SSM_D_INNER + (c2 + 1) * 2 * LANES])
        for s in range(nseq):
            for cc in range(2):
                ext_ref[s, 2 * c2 + cc, SUBLANES:SUBLANES + rows, :] = (
                    xbc[s * rows:(s + 1) * rows, cc * LANES:(cc + 1) * LANES])
    for s in range(nseq):
        for c in range(n_slab):
            conv = cb_ref[:, c * LANES:(c + 1) * LANES]
            for k in range(SSM_CONV):
                conv = conv + cw_ref[k:k + 1, c * LANES:(c + 1) * LANES] * ext_ref[s, c, pl.ds(tail + k, rows), :]
            u_ref[s, c] = conv * _sigmoid(conv)
            ext_ref[s, c, 0:SUBLANES, :] = ext_ref[s, c, rows:rows + SUBLANES, :]

    dt = _softplus(dtr + dtb_ref[...])
    da = dt * aneg_ref[...]
    causal, tril = _causal_masks()
    lane = lax.broadcasted_iota(jnp.int32, (CHUNK, LANES), 1)
    lane1 = lax.broadcasted_iota(jnp.int32, (1, LANES), 1)
    gw = SSM_D_INNER // SSM_GROUPS
    b_slab = SSM_D_INNER // LANES
    c_slab = b_slab + SSM_GROUPS

    for ci in range(rows // CHUNK):
        for s in range(nseq):
            r0 = ci * CHUNK
            q0 = s * rows + r0
            dt_c = dt[q0:q0 + CHUNK]
            a_cs = _cumsum_rows(tril, da[q0:q0 + CHUNK])
            a_cs_t = a_cs.T
            dt_t = dt_c.T
            a_last = a_cs[CHUNK - 1:CHUNK, :]
            ea_last = jnp.exp(a_last)
            ea_x = _dot(jnp.exp(a_cs).astype(BF16), ex_ref[...])
            sw_x = _dot((dt_c * jnp.exp(a_last - a_cs)).astype(BF16), ex_ref[...])
            for g in range(SSM_GROUPS):
                bm_g = u_ref[s, b_slab + g, r0:r0 + CHUNK, :].astype(BF16)
                cm_g = u_ref[s, c_slab + g, r0:r0 + CHUNK, :].astype(BF16)
                cb = _dot_nt(cm_g, bm_g)
                st_g = st_ref[s, :, g * gw:(g + 1) * gw]
                y_off = _dot(cm_g, st_g.astype(BF16))
                yy, xw, eal_x = [], [], []
                for pr in range(2):
                    ha = 4 * g + 2 * pr
                    hb = ha + 1
                    lo = g * gw + pr * LANES
                    xs_p = u_ref[s, 2 * g + pr, r0:r0 + CHUNK, :]
                    xs_pb = xs_p.astype(BF16)
                    rhs = jnp.concatenate([jnp.where(lane < SSM_HEAD_DIM, xs_pb, 0).astype(BF16),
                                           jnp.where(lane >= SSM_HEAD_DIM, xs_pb, 0).astype(BF16)], axis=0)
                    gs = []
                    for h in (ha, hb):
                        colb = jnp.broadcast_to(a_cs[:, h:h + 1], (CHUNK, CHUNK))
                        rowb = jnp.broadcast_to(a_cs_t[h:h + 1, :], (CHUNK, CHUNK))
                        dec = jnp.exp(jnp.where(causal, colb - rowb, -jnp.inf))
                        gs.append((cb * dec * jnp.broadcast_to(dt_t[h:h + 1, :], (CHUNK, CHUNK))).astype(BF16))
                    y_p = (_dot(jnp.concatenate(gs, axis=1), rhs)
                           + y_off[:, pr * LANES:(pr + 1) * LANES] * ea_x[:, lo:lo + LANES]
                           + dexp_ref[:, lo:lo + LANES] * xs_p)
                    zz = z[q0:q0 + CHUNK, lo:lo + LANES]
                    yy.append(y_p * (zz * _sigmoid(zz)))
                    xw.append((xs_p * sw_x[:, lo:lo + LANES]).astype(BF16))
                    eal_x.append(_pair_cols(ea_last, ha, hb, 1, lane1))
                st_ref[s, :, g * gw:(g + 1) * gw] = (st_g * jnp.concatenate(eal_x, axis=1)
                                                     + _dot_tn(bm_g, jnp.concatenate(xw, axis=1)))
                ms = sum(jnp.sum(t * t, axis=-1, keepdims=True) for t in yy) * (1.0 / gw)
                inv = lax.rsqrt(ms + NORM_EPS)
                for pr in range(2):
                    lo = g * gw + pr * LANES
                    y_ref[s, r0:r0 + CHUNK, lo:lo + LANES] = (yy[pr] * inv * nw_ref[:, lo:lo + LANES]).astype(BF16)


def _ssd_mixer(x, w, cw, cb, dtb, aneg, dexp, nw, ex):
    b, s, d = x.shape
    rows = MIX_ROWS
    nseq = MIX_SEQS
    const = lambda shape: pl.BlockSpec(shape, lambda i, j: (0,) * len(shape), pipeline_mode=pl.Buffered(1))
    return pl.pallas_call(
        functools.partial(_ssd_kernel, rows=rows, nseq=nseq),
        grid=(b // nseq, s // rows),
        in_specs=[pl.BlockSpec((nseq, rows, d), lambda i, j: (i, j, 0)),
                  const(w.shape), const(cw.shape), const(cb.shape),
                  const(dtb.shape), const(aneg.shape), const(dexp.shape), const(nw.shape), const(ex.shape)],
        out_specs=pl.BlockSpec((nseq, rows, SSM_D_INNER), lambda i, j: (i, j, 0)),
        out_shape=jax.ShapeDtypeStruct((b, s, SSM_D_INNER), BF16),
        scratch_shapes=[pltpu.VMEM((nseq, SSM_XBC // LANES, SUBLANES + rows, LANES), F32),
                        pltpu.VMEM((nseq, SSM_XBC // LANES, rows, LANES), F32),
                        pltpu.VMEM((nseq, SSM_STATE, SSM_D_INNER), F32)],
        compiler_params=pltpu.CompilerParams(dimension_semantics=("arbitrary", "arbitrary"),
                                             vmem_limit_bytes=VMEM_LIMIT),
        name="ssd_mixer",
    )(x, w, cw, cb, dtb, aneg, dexp, nw, ex)


def _mlstm_kernel(x_ref, w_ref, bif_ref, nw_ref,
                  y_ref, c_ref, m_ref, *, rows, nseq):
    @pl.when(pl.program_id(1) == 0)
    def _():
        c_ref[...] = jnp.zeros(c_ref.shape, F32)
        m_ref[...] = jnp.full(m_ref.shape, -jnp.inf, F32)

    xb = x_ref[...].reshape(nseq * rows, x_ref.shape[-1]).astype(BF16)
    c_k, c_v, c_o, c_g = LSTM_D_QK, 2 * LSTM_D_QK, 2 * LSTM_D_QK + LSTM_D_V, 2 * LSTM_D_QK + 2 * LSTM_D_V
    q = _dot(xb, w_ref[:, 0:c_k])
    k = _dot(xb, w_ref[:, c_k:c_v])
    v = _dot(xb, w_ref[:, c_v:c_o])
    o = _dot(xb, w_ref[:, c_o:c_g])
    gif = _dot(xb, w_ref[:, c_g:c_g + 2 * LANES]) + bif_ref[...]
    causal, tril = _causal_masks()
    lane = lax.broadcasted_iota(jnp.int32, (CHUNK, LANES), 1)
    lane1 = lax.broadcasted_iota(jnp.int32, (1, LANES), 1)
    row = lax.broadcasted_iota(jnp.int32, (LANES, 1), 0)
    head_lane = lane < LSTM_HEADS
    ones = jnp.ones((CHUNK, LANES), BF16)

    for ci in range(rows // CHUNK):
        r0 = ci * CHUNK
        pre = {}
        for s in range(nseq):
            q0 = s * rows + r0
            log_i = gif[q0:q0 + CHUNK, 0:LANES]
            f_pre = gif[q0:q0 + CHUNK, LANES:2 * LANES]
            log_f = jnp.where(head_lane, jnp.minimum(f_pre, 0.0) - jnp.log(1.0 + jnp.exp(-jnp.abs(f_pre))), 0.0)
            fcum = _cumsum_rows(tril, log_f)
            f_tot = fcum[CHUNK - 1:CHUNK, :]
            gk = log_i - fcum
            m_loc = jnp.max(f_tot + gk, axis=0, keepdims=True)
            g_t = gk.T
            m_prev = m_ref[s]
            m_new = jnp.maximum(f_tot + m_prev, m_loc)
            s_prev = jnp.exp(f_tot + m_prev - m_new)
            s_loc = jnp.exp(m_loc - m_new)
            m_ref[s] = jnp.where(lane1 < LSTM_HEADS, m_new, -jnp.inf)
            w_shift = f_tot - m_loc
            pre[s] = (q0, fcum, gk, g_t, m_prev, s_prev, s_loc, w_shift)

        for pr in range(LSTM_HEADS // 2):
            st = {}
            for s in range(nseq):
                q0, fcum, gk, g_t, m_prev, s_prev, s_loc, w_shift = pre[s]
                q_p = q[q0:q0 + CHUNK, pr * LANES:(pr + 1) * LANES]
                k_p = k[q0:q0 + CHUNK, pr * LANES:(pr + 1) * LANES]
                k_pb = k_p.astype(BF16)
                c_pair = c_ref[s, pr]
                c_pair_b = c_pair.astype(BF16)
                c_new = c_pair * jnp.where(row < LSTM_QK,
                                           jnp.broadcast_to(s_prev[:, 2 * pr:2 * pr + 1], (LANES, 1)),
                                           jnp.broadcast_to(s_prev[:, 2 * pr + 1:2 * pr + 2], (LANES, 1)))
                st[s] = [q_p, k_p, k_pb, c_pair_b, c_new]
            for hh in range(2):
                for s in range(nseq):
                    q0, fcum, gk, g_t, m_prev, s_prev, s_loc, w_shift = pre[s]
                    q_p, k_p, k_pb, c_pair_b, c_new = st[s]
                    h = 2 * pr + hh
                    in_head = (lane < LSTM_QK) if hh == 0 else (lane >= LSTM_QK)
                    v_h = v[q0:q0 + CHUNK, h * LSTM_V:(h + 1) * LSTM_V].astype(BF16)
                    v_ext = jnp.concatenate([v_h, ones], axis=1)
                    q_m = jnp.where(in_head, q_p, 0.0)
                    fcol = jnp.broadcast_to(fcum[:, h:h + 1], (CHUNK, CHUNK))
                    gcol = jnp.broadcast_to(gk[:, h:h + 1], (CHUNK, LANES))
                    rowb = jnp.broadcast_to(g_t[h:h + 1, :], (CHUNK, CHUNK))
                    log_d = jnp.where(causal, fcol + rowb, -jnp.inf)
                    inter_log = fcol + m_prev[:, h:h + 1]
                    m_t = jnp.maximum(inter_log, jnp.max(log_d, axis=-1, keepdims=True))
                    scores = _dot_nt(q_m.astype(BF16), k_pb) * jnp.exp(log_d - m_t)
                    inter_w = jnp.exp(inter_log - m_t)
                    lhs = jnp.concatenate([scores.astype(BF16), (q_m * inter_w).astype(BF16)], axis=1)
                    res = _dot(lhs, jnp.concatenate([v_ext, c_pair_b], axis=0))
                    hv = res[:, 0:LSTM_V] / jnp.maximum(jnp.abs(res[:, LSTM_V:2 * LSTM_V]), jnp.exp(-m_t))
                    ms = jnp.mean(hv * hv, axis=-1, keepdims=True)
                    o_h = o[q0:q0 + CHUNK, h * LSTM_V:(h + 1) * LSTM_V]
                    y_ref[s, r0:r0 + CHUNK, h * LSTM_V:(h + 1) * LSTM_V] = (
                        _sigmoid(o_h) * (hv * lax.rsqrt(ms + NORM_EPS) * nw_ref[:, h * LSTM_V:(h + 1) * LSTM_V])
                    ).astype(BF16)
                    kw = jnp.where(in_head, k_p * jnp.exp(gcol + w_shift[:, h:h + 1]), 0.0)
                    st[s][4] = c_new + s_loc[:, h:h + 1] * _dot_tn(kw.astype(BF16), v_ext)
            for s in range(nseq):
                c_ref[s, pr] = st[s][4]


def _mlstm_mixer(x, w, bif, nw):
    b, s, d = x.shape
    rows = MIX_ROWS
    nseq = MIX_SEQS
    const = lambda shape: pl.BlockSpec(shape, lambda i, j: (0,) * len(shape), pipeline_mode=pl.Buffered(1))
    return pl.pallas_call(
        functools.partial(_mlstm_kernel, rows=rows, nseq=nseq),
        grid=(b // nseq, s // rows),
        in_specs=[pl.BlockSpec((nseq, rows, d), lambda i, j: (i, j, 0)),
                  const(w.shape), const(bif.shape), const(nw.shape)],
        out_specs=pl.BlockSpec((nseq, rows, LSTM_D_V), lambda i, j: (i, j, 0)),
        out_shape=jax.ShapeDtypeStruct((b, s, LSTM_D_V), BF16),
        scratch_shapes=[pltpu.VMEM((nseq, LSTM_HEADS // 2, 2 * LSTM_QK, 2 * LSTM_V), F32),
                        pltpu.VMEM((nseq, 1, LANES), F32)],
        compiler_params=pltpu.CompilerParams(dimension_semantics=("arbitrary", "arbitrary"),
                                             vmem_limit_bytes=VMEM_LIMIT),
        name="mlstm_mixer",
    )(x, w, bif, nw)


def _layer_norm(t, g, b):
    mu = jnp.mean(t, axis=-1, keepdims=True)
    tc = t - mu
    var = jnp.mean(tc * tc, axis=-1, keepdims=True)
    return tc * lax.rsqrt(var + NORM_EPS) * g + b


def _pack_pair(a, b):
    pa = lax.bitcast_convert_type(a.astype(BF16).astype(F32), jnp.uint32)
    pb = lax.bitcast_convert_type(b.astype(BF16).astype(F32), jnp.uint32)
    return pa | (pb >> 16)


def _unpack_pair(w):
    return (lax.bitcast_convert_type(w & jnp.uint32(0xFFFF0000), F32),
            lax.bitcast_convert_type(w << 16, F32))


def _route(lt):
    n = lt.shape[1]
    sub = lax.broadcasted_iota(jnp.int32, (MOE_PER_GROUP, n), 0).astype(F32)
    big = float(LANES)
    neg = -jnp.inf
    gl = lt[0:MOE_GROUPS, :]
    gmax = jnp.max(gl, axis=0, keepdims=True)
    gidx = jnp.min(jnp.where(gl == gmax, sub, big), axis=0, keepdims=True)
    grp_p = 1.0 / jnp.sum(jnp.exp(gl - gmax), axis=0, keepdims=True)
    el = lt[MOE_GROUPS:MOE_GROUPS + MOE_PER_GROUP, :]
    for g in range(1, MOE_GROUPS):
        lo = MOE_GROUPS + g * MOE_PER_GROUP
        el = jnp.where(gidx == g, lt[lo:lo + MOE_PER_GROUP, :], el)
    m1 = jnp.max(el, axis=0, keepdims=True)
    i1 = jnp.min(jnp.where(el == m1, sub, big), axis=0, keepdims=True)
    el2 = jnp.where(sub == i1, neg, el)
    m2 = jnp.max(el2, axis=0, keepdims=True)
    i2 = jnp.min(jnp.where(el2 == m2, sub, big), axis=0, keepdims=True)
    e21 = jnp.exp(m2 - m1)
    g1 = grp_p / (1.0 + e21)
    g2 = grp_p * e21 / (1.0 + e21)
    return gidx * MOE_PER_GROUP + i1, gidx * MOE_PER_GROUP + i2, g1, g2


def _merge_kernel(x_ref, ys_ref, yl_ref, wg_ref, wa_ref, wb_ref, wo_ref, g_ref, b_ref,
                  wr_ref, br_ref, eg_ref, eu_ref, ed_ref,
                  x1_ref, xp_ref, tab_ref, cnt_ref, egb_ref, eub_ref, edb_ref, carry_ref,
                  *, alpha, steps_per_tile, cast_steps):
    @pl.when(pl.program_id(0) % steps_per_tile == 0)
    def _():
        carry_ref[...] = jnp.zeros(carry_ref.shape, F32)

    @pl.when(pl.program_id(0) < cast_steps)
    def _():
        egb_ref[...] = eg_ref[...].astype(BF16)
        eub_ref[...] = eu_ref[...].astype(BF16)
        edb_ref[...] = ed_ref[...].astype(BF16)

    x = x_ref[...]
    rows = x.shape[0]
    xb = x.astype(BF16)
    d = x.shape[1]
    merged = (_sigmoid(_dot(xb, wg_ref[:, 0:d])) * _dot(ys_ref[...], wa_ref[...])
              + _sigmoid(_dot(xb, wg_ref[:, d:2 * d])) * _dot(yl_ref[...], wb_ref[...]))
    x1 = _layer_norm(alpha * x + _dot(merged.astype(BF16), wo_ref[...]), g_ref[...], b_ref[...])
    x1_ref[...] = x1
    half = x1.shape[1] // 2
    xp = _pack_pair(x1[:, :half], x1[:, half:])
    for c in range(PACK_SLABS):
        xp_ref[pl.ds(c, rows, stride=PACK_SLABS), :] = xp[:, c * LANES:(c + 1) * LANES]
    x_hi = x1.astype(BF16)
    x_lo = (x1 - x_hi.astype(F32)).astype(BF16)
    by_hi = _dot_nt(wr_ref[...], x_hi)
    lt = by_hi[0:LANES] + by_hi[LANES:2 * LANES] + _dot_nt(wr_ref[0:LANES, :], x_lo) + br_ref[...]
    e1, e2, g1, g2 = _route(lt)
    expert = lax.broadcasted_iota(jnp.int32, (LANES, rows), 0).astype(F32)
    oh1 = expert == e1
    oh2 = expert == e2
    ohs = jnp.where(oh1, 1.0, jnp.where(oh2, 1.0, 0.0))
    r_i = lax.broadcasted_iota(jnp.int32, (rows, rows), 0)
    c_i = lax.broadcasted_iota(jnp.int32, (rows, rows), 1)
    before = jnp.where(r_i < c_i, 1.0, 0.0).astype(BF16)
    carry = carry_ref[...]
    seen = _dot(ohs.astype(BF16), before) + jnp.concatenate([carry] * (rows // LANES), axis=1)
    r1 = jnp.sum(jnp.where(oh1, seen, 0.0), axis=0, keepdims=True)
    r2 = jnp.sum(jnp.where(oh2, seen, 0.0), axis=0, keepdims=True)
    carry = carry + jnp.sum(ohs, axis=1, keepdims=True)
    carry_ref[...] = carry
    cnt_ref[...] = carry.astype(jnp.int32)
    as_int = lambda v: v.astype(jnp.int32)
    as_bits = lambda v: lax.bitcast_convert_type(v, jnp.int32)
    tab_ref[...] = jnp.concatenate(
        [as_int(e1), as_int(e2), as_int(r1) * PACK_SLABS, as_int(r2) * PACK_SLABS, as_bits(g1), as_bits(g2),
         jnp.zeros((2, rows), jnp.int32)], axis=0)


def _merge(x2d, ys, yl, wg, wa, wb, wo, g, b, wr, br, eg, eu, ed, alpha, tile):
    t, d = x2d.shape
    rows = DENSE_ROWS
    spt = tile // rows
    n_steps = t // rows
    per_step = -(-MOE_EXPERTS // n_steps)
    cast_steps = MOE_EXPERTS // per_step
    assert per_step * cast_steps == MOE_EXPERTS and cast_steps <= n_steps
    const = lambda shape: pl.BlockSpec(shape, lambda i: (0,) * len(shape), pipeline_mode=pl.Buffered(1))
    rowblk = lambda w: pl.BlockSpec((rows, w), lambda i: (i, 0))
    expert = lambda w: pl.BlockSpec((per_step,) + w.shape[1:], lambda i: (jnp.minimum(i, cast_steps - 1), 0, 0))
    return pl.pallas_call(
        functools.partial(_merge_kernel, alpha=alpha, steps_per_tile=spt, cast_steps=cast_steps),
        grid=(n_steps,),
        in_specs=[rowblk(d), rowblk(SSM_D_INNER), rowblk(LSTM_D_V),
                  const(wg.shape), const(wa.shape), const(wb.shape), const(wo.shape),
                  const(g.shape), const(b.shape), const(wr.shape), const(br.shape),
                  expert(eg), expert(eu), expert(ed)],
        out_specs=[rowblk(d),
                   pl.BlockSpec((rows * PACK_SLABS, LANES), lambda i: (i, 0)),
                   pl.BlockSpec((SUBLANES, rows), lambda i: (i // spt, i % spt)),
                   pl.BlockSpec((LANES, LANES), lambda i: (i // spt, 0)),
                   expert(eg), expert(eu), expert(ed)],
        out_shape=[jax.ShapeDtypeStruct((t, d), F32),
                   jax.ShapeDtypeStruct((t * PACK_SLABS, LANES), jnp.uint32),
                   jax.ShapeDtypeStruct((t // tile * SUBLANES, tile), jnp.int32),
                   jax.ShapeDtypeStruct((t // tile * LANES, LANES), jnp.int32),
                   jax.ShapeDtypeStruct(eg.shape, BF16), jax.ShapeDtypeStruct(eu.shape, BF16),
                   jax.ShapeDtypeStruct(ed.shape, BF16)],
        scratch_shapes=[pltpu.VMEM((LANES, LANES), F32)],
        compiler_params=pltpu.CompilerParams(dimension_semantics=("arbitrary",), vmem_limit_bytes=VMEM_LIMIT),
        name="merge_ln_route",
    )(x2d, ys, yl, wg, wa, wb, wo, g, b, wr, br, eg, eu, ed)


def _segment_rows(count):
    groups = lax.shift_right_logical(count + (SUBLANES - 1), SUBLANES.bit_length() - 1)
    return groups * (SUBLANES * PACK_SLABS)


def _route_rows_kernel(cnt_ref, tab_ref, out_ref):
    i = pl.program_id(0)
    tab = tab_ref[...]
    starts = jnp.zeros(tab.shape, jnp.int32)
    off = jnp.int32(0)
    for e in range(MOE_EXPERTS):
        starts = jnp.where(tab == e, off, starts)
        off = off + _segment_rows(cnt_ref[i * MOE_EXPERTS + e])
    row = lax.broadcasted_iota(jnp.int32, tab.shape, 0)
    res = jnp.where(lax.shift_right_logical(row, 1) == 1, tab + pltpu.roll(starts, 2, axis=0), tab)
    blocks = tab.shape[1] // LANES
    for c in range(blocks):
        out_ref[pl.ds(c, SUBLANES, stride=blocks), :] = res[:, c * LANES:(c + 1) * LANES]


def _route_rows(tab, cnt, tile):
    n_tiles = tab.shape[0] // SUBLANES
    rows = SUBLANES * tile // LANES
    return pl.pallas_call(
        _route_rows_kernel,
        grid_spec=pltpu.PrefetchScalarGridSpec(
            num_scalar_prefetch=1, grid=(n_tiles,),
            in_specs=[pl.BlockSpec((SUBLANES, tile), lambda i, c: (i, 0))],
            out_specs=pl.BlockSpec((rows, LANES), lambda i, c: (i, 0))),
        out_shape=jax.ShapeDtypeStruct((n_tiles * rows, LANES), jnp.int32),
        compiler_params=pltpu.CompilerParams(dimension_semantics=("arbitrary",), vmem_limit_bytes=VMEM_LIMIT),
        name="route_rows",
    )(cnt, tab)


def _moe_kernel(cnt_ref, tab_ref, xp_hbm, wg_hbm, wu_hbm, wd_hbm, out_hbm, xys, ybuf, off_ref,
                wg_buf, wu_buf, wd_buf, xin, xout, wsem, isem, osem, *, tile, chunk, n_steps):
    i = pl.program_id(0)
    group = SUBLANES * PACK_SLABS
    half = PACK_SLABS * LANES
    weights = ((wg_hbm, wg_buf), (wu_hbm, wu_buf), (wd_hbm, wd_buf))

    def weight_copies(expert, into):
        copies = []
        for k, (hbm, buf) in enumerate(weights):
            part = buf.shape[1] // MOE_WEIGHT_SPLIT
            for q in range(MOE_WEIGHT_SPLIT):
                copies.append(pltpu.make_async_copy(hbm.at[expert, pl.ds(q * part, part)],
                                                    buf.at[into, pl.ds(q * part, part)], wsem.at[k, into]))
        return copies

    @pl.when(i == 0)
    def _():
        for ahead in range(MOE_WEIGHT_BUFFERS - 1):
            for cp in weight_copies(ahead % MOE_EXPERTS, ahead):
                cp.start()

    n_chunks = tile // chunk
    rolled = lambda trips: trips + jnp.minimum(i, 0)

    def in_buf(c):
        return pl.multiple_of((c % 2) * (chunk * PACK_SLABS), group)

    def out_buf(c):
        return pl.multiple_of((c % 2) * (chunk * OUT_SLABS), SUBLANES)

    def in_copy(c):
        row0 = pl.multiple_of((i * tile + c * chunk) * PACK_SLABS, group)
        return pltpu.make_async_copy(xp_hbm.at[pl.ds(row0, chunk * PACK_SLABS)],
                                     xin.at[pl.ds(in_buf(c), chunk * PACK_SLABS)], isem.at[c % 2])

    def out_copy(c):
        row0 = pl.multiple_of((i * tile + c * chunk) * OUT_SLABS, SUBLANES)
        return pltpu.make_async_copy(xout.at[pl.ds(out_buf(c), chunk * OUT_SLABS)],
                                     out_hbm.at[pl.ds(row0, chunk * OUT_SLABS)], osem.at[c % 2])

    def sort_rows():
        in_copy(0).start()

        def offsets(e, acc):
            padded = _segment_rows(cnt_ref[i * MOE_EXPERTS + e])
            off_ref[e] = acc
            end = acc + padded

            @pl.when(padded > 0)
            def _():
                xys[pl.ds(pl.multiple_of(end - group, group), group), :] = jnp.zeros((group, LANES), jnp.uint32)
            return end

        total = lax.fori_loop(0, MOE_EXPERTS, offsets, 0)
        off_ref[MOE_EXPERTS] = total
        tail = MOE_BLOCKS[-1] * PACK_SLABS
        xys[pl.ds(pl.multiple_of(total, group), tail), :] = jnp.zeros((tail, LANES), jnp.uint32)

        def distribute_chunk(c, carry):
            @pl.when(c + 1 < n_chunks)
            def _():
                in_copy(c + 1).start()
            in_copy(c).wait()

            g0 = c * (chunk // SUBLANES)
            shift = in_buf(c) - c * (chunk * PACK_SLABS)

            def distribute(tg, carry2):
                for u in range(SUBLANES):
                    t = tg * SUBLANES + u
                    d1 = pl.multiple_of(tab_ref[2 * tile + t], PACK_SLABS)
                    d2 = pl.multiple_of(tab_ref[3 * tile + t], PACK_SLABS)
                    row = xin[pl.ds(pl.multiple_of(shift + t * PACK_SLABS, PACK_SLABS), PACK_SLABS), :]
                    xys[pl.ds(d1, PACK_SLABS), :] = row
                    xys[pl.ds(d2, PACK_SLABS), :] = row
                return carry2

            lax.fori_loop(g0, g0 + rolled(chunk // SUBLANES), distribute, 0)
            return carry

        lax.fori_loop(0, rolled(n_chunks), distribute_chunk, 0)

    def expert_block(refs, base, rows, valid):
        wg_ref, wu_ref, wd_ref = refs
        words = [xys[pl.ds(base + c, rows, stride=PACK_SLABS), :] for c in range(PACK_SLABS)]
        parts = [_unpack_pair(w) for w in words]
        x_hi = jnp.concatenate([p[0].astype(BF16) for p in parts], axis=1)
        x_lo = jnp.concatenate([p[1].astype(BF16) for p in parts], axis=1)
        hg = _dot(x_hi, wg_ref[0:half, :]) + _dot(x_lo, wg_ref[half:2 * half, :])
        hu = _dot(x_hi, wu_ref[0:half, :]) + _dot(x_lo, wu_ref[half:2 * half, :])
        y = _dot((hg * _sigmoid(hg) * hu).astype(BF16), wd_ref[...])
        for c in range(PACK_SLABS):
            ybuf[pl.ds(c, rows, stride=PACK_SLABS), :] = _pack_pair(
                y[:, c * LANES:(c + 1) * LANES], y[:, half + c * LANES:half + (c + 1) * LANES])

        def copy(g, c):
            r = pl.multiple_of(g * group, group)
            xys[pl.ds(base + r, group), :] = ybuf[pl.ds(r, group), :]
            return c

        lax.fori_loop(0, valid // group, copy, 0)

    def expert_step(j, carry):
        step = i * MOE_EXPERTS + j
        slot = step % MOE_WEIGHT_BUFFERS
        nxt = step + MOE_WEIGHT_BUFFERS - 1

        @pl.when(nxt < n_steps)
        def _():
            for cp in weight_copies(nxt % MOE_EXPERTS, nxt % MOE_WEIGHT_BUFFERS):
                cp.start()

        for cp in weight_copies(j, slot):
            cp.wait()
        refs = (wg_buf.at[slot], wu_buf.at[slot], wd_buf.at[slot])
        start = off_ref[j]
        n = off_ref[j + 1] - start
        lo = 0
        for rows in MOE_BLOCKS:
            hi = rows * PACK_SLABS

            @pl.when(jnp.logical_and(n > lo, n <= hi))
            def _(rows=rows):
                expert_block(refs, pl.multiple_of(start, group), rows, n)
            lo = hi

        @pl.when(n > lo)
        def _():
            span = MOE_BLOCKS[0] * PACK_SLABS

            def block(bi, carry2):
                expert_block(refs, pl.multiple_of(start + bi * span, group), MOE_BLOCKS[0],
                             jnp.minimum(span, n - bi * span))
                return carry2

            lax.fori_loop(0, (n + span - 1) // span, block, 0)
        return carry

    def combine_rows():
        def combine_chunk(c, carry):
            @pl.when(c >= 2)
            def _():
                out_copy(c - 2).wait()

            g0 = c * (chunk // SUBLANES)
            shift = out_buf(c) - c * (chunk * OUT_SLABS)

            def combine(tg, carry2):
                for u in range(SUBLANES):
                    t = tg * SUBLANES + u
                    g1 = lax.bitcast_convert_type(tab_ref[4 * tile + t], F32)
                    g2 = lax.bitcast_convert_type(tab_ref[5 * tile + t], F32)
                    a_hi, a_lo = _unpack_pair(
                        xys[pl.ds(pl.multiple_of(tab_ref[2 * tile + t], PACK_SLABS), PACK_SLABS), :])
                    b_hi, b_lo = _unpack_pair(
                        xys[pl.ds(pl.multiple_of(tab_ref[3 * tile + t], PACK_SLABS), PACK_SLABS), :])
                    xout[pl.ds(pl.multiple_of(shift + t * OUT_SLABS, OUT_SLABS), OUT_SLABS), :] = jnp.concatenate(
                        [g1 * a_hi + g2 * b_hi, g1 * a_lo + g2 * b_lo], axis=0)
                return carry2

            lax.fori_loop(g0, g0 + rolled(chunk // SUBLANES), combine, 0)
            out_copy(c).start()
            return carry

        lax.fori_loop(0, rolled(n_chunks), combine_chunk, 0)
        for c in range(max(0, n_chunks - 2), n_chunks):
            out_copy(c).wait()

    sort_rows()
    lax.fori_loop(0, rolled(MOE_EXPERTS), expert_step, 0)
    combine_rows()


def _moe(xp, tab, cnt, wg, wu, wd, tile):
    t = xp.shape[0] // PACK_SLABS
    d = D_MODEL
    n_tiles = t // tile
    chunk = min(MOE_CHUNK, tile)
    seg_rows = 2 * tile + MOE_EXPERTS * SUBLANES + MOE_BLOCKS[-1]
    grid_spec = pltpu.PrefetchScalarGridSpec(
        num_scalar_prefetch=1,
        grid=(n_tiles,),
        in_specs=[pl.BlockSpec((SUBLANES * tile,), lambda i, c: (i,), memory_space=pltpu.SMEM,
                               pipeline_mode=pl.Buffered(1)),
                  pl.BlockSpec(memory_space=pl.ANY), pl.BlockSpec(memory_space=pl.ANY),
                  pl.BlockSpec(memory_space=pl.ANY), pl.BlockSpec(memory_space=pl.ANY)],
        out_specs=pl.BlockSpec(memory_space=pl.ANY),
        scratch_shapes=[pltpu.VMEM((seg_rows * PACK_SLABS, LANES), jnp.uint32),
                        pltpu.VMEM((MOE_BLOCKS[-1] * PACK_SLABS, LANES), jnp.uint32),
                        pltpu.SMEM((LANES,), jnp.int32),
                        pltpu.VMEM((MOE_WEIGHT_BUFFERS, d, MOE_D_FF), BF16),
                        pltpu.VMEM((MOE_WEIGHT_BUFFERS, d, MOE_D_FF), BF16),
                        pltpu.VMEM((MOE_WEIGHT_BUFFERS, MOE_D_FF, d), BF16),
                        pltpu.VMEM((2 * chunk * PACK_SLABS, LANES), jnp.uint32),
                        pltpu.VMEM((2 * chunk * OUT_SLABS, LANES), F32),
                        pltpu.SemaphoreType.DMA((3, MOE_WEIGHT_BUFFERS)),
                        pltpu.SemaphoreType.DMA((2,)), pltpu.SemaphoreType.DMA((2,))],
    )
    return pl.pallas_call(
        functools.partial(_moe_kernel, tile=tile, chunk=chunk, n_steps=n_tiles * MOE_EXPERTS),
        grid_spec=grid_spec,
        out_shape=jax.ShapeDtypeStruct((t * OUT_SLABS, LANES), F32),
        compiler_params=pltpu.CompilerParams(dimension_semantics=("arbitrary",),
                                             vmem_limit_bytes=MOE_VMEM_LIMIT),
        name="moe_experts",
    )(cnt, tab, xp, wg, wu, wd)


def _final_kernel(x1_ref, moe_ref, p_ref, g_ref, b_ref, wpg_ref, wpp_ref, out_ref, *, alpha):
    rows = x1_ref.shape[0]
    moe = jnp.concatenate([moe_ref[pl.ds(c, rows, stride=OUT_SLABS), :] for c in range(OUT_SLABS)], axis=1)
    x2 = _layer_norm(alpha * x1_ref[...] + moe, g_ref[...], b_ref[...])
    gate = _sigmoid(_dot(x2.astype(BF16), wpg_ref[...]))
    out_ref[...] = x2 + gate * _dot(p_ref[...].astype(BF16), wpp_ref[...])


def _final(x1, moe, p2d, g, b, wpg, wpp, alpha):
    t, d = x1.shape
    rows = DENSE_ROWS
    const = lambda shape: pl.BlockSpec(shape, lambda i: (0,) * len(shape))
    rowblk = lambda w: pl.BlockSpec((rows, w), lambda i: (i, 0))
    return pl.pallas_call(
        functools.partial(_final_kernel, alpha=alpha),
        grid=(t // rows,),
        in_specs=[rowblk(d), pl.BlockSpec((rows * OUT_SLABS, LANES), lambda i: (i, 0)), rowblk(p2d.shape[1]),
                  const(g.shape), const(b.shape), const(wpg.shape), const(wpp.shape)],
        out_specs=rowblk(d),
        out_shape=jax.ShapeDtypeStruct((t, d), F32),
        compiler_params=pltpu.CompilerParams(dimension_semantics=("arbitrary",), vmem_limit_bytes=VMEM_LIMIT),
        name="final_ln_ple",
    )(x1, moe, p2d, g, b, wpg, wpp)


def _proj_weights_kernel(w_ref, ssd_ref, lstm_ref, gate_ref):
    w = w_ref[...]
    cols = [0]
    for sz in IN_PROJ_SIZES:
        cols.append(cols[-1] + sz)
    lane = lax.broadcasted_iota(jnp.int32, (w.shape[0], LANES), 1)
    gate_block = lambda c: jnp.where(lane < LSTM_HEADS, w[:, c:c + LANES], 0.0)
    ssd_ref[...] = w[:, 0:cols[2] + LANES].astype(BF16)
    lstm_ref[...] = jnp.concatenate(
        [w[:, cols[3]:cols[4]] * (LSTM_QK ** -0.5), w[:, cols[4]:cols[7]], gate_block(cols[7]), gate_block(cols[8])],
        axis=1).astype(BF16)
    gate_ref[...] = w[:, cols[9]:cols[11]].astype(BF16)


def _proj_weights(w_all, layer):
    _, d, n = w_all.shape
    rows = LANES
    widths = (IN_PROJ_SIZES[0] + IN_PROJ_SIZES[1] + LANES, 2 * LSTM_D_QK + 2 * LSTM_D_V + 2 * LANES, 2 * D_MODEL)
    return pl.pallas_call(
        _proj_weights_kernel,
        grid=(d // rows,),
        in_specs=[pl.BlockSpec((None, rows, n), lambda i: (layer, i, 0))],
        out_specs=[pl.BlockSpec((rows, wd), lambda i: (i, 0)) for wd in widths],
        out_shape=[jax.ShapeDtypeStruct((d, wd), BF16) for wd in widths],
        compiler_params=pltpu.CompilerParams(dimension_semantics=("arbitrary",), vmem_limit_bytes=VMEM_LIMIT),
        name="proj_weights",
    )(w_all.astype(F32))


def _pad_lanes(w, width=LANES):
    return jnp.pad(w, ((0, 0), (0, width - w.shape[1])))


def _row(v, width=None):
    v = v.astype(F32).reshape(1, -1)
    return v if width is None else _pad_lanes(v, width)


def kernel(x, p, w_in, ssm_conv_w, ssm_conv_b, ssm_dt_bias, ssm_a_log, ssm_d, ssm_norm_w, lstm_i_bias, lstm_f_bias, lstm_norm_w, w_branch_ssm, w_branch_lstm, w_out, ln1_g, ln1_b, moe_w_group, moe_b_group, moe_w_expert, moe_b_expert, moe_w_gate, moe_w_up, moe_w_down, ln2_g, ln2_b, ple_w_proj, ple_w_gate):
    depth = w_in.shape[0]
    bsz, seq, d = x.shape
    t = bsz * seq
    alpha = (2.0 * depth) ** 0.25
    head_expand = (jnp.arange(LANES, dtype=jnp.int32)[:, None]
                   == jnp.arange(SSM_D_INNER, dtype=jnp.int32)[None, :] // SSM_HEAD_DIM).astype(BF16)
    for i in range(depth):
        bf = lambda w: w.astype(BF16)
        w_ssd, w_lstm, w_gates = _proj_weights(w_in, i)
        y_ssm = _ssd_mixer(
            x, w_ssd,
            ssm_conv_w[i].astype(F32), _row(ssm_conv_b[i]), _row(ssm_dt_bias[i], LANES),
            _row(-jnp.exp(ssm_a_log[i].astype(F32)), LANES),
            _row(jnp.repeat(ssm_d[i].astype(F32), SSM_HEAD_DIM)), _row(ssm_norm_w[i]), head_expand)
        b_if = jnp.concatenate([_row(lstm_i_bias[i], LANES), _row(lstm_f_bias[i], LANES)], axis=1)
        y_lstm = _mlstm_mixer(x, w_lstm, b_if, _row(lstm_norm_w[i]))
        w_r = _pad_lanes(jnp.concatenate([moe_w_group[i], moe_w_expert[i]], axis=1).astype(F32)).T
        w_r_hi = w_r.astype(BF16)
        w_r_cat = jnp.concatenate([w_r_hi, (w_r - w_r_hi.astype(F32)).astype(BF16)], axis=0)
        b_r = jnp.broadcast_to(_row(jnp.concatenate([moe_b_group[i], moe_b_expert[i]]), LANES).T, (LANES, DENSE_ROWS))
        tile = min(MOE_TILE, t)
        x1, xp, tab, cnt, e_gate, e_up, e_down = _merge(
            x.reshape(t, d), y_ssm.reshape(t, -1), y_lstm.reshape(t, -1),
            w_gates, bf(w_branch_ssm[i]), bf(w_branch_lstm[i]), bf(w_out[i]),
            _row(ln1_g[i]), _row(ln1_b[i]), w_r_cat, b_r,
            moe_w_gate[i].astype(F32), moe_w_up[i].astype(F32), moe_w_down[i].astype(F32), alpha, tile)
        cnt = cnt.reshape(t // tile, LANES, LANES)[:, :MOE_EXPERTS, 0].reshape(-1)
        moe = _moe(xp, _route_rows(tab, cnt, tile).reshape(-1), cnt, e_gate, e_up, e_down, tile)
        x = _final(x1, moe, p[i].reshape(t, -1), _row(ln2_g[i]), _row(ln2_b[i]),
                   bf(ple_w_gate[i]), bf(ple_w_proj[i]), alpha).reshape(bsz, seq, d)
    return x
```

```python
import functools

import jax
import jax.numpy as jnp
from jax import lax
from jax.experimental import pallas as pl
from jax.experimental.pallas import tpu as pltpu

F32 = jnp.float32
BF16 = jnp.bfloat16

D_MODEL = 1024
PLE_DIM = 256
SSM_D_INNER = 1024
SSM_HEAD_DIM = 64
SSM_HEADS = 16
SSM_GROUPS = 4
SSM_STATE = 128
SSM_CONV = 4
SSM_XBC = SSM_D_INNER + 2 * SSM_GROUPS * SSM_STATE
LSTM_HEADS = 8
LSTM_QK = 64
LSTM_V = 128
LSTM_D_QK = LSTM_HEADS * LSTM_QK
LSTM_D_V = LSTM_HEADS * LSTM_V
CHUNK = 128
MOE_GROUPS = 8
MOE_PER_GROUP = 8
MOE_EXPERTS = 64
MOE_D_FF = 512
NORM_EPS = 1e-5
IN_PROJ_SIZES = (SSM_D_INNER, SSM_XBC, SSM_HEADS, LSTM_D_QK, LSTM_D_QK, LSTM_D_V, LSTM_D_V,
                 LSTM_HEADS, LSTM_HEADS, D_MODEL, D_MODEL)

LANES = 128
SUBLANES = 8
VMEM_LIMIT = 56 * 1024 * 1024
MOE_VMEM_LIMIT = 60 * 1024 * 1024

MIX_ROWS = 256
MIX_SEQS = 2
DENSE_ROWS = 512
MOE_TILE = 8192
MOE_CHUNK = 256
MOE_BLOCKS = (192, 256, 320, 384, 448)
MOE_WEIGHT_BUFFERS = 4
MOE_WEIGHT_SPLIT = 4
PACK_SLABS = D_MODEL // 2 // LANES
OUT_SLABS = D_MODEL // LANES


def _dot(a, b):
    return jnp.dot(a, b, preferred_element_type=F32)


def _dot_nt(a, b):
    return lax.dot_general(a, b, (((1,), (1,)), ((), ())), preferred_element_type=F32)


def _dot_tn(a, b):
    return lax.dot_general(a, b, (((0,), (0,)), ((), ())), preferred_element_type=F32)


def _sigmoid(x):
    return 0.5 + 0.5 * jnp.tanh(0.5 * x)


def _silu(x):
    h = 0.5 * x
    return h + h * jnp.tanh(h)


def _softplus(x):
    return jnp.maximum(x, 0.0) + jnp.log(1.0 + jnp.exp(-jnp.abs(x)))


def _split3(x):
    hi = x.astype(BF16)
    r1 = x - hi.astype(F32)
    mid = r1.astype(BF16)
    lo = (r1 - mid.astype(F32)).astype(BF16)
    return hi, mid, lo


def _cumsum_rows(tril, x):
    hi, mid, lo = _split3(x)
    return _dot(tril, hi) + _dot(tril, mid) + _dot(tril, lo)


def _causal_masks():
    r = lax.broadcasted_iota(jnp.int32, (CHUNK, CHUNK), 0)
    c = lax.broadcasted_iota(jnp.int32, (CHUNK, CHUNK), 1)
    causal = r >= c
    return causal, jnp.where(causal, 1.0, 0.0).astype(BF16)


def _pair_cols(v, ha, hb, rows, lane):
    a = jnp.broadcast_to(v[:, ha:ha + 1], (rows, LANES))
    b = jnp.broadcast_to(v[:, hb:hb + 1], (rows, LANES))
    return jnp.where(lane < SSM_HEAD_DIM, a, b)


def _ssd_kernel(x_ref, w_ref, cw_ref, cb_ref, dtb_ref, aneg_ref, dexp_ref, nw_ref, ex_ref,
                y_ref, ext_ref, u_ref, st_ref, *, rows, nseq):
    n_slab = SSM_XBC // LANES
    tail = SUBLANES - SSM_CONV + 1

    @pl.when(pl.program_id(1) == 0)
    def _():
        ext_ref[:, :, 0:SUBLANES, :] = jnp.zeros((nseq, n_slab, SUBLANES, LANES), F32)
        st_ref[...] = jnp.zeros(st_ref.shape, F32)

    xb = x_ref[...].reshape(nseq * rows, x_ref.shape[-1]).astype(BF16)
    z = _dot(xb, w_ref[:, 0:SSM_D_INNER])
    dt_col = SSM_D_INNER + SSM_XBC
    dtr = _dot(xb, w_ref[:, dt_col:dt_col + LANES])
    dtr = jnp.where(lax.broadcasted_iota(jnp.int32, dtr.shape, 1) < SSM_HEADS, dtr, 0.0)
    for c2 in range(n_slab // 2):
        xbc = _dot(xb, w_ref[:, SSM_D_INNER + c2 * 2 * LANES:SSM_D_INNER + (c2 + 1) * 2 * LANES])
        for s in range(nseq):
            for cc in range(2):
                ext_ref[s, 2 * c2 + cc, SUBLANES:SUBLANES + rows, :] = (
                    xbc[s * rows:(s + 1) * rows, cc * LANES:(cc + 1) * LANES])
    for s in range(nseq):
        for c in range(n_slab):
            conv = cb_ref[:, c * LANES:(c + 1) * LANES]
            for k in range(SSM_CONV):
                conv = conv + cw_ref[k:k + 1, c * LANES:(c + 1) * LANES] * ext_ref[s, c, pl.ds(tail + k, rows), :]
            u_ref[s, c] = _silu(conv)
            ext_ref[s, c, 0:SUBLANES, :] = ext_ref[s, c, rows:rows + SUBLANES, :]

    dt = _softplus(dtr + dtb_ref[...])
    da = dt * aneg_ref[...]
    causal, tril = _causal_masks()
    lane = lax.broadcasted_iota(jnp.int32, (CHUNK, LANES), 1)
    lane1 = lax.broadcasted_iota(jnp.int32, (1, LANES), 1)
    gw = SSM_D_INNER // SSM_GROUPS
    b_slab = SSM_D_INNER // LANES
    c_slab = b_slab + SSM_GROUPS

    for ci in range(rows // CHUNK):
        for s in range(nseq):
            r0 = ci * CHUNK
            q0 = s * rows + r0
            dt_c = dt[q0:q0 + CHUNK]
            a_cs = _cumsum_rows(tril, da[q0:q0 + CHUNK])
            a_cs_t = a_cs.T
            dt_t = dt_c.T
            a_last = a_cs[CHUNK - 1:CHUNK, :]
            ea_last = jnp.exp(a_last)
            ea_x = _dot(jnp.exp(a_cs).astype(BF16), ex_ref[...])
            sw_x = _dot((dt_c * jnp.exp(a_last - a_cs)).astype(BF16), ex_ref[...])
            for g in range(SSM_GROUPS):
                bm_g = u_ref[s, b_slab + g, r0:r0 + CHUNK, :].astype(BF16)
                cm_g = u_ref[s, c_slab + g, r0:r0 + CHUNK, :].astype(BF16)
                cb = _dot_nt(cm_g, bm_g)
                st_g = st_ref[s, :, g * gw:(g + 1) * gw]
                y_off = _dot(cm_g, st_g.astype(BF16))
                yy, xw, eal_x = [], [], []
                for pr in range(2):
                    ha = 4 * g + 2 * pr
                    hb = ha + 1
                    lo = g * gw + pr * LANES
                    xs_p = u_ref[s, 2 * g + pr, r0:r0 + CHUNK, :]
                    xs_pb = xs_p.astype(BF16)
                    rhs = jnp.concatenate([jnp.where(lane < SSM_HEAD_DIM, xs_pb, 0).astype(BF16),
                                           jnp.where(lane >= SSM_HEAD_DIM, xs_pb, 0).astype(BF16)], axis=0)
                    gs = []
                    for h in (ha, hb):
                        colb = jnp.broadcast_to(a_cs[:, h:h + 1], (CHUNK, CHUNK))
                        rowb = jnp.broadcast_to(a_cs_t[h:h + 1, :], (CHUNK, CHUNK))
                        dec = jnp.exp(jnp.where(causal, colb - rowb, -jnp.inf))
                        gs.append((cb * dec * jnp.broadcast_to(dt_t[h:h + 1, :], (CHUNK, CHUNK))).astype(BF16))
                    y_p = (_dot(jnp.concatenate(gs, axis=1), rhs)
                           + y_off[:, pr * LANES:(pr + 1) * LANES] * ea_x[:, lo:lo + LANES]
                           + dexp_ref[:, lo:lo + LANES] * xs_p)
                    zz = z[q0:q0 + CHUNK, lo:lo + LANES]
                    yy.append(y_p * _silu(zz))
                    xw.append((xs_p * sw_x[:, lo:lo + LANES]).astype(BF16))
                    eal_x.append(_pair_cols(ea_last, ha, hb, 1, lane1))
                st_ref[s, :, g * gw:(g + 1) * gw] = (st_g * jnp.concatenate(eal_x, axis=1)
                                                     + _dot_tn(bm_g, jnp.concatenate(xw, axis=1)))
                ms = sum(jnp.sum(t * t, axis=-1, keepdims=True) for t in yy) * (1.0 / gw)
                inv = lax.rsqrt(ms + NORM_EPS)
                for pr in range(2):
                    lo = g * gw + pr * LANES
                    y_ref[s, r0:r0 + CHUNK, lo:lo + LANES] = (yy[pr] * inv * nw_ref[:, lo:lo + LANES]).astype(BF16)


def _ssd_mixer(x, w, cw, cb, dtb, aneg, dexp, nw, ex):
    b, s, d = x.shape
    rows = MIX_ROWS
    nseq = MIX_SEQS
    const = lambda shape: pl.BlockSpec(shape, lambda i, j: (0,) * len(shape), pipeline_mode=pl.Buffered(1))
    return pl.pallas_call(
        functools.partial(_ssd_kernel, rows=rows, nseq=nseq),
        grid=(b // nseq, s // rows),
        in_specs=[pl.BlockSpec((nseq, rows, d), lambda i, j: (i, j, 0)),
                  const(w.shape), const(cw.shape), const(cb.shape),
                  const(dtb.shape), const(aneg.shape), const(dexp.shape), const(nw.shape), const(ex.shape)],
        out_specs=pl.BlockSpec((nseq, rows, SSM_D_INNER), lambda i, j: (i, j, 0)),
        out_shape=jax.ShapeDtypeStruct((b, s, SSM_D_INNER), BF16),
        scratch_shapes=[pltpu.VMEM((nseq, SSM_XBC // LANES, SUBLANES + rows, LANES), F32),
                        pltpu.VMEM((nseq, SSM_XBC // LANES, rows, LANES), F32),
                        pltpu.VMEM((nseq, SSM_STATE, SSM_D_INNER), F32)],
        compiler_params=pltpu.CompilerParams(dimension_semantics=("arbitrary", "arbitrary"),
                                             vmem_limit_bytes=VMEM_LIMIT),
        name="ssd_mixer",
    )(x, w, cw, cb, dtb, aneg, dexp, nw, ex)


def _mlstm_kernel(x_ref, w_ref, bif_ref, nw_ref,
                  y_ref, c_ref, m_ref, *, rows, nseq):
    @pl.when(pl.program_id(1) == 0)
    def _():
        c_ref[...] = jnp.zeros(c_ref.shape, F32)
        m_ref[...] = jnp.full(m_ref.shape, -jnp.inf, F32)

    xb = x_ref[...].reshape(nseq * rows, x_ref.shape[-1]).astype(BF16)
    c_k, c_v, c_o, c_g = LSTM_D_QK, 2 * LSTM_D_QK, 2 * LSTM_D_QK + LSTM_D_V, 2 * LSTM_D_QK + 2 * LSTM_D_V
    q = _dot(xb, w_ref[:, 0:c_k])
    k = _dot(xb, w_ref[:, c_k:c_v])
    v = _dot(xb, w_ref[:, c_v:c_o])
    o = _dot(xb, w_ref[:, c_o:c_g])
    gif = _dot(xb, w_ref[:, c_g:c_g + 2 * LANES]) + bif_ref[...]
    causal, tril = _causal_masks()
    lane = lax.broadcasted_iota(jnp.int32, (CHUNK, LANES), 1)
    lane1 = lax.broadcasted_iota(jnp.int32, (1, LANES), 1)
    row = lax.broadcasted_iota(jnp.int32, (LANES, 1), 0)
    head_lane = lane < LSTM_HEADS
    ones = jnp.ones((CHUNK, LANES), BF16)

    for ci in range(rows // CHUNK):
        r0 = ci * CHUNK
        pre = {}
        for s in range(nseq):
            q0 = s * rows + r0
            log_i = gif[q0:q0 + CHUNK, 0:LANES]
            f_pre = gif[q0:q0 + CHUNK, LANES:2 * LANES]
            log_f = jnp.where(head_lane, jnp.minimum(f_pre, 0.0) - jnp.log(1.0 + jnp.exp(-jnp.abs(f_pre))), 0.0)
            fcum = _cumsum_rows(tril, log_f)
            f_tot = fcum[CHUNK - 1:CHUNK, :]
            gk = log_i - fcum
            m_loc = jnp.max(f_tot + gk, axis=0, keepdims=True)
            g_t = gk.T
            m_prev = m_ref[s]
            m_new = jnp.maximum(f_tot + m_prev, m_loc)
            s_prev = jnp.exp(f_tot + m_prev - m_new)
            s_loc = jnp.exp(m_loc - m_new)
            m_ref[s] = jnp.where(lane1 < LSTM_HEADS, m_new, -jnp.inf)
            w_end = jnp.exp(gk + (f_tot - m_loc))
            pre[s] = (q0, fcum, gk, g_t, m_prev, s_prev, s_loc, w_end)

        for pr in range(LSTM_HEADS // 2):
            st = {}
            for s in range(nseq):
                q0, fcum, gk, g_t, m_prev, s_prev, s_loc, w_end = pre[s]
                q_p = q[q0:q0 + CHUNK, pr * LANES:(pr + 1) * LANES]
                k_p = k[q0:q0 + CHUNK, pr * LANES:(pr + 1) * LANES]
                k_pb = k_p.astype(BF16)
                c_pair = c_ref[s, pr]
                c_pair_b = c_pair.astype(BF16)
                c_new = c_pair * jnp.where(row < LSTM_QK,
                                           jnp.broadcast_to(s_prev[:, 2 * pr:2 * pr + 1], (LANES, 1)),
                                           jnp.broadcast_to(s_prev[:, 2 * pr + 1:2 * pr + 2], (LANES, 1)))
                st[s] = [q_p, k_p, k_pb, c_pair_b, c_new]
            for hh in range(2):
                for s in range(nseq):
                    q0, fcum, gk, g_t, m_prev, s_prev, s_loc, w_end = pre[s]
                    q_p, k_p, k_pb, c_pair_b, c_new = st[s]
                    h = 2 * pr + hh
                    in_head = (lane < LSTM_QK) if hh == 0 else (lane >= LSTM_QK)
                    v_h = v[q0:q0 + CHUNK, h * LSTM_V:(h + 1) * LSTM_V].astype(BF16)
                    v_ext = jnp.concatenate([v_h, ones], axis=1)
                    q_m = jnp.where(in_head, q_p, 0.0)
                    fcol = jnp.broadcast_to(fcum[:, h:h + 1], (CHUNK, CHUNK))
                    rowb = jnp.broadcast_to(g_t[h:h + 1, :], (CHUNK, CHUNK))
                    log_d = jnp.where(causal, fcol + rowb, -jnp.inf)
                    inter_log = fcol + m_prev[:, h:h + 1]
                    m_t = jnp.maximum(inter_log, jnp.max(log_d, axis=-1, keepdims=True))
                    scores = _dot_nt(q_m.astype(BF16), k_pb) * jnp.exp(log_d - m_t)
                    inter_w = jnp.exp(inter_log - m_t)
                    lhs = jnp.concatenate([scores.astype(BF16), (q_m * inter_w).astype(BF16)], axis=1)
                    res = _dot(lhs, jnp.concatenate([v_ext, c_pair_b], axis=0))
                    hv = res[:, 0:LSTM_V] / jnp.maximum(jnp.abs(res[:, LSTM_V:2 * LSTM_V]), jnp.exp(-m_t))
                    ms = jnp.mean(hv * hv, axis=-1, keepdims=True)
                    o_h = o[q0:q0 + CHUNK, h * LSTM_V:(h + 1) * LSTM_V]
                    y_ref[s, r0:r0 + CHUNK, h * LSTM_V:(h + 1) * LSTM_V] = (
                        _sigmoid(o_h) * (hv * lax.rsqrt(ms + NORM_EPS) * nw_ref[:, h * LSTM_V:(h + 1) * LSTM_V])
                    ).astype(BF16)
                    kw = jnp.where(in_head, k_p * jnp.broadcast_to(w_end[:, h:h + 1], (CHUNK, LANES)), 0.0)
                    st[s][4] = c_new + s_loc[:, h:h + 1] * _dot_tn(kw.astype(BF16), v_ext)
            for s in range(nseq):
                c_ref[s, pr] = st[s][4]


def _mlstm_mixer(x, w, bif, nw):
    b, s, d = x.shape
    rows = MIX_ROWS
    nseq = MIX_SEQS
    const = lambda shape: pl.BlockSpec(shape, lambda i, j: (0,) * len(shape), pipeline_mode=pl.Buffered(1))
    return pl.pallas_call(
        functools.partial(_mlstm_kernel, rows=rows, nseq=nseq),
        grid=(b // nseq, s // rows),
        in_specs=[pl.BlockSpec((nseq, rows, d), lambda i, j: (i, j, 0)),
                  const(w.shape), const(bif.shape), const(nw.shape)],
        out_specs=pl.BlockSpec((nseq, rows, LSTM_D_V), lambda i, j: (i, j, 0)),
        out_shape=jax.ShapeDtypeStruct((b, s, LSTM_D_V), BF16),
        scratch_shapes=[pltpu.VMEM((nseq, LSTM_HEADS // 2, 2 * LSTM_QK, 2 * LSTM_V), F32),
                        pltpu.VMEM((nseq, 1, LANES), F32)],
        compiler_params=pltpu.CompilerParams(dimension_semantics=("arbitrary", "arbitrary"),
                                             vmem_limit_bytes=VMEM_LIMIT),
        name="mlstm_mixer",
    )(x, w, bif, nw)


def _layer_norm(t, g, b):
    mu = jnp.mean(t, axis=-1, keepdims=True)
    tc = t - mu
    var = jnp.mean(tc * tc, axis=-1, keepdims=True)
    return tc * lax.rsqrt(var + NORM_EPS) * g + b


def _pack_pair(a, b):
    pa = lax.bitcast_convert_type(a.astype(BF16).astype(F32), jnp.uint32)
    pb = lax.bitcast_convert_type(b.astype(BF16).astype(F32), jnp.uint32)
    return pa | (pb >> 16)


def _unpack_pair(w):
    return (lax.bitcast_convert_type(w & jnp.uint32(0xFFFF0000), F32),
            lax.bitcast_convert_type(w << 16, F32))


def _route(lt):
    n = lt.shape[1]
    sub = lax.broadcasted_iota(jnp.int32, (MOE_PER_GROUP, n), 0).astype(F32)
    big = float(LANES)
    neg = -jnp.inf
    gl = lt[0:MOE_GROUPS, :]
    gmax = jnp.max(gl, axis=0, keepdims=True)
    gidx = jnp.min(jnp.where(gl == gmax, sub, big), axis=0, keepdims=True)
    grp_p = 1.0 / jnp.sum(jnp.exp(gl - gmax), axis=0, keepdims=True)
    el = lt[MOE_GROUPS:MOE_GROUPS + MOE_PER_GROUP, :]
    for g in range(1, MOE_GROUPS):
        lo = MOE_GROUPS + g * MOE_PER_GROUP
        el = jnp.where(gidx == g, lt[lo:lo + MOE_PER_GROUP, :], el)
    m1 = jnp.max(el, axis=0, keepdims=True)
    i1 = jnp.min(jnp.where(el == m1, sub, big), axis=0, keepdims=True)
    el2 = jnp.where(sub == i1, neg, el)
    m2 = jnp.max(el2, axis=0, keepdims=True)
    i2 = jnp.min(jnp.where(el2 == m2, sub, big), axis=0, keepdims=True)
    e21 = jnp.exp(m2 - m1)
    g1 = grp_p / (1.0 + e21)
    g2 = grp_p * e21 / (1.0 + e21)
    return gidx * MOE_PER_GROUP + i1, gidx * MOE_PER_GROUP + i2, g1, g2


def _merge_kernel(x_ref, ys_ref, yl_ref, wg_ref, wa_ref, wb_ref, wo_ref, g_ref, b_ref,
                  wr_ref, br_ref, eg_ref, eu_ref, ed_ref,
                  x1_ref, xp_ref, tab_ref, cnt_ref, egb_ref, eub_ref, edb_ref, carry_ref,
                  *, alpha, steps_per_tile, cast_steps):
    @pl.when(pl.program_id(0) % steps_per_tile == 0)
    def _():
        carry_ref[...] = jnp.zeros(carry_ref.shape, F32)

    @pl.when(pl.program_id(0) < cast_steps)
    def _():
        egb_ref[...] = eg_ref[...].astype(BF16)
        eub_ref[...] = eu_ref[...].astype(BF16)
        edb_ref[...] = ed_ref[...].astype(BF16)

    x = x_ref[...]
    rows = x.shape[0]
    xb = x.astype(BF16)
    d = x.shape[1]
    merged = (_sigmoid(_dot(xb, wg_ref[:, 0:d])) * _dot(ys_ref[...], wa_ref[...])
              + _sigmoid(_dot(xb, wg_ref[:, d:2 * d])) * _dot(yl_ref[...], wb_ref[...]))
    x1 = _layer_norm(alpha * x + _dot(merged.astype(BF16), wo_ref[...]), g_ref[...], b_ref[...])
    x1_ref[...] = x1
    half = x1.shape[1] // 2
    xp = _pack_pair(x1[:, :half], x1[:, half:])
    for c in range(PACK_SLABS):
        xp_ref[pl.ds(c, rows, stride=PACK_SLABS), :] = xp[:, c * LANES:(c + 1) * LANES]
    x_hi = x1.astype(BF16)
    x_lo = (x1 - x_hi.astype(F32)).astype(BF16)
    by_hi = _dot_nt(wr_ref[...], x_hi)
    lt = by_hi[0:LANES] + by_hi[LANES:2 * LANES] + _dot_nt(wr_ref[0:LANES, :], x_lo) + br_ref[...]
    e1, e2, g1, g2 = _route(lt)
    expert = lax.broadcasted_iota(jnp.int32, (LANES, rows), 0).astype(F32)
    oh1 = expert == e1
    oh2 = expert == e2
    ohs = jnp.where(oh1, 1.0, jnp.where(oh2, 1.0, 0.0))
    r_i = lax.broadcasted_iota(jnp.int32, (rows, rows), 0)
    c_i = lax.broadcasted_iota(jnp.int32, (rows, rows), 1)
    before = jnp.where(r_i < c_i, 1.0, 0.0).astype(BF16)
    carry = carry_ref[...]
    seen = _dot(ohs.astype(BF16), before) + jnp.concatenate([carry] * (rows // LANES), axis=1)
    r1 = jnp.sum(jnp.where(oh1, seen, 0.0), axis=0, keepdims=True)
    r2 = jnp.sum(jnp.where(oh2, seen, 0.0), axis=0, keepdims=True)
    carry = carry + jnp.sum(ohs, axis=1, keepdims=True)
    carry_ref[...] = carry
    cnt_ref[...] = carry.astype(jnp.int32)
    as_int = lambda v: v.astype(jnp.int32)
    as_bits = lambda v: lax.bitcast_convert_type(v, jnp.int32)
    tab_ref[...] = jnp.concatenate(
        [as_int(e1), as_int(e2), as_int(r1) * PACK_SLABS, as_int(r2) * PACK_SLABS, as_bits(g1), as_bits(g2),
         jnp.zeros((2, rows), jnp.int32)], axis=0)


def _merge(x2d, ys, yl, wg, wa, wb, wo, g, b, wr, br, eg, eu, ed, alpha, tile):
    t, d = x2d.shape
    rows = DENSE_ROWS
    spt = tile // rows
    n_steps = t // rows
    per_step = -(-MOE_EXPERTS // n_steps)
    cast_steps = MOE_EXPERTS // per_step
    assert per_step * cast_steps == MOE_EXPERTS and cast_steps <= n_steps
    const = lambda shape: pl.BlockSpec(shape, lambda i: (0,) * len(shape), pipeline_mode=pl.Buffered(1))
    rowblk = lambda w: pl.BlockSpec((rows, w), lambda i: (i, 0))
    expert = lambda w: pl.BlockSpec((per_step,) + w.shape[1:], lambda i: (jnp.minimum(i, cast_steps - 1), 0, 0))
    return pl.pallas_call(
        functools.partial(_merge_kernel, alpha=alpha, steps_per_tile=spt, cast_steps=cast_steps),
        grid=(n_steps,),
        in_specs=[rowblk(d), rowblk(SSM_D_INNER), rowblk(LSTM_D_V),
                  const(wg.shape), const(wa.shape), const(wb.shape), const(wo.shape),
                  const(g.shape), const(b.shape), const(wr.shape), const(br.shape),
                  expert(eg), expert(eu), expert(ed)],
        out_specs=[rowblk(d),
                   pl.BlockSpec((rows * PACK_SLABS, LANES), lambda i: (i, 0)),
                   pl.BlockSpec((SUBLANES, rows), lambda i: (i // spt, i % spt)),
                   pl.BlockSpec((LANES, LANES), lambda i: (i // spt, 0)),
                   expert(eg), expert(eu), expert(ed)],
        out_shape=[jax.ShapeDtypeStruct((t, d), F32),
                   jax.ShapeDtypeStruct((t * PACK_SLABS, LANES), jnp.uint32),
                   jax.ShapeDtypeStruct((t // tile * SUBLANES, tile), jnp.int32),
                   jax.ShapeDtypeStruct((t // tile * LANES, LANES), jnp.int32),
                   jax.ShapeDtypeStruct(eg.shape, BF16), jax.ShapeDtypeStruct(eu.shape, BF16),
                   jax.ShapeDtypeStruct(ed.shape, BF16)],
        scratch_shapes=[pltpu.VMEM((LANES, LANES), F32)],
        compiler_params=pltpu.CompilerParams(dimension_semantics=("arbitrary",), vmem_limit_bytes=VMEM_LIMIT),
        name="merge_ln_route",
    )(x2d, ys, yl, wg, wa, wb, wo, g, b, wr, br, eg, eu, ed)


def _segment_rows(count):
    groups = lax.shift_right_logical(count + (SUBLANES - 1), SUBLANES.bit_length() - 1)
    return groups * (SUBLANES * PACK_SLABS)


def _route_rows_kernel(cnt_ref, tab_ref, out_ref):
    i = pl.program_id(0)
    tab = tab_ref[...]
    starts = jnp.zeros(tab.shape, jnp.int32)
    off = jnp.int32(0)
    for e in range(MOE_EXPERTS):
        starts = jnp.where(tab == e, off, starts)
        off = off + _segment_rows(cnt_ref[i * MOE_EXPERTS + e])
    row = lax.broadcasted_iota(jnp.int32, tab.shape, 0)
    res = jnp.where(lax.shift_right_logical(row, 1) == 1, tab + pltpu.roll(starts, 2, axis=0), tab)
    blocks = tab.shape[1] // LANES
    for c in range(blocks):
        out_ref[pl.ds(c, SUBLANES, stride=blocks), :] = res[:, c * LANES:(c + 1) * LANES]


def _route_rows(tab, cnt, tile):
    n_tiles = tab.shape[0] // SUBLANES
    rows = SUBLANES * tile // LANES
    return pl.pallas_call(
        _route_rows_kernel,
        grid_spec=pltpu.PrefetchScalarGridSpec(
            num_scalar_prefetch=1, grid=(n_tiles,),
            in_specs=[pl.BlockSpec((SUBLANES, tile), lambda i, c: (i, 0))],
            out_specs=pl.BlockSpec((rows, LANES), lambda i, c: (i, 0))),
        out_shape=jax.ShapeDtypeStruct((n_tiles * rows, LANES), jnp.int32),
        compiler_params=pltpu.CompilerParams(dimension_semantics=("arbitrary",), vmem_limit_bytes=VMEM_LIMIT),
        name="route_rows",
    )(cnt, tab)


def _moe_kernel(cnt_ref, tab_ref, xp_hbm, wg_hbm, wu_hbm, wd_hbm, out_hbm, xys, ybuf, off_ref,
                wg_buf, wu_buf, wd_buf, xin, xout, wsem, isem, osem, *, tile, chunk, n_steps):
    i = pl.program_id(0)
    group = SUBLANES * PACK_SLABS
    half = PACK_SLABS * LANES
    weights = ((wg_hbm, wg_buf), (wu_hbm, wu_buf), (wd_hbm, wd_buf))

    def weight_copies(expert, into):
        copies = []
        for k, (hbm, buf) in enumerate(weights):
            part = buf.shape[1] // MOE_WEIGHT_SPLIT
            for q in range(MOE_WEIGHT_SPLIT):
                copies.append(pltpu.make_async_copy(hbm.at[expert, pl.ds(q * part, part)],
                                                    buf.at[into, pl.ds(q * part, part)], wsem.at[k, into]))
        return copies

    @pl.when(i == 0)
    def _():
        for ahead in range(MOE_WEIGHT_BUFFERS - 1):
            for cp in weight_copies(ahead % MOE_EXPERTS, ahead):
                cp.start()

    n_chunks = tile // chunk
    rolled = lambda trips: trips + jnp.minimum(i, 0)

    def in_buf(c):
        return pl.multiple_of((c % 2) * (chunk * PACK_SLABS), group)

    def out_buf(c):
        return pl.multiple_of((c % 2) * (chunk * OUT_SLABS), SUBLANES)

    def in_copy(c):
        row0 = pl.multiple_of((i * tile + c * chunk) * PACK_SLABS, group)
        return pltpu.make_async_copy(xp_hbm.at[pl.ds(row0, chunk * PACK_SLABS)],
                                     xin.at[pl.ds(in_buf(c), chunk * PACK_SLABS)], isem.at[c % 2])

    def out_copy(c):
        row0 = pl.multiple_of((i * tile + c * chunk) * OUT_SLABS, SUBLANES)
        return pltpu.make_async_copy(xout.at[pl.ds(out_buf(c), chunk * OUT_SLABS)],
                                     out_hbm.at[pl.ds(row0, chunk * OUT_SLABS)], osem.at[c % 2])

    def sort_rows():
        in_copy(0).start()

        def offsets(e, acc):
            padded = _segment_rows(cnt_ref[i * MOE_EXPERTS + e])
            off_ref[e] = acc
            end = acc + padded

            @pl.when(padded > 0)
            def _():
                xys[pl.ds(pl.multiple_of(end - group, group), group), :] = jnp.zeros((group, LANES), jnp.uint32)
            return end

        total = lax.fori_loop(0, MOE_EXPERTS, offsets, 0)
        off_ref[MOE_EXPERTS] = total
        tail = MOE_BLOCKS[-1] * PACK_SLABS
        xys[pl.ds(pl.multiple_of(total, group), tail), :] = jnp.zeros((tail, LANES), jnp.uint32)

        def distribute_chunk(c, carry):
            @pl.when(c + 1 < n_chunks)
            def _():
                in_copy(c + 1).start()
            in_copy(c).wait()

            g0 = c * (chunk // SUBLANES)
            shift = in_buf(c) - c * (chunk * PACK_SLABS)

            def distribute(tg, carry2):
                for u in range(SUBLANES):
                    t = tg * SUBLANES + u
                    d1 = pl.multiple_of(tab_ref[2 * tile + t], PACK_SLABS)
                    d2 = pl.multiple_of(tab_ref[3 * tile + t], PACK_SLABS)
                    row = xin[pl.ds(pl.multiple_of(shift + t * PACK_SLABS, PACK_SLABS), PACK_SLABS), :]
                    xys[pl.ds(d1, PACK_SLABS), :] = row
                    xys[pl.ds(d2, PACK_SLABS), :] = row
                return carry2

            lax.fori_loop(g0, g0 + rolled(chunk // SUBLANES), distribute, 0)
            return carry

        lax.fori_loop(0, rolled(n_chunks), distribute_chunk, 0)

    def expert_block(refs, base, rows, valid):
        wg_ref, wu_ref, wd_ref = refs
        words = [xys[pl.ds(base + c, rows, stride=PACK_SLABS), :] for c in range(PACK_SLABS)]
        parts = [_unpack_pair(w) for w in words]
        x_hi = jnp.concatenate([p[0].astype(BF16) for p in parts], axis=1)
        x_lo = jnp.concatenate([p[1].astype(BF16) for p in parts], axis=1)
        hg = _dot(x_hi, wg_ref[0:half, :]) + _dot(x_lo, wg_ref[half:2 * half, :])
        hu = _dot(x_hi, wu_ref[0:half, :]) + _dot(x_lo, wu_ref[half:2 * half, :])
        y = _dot((_silu(hg) * hu).astype(BF16), wd_ref[...])
        for c in range(PACK_SLABS):
            ybuf[pl.ds(c, rows, stride=PACK_SLABS), :] = _pack_pair(
                y[:, c * LANES:(c + 1) * LANES], y[:, half + c * LANES:half + (c + 1) * LANES])

        def copy(g, c):
            r = pl.multiple_of(g * group, group)
            xys[pl.ds(base + r, group), :] = ybuf[pl.ds(r, group), :]
            return c

        lax.fori_loop(0, valid // group, copy, 0)

    def expert_step(j, carry):
        step = i * MOE_EXPERTS + j
        slot = step % MOE_WEIGHT_BUFFERS
        nxt = step + MOE_WEIGHT_BUFFERS - 1

        @pl.when(nxt < n_steps)
        def _():
            for cp in weight_copies(nxt % MOE_EXPERTS, nxt % MOE_WEIGHT_BUFFERS):
                cp.start()

        for cp in weight_copies(j, slot):
            cp.wait()
        refs = (wg_buf.at[slot], wu_buf.at[slot], wd_buf.at[slot])
        start = off_ref[j]
        n = off_ref[j + 1] - start
        lo = 0
        for rows in MOE_BLOCKS:
            hi = rows * PACK_SLABS

            @pl.when(jnp.logical_and(n > lo, n <= hi))
            def _(rows=rows):
                expert_block(refs, pl.multiple_of(start, group), rows, n)
            lo = hi

        @pl.when(n > lo)
        def _():
            span = MOE_BLOCKS[0] * PACK_SLABS

            def block(bi, carry2):
                expert_block(refs, pl.multiple_of(start + bi * span, group), MOE_BLOCKS[0],
                             jnp.minimum(span, n - bi * span))
                return carry2

            lax.fori_loop(0, (n + span - 1) // span, block, 0)
        return carry

    def combine_rows():
        def combine_chunk(c, carry):
            @pl.when(c >= 2)
            def _():
                out_copy(c - 2).wait()

            g0 = c * (chunk // SUBLANES)
            shift = out_buf(c) - c * (chunk * OUT_SLABS)

            def combine(tg, carry2):
                for u in range(SUBLANES):
                    t = tg * SUBLANES + u
                    g1 = lax.bitcast_convert_type(tab_ref[4 * tile + t], F32)
                    g2 = lax.bitcast_convert_type(tab_ref[5 * tile + t], F32)
                    a_hi, a_lo = _unpack_pair(
                        xys[pl.ds(pl.multiple_of(tab_ref[2 * tile + t], PACK_SLABS), PACK_SLABS), :])
                    b_hi, b_lo = _unpack_pair(
                        xys[pl.ds(pl.multiple_of(tab_ref[3 * tile + t], PACK_SLABS), PACK_SLABS), :])
                    xout[pl.ds(pl.multiple_of(shift + t * OUT_SLABS, OUT_SLABS), OUT_SLABS), :] = jnp.concatenate(
                        [g1 * a_hi + g2 * b_hi, g1 * a_lo + g2 * b_lo], axis=0)
                return carry2

            lax.fori_loop(g0, g0 + rolled(chunk // SUBLANES), combine, 0)
            out_copy(c).start()
            return carry

        lax.fori_loop(0, rolled(n_chunks), combine_chunk, 0)
        for c in range(max(0, n_chunks - 2), n_chunks):
            out_copy(c).wait()

    sort_rows()
    lax.fori_loop(0, rolled(MOE_EXPERTS), expert_step, 0)
    combine_rows()


def _moe(xp, tab, cnt, wg, wu, wd, tile):
    t = xp.shape[0] // PACK_SLABS
    d = D_MODEL
    n_tiles = t // tile
    chunk = min(MOE_CHUNK, tile)
    seg_rows = 2 * tile + MOE_EXPERTS * SUBLANES + MOE_BLOCKS[-1]
    grid_spec = pltpu.PrefetchScalarGridSpec(
        num_scalar_prefetch=1,
        grid=(n_tiles,),
        in_specs=[pl.BlockSpec((SUBLANES * tile,), lambda i, c: (i,), memory_space=pltpu.SMEM,
                               pipeline_mode=pl.Buffered(1)),
                  pl.BlockSpec(memory_space=pl.ANY), pl.BlockSpec(memory_space=pl.ANY),
                  pl.BlockSpec(memory_space=pl.ANY), pl.BlockSpec(memory_space=pl.ANY)],
        out_specs=pl.BlockSpec(memory_space=pl.ANY),
        scratch_shapes=[pltpu.VMEM((seg_rows * PACK_SLABS, LANES), jnp.uint32),
                        pltpu.VMEM((MOE_BLOCKS[-1] * PACK_SLABS, LANES), jnp.uint32),
                        pltpu.SMEM((LANES,), jnp.int32),
                        pltpu.VMEM((MOE_WEIGHT_BUFFERS, d, MOE_D_FF), BF16),
                        pltpu.VMEM((MOE_WEIGHT_BUFFERS, d, MOE_D_FF), BF16),
                        pltpu.VMEM((MOE_WEIGHT_BUFFERS, MOE_D_FF, d), BF16),
                        pltpu.VMEM((2 * chunk * PACK_SLABS, LANES), jnp.uint32),
                        pltpu.VMEM((2 * chunk * OUT_SLABS, LANES), F32),
                        pltpu.SemaphoreType.DMA((3, MOE_WEIGHT_BUFFERS)),
                        pltpu.SemaphoreType.DMA((2,)), pltpu.SemaphoreType.DMA((2,))],
    )
    return pl.pallas_call(
        functools.partial(_moe_kernel, tile=tile, chunk=chunk, n_steps=n_tiles * MOE_EXPERTS),
        grid_spec=grid_spec,
        out_shape=jax.ShapeDtypeStruct((t * OUT_SLABS, LANES), F32),
        compiler_params=pltpu.CompilerParams(dimension_semantics=("arbitrary",),
                                             vmem_limit_bytes=MOE_VMEM_LIMIT),
        name="moe_experts",
    )(cnt, tab, xp, wg, wu, wd)


def _final_kernel(x1_ref, moe_ref, p_ref, g_ref, b_ref, wpg_ref, wpp_ref, out_ref, *, alpha):
    rows = x1_ref.shape[0]
    moe = jnp.concatenate([moe_ref[pl.ds(c, rows, stride=OUT_SLABS), :] for c in range(OUT_SLABS)], axis=1)
    x2 = _layer_norm(alpha * x1_ref[...] + moe, g_ref[...], b_ref[...])
    gate = _sigmoid(_dot(x2.astype(BF16), wpg_ref[...]))
    out_ref[...] = x2 + gate * _dot(p_ref[...].astype(BF16), wpp_ref[...])


def _final(x1, moe, p2d, g, b, wpg, wpp, alpha):
    t, d = x1.shape
    rows = DENSE_ROWS
    const = lambda shape: pl.BlockSpec(shape, lambda i: (0,) * len(shape))
    rowblk = lambda w: pl.BlockSpec((rows, w), lambda i: (i, 0))
    return pl.pallas_call(
        functools.partial(_final_kernel, alpha=alpha),
        grid=(t // rows,),
        in_specs=[rowblk(d), pl.BlockSpec((rows * OUT_SLABS, LANES), lambda i: (i, 0)), rowblk(p2d.shape[1]),
                  const(g.shape), const(b.shape), const(wpg.shape), const(wpp.shape)],
        out_specs=rowblk(d),
        out_shape=jax.ShapeDtypeStruct((t, d), F32),
        compiler_params=pltpu.CompilerParams(dimension_semantics=("arbitrary",), vmem_limit_bytes=VMEM_LIMIT),
        name="final_ln_ple",
    )(x1, moe, p2d, g, b, wpg, wpp)


def _proj_weights_kernel(w_ref, ssd_ref, lstm_ref, gate_ref):
    w = w_ref[...]
    cols = [0]
    for sz in IN_PROJ_SIZES:
        cols.append(cols[-1] + sz)
    lane = lax.broadcasted_iota(jnp.int32, (w.shape[0], LANES), 1)
    gate_block = lambda c: jnp.where(lane < LSTM_HEADS, w[:, c:c + LANES], 0.0)
    ssd_ref[...] = w[:, 0:cols[2] + LANES].astype(BF16)
    lstm_ref[...] = jnp.concatenate(
        [w[:, cols[3]:cols[4]] * (LSTM_QK ** -0.5), w[:, cols[4]:cols[7]], gate_block(cols[7]), gate_block(cols[8])],
        axis=1).astype(BF16)
    gate_ref[...] = w[:, cols[9]:cols[11]].astype(BF16)


def _proj_weights(w_all, layer):
    _, d, n = w_all.shape
    rows = LANES
    widths = (IN_PROJ_SIZES[0] + IN_PROJ_SIZES[1] + LANES, 2 * LSTM_D_QK + 2 * LSTM_D_V + 2 * LANES, 2 * D_MODEL)
    return pl.pallas_call(
        _proj_weights_kernel,
        grid=(d // rows,),
        in_specs=[pl.BlockSpec((None, rows, n), lambda i: (layer, i, 0))],
        out_specs=[pl.BlockSpec((rows, wd), lambda i: (i, 0)) for wd in widths],
        out_shape=[jax.ShapeDtypeStruct((d, wd), BF16) for wd in widths],
        compiler_params=pltpu.CompilerParams(dimension_semantics=("arbitrary",), vmem_limit_bytes=VMEM_LIMIT),
        name="proj_weights",
    )(w_all.astype(F32))


def _pad_lanes(w, width=LANES):
    return jnp.pad(w, ((0, 0), (0, width - w.shape[1])))


def _row(v, width=None):
    v = v.astype(F32).reshape(1, -1)
    return v if width is None else _pad_lanes(v, width)


def kernel(x, p, w_in, ssm_conv_w, ssm_conv_b, ssm_dt_bias, ssm_a_log, ssm_d, ssm_norm_w, lstm_i_bias, lstm_f_bias, lstm_norm_w, w_branch_ssm, w_branch_lstm, w_out, ln1_g, ln1_b, moe_w_group, moe_b_group, moe_w_expert, moe_b_expert, moe_w_gate, moe_w_up, moe_w_down, ln2_g, ln2_b, ple_w_proj, ple_w_gate):
    depth = w_in.shape[0]
    bsz, seq, d = x.shape
    t = bsz * seq
    alpha = (2.0 * depth) ** 0.25
    head_expand = (jnp.arange(LANES, dtype=jnp.int32)[:, None]
                   == jnp.arange(SSM_D_INNER, dtype=jnp.int32)[None, :] // SSM_HEAD_DIM).astype(BF16)
    for i in range(depth):
        bf = lambda w: w.astype(BF16)
        w_ssd, w_lstm, w_gates = _proj_weights(w_in, i)
        y_ssm = _ssd_mixer(
            x, w_ssd,
            ssm_conv_w[i].astype(F32), _row(ssm_conv_b[i]), _row(ssm_dt_bias[i], LANES),
            _row(-jnp.exp(ssm_a_log[i].astype(F32)), LANES),
            _row(jnp.repeat(ssm_d[i].astype(F32), SSM_HEAD_DIM)), _row(ssm_norm_w[i]), head_expand)
        b_if = jnp.concatenate([_row(lstm_i_bias[i], LANES), _row(lstm_f_bias[i], LANES)], axis=1)
        y_lstm = _mlstm_mixer(x, w_lstm, b_if, _row(lstm_norm_w[i]))
        w_r = _pad_lanes(jnp.concatenate([moe_w_group[i], moe_w_expert[i]], axis=1).astype(F32)).T
        w_r_hi = w_r.astype(BF16)
        w_r_cat = jnp.concatenate([w_r_hi, (w_r - w_r_hi.astype(F32)).astype(BF16)], axis=0)
        b_r = jnp.broadcast_to(_row(jnp.concatenate([moe_b_group[i], moe_b_expert[i]]), LANES).T, (LANES, DENSE_ROWS))
        tile = min(MOE_TILE, t)
        x1, xp, tab, cnt, e_gate, e_up, e_down = _merge(
            x.reshape(t, d), y_ssm.reshape(t, -1), y_lstm.reshape(t, -1),
            w_gates, bf(w_branch_ssm[i]), bf(w_branch_lstm[i]), bf(w_out[i]),
            _row(ln1_g[i]), _row(ln1_b[i]), w_r_cat, b_r,
            moe_w_gate[i].astype(F32), moe_w_up[i].astype(F32), moe_w_down[i].astype(F32), alpha, tile)
        cnt = cnt.reshape(t // tile, LANES, LANES)[:, :MOE_EXPERTS, 0].reshape(-1)
        moe = _moe(xp, _route_rows(tab, cnt, tile).reshape(-1), cnt, e_gate, e_up, e_down, tile)
        x = _final(x1, moe, p[i].reshape(t, -1), _row(ln2_g[i]), _row(ln2_b[i]),
                   bf(ple_w_gate[i]), bf(ple_w_proj[i]), alpha).reshape(bsz, seq, d)
    return x
```

```python
import functools

import jax
import jax.numpy as jnp
from jax import lax
from jax.experimental import pallas as pl
from jax.experimental.pallas import tpu as pltpu

F32 = jnp.float32
BF16 = jnp.bfloat16

D_MODEL = 1024
PLE_DIM = 256
SSM_D_INNER = 1024
SSM_HEAD_DIM = 64
SSM_HEADS = 16
SSM_GROUPS = 4
SSM_STATE = 128
SSM_CONV = 4
SSM_XBC = SSM_D_INNER + 2 * SSM_GROUPS * SSM_STATE
LSTM_HEADS = 8
LSTM_QK = 64
LSTM_V = 128
LSTM_D_QK = LSTM_HEADS * LSTM_QK
LSTM_D_V = LSTM_HEADS * LSTM_V
CHUNK = 128
MOE_GROUPS = 8
MOE_PER_GROUP = 8
MOE_EXPERTS = 64
MOE_D_FF = 512
NORM_EPS = 1e-5
IN_PROJ_SIZES = (SSM_D_INNER, SSM_XBC, SSM_HEADS, LSTM_D_QK, LSTM_D_QK, LSTM_D_V, LSTM_D_V,
                 LSTM_HEADS, LSTM_HEADS, D_MODEL, D_MODEL)

LANES = 128
SUBLANES = 8
VMEM_LIMIT = 56 * 1024 * 1024
MOE_VMEM_LIMIT = 60 * 1024 * 1024

MIX_ROWS = 256
MIX_SEQS = 2
DENSE_ROWS = 512
MOE_TILE = 8192
MOE_CHUNK = 256
MOE_BLOCKS = (192, 256, 320, 384, 448)
MOE_WEIGHT_BUFFERS = 4
MOE_WEIGHT_SPLIT = 4
PACK_SLABS = D_MODEL // 2 // LANES
OUT_SLABS = D_MODEL // LANES


def _dot(a, b):
    return jnp.dot(a, b, preferred_element_type=F32)


def _dot_nt(a, b):
    return lax.dot_general(a, b, (((1,), (1,)), ((), ())), preferred_element_type=F32)


def _dot_tn(a, b):
    return lax.dot_general(a, b, (((0,), (0,)), ((), ())), preferred_element_type=F32)


def _sigmoid(x):
    return 0.5 + 0.5 * jnp.tanh(0.5 * x)


def _silu(x):
    h = 0.5 * x
    return h + h * jnp.tanh(h)


def _softplus(x):
    return jnp.maximum(x, 0.0) + jnp.log(1.0 + jnp.exp(-jnp.abs(x)))


def _split3(x):
    hi = x.astype(BF16)
    r1 = x - hi.astype(F32)
    mid = r1.astype(BF16)
    lo = (r1 - mid.astype(F32)).astype(BF16)
    return hi, mid, lo


def _cumsum_rows(tril, x):
    hi, mid, lo = _split3(x)
    return _dot(tril, hi) + _dot(tril, mid) + _dot(tril, lo)


def _causal_masks():
    r = lax.broadcasted_iota(jnp.int32, (CHUNK, CHUNK), 0)
    c = lax.broadcasted_iota(jnp.int32, (CHUNK, CHUNK), 1)
    causal = r >= c
    return causal, jnp.where(causal, 1.0, 0.0).astype(BF16)


def _pair_cols(v, ha, hb, rows, lane):
    a = jnp.broadcast_to(v[:, ha:ha + 1], (rows, LANES))
    b = jnp.broadcast_to(v[:, hb:hb + 1], (rows, LANES))
    return jnp.where(lane < SSM_HEAD_DIM, a, b)


def _ssd_kernel(x_ref, w_ref, cw_ref, cb_ref, dtb_ref, aneg_ref, dexp_ref, nw_ref, ex_ref,
                y_ref, ext_ref, u_ref, st_ref, *, rows, nseq):
    n_slab = SSM_XBC // LANES
    tail = SUBLANES - SSM_CONV + 1

    @pl.when(pl.program_id(1) == 0)
    def _():
        ext_ref[:, :, 0:SUBLANES, :] = jnp.zeros((nseq, n_slab, SUBLANES, LANES), F32)
        st_ref[...] = jnp.zeros(st_ref.shape, F32)

    xb = x_ref[...].reshape(nseq * rows, x_ref.shape[-1]).astype(BF16)
    z = _dot(xb, w_ref[:, 0:SSM_D_INNER])
    dt_col = SSM_D_INNER + SSM_XBC
    dtr = _dot(xb, w_ref[:, dt_col:dt_col + LANES])
    dtr = jnp.where(lax.broadcasted_iota(jnp.int32, dtr.shape, 1) < SSM_HEADS, dtr, 0.0)
    for c2 in range(n_slab // 2):
        xbc = _dot(xb, w_ref[:, SSM_D_INNER + c2 * 2 * LANES:SSM_D_INNER + (c2 + 1) * 2 * LANES])
        for s in range(nseq):
            for cc in range(2):
                ext_ref[s, 2 * c2 + cc, SUBLANES:SUBLANES + rows, :] = (
                    xbc[s * rows:(s + 1) * rows, cc * LANES:(cc + 1) * LANES])
    for s in range(nseq):
        for c in range(n_slab):
            conv = cb_ref[:, c * LANES:(c + 1) * LANES]
            for k in range(SSM_CONV):
                conv = conv + cw_ref[k:k + 1, c * LANES:(c + 1) * LANES] * ext_ref[s, c, pl.ds(tail + k, rows), :]
            u_ref[s, c] = _silu(conv)
            ext_ref[s, c, 0:SUBLANES, :] = ext_ref[s, c, rows:rows + SUBLANES, :]

    dt = _softplus(dtr + dtb_ref[...])
    da = dt * aneg_ref[...]
    causal, tril = _causal_masks()
    lane = lax.broadcasted_iota(jnp.int32, (CHUNK, LANES), 1)
    lane1 = lax.broadcasted_iota(jnp.int32, (1, LANES), 1)
    gw = SSM_D_INNER // SSM_GROUPS
    b_slab = SSM_D_INNER // LANES
    c_slab = b_slab + SSM_GROUPS

    for ci in range(rows // CHUNK):
        for s in range(nseq):
            r0 = ci * CHUNK
            q0 = s * rows + r0
            dt_c = dt[q0:q0 + CHUNK]
            a_cs = _cumsum_rows(tril, da[q0:q0 + CHUNK])
            key_t = (a_cs - jnp.log(dt_c)).T
            a_last = a_cs[CHUNK - 1:CHUNK, :]
            ea_last = jnp.exp(a_last)
            ea_x = _dot(jnp.exp(a_cs).astype(BF16), ex_ref[...])
            sw_x = _dot((dt_c * jnp.exp(a_last - a_cs)).astype(BF16), ex_ref[...])
            for g in range(SSM_GROUPS):
                bm_g = u_ref[s, b_slab + g, r0:r0 + CHUNK, :].astype(BF16)
                cm_g = u_ref[s, c_slab + g, r0:r0 + CHUNK, :].astype(BF16)
                cb = _dot_nt(cm_g, bm_g)
                st_g = st_ref[s, :, g * gw:(g + 1) * gw]
                y_off = _dot(cm_g, st_g.astype(BF16))
                xw, eal_x = [], []
                for pr in range(2):
                    lo = g * gw + pr * LANES
                    xw.append((u_ref[s, 2 * g + pr, r0:r0 + CHUNK, :] * sw_x[:, lo:lo + LANES]).astype(BF16))
                    eal_x.append(_pair_cols(ea_last, 4 * g + 2 * pr, 4 * g + 2 * pr + 1, 1, lane1))
                st_ref[s, :, g * gw:(g + 1) * gw] = (st_g * jnp.concatenate(eal_x, axis=1)
                                                     + _dot_tn(bm_g, jnp.concatenate(xw, axis=1)))
                yy = []
                for pr in range(2):
                    ha = 4 * g + 2 * pr
                    hb = ha + 1
                    lo = g * gw + pr * LANES
                    xs_p = u_ref[s, 2 * g + pr, r0:r0 + CHUNK, :]
                    xs_pb = xs_p.astype(BF16)
                    rhs = jnp.concatenate([jnp.where(lane < SSM_HEAD_DIM, xs_pb, 0).astype(BF16),
                                           jnp.where(lane >= SSM_HEAD_DIM, xs_pb, 0).astype(BF16)], axis=0)
                    gs = []
                    for h in (ha, hb):
                        colb = jnp.broadcast_to(a_cs[:, h:h + 1], (CHUNK, CHUNK))
                        rowb = jnp.broadcast_to(key_t[h:h + 1, :], (CHUNK, CHUNK))
                        gs.append((cb * jnp.exp(jnp.where(causal, colb - rowb, -jnp.inf))).astype(BF16))
                    y_p = (_dot(jnp.concatenate(gs, axis=1), rhs)
                           + y_off[:, pr * LANES:(pr + 1) * LANES] * ea_x[:, lo:lo + LANES]
                           + dexp_ref[:, lo:lo + LANES] * xs_p)
                    zz = z[q0:q0 + CHUNK, lo:lo + LANES]
                    yy.append(y_p * _silu(zz))
                ms = sum(jnp.sum(t * t, axis=-1, keepdims=True) for t in yy) * (1.0 / gw)
                inv = lax.rsqrt(ms + NORM_EPS)
                for pr in range(2):
                    lo = g * gw + pr * LANES
                    y_ref[s, r0:r0 + CHUNK, lo:lo + LANES] = (yy[pr] * inv * nw_ref[:, lo:lo + LANES]).astype(BF16)


def _ssd_mixer(x, w, cw, cb, dtb, aneg, dexp, nw, ex):
    b, s, d = x.shape
    rows = MIX_ROWS
    nseq = MIX_SEQS
    const = lambda shape: pl.BlockSpec(shape, lambda i, j: (0,) * len(shape), pipeline_mode=pl.Buffered(1))
    return pl.pallas_call(
        functools.partial(_ssd_kernel, rows=rows, nseq=nseq),
        grid=(b // nseq, s // rows),
        in_specs=[pl.BlockSpec((nseq, rows, d), lambda i, j: (i, j, 0)),
                  const(w.shape), const(cw.shape), const(cb.shape),
                  const(dtb.shape), const(aneg.shape), const(dexp.shape), const(nw.shape), const(ex.shape)],
        out_specs=pl.BlockSpec((nseq, rows, SSM_D_INNER), lambda i, j: (i, j, 0)),
        out_shape=jax.ShapeDtypeStruct((b, s, SSM_D_INNER), BF16),
        scratch_shapes=[pltpu.VMEM((nseq, SSM_XBC // LANES, SUBLANES + rows, LANES), F32),
                        pltpu.VMEM((nseq, SSM_XBC // LANES, rows, LANES), F32),
                        pltpu.VMEM((nseq, SSM_STATE, SSM_D_INNER), F32)],
        compiler_params=pltpu.CompilerParams(dimension_semantics=("arbitrary", "arbitrary"),
                                             vmem_limit_bytes=VMEM_LIMIT),
        name="ssd_mixer",
    )(x, w, cw, cb, dtb, aneg, dexp, nw, ex)


def _mlstm_kernel(x_ref, w_ref, bif_ref, nw_ref,
                  y_ref, c_ref, m_ref, *, rows, nseq):
    @pl.when(pl.program_id(1) == 0)
    def _():
        c_ref[...] = jnp.zeros(c_ref.shape, F32)
        m_ref[...] = jnp.full(m_ref.shape, -jnp.inf, F32)

    xb = x_ref[...].reshape(nseq * rows, x_ref.shape[-1]).astype(BF16)
    c_k, c_v, c_o, c_g = LSTM_D_QK, 2 * LSTM_D_QK, 2 * LSTM_D_QK + LSTM_D_V, 2 * LSTM_D_QK + 2 * LSTM_D_V
    q = _dot(xb, w_ref[:, 0:c_k])
    k = _dot(xb, w_ref[:, c_k:c_v])
    v = _dot(xb, w_ref[:, c_v:c_o])
    o = _dot(xb, w_ref[:, c_o:c_g])
    gif = _dot(xb, w_ref[:, c_g:c_g + 2 * LANES]) + bif_ref[...]
    causal, tril = _causal_masks()
    lane = lax.broadcasted_iota(jnp.int32, (CHUNK, LANES), 1)
    lane1 = lax.broadcasted_iota(jnp.int32, (1, LANES), 1)
    row = lax.broadcasted_iota(jnp.int32, (LANES, 1), 0)
    head_lane = lane < LSTM_HEADS
    ones = jnp.ones((CHUNK, LANES), BF16)

    for ci in range(rows // CHUNK):
        r0 = ci * CHUNK
        pre = {}
        for s in range(nseq):
            q0 = s * rows + r0
            log_i = gif[q0:q0 + CHUNK, 0:LANES]
            f_pre = gif[q0:q0 + CHUNK, LANES:2 * LANES]
            log_f = jnp.where(head_lane, jnp.minimum(f_pre, 0.0) - jnp.log(1.0 + jnp.exp(-jnp.abs(f_pre))), 0.0)
            fcum = _cumsum_rows(tril, log_f)
            f_tot = fcum[CHUNK - 1:CHUNK, :]
            gk = log_i - fcum
            m_loc = jnp.max(f_tot + gk, axis=0, keepdims=True)
            g_t = gk.T
            m_prev = m_ref[s]
            m_new = jnp.maximum(f_tot + m_prev, m_loc)
            s_prev = jnp.exp(f_tot + m_prev - m_new)
            s_loc = jnp.exp(m_loc - m_new)
            m_ref[s] = jnp.where(lane1 < LSTM_HEADS, m_new, -jnp.inf)
            w_end = jnp.exp(gk + (f_tot - m_loc))
            pre[s] = (q0, fcum, gk, g_t, m_prev, s_prev, s_loc, w_end)

        for pr in range(LSTM_HEADS // 2):
            st = {}
            for s in range(nseq):
                q0, fcum, gk, g_t, m_prev, s_prev, s_loc, w_end = pre[s]
                q_p = q[q0:q0 + CHUNK, pr * LANES:(pr + 1) * LANES]
                k_p = k[q0:q0 + CHUNK, pr * LANES:(pr + 1) * LANES]
                k_pb = k_p.astype(BF16)
                c_pair = c_ref[s, pr]
                c_pair_b = c_pair.astype(BF16)
                c_new = c_pair * jnp.where(row < LSTM_QK,
                                           jnp.broadcast_to(s_prev[:, 2 * pr:2 * pr + 1], (LANES, 1)),
                                           jnp.broadcast_to(s_prev[:, 2 * pr + 1:2 * pr + 2], (LANES, 1)))
                st[s] = [q_p, k_p, k_pb, c_pair_b, c_new]
            for hh in range(2):
                for s in range(nseq):
                    q0, fcum, gk, g_t, m_prev, s_prev, s_loc, w_end = pre[s]
                    q_p, k_p, k_pb, c_pair_b, c_new = st[s]
                    h = 2 * pr + hh
                    in_head = (lane < LSTM_QK) if hh == 0 else (lane >= LSTM_QK)
                    v_h = v[q0:q0 + CHUNK, h * LSTM_V:(h + 1) * LSTM_V].astype(BF16)
                    v_ext = jnp.concatenate([v_h, ones], axis=1)
                    kw = jnp.where(in_head, k_p * jnp.broadcast_to(w_end[:, h:h + 1], (CHUNK, LANES)), 0.0)
                    st[s][4] = c_new + s_loc[:, h:h + 1] * _dot_tn(kw.astype(BF16), v_ext)
                    q_m = jnp.where(in_head, q_p, 0.0)
                    fcol =jnp.broadcast_to(fcum[:, h:h + 1], (CHUNK, CHUNK))
                    rowb = jnp.broadcast_to(g_t[h:h + 1, :], (CHUNK, CHUNK))
                    log_d = jnp.where(causal, fcol + rowb, -jnp.inf)
                    inter_log = fcol + m_prev[:, h:h + 1]
                    m_t = jnp.maximum(inter_log, jnp.max(log_d, axis=-1, keepdims=True))
                    scores = _dot_nt(q_m.astype(BF16), k_pb) * jnp.exp(log_d - m_t)
                    inter_w = jnp.exp(inter_log - m_t)
                    lhs = jnp.concatenate([scores.astype(BF16), (q_m * inter_w).astype(BF16)], axis=1)
                    res = _dot(lhs, jnp.concatenate([v_ext, c_pair_b], axis=0))
                    hv = res[:, 0:LSTM_V] / jnp.maximum(jnp.abs(res[:, LSTM_V:2 * LSTM_V]), jnp.exp(-m_t))
                    ms = jnp.mean(hv * hv, axis=-1, keepdims=True)
                    o_h = o[q0:q0 + CHUNK, h * LSTM_V:(h + 1) * LSTM_V]
                    y_ref[s, r0:r0 + CHUNK, h * LSTM_V:(h + 1) * LSTM_V] = (
                        _sigmoid(o_h) * (hv * lax.rsqrt(ms + NORM_EPS) * nw_ref[:, h * LSTM_V:(h + 1) * LSTM_V])
                    ).astype(BF16)
            for s in range(nseq):
                c_ref[s, pr] = st[s][4]


def _mlstm_mixer(x, w, bif, nw):
    b, s, d = x.shape
    rows = MIX_ROWS
    nseq = MIX_SEQS
    const = lambda shape: pl.BlockSpec(shape, lambda i, j: (0,) * len(shape), pipeline_mode=pl.Buffered(1))
    return pl.pallas_call(
        functools.partial(_mlstm_kernel, rows=rows, nseq=nseq),
        grid=(b // nseq, s // rows),
        in_specs=[pl.BlockSpec((nseq, rows, d), lambda i, j: (i, j, 0)),
                  const(w.shape), const(bif.shape), const(nw.shape)],
        out_specs=pl.BlockSpec((nseq, rows, LSTM_D_V), lambda i, j: (i, j, 0)),
        out_shape=jax.ShapeDtypeStruct((b, s, LSTM_D_V), BF16),
        scratch_shapes=[pltpu.VMEM((nseq, LSTM_HEADS // 2, 2 * LSTM_QK, 2 * LSTM_V), F32),
                        pltpu.VMEM((nseq, 1, LANES), F32)],
        compiler_params=pltpu.CompilerParams(dimension_semantics=("arbitrary", "arbitrary"),
                                             vmem_limit_bytes=VMEM_LIMIT),
        name="mlstm_mixer",
    )(x, w, bif, nw)


def _layer_norm(t, g, b):
    mu = jnp.mean(t, axis=-1, keepdims=True)
    tc = t - mu
    var = jnp.mean(tc * tc, axis=-1, keepdims=True)
    return tc * lax.rsqrt(var + NORM_EPS) * g + b


def _pack_pair(a, b):
    pa = lax.bitcast_convert_type(a.astype(BF16).astype(F32), jnp.uint32)
    pb = lax.bitcast_convert_type(b.astype(BF16).astype(F32), jnp.uint32)
    return pa | (pb >> 16)


def _unpack_pair(w):
    return (lax.bitcast_convert_type(w & jnp.uint32(0xFFFF0000), F32),
            lax.bitcast_convert_type(w << 16, F32))


def _route(lt):
    n = lt.shape[1]
    sub = lax.broadcasted_iota(jnp.int32, (MOE_PER_GROUP, n), 0).astype(F32)
    big = float(LANES)
    neg = -jnp.inf
    gl = lt[0:MOE_GROUPS, :]
    gmax = jnp.max(gl, axis=0, keepdims=True)
    gidx = jnp.min(jnp.where(gl == gmax, sub, big), axis=0, keepdims=True)
    grp_p = 1.0 / jnp.sum(jnp.exp(gl - gmax), axis=0, keepdims=True)
    el = lt[MOE_GROUPS:MOE_GROUPS + MOE_PER_GROUP, :]
    for g in range(1, MOE_GROUPS):
        lo = MOE_GROUPS + g * MOE_PER_GROUP
        el = jnp.where(gidx == g, lt[lo:lo + MOE_PER_GROUP, :], el)
    m1 = jnp.max(el, axis=0, keepdims=True)
    i1 = jnp.min(jnp.where(el == m1, sub, big), axis=0, keepdims=True)
    el2 = jnp.where(sub == i1, neg, el)
    m2 = jnp.max(el2, axis=0, keepdims=True)
    i2 = jnp.min(jnp.where(el2 == m2, sub, big), axis=0, keepdims=True)
    e21 = jnp.exp(m2 - m1)
    g1 = grp_p / (1.0 + e21)
    g2 = grp_p * e21 / (1.0 + e21)
    return gidx * MOE_PER_GROUP + i1, gidx * MOE_PER_GROUP + i2, g1, g2


def _merge_kernel(x_ref, ys_ref, yl_ref, wg_ref, wa_ref, wb_ref, wo_ref, g_ref, b_ref,
                  wr_ref, br_ref, eg_ref, eu_ref, ed_ref,
                  x1_ref, xp_ref, tab_ref, cnt_ref, egb_ref, eub_ref, edb_ref, carry_ref,
                  *, alpha, steps_per_tile, cast_steps):
    @pl.when(pl.program_id(0) % steps_per_tile == 0)
    def _():
        carry_ref[...] = jnp.zeros(carry_ref.shape, F32)

    @pl.when(pl.program_id(0) < cast_steps)
    def _():
        egb_ref[...] = eg_ref[...].astype(BF16)
        eub_ref[...] = eu_ref[...].astype(BF16)
        edb_ref[...] = ed_ref[...].astype(BF16)

    x = x_ref[...]
    rows = x.shape[0]
    xb = x.astype(BF16)
    d = x.shape[1]
    merged = (_sigmoid(_dot(xb, wg_ref[:, 0:d])) * _dot(ys_ref[...], wa_ref[...])
              + _sigmoid(_dot(xb, wg_ref[:, d:2 * d])) * _dot(yl_ref[...], wb_ref[...]))
    x1 = _layer_norm(alpha * x + _dot(merged.astype(BF16), wo_ref[...]), g_ref[...], b_ref[...])
    x1_ref[...] = x1
    half = x1.shape[1] // 2
    xp = _pack_pair(x1[:, :half], x1[:, half:])
    for c in range(PACK_SLABS):
        xp_ref[pl.ds(c, rows, stride=PACK_SLABS), :] = xp[:, c * LANES:(c + 1) * LANES]
    x_hi = x1.astype(BF16)
    x_lo = (x1 - x_hi.astype(F32)).astype(BF16)
    by_hi = _dot_nt(wr_ref[...], x_hi)
    lt = by_hi[0:LANES] + by_hi[LANES:2 * LANES] + _dot_nt(wr_ref[0:LANES, :], x_lo) + br_ref[...]
    e1, e2, g1, g2 = _route(lt)
    expert = lax.broadcasted_iota(jnp.int32, (LANES, rows), 0).astype(F32)
    oh1 = expert == e1
    oh2 = expert == e2
    ohs = jnp.where(oh1, 1.0, jnp.where(oh2, 1.0, 0.0))
    r_i = lax.broadcasted_iota(jnp.int32, (rows, rows), 0)
    c_i = lax.broadcasted_iota(jnp.int32, (rows, rows), 1)
    before = jnp.where(r_i < c_i, 1.0, 0.0).astype(BF16)
    carry = carry_ref[...]
    seen = _dot(ohs.astype(BF16), before) + jnp.concatenate([carry] * (rows // LANES), axis=1)
    r1 = jnp.sum(jnp.where(oh1, seen, 0.0), axis=0, keepdims=True)
    r2 = jnp.sum(jnp.where(oh2, seen, 0.0), axis=0, keepdims=True)
    carry = carry + jnp.sum(ohs, axis=1, keepdims=True)
    carry_ref[...] = carry
    cnt_ref[...] = carry.astype(jnp.int32)
    as_int = lambda v: v.astype(jnp.int32)
    as_bits = lambda v: lax.bitcast_convert_type(v, jnp.int32)
    tab_ref[...] = jnp.concatenate(
        [as_int(e1), as_int(e2), as_int(r1) * PACK_SLABS, as_int(r2) * PACK_SLABS, as_bits(g1), as_bits(g2),
         jnp.zeros((2, rows), jnp.int32)], axis=0)


def _merge(x2d, ys, yl, wg, wa, wb, wo, g, b, wr, br, eg, eu, ed, alpha, tile):
    t, d = x2d.shape
    rows = DENSE_ROWS
    spt = tile // rows
    n_steps = t // rows
    per_step = -(-MOE_EXPERTS // n_steps)
    cast_steps = MOE_EXPERTS // per_step
    assert per_step * cast_steps == MOE_EXPERTS and cast_steps <= n_steps
    const = lambda shape: pl.BlockSpec(shape, lambda i: (0,) * len(shape), pipeline_mode=pl.Buffered(1))
    rowblk = lambda w: pl.BlockSpec((rows, w), lambda i: (i, 0))
    expert = lambda w: pl.BlockSpec((per_step,) + w.shape[1:], lambda i: (jnp.minimum(i, cast_steps - 1), 0, 0))
    return pl.pallas_call(
        functools.partial(_merge_kernel, alpha=alpha, steps_per_tile=spt, cast_steps=cast_steps),
        grid=(n_steps,),
        in_specs=[rowblk(d), rowblk(SSM_D_INNER), rowblk(LSTM_D_V),
                  const(wg.shape), const(wa.shape), const(wb.shape), const(wo.shape),
                  const(g.shape), const(b.shape), const(wr.shape), const(br.shape),
                  expert(eg), expert(eu), expert(ed)],
        out_specs=[rowblk(d),
                   pl.BlockSpec((rows * PACK_SLABS, LANES), lambda i: (i, 0)),
                   pl.BlockSpec((SUBLANES, rows), lambda i: (i // spt, i % spt)),
                   pl.BlockSpec((LANES, LANES), lambda i: (i // spt, 0)),
                   expert(eg), expert(eu), expert(ed)],
        out_shape=[jax.ShapeDtypeStruct((t, d), F32),
                   jax.ShapeDtypeStruct((t * PACK_SLABS, LANES), jnp.uint32),
                   jax.ShapeDtypeStruct((t // tile * SUBLANES, tile), jnp.int32),
                   jax.ShapeDtypeStruct((t // tile * LANES, LANES), jnp.int32),
                   jax.ShapeDtypeStruct(eg.shape, BF16), jax.ShapeDtypeStruct(eu.shape, BF16),
                   jax.ShapeDtypeStruct(ed.shape, BF16)],
        scratch_shapes=[pltpu.VMEM((LANES, LANES), F32)],
        compiler_params=pltpu.CompilerParams(dimension_semantics=("arbitrary",), vmem_limit_bytes=VMEM_LIMIT),
        name="merge_ln_route",
    )(x2d, ys, yl, wg, wa, wb, wo, g, b, wr, br, eg, eu, ed)


def _segment_rows(count):
    groups = lax.shift_right_logical(count + (SUBLANES - 1), SUBLANES.bit_length() - 1)
    return groups * (SUBLANES * PACK_SLABS)


def _route_rows_kernel(cnt_ref, tab_ref, out_ref):
    i = pl.program_id(0)
    tab = tab_ref[...]
    starts = jnp.zeros(tab.shape, jnp.int32)
    off = jnp.int32(0)
    for e in range(MOE_EXPERTS):
        starts = jnp.where(tab == e, off, starts)
        off = off + _segment_rows(cnt_ref[i * MOE_EXPERTS + e])
    row = lax.broadcasted_iota(jnp.int32, tab.shape, 0)
    res = jnp.where(lax.shift_right_logical(row, 1) == 1, tab + pltpu.roll(starts, 2, axis=0), tab)
    blocks = tab.shape[1] // LANES
    for c in range(blocks):
        out_ref[pl.ds(c, SUBLANES, stride=blocks), :] = res[:, c * LANES:(c + 1) * LANES]


def _route_rows(tab, cnt, tile):
    n_tiles = tab.shape[0] // SUBLANES
    rows = SUBLANES * tile // LANES
    return pl.pallas_call(
        _route_rows_kernel,
        grid_spec=pltpu.PrefetchScalarGridSpec(
            num_scalar_prefetch=1, grid=(n_tiles,),
            in_specs=[pl.BlockSpec((SUBLANES, tile), lambda i, c: (i, 0))],
            out_specs=pl.BlockSpec((rows, LANES), lambda i, c: (i, 0))),
        out_shape=jax.ShapeDtypeStruct((n_tiles * rows, LANES), jnp.int32),
        compiler_params=pltpu.CompilerParams(dimension_semantics=("arbitrary",), vmem_limit_bytes=VMEM_LIMIT),
        name="route_rows",
    )(cnt, tab)


def _moe_kernel(cnt_ref, tab_ref, xp_hbm, wg_hbm, wu_hbm, wd_hbm, out_hbm, xys, ybuf, off_ref,
                wg_buf, wu_buf, wd_buf, xin, xout, wsem, isem, osem, *, tile, chunk, n_steps):
    i = pl.program_id(0)
    group = SUBLANES * PACK_SLABS
    half = PACK_SLABS * LANES
    weights = ((wg_hbm, wg_buf), (wu_hbm, wu_buf), (wd_hbm, wd_buf))

    def weight_copies(expert, into):
        copies = []
        for k, (hbm, buf) in enumerate(weights):
            part = buf.shape[1] // MOE_WEIGHT_SPLIT
            for q in range(MOE_WEIGHT_SPLIT):
                copies.append(pltpu.make_async_copy(hbm.at[expert, pl.ds(q * part, part)],
                                                    buf.at[into, pl.ds(q * part, part)], wsem.at[k, into]))
        return copies

    @pl.when(i == 0)
    def _():
        for ahead in range(MOE_WEIGHT_BUFFERS - 1):
            for cp in weight_copies(ahead % MOE_EXPERTS, ahead):
                cp.start()

    n_chunks = tile // chunk
    rolled = lambda trips: trips + jnp.minimum(i, 0)

    def in_buf(c):
        return pl.multiple_of((c % 2) * (chunk * PACK_SLABS), group)

    def out_buf(c):
        return pl.multiple_of((c % 2) * (chunk * OUT_SLABS), SUBLANES)

    def in_copy(c):
        row0 = pl.multiple_of((i * tile + c * chunk) * PACK_SLABS, group)
        return pltpu.make_async_copy(xp_hbm.at[pl.ds(row0, chunk * PACK_SLABS)],
                                     xin.at[pl.ds(in_buf(c), chunk * PACK_SLABS)], isem.at[c % 2])

    def out_copy(c):
        row0 = pl.multiple_of((i * tile + c * chunk) * OUT_SLABS, SUBLANES)
        return pltpu.make_async_copy(xout.at[pl.ds(out_buf(c), chunk * OUT_SLABS)],
                                     out_hbm.at[pl.ds(row0, chunk * OUT_SLABS)], osem.at[c % 2])

    def sort_rows():
        in_copy(0).start()

        def offsets(e, acc):
            padded = _segment_rows(cnt_ref[i * MOE_EXPERTS + e])
            off_ref[e] = acc
            end = acc + padded

            @pl.when(padded > 0)
            def _():
                xys[pl.ds(pl.multiple_of(end - group, group), group), :] = jnp.zeros((group, LANES), jnp.uint32)
            return end

        total = lax.fori_loop(0, MOE_EXPERTS, offsets, 0)
        off_ref[MOE_EXPERTS] = total
        tail = MOE_BLOCKS[-1] * PACK_SLABS
        xys[pl.ds(pl.multiple_of(total, group), tail), :] = jnp.zeros((tail, LANES), jnp.uint32)

        def distribute_chunk(c, carry):
            @pl.when(c + 1 < n_chunks)
            def _():
                in_copy(c + 1).start()
            in_copy(c).wait()

            g0 = c * (chunk // SUBLANES)
            shift = in_buf(c) - c * (chunk * PACK_SLABS)

            def distribute(tg, carry2):
                for u in range(SUBLANES):
                    t = tg * SUBLANES + u
                    d1 = pl.multiple_of(tab_ref[2 * tile + t], PACK_SLABS)
                    d2 = pl.multiple_of(tab_ref[3 * tile + t], PACK_SLABS)
                    row = xin[pl.ds(pl.multiple_of(shift + t * PACK_SLABS, PACK_SLABS), PACK_SLABS), :]
                    xys[pl.ds(d1, PACK_SLABS), :] = row
                    xys[pl.ds(d2, PACK_SLABS), :] = row
                return carry2

            lax.fori_loop(g0, g0 + rolled(chunk // SUBLANES), distribute, 0)
            return carry

        lax.fori_loop(0, rolled(n_chunks), distribute_chunk, 0)

    def expert_block(refs, base, rows, valid):
        wg_ref, wu_ref, wd_ref = refs
        words = [xys[pl.ds(base + c, rows, stride=PACK_SLABS), :] for c in range(PACK_SLABS)]
        parts = [_unpack_pair(w) for w in words]
        x_hi = jnp.concatenate([p[0].astype(BF16) for p in parts], axis=1)
        x_lo = jnp.concatenate([p[1].astype(BF16) for p in parts], axis=1)
        hg = _dot(x_hi, wg_ref[0:half, :]) + _dot(x_lo, wg_ref[half:2 * half, :])
        hu = _dot(x_hi, wu_ref[0:half, :]) + _dot(x_lo, wu_ref[half:2 * half, :])
        y = _dot((_silu(hg) * hu).astype(BF16), wd_ref[...])
        for c in range(PACK_SLABS):
            ybuf[pl.ds(c, rows, stride=PACK_SLABS), :] = _pack_pair(
                y[:, c * LANES:(c + 1) * LANES], y[:, half + c * LANES:half + (c + 1) * LANES])

        def copy(g, c):
            r = pl.multiple_of(g * group, group)
            xys[pl.ds(base + r, group), :] = ybuf[pl.ds(r, group), :]
            return c

        lax.fori_loop(0, valid // group, copy, 0)

    def expert_step(j, carry):
        step = i * MOE_EXPERTS + j
        slot = step % MOE_WEIGHT_BUFFERS
        nxt = step + MOE_WEIGHT_BUFFERS - 1

        @pl.when(nxt < n_steps)
        def _():
            for cp in weight_copies(nxt % MOE_EXPERTS, nxt % MOE_WEIGHT_BUFFERS):
                cp.start()

        for cp in weight_copies(j, slot):
            cp.wait()
        refs = (wg_buf.at[slot], wu_buf.at[slot], wd_buf.at[slot])
        start = off_ref[j]
        n = off_ref[j + 1] - start
        lo = 0
        for rows in MOE_BLOCKS:
            hi = rows * PACK_SLABS

            @pl.when(jnp.logical_and(n > lo, n <= hi))
            def _(rows=rows):
                expert_block(refs, pl.multiple_of(start, group), rows, n)
            lo = hi

        @pl.when(n > lo)
        def _():
            span = MOE_BLOCKS[0] * PACK_SLABS

            def block(bi, carry2):
                expert_block(refs, pl.multiple_of(start + bi * span, group), MOE_BLOCKS[0],
                             jnp.minimum(span, n - bi * span))
                return carry2

            lax.fori_loop(0, (n + span - 1) // span, block, 0)
        return carry

    def combine_rows():
        def combine_chunk(c, carry):
            @pl.when(c >= 2)
            def _():
                out_copy(c - 2).wait()

            g0 = c * (chunk // SUBLANES)
            shift = out_buf(c) - c * (chunk * OUT_SLABS)

            def combine(tg, carry2):
                for u in range(SUBLANES):
                    t = tg * SUBLANES + u
                    g1 = lax.bitcast_convert_type(tab_ref[4 * tile + t], F32)
                    g2 = lax.bitcast_convert_type(tab_ref[5 * tile + t], F32)
                    a_hi, a_lo = _unpack_pair(
                        xys[pl.ds(pl.multiple_of(tab_ref[2 * tile + t], PACK_SLABS), PACK_SLABS), :])
                    b_hi, b_lo = _unpack_pair(
                        xys[pl.ds(pl.multiple_of(tab_ref[3 * tile + t], PACK_SLABS), PACK_SLABS), :])
                    xout[pl.ds(pl.multiple_of(shift + t * OUT_SLABS, OUT_SLABS), OUT_SLABS), :] = jnp.concatenate(
                        [g1 * a_hi + g2 * b_hi, g1 * a_lo + g2 * b_lo], axis=0)
                return carry2

            lax.fori_loop(g0, g0 + rolled(chunk // SUBLANES), combine, 0)
            out_copy(c).start()
            return carry

        lax.fori_loop(0, rolled(n_chunks), combine_chunk, 0)
        for c in range(max(0, n_chunks - 2), n_chunks):
            out_copy(c).wait()

    sort_rows()
    lax.fori_loop(0, rolled(MOE_EXPERTS), expert_step, 0)
    combine_rows()


def _moe(xp, tab, cnt, wg, wu, wd, tile):
    t = xp.shape[0] // PACK_SLABS
    d = D_MODEL
    n_tiles = t // tile
    chunk = min(MOE_CHUNK, tile)
    seg_rows = 2 * tile + MOE_EXPERTS * SUBLANES + MOE_BLOCKS[-1]
    grid_spec = pltpu.PrefetchScalarGridSpec(
        num_scalar_prefetch=1,
        grid=(n_tiles,),
        in_specs=[pl.BlockSpec((SUBLANES * tile,), lambda i, c: (i,), memory_space=pltpu.SMEM,
                               pipeline_mode=pl.Buffered(1)),
                  pl.BlockSpec(memory_space=pl.ANY), pl.BlockSpec(memory_space=pl.ANY),
                  pl.BlockSpec(memory_space=pl.ANY), pl.BlockSpec(memory_space=pl.ANY)],
        out_specs=pl.BlockSpec(memory_space=pl.ANY),
        scratch_shapes=[pltpu.VMEM((seg_rows * PACK_SLABS, LANES), jnp.uint32),
                        pltpu.VMEM((MOE_BLOCKS[-1] * PACK_SLABS, LANES), jnp.uint32),
                        pltpu.SMEM((LANES,), jnp.int32),
                        pltpu.VMEM((MOE_WEIGHT_BUFFERS, d, MOE_D_FF), BF16),
                        pltpu.VMEM((MOE_WEIGHT_BUFFERS, d, MOE_D_FF), BF16),
                        pltpu.VMEM((MOE_WEIGHT_BUFFERS, MOE_D_FF, d), BF16),
                        pltpu.VMEM((2 * chunk * PACK_SLABS, LANES), jnp.uint32),
                        pltpu.VMEM((2 * chunk * OUT_SLABS, LANES), F32),
                        pltpu.SemaphoreType.DMA((3, MOE_WEIGHT_BUFFERS)),
                        pltpu.SemaphoreType.DMA((2,)), pltpu.SemaphoreType.DMA((2,))],
    )
    return pl.pallas_call(
        functools.partial(_moe_kernel, tile=tile, chunk=chunk, n_steps=n_tiles * MOE_EXPERTS),
        grid_spec=grid_spec,
        out_shape=jax.ShapeDtypeStruct((t * OUT_SLABS, LANES), F32),
        compiler_params=pltpu.CompilerParams(dimension_semantics=("arbitrary",),
                                             vmem_limit_bytes=MOE_VMEM_LIMIT),
        name="moe_experts",
    )(cnt, tab, xp, wg, wu, wd)


def _final_kernel(x1_ref, moe_ref, p_ref, g_ref, b_ref, wpg_ref, wpp_ref, out_ref, *, alpha):
    rows = x1_ref.shape[0]
    moe = jnp.concatenate([moe_ref[pl.ds(c, rows, stride=OUT_SLABS), :] for c in range(OUT_SLABS)], axis=1)
    x2 = _layer_norm(alpha * x1_ref[...] + moe, g_ref[...], b_ref[...])
    gate = _sigmoid(_dot(x2.astype(BF16), wpg_ref[...]))
    out_ref[...] = x2 + gate * _dot(p_ref[...].astype(BF16), wpp_ref[...])


def _final(x1, moe, p2d, g, b, wpg, wpp, alpha):
    t, d = x1.shape
    rows = DENSE_ROWS
    const = lambda shape: pl.BlockSpec(shape, lambda i: (0,) * len(shape))
    rowblk = lambda w: pl.BlockSpec((rows, w), lambda i: (i, 0))
    return pl.pallas_call(
        functools.partial(_final_kernel, alpha=alpha),
        grid=(t // rows,),
        in_specs=[rowblk(d), pl.BlockSpec((rows * OUT_SLABS, LANES), lambda i: (i, 0)), rowblk(p2d.shape[1]),
                  const(g.shape), const(b.shape), const(wpg.shape), const(wpp.shape)],
        out_specs=rowblk(d),
        out_shape=jax.ShapeDtypeStruct((t, d), F32),
        compiler_params=pltpu.CompilerParams(dimension_semantics=("arbitrary",), vmem_limit_bytes=VMEM_LIMIT),
        name="final_ln_ple",
    )(x1, moe, p2d, g, b, wpg, wpp)


def _proj_weights_kernel(w_ref, ssd_ref, lstm_ref, gate_ref):
    w = w_ref[...]
    cols = [0]
    for sz in IN_PROJ_SIZES:
        cols.append(cols[-1] + sz)
    lane = lax.broadcasted_iota(jnp.int32, (w.shape[0], LANES), 1)
    gate_block = lambda c: jnp.where(lane < LSTM_HEADS, w[:, c:c + LANES], 0.0)
    ssd_ref[...] = w[:, 0:cols[2] + LANES].astype(BF16)
    lstm_ref[...] = jnp.concatenate(
        [w[:, cols[3]:cols[4]] * (LSTM_QK ** -0.5), w[:, cols[4]:cols[7]], gate_block(cols[7]), gate_block(cols[8])],
        axis=1).astype(BF16)
    gate_ref[...] = w[:, cols[9]:cols[11]].astype(BF16)


def _proj_weights(w_all, layer):
    _, d, n = w_all.shape
    rows = LANES
    widths = (IN_PROJ_SIZES[0] + IN_PROJ_SIZES[1] + LANES, 2 * LSTM_D_QK + 2 * LSTM_D_V + 2 * LANES, 2 * D_MODEL)
    return pl.pallas_call(
        _proj_weights_kernel,
        grid=(d // rows,),
        in_specs=[pl.BlockSpec((None, rows, n), lambda i: (layer, i, 0))],
        out_specs=[pl.BlockSpec((rows, wd), lambda i: (i, 0)) for wd in widths],
        out_shape=[jax.ShapeDtypeStruct((d, wd), BF16) for wd in widths],
        compiler_params=pltpu.CompilerParams(dimension_semantics=("arbitrary",), vmem_limit_bytes=VMEM_LIMIT),
        name="proj_weights",
    )(w_all.astype(F32))


def _pad_lanes(w, width=LANES):
    return jnp.pad(w, ((0, 0), (0, width - w.shape[1])))


def _row(v, width=None):
    v = v.astype(F32).reshape(1, -1)
    return v if width is None else _pad_lanes(v, width)


def kernel(x, p, w_in, ssm_conv_w, ssm_conv_b, ssm_dt_bias, ssm_a_log, ssm_d, ssm_norm_w, lstm_i_bias, lstm_f_bias, lstm_norm_w, w_branch_ssm, w_branch_lstm, w_out, ln1_g, ln1_b, moe_w_group, moe_b_group, moe_w_expert, moe_b_expert, moe_w_gate, moe_w_up, moe_w_down, ln2_g, ln2_b, ple_w_proj, ple_w_gate):
    depth = w_in.shape[0]
    bsz, seq, d = x.shape
    t = bsz * seq
    alpha = (2.0 * depth) ** 0.25
    head_expand = (jnp.arange(LANES, dtype=jnp.int32)[:, None]
                   == jnp.arange(SSM_D_INNER, dtype=jnp.int32)[None, :] // SSM_HEAD_DIM).astype(BF16)
    for i in range(depth):
        bf = lambda w: w.astype(BF16)
        w_ssd, w_lstm, w_gates = _proj_weights(w_in, i)
        y_ssm = _ssd_mixer(
            x, w_ssd,
            ssm_conv_w[i].astype(F32), _row(ssm_conv_b[i]), _row(ssm_dt_bias[i], LANES),
            _row(-jnp.exp(ssm_a_log[i].astype(F32)), LANES),
            _row(jnp.repeat(ssm_d[i].astype(F32), SSM_HEAD_DIM)), _row(ssm_norm_w[i]), head_expand)
        b_if = jnp.concatenate([_row(lstm_i_bias[i], LANES), _row(lstm_f_bias[i], LANES)], axis=1)
        y_lstm = _mlstm_mixer(x, w_lstm, b_if, _row(lstm_norm_w[i]))
        w_r = _pad_lanes(jnp.concatenate([moe_w_group[i], moe_w_expert[i]], axis=1).astype(F32)).T
        w_r_hi = w_r.astype(BF16)
        w_r_cat = jnp.concatenate([w_r_hi, (w_r - w_r_hi.astype(F32)).astype(BF16)], axis=0)
        b_r = jnp.broadcast_to(_row(jnp.concatenate([moe_b_group[i], moe_b_expert[i]]), LANES).T, (LANES, DENSE_ROWS))
        tile = min(MOE_TILE, t)
        x1, xp, tab, cnt, e_gate, e_up, e_down = _merge(
            x.reshape(t, d), y_ssm.reshape(t, -1), y_lstm.reshape(t, -1),
            w_gates, bf(w_branch_ssm[i]), bf(w_branch_lstm[i]), bf(w_out[i]),
            _row(ln1_g[i]), _row(ln1_b[i]), w_r_cat, b_r,
            moe_w_gate[i].astype(F32), moe_w_up[i].astype(F32), moe_w_down[i].astype(F32), alpha, tile)
        cnt = cnt.reshape(t // tile, LANES, LANES)[:, :MOE_EXPERTS, 0].reshape(-1)
        moe = _moe(xp, _route_rows(tab, cnt, tile).reshape(-1), cnt, e_gate, e_up, e_down, tile)
        x = _final(x1, moe, p[i].reshape(t, -1), _row(ln2_g[i]), _row(ln2_b[i]),
                   bf(ple_w_gate[i]), bf(ple_w_proj[i]), alpha).reshape(bsz, seq, d)
    return x
```

```python
import functools

import jax
import jax.numpy as jnp
from jax import lax
from jax.experimental import pallas as pl
from jax.experimental.pallas import tpu as pltpu

F32 = jnp.float32
BF16 = jnp.bfloat16

D_MODEL = 1024
PLE_DIM = 256
SSM_D_INNER = 1024
SSM_HEAD_DIM = 64
SSM_HEADS = 16
SSM_GROUPS = 4
SSM_STATE = 128
SSM_CONV = 4
SSM_XBC = SSM_D_INNER + 2 * SSM_GROUPS * SSM_STATE
LSTM_HEADS = 8
LSTM_QK = 64
LSTM_V = 128
LSTM_D_QK = LSTM_HEADS * LSTM_QK
LSTM_D_V = LSTM_HEADS * LSTM_V
CHUNK = 128
MOE_GROUPS = 8
MOE_PER_GROUP = 8
MOE_EXPERTS = 64
MOE_D_FF = 512
NORM_EPS = 1e-5
IN_PROJ_SIZES = (SSM_D_INNER, SSM_XBC, SSM_HEADS, LSTM_D_QK, LSTM_D_QK, LSTM_D_V, LSTM_D_V,
                 LSTM_HEADS, LSTM_HEADS, D_MODEL, D_MODEL)

LANES = 128
SUBLANES = 8
VMEM_LIMIT = 56 * 1024 * 1024
MOE_VMEM_LIMIT = 60 * 1024 * 1024

MIX_ROWS = 256
MIX_SEQS = 2
DENSE_ROWS = 512
MOE_TILE = 8192
MOE_CHUNK = 256
MOE_BLOCKS = (192, 256, 320, 384, 448)
MOE_WEIGHT_BUFFERS = 4
MOE_WEIGHT_SPLIT = 4
PACK_SLABS = D_MODEL // 2 // LANES
OUT_SLABS = D_MODEL // LANES


def _dot(a, b):
    return jnp.dot(a, b, preferred_element_type=F32)


def _dot_nt(a, b):
    return lax.dot_general(a, b, (((1,), (1,)), ((), ())), preferred_element_type=F32)


def _dot_tn(a, b):
    return lax.dot_general(a, b, (((0,), (0,)), ((), ())), preferred_element_type=F32)


def _sigmoid(x):
    return 0.5 + 0.5 * jnp.tanh(0.5 * x)


def _silu(x):
    h = 0.5 * x
    return h + h * jnp.tanh(h)


def _softplus(x):
    return jnp.maximum(x, 0.0) + jnp.log(1.0 + jnp.exp(-jnp.abs(x)))


def _split3(x):
    hi = x.astype(BF16)
    r1 = x - hi.astype(F32)
    mid = r1.astype(BF16)
    lo = (r1 - mid.astype(F32)).astype(BF16)
    return hi, mid, lo


def _cumsum_rows(tril, x):
    hi, mid, lo = _split3(x)
    return _dot(tril, hi) + _dot(tril, mid) + _dot(tril, lo)


def _causal_masks():
    r = lax.broadcasted_iota(jnp.int32, (CHUNK, CHUNK), 0)
    c = lax.broadcasted_iota(jnp.int32, (CHUNK, CHUNK), 1)
    causal = r >= c
    return causal, jnp.where(causal, 1.0, 0.0).astype(BF16)


def _pair_cols(v, ha, hb, rows, lane):
    a = jnp.broadcast_to(v[:, ha:ha + 1], (rows, LANES))
    b = jnp.broadcast_to(v[:, hb:hb + 1], (rows, LANES))
    return jnp.where(lane < SSM_HEAD_DIM, a, b)


def _ssd_kernel(x_ref, w_ref, cw_ref, cb_ref, dtb_ref, aneg_ref, dexp_ref, nw_ref, ex_ref,
                y_ref, ext_ref, u_ref, st_ref, *, rows, nseq):
    n_slab = SSM_XBC // LANES
    tail = SUBLANES - SSM_CONV + 1

    @pl.when(pl.program_id(1) == 0)
    def _():
        ext_ref[:, :, 0:SUBLANES, :] = jnp.zeros((nseq, n_slab, SUBLANES, LANES), F32)
        st_ref[...] = jnp.zeros(st_ref.shape, F32)

    xb = x_ref[...].reshape(nseq * rows, x_ref.shape[-1]).astype(BF16)
    z = _dot(xb, w_ref[:, 0:SSM_D_INNER])
    dt_col = SSM_D_INNER + SSM_XBC
    dtr = _dot(xb, w_ref[:, dt_col:dt_col + LANES])
    dtr = jnp.where(lax.broadcasted_iota(jnp.int32, dtr.shape, 1) < SSM_HEADS, dtr, 0.0)
    for c2 in range(n_slab // 2):
        xbc = _dot(xb, w_ref[:, SSM_D_INNER + c2 * 2 * LANES:SSM_D_INNER + (c2 + 1) * 2 * LANES])
        for s in range(nseq):
            for cc in range(2):
                ext_ref[s, 2 * c2 + cc, SUBLANES:SUBLANES + rows, :] = (
                    xbc[s * rows:(s + 1) * rows, cc * LANES:(cc + 1) * LANES])
    for s in range(nseq):
        for c in range(n_slab):
            conv = cb_ref[:, c * LANES:(c + 1) * LANES]
            for k in range(SSM_CONV):
                conv = conv + cw_ref[k:k + 1, c * LANES:(c + 1) * LANES] * ext_ref[s, c, pl.ds(tail + k, rows), :]
            u_ref[s, c] = _silu(conv)
            ext_ref[s, c, 0:SUBLANES, :] = ext_ref[s, c, rows:rows + SUBLANES, :]

    dt = _softplus(dtr + dtb_ref[...])
    da = dt * aneg_ref[...]
    causal, tril = _causal_masks()
    lane = lax.broadcasted_iota(jnp.int32, (CHUNK, LANES), 1)
    lane1 = lax.broadcasted_iota(jnp.int32, (1, LANES), 1)
    gw = SSM_D_INNER // SSM_GROUPS
    b_slab = SSM_D_INNER // LANES
    c_slab = b_slab + SSM_GROUPS

    for ci in range(rows // CHUNK):
        for s in range(nseq):
            r0 = ci * CHUNK
            q0 = s * rows + r0
            dt_c = dt[q0:q0 + CHUNK]
            a_cs = _cumsum_rows(tril, da[q0:q0 + CHUNK])
            key_t = (a_cs - jnp.log(dt_c)).T
            a_last = a_cs[CHUNK - 1:CHUNK, :]
            ea_last = jnp.exp(a_last)
            ea_x = _dot(jnp.exp(a_cs).astype(BF16), ex_ref[...])
            sw_x = _dot((dt_c * jnp.exp(a_last - a_cs)).astype(BF16), ex_ref[...])
            for g in range(SSM_GROUPS):
                bm_g = u_ref[s, b_slab + g, r0:r0 + CHUNK, :].astype(BF16)
                cm_g = u_ref[s, c_slab + g, r0:r0 + CHUNK, :].astype(BF16)
                cb = _dot_nt(cm_g, bm_g)
                st_g = st_ref[s, :, g * gw:(g + 1) * gw]
                y_off = _dot(cm_g, st_g.astype(BF16))
                xw, eal_x = [], []
                for pr in range(2):
                    lo = g * gw + pr * LANES
                    xw.append((u_ref[s, 2 * g + pr, r0:r0 + CHUNK, :] * sw_x[:, lo:lo + LANES]).astype(BF16))
                    eal_x.append(_pair_cols(ea_last, 4 * g + 2 * pr, 4 * g + 2 * pr + 1, 1, lane1))
                st_ref[s, :, g * gw:(g + 1) * gw] = (st_g * jnp.concatenate(eal_x, axis=1)
                                                     + _dot_tn(bm_g, jnp.concatenate(xw, axis=1)))
                yy = []
                for pr in range(2):
                    ha = 4 * g + 2 * pr
                    hb = ha + 1
                    lo = g * gw + pr * LANES
                    xs_p = u_ref[s, 2 * g + pr, r0:r0 + CHUNK, :]
                    xs_pb = xs_p.astype(BF16)
                    rhs = jnp.concatenate([jnp.where(lane < SSM_HEAD_DIM, xs_pb, 0).astype(BF16),
                                           jnp.where(lane >= SSM_HEAD_DIM, xs_pb, 0).astype(BF16)], axis=0)
                    gs = []
                    for h in (ha, hb):
                        colb = jnp.broadcast_to(a_cs[:, h:h + 1], (CHUNK, CHUNK))
                        rowb = jnp.broadcast_to(key_t[h:h + 1, :], (CHUNK, CHUNK))
                        gs.append((cb * jnp.exp(jnp.where(causal, colb - rowb, -jnp.inf))).astype(BF16))
                    y_p = (_dot(jnp.concatenate(gs, axis=1), rhs)
                           + y_off[:, pr * LANES:(pr + 1) * LANES] * ea_x[:, lo:lo + LANES]
                           + dexp_ref[:, lo:lo + LANES] * xs_p)
                    zz = z[q0:q0 + CHUNK, lo:lo + LANES]
                    yy.append(y_p * _silu(zz))
                ms = sum(jnp.sum(t * t, axis=-1, keepdims=True) for t in yy) * (1.0 / gw)
                inv = lax.rsqrt(ms + NORM_EPS)
                for pr in range(2):
                    lo = g * gw + pr * LANES
                    y_ref[s, r0:r0 + CHUNK, lo:lo + LANES] = (yy[pr] * inv * nw_ref[:, lo:lo + LANES]).astype(BF16)


def _ssd_mixer(x, w, cw, cb, dtb, aneg, dexp, nw, ex):
    b, s, d = x.shape
    rows = MIX_ROWS
    nseq = MIX_SEQS
    const = lambda shape: pl.BlockSpec(shape, lambda i, j: (0,) * len(shape), pipeline_mode=pl.Buffered(1))
    return pl.pallas_call(
        functools.partial(_ssd_kernel, rows=rows, nseq=nseq),
        grid=(b // nseq, s // rows),
        in_specs=[pl.BlockSpec((nseq, rows, d), lambda i, j: (i, j, 0)),
                  const(w.shape), const(cw.shape), const(cb.shape),
                  const(dtb.shape), const(aneg.shape), const(dexp.shape), const(nw.shape), const(ex.shape)],
        out_specs=pl.BlockSpec((nseq, rows, SSM_D_INNER), lambda i, j: (i, j, 0)),
        out_shape=jax.ShapeDtypeStruct((b, s, SSM_D_INNER), BF16),
        scratch_shapes=[pltpu.VMEM((nseq, SSM_XBC // LANES, SUBLANES + rows, LANES), F32),
                        pltpu.VMEM((nseq, SSM_XBC // LANES, rows, LANES), F32),
                        pltpu.VMEM((nseq, SSM_STATE, SSM_D_INNER), F32)],
        compiler_params=pltpu.CompilerParams(dimension_semantics=("arbitrary", "arbitrary"),
                                             vmem_limit_bytes=VMEM_LIMIT),
        name="ssd_mixer",
    )(x, w, cw, cb, dtb, aneg, dexp, nw, ex)


def _mlstm_kernel(x_ref, w_ref, bif_ref, nw_ref,
                  y_ref, c_ref, m_ref, *, rows, nseq):
    @pl.when(pl.program_id(1) == 0)
    def _():
        c_ref[...] = jnp.zeros(c_ref.shape, F32)
        m_ref[...] = jnp.full(m_ref.shape, -jnp.inf, F32)

    xb = x_ref[...].reshape(nseq * rows, x_ref.shape[-1]).astype(BF16)
    c_k, c_v, c_o, c_g = LSTM_D_QK, 2 * LSTM_D_QK, 2 * LSTM_D_QK + LSTM_D_V, 2 * LSTM_D_QK + 2 * LSTM_D_V
    q = _dot(xb, w_ref[:, 0:c_k])
    k = _dot(xb, w_ref[:, c_k:c_v])
    v = _dot(xb, w_ref[:, c_v:c_o])
    o = _dot(xb, w_ref[:, c_o:c_g])
    gif = _dot(xb, w_ref[:, c_g:c_g + 2 * LANES]) + bif_ref[...]
    causal, tril = _causal_masks()
    lane = lax.broadcasted_iota(jnp.int32, (CHUNK, LANES), 1)
    lane1 = lax.broadcasted_iota(jnp.int32, (1, LANES), 1)
    row = lax.broadcasted_iota(jnp.int32, (LANES, 1), 0)
    head_lane = lane < LSTM_HEADS
    ones = jnp.ones((CHUNK, LANES), BF16)

    for ci in range(rows // CHUNK):
        r0 = ci * CHUNK
        pre = {}
        for s in range(nseq):
            q0 = s * rows + r0
            log_i = gif[q0:q0 + CHUNK, 0:LANES]
            f_pre = gif[q0:q0 + CHUNK, LANES:2 * LANES]
            log_f = jnp.where(head_lane, jnp.minimum(f_pre, 0.0) - jnp.log(1.0 + jnp.exp(-jnp.abs(f_pre))), 0.0)
            fcum = _cumsum_rows(tril, log_f)
            f_tot = fcum[CHUNK - 1:CHUNK, :]
            gk = log_i - fcum
            m_loc = jnp.max(f_tot + gk, axis=0, keepdims=True)
            g_t = gk.T
            m_prev = m_ref[s]
            m_new = jnp.maximum(f_tot + m_prev, m_loc)
            s_prev = jnp.exp(f_tot + m_prev - m_new)
            s_loc = jnp.exp(m_loc - m_new)
            m_ref[s] = jnp.where(lane1 < LSTM_HEADS, m_new, -jnp.inf)
            w_end = jnp.exp(gk + (f_tot - m_loc))
            pre[s] = (q0, fcum, gk, g_t, m_prev, s_prev, s_loc, w_end)

        for pr in range(LSTM_HEADS // 2):
            st = {}
            for s in range(nseq):
                q0, fcum, gk, g_t, m_prev, s_prev, s_loc, w_end = pre[s]
                q_p = q[q0:q0 + CHUNK, pr * LANES:(pr + 1) * LANES]
                k_p = k[q0:q0 + CHUNK, pr * LANES:(pr + 1) * LANES]
                k_pb = k_p.astype(BF16)
                c_pair = c_ref[s, pr]
                c_pair_b = c_pair.astype(BF16)
                c_new = c_pair * jnp.where(row < LSTM_QK,
                                           jnp.broadcast_to(s_prev[:, 2 * pr:2 * pr + 1], (LANES, 1)),
                                           jnp.broadcast_to(s_prev[:, 2 * pr + 1:2 * pr + 2], (LANES, 1)))
                st[s] = [q_p, k_p, k_pb, c_pair_b, c_new]
            for hh in range(2):
                for s in range(nseq):
                    q0, fcum, gk, g_t, m_prev, s_prev, s_loc, w_end = pre[s]
                    q_p, k_p, k_pb, c_pair_b, c_new = st[s]
                    h = 2 * pr + hh
                    in_head = (lane < LSTM_QK) if hh == 0 else (lane >= LSTM_QK)
                    v_h = v[q0:q0 + CHUNK, h * LSTM_V:(h + 1) * LSTM_V].astype(BF16)
                    v_ext = jnp.concatenate([v_h, ones], axis=1)
                    q_m = jnp.where(in_head, q_p, 0.0)
                    fcol = jnp.broadcast_to(fcum[:, h:h + 1], (CHUNK, CHUNK))
                    rowb = jnp.broadcast_to(g_t[h:h + 1, :], (CHUNK, CHUNK))
                    log_d = jnp.where(causal, fcol + rowb, -jnp.inf)
                    inter_log = fcol + m_prev[:, h:h + 1]
                    m_t = jnp.maximum(inter_log, jnp.max(log_d, axis=-1, keepdims=True))
                    scores = _dot_nt(q_m.astype(BF16), k_pb) * jnp.exp(log_d - m_t)
                    inter_w = jnp.exp(inter_log - m_t)
                    lhs = jnp.concatenate([scores.astype(BF16), (q_m * inter_w).astype(BF16)], axis=1)
                    res = _dot(lhs, jnp.concatenate([v_ext, c_pair_b], axis=0))
                    hv = res[:, 0:LSTM_V] / jnp.maximum(jnp.abs(res[:, LSTM_V:2 * LSTM_V]), jnp.exp(-m_t))
                    ms = jnp.mean(hv * hv, axis=-1, keepdims=True)
                    o_h = o[q0:q0 + CHUNK, h * LSTM_V:(h + 1) * LSTM_V]
                    y_ref[s, r0:r0 + CHUNK, h * LSTM_V:(h + 1) * LSTM_V] = (
                        _sigmoid(o_h) * (hv * lax.rsqrt(ms + NORM_EPS) * nw_ref[:, h * LSTM_V:(h + 1) * LSTM_V])
                    ).astype(BF16)
                    kw = jnp.where(in_head, k_p * jnp.broadcast_to(w_end[:, h:h + 1], (CHUNK, LANES)), 0.0)
                    st[s][4] = c_new + s_loc[:, h:h + 1] * _dot_tn(kw.astype(BF16), v_ext)
            for s in range(nseq):
                c_ref[s, pr] = st[s][4]


def _mlstm_mixer(x, w, bif, nw):
    b, s, d = x.shape
    rows = MIX_ROWS
    nseq = MIX_SEQS
    const = lambda shape: pl.BlockSpec(shape, lambda i, j: (0,) * len(shape), pipeline_mode=pl.Buffered(1))
    return pl.pallas_call(
        functools.partial(_mlstm_kernel, rows=rows, nseq=nseq),
        grid=(b // nseq, s // rows),
        in_specs=[pl.BlockSpec((nseq, rows, d), lambda i, j: (i, j, 0)),
                  const(w.shape), const(bif.shape), const(nw.shape)],
        out_specs=pl.BlockSpec((nseq, rows, LSTM_D_V), lambda i, j: (i, j, 0)),
        out_shape=jax.ShapeDtypeStruct((b, s, LSTM_D_V), BF16),
        scratch_shapes=[pltpu.VMEM((nseq, LSTM_HEADS // 2, 2 * LSTM_QK, 2 * LSTM_V), F32),
                        pltpu.VMEM((nseq, 1, LANES), F32)],
        compiler_params=pltpu.CompilerParams(dimension_semantics=("arbitrary", "arbitrary"),
                                             vmem_limit_bytes=VMEM_LIMIT),
        name="mlstm_mixer",
    )(x, w, bif, nw)


def _layer_norm(t, g, b):
    mu = jnp.mean(t, axis=-1, keepdims=True)
    tc = t - mu
    var = jnp.mean(tc * tc, axis=-1, keepdims=True)
    return tc * lax.rsqrt(var + NORM_EPS) * g + b


def _pack_pair(a, b):
    pa = lax.bitcast_convert_type(a.astype(BF16).astype(F32), jnp.uint32)
    pb = lax.bitcast_convert_type(b.astype(BF16).astype(F32), jnp.uint32)
    return pa | (pb >> 16)


def _unpack_pair(w):
    return (lax.bitcast_convert_type(w & jnp.uint32(0xFFFF0000), F32),
            lax.bitcast_convert_type(w << 16, F32))


def _route(lt):
    n = lt.shape[1]
    sub = lax.broadcasted_iota(jnp.int32, (MOE_PER_GROUP, n), 0).astype(F32)
    big = float(LANES)
    neg = -jnp.inf
    gl = lt[0:MOE_GROUPS, :]
    gmax = jnp.max(gl, axis=0, keepdims=True)
    gidx = jnp.min(jnp.where(gl == gmax, sub, big), axis=0, keepdims=True)
    grp_p = 1.0 / jnp.sum(jnp.exp(gl - gmax), axis=0, keepdims=True)
    el = lt[MOE_GROUPS:MOE_GROUPS + MOE_PER_GROUP, :]
    for g in range(1, MOE_GROUPS):
        lo = MOE_GROUPS + g * MOE_PER_GROUP
        el = jnp.where(gidx == g, lt[lo:lo + MOE_PER_GROUP, :], el)
    m1 = jnp.max(el, axis=0, keepdims=True)
    i1 = jnp.min(jnp.where(el == m1, sub, big), axis=0, keepdims=True)
    el2 = jnp.where(sub == i1, neg, el)
    m2 = jnp.max(el2, axis=0, keepdims=True)
    i2 = jnp.min(jnp.where(el2 == m2, sub, big), axis=0, keepdims=True)
    e21 = jnp.exp(m2 - m1)
    g1 = grp_p / (1.0 + e21)
    g2 = grp_p * e21 / (1.0 + e21)
    return gidx * MOE_PER_GROUP + i1, gidx * MOE_PER_GROUP + i2, g1, g2


def _merge_kernel(x_ref, ys_ref, yl_ref, wg_ref, wa_ref, wb_ref, wo_ref, g_ref, b_ref,
                  wr_ref, br_ref, eg_ref, eu_ref, ed_ref,
                  x1_ref, xp_ref, tab_ref, cnt_ref, egb_ref, eub_ref, edb_ref, carry_ref,
                  *, alpha, steps_per_tile, cast_steps):
    @pl.when(pl.program_id(0) % steps_per_tile == 0)
    def _():
        carry_ref[...] = jnp.zeros(carry_ref.shape, F32)

    @pl.when(pl.program_id(0) < cast_steps)
    def _():
        egb_ref[...] = eg_ref[...].astype(BF16)
        eub_ref[...] = eu_ref[...].astype(BF16)
        edb_ref[...] = ed_ref[...].astype(BF16)

    x = x_ref[...]
    rows = x.shape[0]
    xb = x.astype(BF16)
    d = x.shape[1]
    merged = (_sigmoid(_dot(xb, wg_ref[:, 0:d])) * _dot(ys_ref[...], wa_ref[...])
              + _sigmoid(_dot(xb, wg_ref[:, d:2 * d])) * _dot(yl_ref[...], wb_ref[...]))
    x1 = _layer_norm(alpha * x + _dot(merged.astype(BF16), wo_ref[...]), g_ref[...], b_ref[...])
    x1_ref[...] = x1
    half = x1.shape[1] // 2
    xp = _pack_pair(x1[:, :half], x1[:, half:])
    for c in range(PACK_SLABS):
        xp_ref[pl.ds(c, rows, stride=PACK_SLABS), :] = xp[:, c * LANES:(c + 1) * LANES]
    x_hi = x1.astype(BF16)
    x_lo = (x1 - x_hi.astype(F32)).astype(BF16)
    by_hi = _dot_nt(wr_ref[...], x_hi)
    lt = by_hi[0:LANES] + by_hi[LANES:2 * LANES] + _dot_nt(wr_ref[0:LANES, :], x_lo) + br_ref[...]
    e1, e2, g1, g2 = _route(lt)
    expert = lax.broadcasted_iota(jnp.int32, (LANES, rows), 0).astype(F32)
    oh1 = expert == e1
    oh2 = expert == e2
    ohs = jnp.where(oh1, 1.0, jnp.where(oh2, 1.0, 0.0))
    r_i = lax.broadcasted_iota(jnp.int32, (rows, rows), 0)
    c_i = lax.broadcasted_iota(jnp.int32, (rows, rows), 1)
    before = jnp.where(r_i < c_i, 1.0, 0.0).astype(BF16)
    carry = carry_ref[...]
    seen = _dot(ohs.astype(BF16), before) + jnp.concatenate([carry] * (rows // LANES), axis=1)
    r1 = jnp.sum(jnp.where(oh1, seen, 0.0), axis=0, keepdims=True)
    r2 = jnp.sum(jnp.where(oh2, seen, 0.0), axis=0, keepdims=True)
    carry = carry + jnp.sum(ohs, axis=1, keepdims=True)
    carry_ref[...] = carry
    cnt_ref[...] = carry.astype(jnp.int32)
    as_int = lambda v: v.astype(jnp.int32)
    as_bits = lambda v: lax.bitcast_convert_type(v, jnp.int32)
    tab_ref[...] = jnp.concatenate(
        [as_int(e1), as_int(e2), as_int(r1) * PACK_SLABS, as_int(r2) * PACK_SLABS, as_bits(g1), as_bits(g2),
         jnp.zeros((2, rows), jnp.int32)], axis=0)


def _merge(x2d, ys, yl, wg, wa, wb, wo, g, b, wr, br, eg, eu, ed, alpha, tile):
    t, d = x2d.shape
    rows = DENSE_ROWS
    spt = tile // rows
    n_steps = t // rows
    per_step = -(-MOE_EXPERTS // n_steps)
    cast_steps = MOE_EXPERTS // per_step
    assert per_step * cast_steps == MOE_EXPERTS and cast_steps <= n_steps
    const = lambda shape: pl.BlockSpec(shape, lambda i: (0,) * len(shape), pipeline_mode=pl.Buffered(1))
    rowblk = lambda w: pl.BlockSpec((rows, w), lambda i: (i, 0))
    expert = lambda w: pl.BlockSpec((per_step,) + w.shape[1:], lambda i: (jnp.minimum(i, cast_steps - 1), 0, 0))
    return pl.pallas_call(
        functools.partial(_merge_kernel, alpha=alpha, steps_per_tile=spt, cast_steps=cast_steps),
        grid=(n_steps,),
        in_specs=[rowblk(d), rowblk(SSM_D_INNER), rowblk(LSTM_D_V),
                  const(wg.shape), const(wa.shape), const(wb.shape), const(wo.shape),
                  const(g.shape), const(b.shape), const(wr.shape), const(br.shape),
                  expert(eg), expert(eu), expert(ed)],
        out_specs=[rowblk(d),
                   pl.BlockSpec((rows * PACK_SLABS, LANES), lambda i: (i, 0)),
                   pl.BlockSpec((SUBLANES, rows), lambda i: (i // spt, i % spt)),
                   pl.BlockSpec((LANES, LANES), lambda i: (i // spt, 0)),
                   expert(eg), expert(eu), expert(ed)],
        out_shape=[jax.ShapeDtypeStruct((t, d), F32),
                   jax.ShapeDtypeStruct((t * PACK_SLABS, LANES), jnp.uint32),
                   jax.ShapeDtypeStruct((t // tile * SUBLANES, tile), jnp.int32),
                   jax.ShapeDtypeStruct((t // tile * LANES, LANES), jnp.int32),
                   jax.ShapeDtypeStruct(eg.shape, BF16), jax.ShapeDtypeStruct(eu.shape, BF16),
                   jax.ShapeDtypeStruct(ed.shape, BF16)],
        scratch_shapes=[pltpu.VMEM((LANES, LANES), F32)],
        compiler_params=pltpu.CompilerParams(dimension_semantics=("arbitrary",), vmem_limit_bytes=VMEM_LIMIT),
        name="merge_ln_route",
    )(x2d, ys, yl, wg, wa, wb, wo, g, b, wr, br, eg, eu, ed)


def _segment_rows(count):
    groups = lax.shift_right_logical(count + (SUBLANES - 1), SUBLANES.bit_length() - 1)
    return groups * (SUBLANES * PACK_SLABS)


def _route_rows_kernel(cnt_ref, tab_ref, out_ref):
    i = pl.program_id(0)
    tab = tab_ref[...]
    starts = jnp.zeros(tab.shape, jnp.int32)
    off = jnp.int32(0)
    for e in range(MOE_EXPERTS):
        starts = jnp.where(tab == e, off, starts)
        off = off + _segment_rows(cnt_ref[i * MOE_EXPERTS + e])
    row = lax.broadcasted_iota(jnp.int32, tab.shape, 0)
    res = jnp.where(lax.shift_right_logical(row, 1) == 1, tab + pltpu.roll(starts, 2, axis=0), tab)
    blocks = tab.shape[1] // LANES
    for c in range(blocks):
        out_ref[pl.ds(c, SUBLANES, stride=blocks), :] = res[:, c * LANES:(c + 1) * LANES]


def _route_rows(tab, cnt, tile):
    n_tiles = tab.shape[0] // SUBLANES
    rows = SUBLANES * tile // LANES
    return pl.pallas_call(
        _route_rows_kernel,
        grid_spec=pltpu.PrefetchScalarGridSpec(
            num_scalar_prefetch=1, grid=(n_tiles,),
            in_specs=[pl.BlockSpec((SUBLANES, tile), lambda i, c: (i, 0))],
            out_specs=pl.BlockSpec((rows, LANES), lambda i, c: (i, 0))),
        out_shape=jax.ShapeDtypeStruct((n_tiles * rows, LANES), jnp.int32),
        compiler_params=pltpu.CompilerParams(dimension_semantics=("arbitrary",), vmem_limit_bytes=VMEM_LIMIT),
        name="route_rows",
    )(cnt, tab)


def _moe_kernel(cnt_ref, tab_ref, xp_hbm, wg_hbm, wu_hbm, wd_hbm, out_hbm, xys, ybuf, off_ref,
                wg_buf, wu_buf, wd_buf, xin, xout, wsem, isem, osem, *, tile, chunk, n_steps):
    i = pl.program_id(0)
    group = SUBLANES * PACK_SLABS
    half = PACK_SLABS * LANES
    weights = ((wg_hbm, wg_buf), (wu_hbm, wu_buf), (wd_hbm, wd_buf))

    def weight_copies(expert, into):
        copies = []
        for k, (hbm, buf) in enumerate(weights):
            part = buf.shape[1] // MOE_WEIGHT_SPLIT
            for q in range(MOE_WEIGHT_SPLIT):
                copies.append(pltpu.make_async_copy(hbm.at[expert, pl.ds(q * part, part)],
                                                    buf.at[into, pl.ds(q * part, part)], wsem.at[k, into]))
        return copies

    @pl.when(i == 0)
    def _():
        for ahead in range(MOE_WEIGHT_BUFFERS - 1):
            for cp in weight_copies(ahead % MOE_EXPERTS, ahead):
                cp.start()

    n_chunks = tile // chunk
    rolled = lambda trips: trips + jnp.minimum(i, 0)

    def in_buf(c):
        return pl.multiple_of((c % 2) * (chunk * PACK_SLABS), group)

    def out_buf(c):
        return pl.multiple_of((c % 2) * (chunk * OUT_SLABS), SUBLANES)

    def in_copy(c):
        row0 = pl.multiple_of((i * tile + c * chunk) * PACK_SLABS, group)
        return pltpu.make_async_copy(xp_hbm.at[pl.ds(row0, chunk * PACK_SLABS)],
                                     xin.at[pl.ds(in_buf(c), chunk * PACK_SLABS)], isem.at[c % 2])

    def out_copy(c):
        row0 = pl.multiple_of((i * tile + c * chunk) * OUT_SLABS, SUBLANES)
        return pltpu.make_async_copy(xout.at[pl.ds(out_buf(c), chunk * OUT_SLABS)],
                                     out_hbm.at[pl.ds(row0, chunk * OUT_SLABS)], osem.at[c % 2])

    def sort_rows():
        in_copy(0).start()

        def offsets(e, acc):
            padded = _segment_rows(cnt_ref[i * MOE_EXPERTS + e])
            off_ref[e] = acc
            end = acc + padded

            @pl.when(padded > 0)
            def _():
                xys[pl.ds(pl.multiple_of(end - group, group), group), :] = jnp.zeros((group, LANES), jnp.uint32)
            return end

        total = lax.fori_loop(0, MOE_EXPERTS, offsets, 0)
        off_ref[MOE_EXPERTS] = total
        tail = MOE_BLOCKS[-1] * PACK_SLABS
        xys[pl.ds(pl.multiple_of(total, group), tail), :] = jnp.zeros((tail, LANES), jnp.uint32)

        def distribute_chunk(c, carry):
            @pl.when(c + 1 < n_chunks)
            def _():
                in_copy(c + 1).start()
            in_copy(c).wait()

            g0 = c * (chunk // SUBLANES)
            shift = in_buf(c) - c * (chunk * PACK_SLABS)

            def distribute(tg, carry2):
                for u in range(SUBLANES):
                    t = tg * SUBLANES + u
                    d1 = pl.multiple_of(tab_ref[2 * tile + t], PACK_SLABS)
                    d2 = pl.multiple_of(tab_ref[3 * tile + t], PACK_SLABS)
                    row = xin[pl.ds(pl.multiple_of(shift + t * PACK_SLABS, PACK_SLABS), PACK_SLABS), :]
                    xys[pl.ds(d1, PACK_SLABS), :] = row
                    xys[pl.ds(d2, PACK_SLABS), :] = row
                return carry2

            lax.fori_loop(g0, g0 + rolled(chunk // SUBLANES), distribute, 0)
            return carry

        lax.fori_loop(0, rolled(n_chunks), distribute_chunk, 0)

    def expert_block(refs, base, rows, valid):
        wg_ref, wu_ref, wd_ref = refs
        words = [xys[pl.ds(base + c, rows, stride=PACK_SLABS), :] for c in range(PACK_SLABS)]
        parts = [_unpack_pair(w) for w in words]
        x_hi = jnp.concatenate([p[0].astype(BF16) for p in parts], axis=1)
        x_lo = jnp.concatenate([p[1].astype(BF16) for p in parts], axis=1)
        hg = _dot(x_hi, wg_ref[0:half, :]) + _dot(x_lo, wg_ref[half:2 * half, :])
        hu = _dot(x_hi, wu_ref[0:half, :]) + _dot(x_lo, wu_ref[half:2 * half, :])
        y = _dot((_silu(hg) * hu).astype(BF16), wd_ref[...])
        for c in range(PACK_SLABS):
            ybuf[pl.ds(c, rows, stride=PACK_SLABS), :] = _pack_pair(
                y[:, c * LANES:(c + 1) * LANES], y[:, half + c * LANES:half + (c + 1) * LANES])

        def copy(g, c):
            r = pl.multiple_of(g * group, group)
            xys[pl.ds(base + r, group), :] = ybuf[pl.ds(r, group), :]
            return c

        lax.fori_loop(0, valid // group, copy, 0)

    def expert_step(j, carry):
        step = i * MOE_EXPERTS + j
        slot = step % MOE_WEIGHT_BUFFERS
        nxt = step + MOE_WEIGHT_BUFFERS - 1

        @pl.when(nxt < n_steps)
        def _():
            for cp in weight_copies(nxt % MOE_EXPERTS, nxt % MOE_WEIGHT_BUFFERS):
                cp.start()

        for cp in weight_copies(j, slot):
            cp.wait()
        refs = (wg_buf.at[slot], wu_buf.at[slot], wd_buf.at[slot])
        start = off_ref[j]
        n = off_ref[j + 1] - start
        lo = 0
        for rows in MOE_BLOCKS:
            hi = rows * PACK_SLABS

            @pl.when(jnp.logical_and(n > lo, n <= hi))
            def _(rows=rows):
                expert_block(refs, pl.multiple_of(start, group), rows, n)
            lo = hi

        @pl.when(n > lo)
        def _():
            span = MOE_BLOCKS[0] * PACK_SLABS

            def block(bi, carry2):
                expert_block(refs, pl.multiple_of(start + bi * span, group), MOE_BLOCKS[0],
                             jnp.minimum(span, n - bi * span))
                return carry2

            lax.fori_loop(0, (n + span - 1) // span, block, 0)
        return carry

    def combine_rows():
        def combine_chunk(c, carry):
            @pl.when(c >= 2)
            def _():
                out_copy(c - 2).wait()

            g0 = c * (chunk // SUBLANES)
            shift = out_buf(c) - c * (chunk * OUT_SLABS)

            def combine(tg, carry2):
                for u in range(SUBLANES):
                    t = tg * SUBLANES + u
                    g1 = lax.bitcast_convert_type(tab_ref[4 * tile + t], F32)
                    g2 = lax.bitcast_convert_type(tab_ref[5 * tile + t], F32)
                    a_hi, a_lo = _unpack_pair(
                        xys[pl.ds(pl.multiple_of(tab_ref[2 * tile + t], PACK_SLABS), PACK_SLABS), :])
                    b_hi, b_lo = _unpack_pair(
                        xys[pl.ds(pl.multiple_of(tab_ref[3 * tile + t], PACK_SLABS), PACK_SLABS), :])
                    xout[pl.ds(pl.multiple_of(shift + t * OUT_SLABS, OUT_SLABS), OUT_SLABS), :] = jnp.concatenate(
                        [g1 * a_hi + g2 * b_hi, g1 * a_lo + g2 * b_lo], axis=0)
                return carry2

            lax.fori_loop(g0, g0 + rolled(chunk // SUBLANES), combine, 0)
            out_copy(c).start()
            return carry

        lax.fori_loop(0, rolled(n_chunks), combine_chunk, 0)
        for c in range(max(0, n_chunks - 2), n_chunks):
            out_copy(c).wait()

    sort_rows()
    lax.fori_loop(0, rolled(MOE_EXPERTS), expert_step, 0)
    combine_rows()


def _moe(xp, tab, cnt, wg, wu, wd, tile):
    t = xp.shape[0] // PACK_SLABS
    d = D_MODEL
    n_tiles = t // tile
    chunk = min(MOE_CHUNK, tile)
    seg_rows = 2 * tile + MOE_EXPERTS * SUBLANES + MOE_BLOCKS[-1]
    grid_spec = pltpu.PrefetchScalarGridSpec(
        num_scalar_prefetch=1,
        grid=(n_tiles,),
        in_specs=[pl.BlockSpec((SUBLANES * tile,), lambda i, c: (i,), memory_space=pltpu.SMEM,
                               pipeline_mode=pl.Buffered(1)),
                  pl.BlockSpec(memory_space=pl.ANY), pl.BlockSpec(memory_space=pl.ANY),
                  pl.BlockSpec(memory_space=pl.ANY), pl.BlockSpec(memory_space=pl.ANY)],
        out_specs=pl.BlockSpec(memory_space=pl.ANY),
        scratch_shapes=[pltpu.VMEM((seg_rows * PACK_SLABS, LANES), jnp.uint32),
                        pltpu.VMEM((MOE_BLOCKS[-1] * PACK_SLABS, LANES), jnp.uint32),
                        pltpu.SMEM((LANES,), jnp.int32),
                        pltpu.VMEM((MOE_WEIGHT_BUFFERS, d, MOE_D_FF), BF16),
                        pltpu.VMEM((MOE_WEIGHT_BUFFERS, d, MOE_D_FF), BF16),
                        pltpu.VMEM((MOE_WEIGHT_BUFFERS, MOE_D_FF, d), BF16),
                        pltpu.VMEM((2 * chunk * PACK_SLABS, LANES), jnp.uint32),
                        pltpu.VMEM((2 * chunk * OUT_SLABS, LANES), F32),
                        pltpu.SemaphoreType.DMA((3, MOE_WEIGHT_BUFFERS)),
                        pltpu.SemaphoreType.DMA((2,)), pltpu.SemaphoreType.DMA((2,))],
    )
    return pl.pallas_call(
        functools.partial(_moe_kernel, tile=tile, chunk=chunk, n_steps=n_tiles * MOE_EXPERTS),
        grid_spec=grid_spec,
        out_shape=jax.ShapeDtypeStruct((t * OUT_SLABS, LANES), F32),
        compiler_params=pltpu.CompilerParams(dimension_semantics=("arbitrary",),
                                             vmem_limit_bytes=MOE_VMEM_LIMIT),
        name="moe_experts",
    )(cnt, tab, xp, wg, wu, wd)


def _final_kernel(x1_ref, moe_ref, p_ref, g_ref, b_ref, wpg_ref, wpp_ref, out_ref, *, alpha):
    rows = x1_ref.shape[0]
    moe = jnp.concatenate([moe_ref[pl.ds(c, rows, stride=OUT_SLABS), :] for c in range(OUT_SLABS)], axis=1)
    x2 = _layer_norm(alpha * x1_ref[...] + moe, g_ref[...], b_ref[...])
    gate = _sigmoid(_dot(x2.astype(BF16), wpg_ref[...]))
    out_ref[...] = x2 + gate * _dot(p_ref[...].astype(BF16), wpp_ref[...])


def _final(x1, moe, p2d, g, b, wpg, wpp, alpha):
    t, d = x1.shape
    rows = DENSE_ROWS
    const = lambda shape: pl.BlockSpec(shape, lambda i: (0,) * len(shape))
    rowblk = lambda w: pl.BlockSpec((rows, w), lambda i: (i, 0))
    return pl.pallas_call(
        functools.partial(_final_kernel, alpha=alpha),
        grid=(t // rows,),
        in_specs=[rowblk(d), pl.BlockSpec((rows * OUT_SLABS, LANES), lambda i: (i, 0)), rowblk(p2d.shape[1]),
                  const(g.shape), const(b.shape), const(wpg.shape), const(wpp.shape)],
        out_specs=rowblk(d),
        out_shape=jax.ShapeDtypeStruct((t, d), F32),
        compiler_params=pltpu.CompilerParams(dimension_semantics=("arbitrary",), vmem_limit_bytes=VMEM_LIMIT),
        name="final_ln_ple",
    )(x1, moe, p2d, g, b, wpg, wpp)


def _proj_weights_kernel(w_ref, ssd_ref, lstm_ref, gate_ref):
    w = w_ref[...]
    cols = [0]
    for sz in IN_PROJ_SIZES:
        cols.append(cols[-1] + sz)
    lane = lax.broadcasted_iota(jnp.int32, (w.shape[0], LANES), 1)
    gate_block = lambda c: jnp.where(lane < LSTM_HEADS, w[:, c:c + LANES], 0.0)
    ssd_ref[...] = w[:, 0:cols[2] + LANES].astype(BF16)
    lstm_ref[...] = jnp.concatenate(
        [w[:, cols[3]:cols[4]] * (LSTM_QK ** -0.5), w[:, cols[4]:cols[7]], gate_block(cols[7]), gate_block(cols[8])],
        axis=1).astype(BF16)
    gate_ref[...] = w[:, cols[9]:cols[11]].astype(BF16)


def _proj_weights(w_all, layer):
    _, d, n = w_all.shape
    rows = LANES
    widths = (IN_PROJ_SIZES[0] + IN_PROJ_SIZES[1] + LANES, 2 * LSTM_D_QK + 2 * LSTM_D_V + 2 * LANES, 2 * D_MODEL)
    return pl.pallas_call(
        _proj_weights_kernel,
        grid=(d // rows,),
        in_specs=[pl.BlockSpec((None, rows, n), lambda i: (layer, i, 0))],
        out_specs=[pl.BlockSpec((rows, wd), lambda i: (i, 0)) for wd in widths],
        out_shape=[jax.ShapeDtypeStruct((d, wd), BF16) for wd in widths],
        compiler_params=pltpu.CompilerParams(dimension_semantics=("arbitrary",), vmem_limit_bytes=VMEM_LIMIT),
        name="proj_weights",
    )(w_all.astype(F32))


def _pad_lanes(w, width=LANES):
    return jnp.pad(w, ((0, 0), (0, width - w.shape[1])))


def _row(v, width=None):
    v = v.astype(F32).reshape(1, -1)
    return v if width is None else _pad_lanes(v, width)


def kernel(x, p, w_in, ssm_conv_w, ssm_conv_b, ssm_dt_bias, ssm_a_log, ssm_d, ssm_norm_w, lstm_i_bias, lstm_f_bias, lstm_norm_w, w_branch_ssm, w_branch_lstm, w_out, ln1_g, ln1_b, moe_w_group, moe_b_group, moe_w_expert, moe_b_expert, moe_w_gate, moe_w_up, moe_w_down, ln2_g, ln2_b, ple_w_proj, ple_w_gate):
    depth = w_in.shape[0]
    bsz, seq, d = x.shape
    t = bsz * seq
    alpha = (2.0 * depth) ** 0.25
    head_expand = (jnp.arange(LANES, dtype=jnp.int32)[:, None]
                   == jnp.arange(SSM_D_INNER, dtype=jnp.int32)[None, :] // SSM_HEAD_DIM).astype(BF16)
    for i in range(depth):
        bf = lambda w: w.astype(BF16)
        w_ssd, w_lstm, w_gates = _proj_weights(w_in, i)
        y_ssm = _ssd_mixer(
            x, w_ssd,
            ssm_conv_w[i].astype(F32), _row(ssm_conv_b[i]), _row(ssm_dt_bias[i], LANES),
            _row(-jnp.exp(ssm_a_log[i].astype(F32)), LANES),
            _row(jnp.repeat(ssm_d[i].astype(F32), SSM_HEAD_DIM)), _row(ssm_norm_w[i]), head_expand)
        b_if = jnp.concatenate([_row(lstm_i_bias[i], LANES), _row(lstm_f_bias[i], LANES)], axis=1)
        y_lstm = _mlstm_mixer(x, w_lstm, b_if, _row(lstm_norm_w[i]))
        w_r = _pad_lanes(jnp.concatenate([moe_w_group[i], moe_w_expert[i]], axis=1).astype(F32)).T
        w_r_hi = w_r.astype(BF16)
        w_r_cat = jnp.concatenate([w_r_hi, (w_r - w_r_hi.astype(F32)).astype(BF16)], axis=0)
        b_r = jnp.broadcast_to(_row(jnp.concatenate([moe_b_group[i], moe_b_expert[i]]), LANES).T, (LANES, DENSE_ROWS))
        tile = min(MOE_TILE, t)
        x1, xp, tab, cnt, e_gate, e_up, e_down = _merge(
            x.reshape(t, d), y_ssm.reshape(t, -1), y_lstm.reshape(t, -1),
            w_gates, bf(w_branch_ssm[i]), bf(w_branch_lstm[i]), bf(w_out[i]),
            _row(ln1_g[i]), _row(ln1_b[i]), w_r_cat, b_r,
            moe_w_gate[i].astype(F32), moe_w_up[i].astype(F32), moe_w_down[i].astype(F32), alpha, tile)
        cnt = cnt.reshape(t // tile, LANES, LANES)[:, :MOE_EXPERTS, 0].reshape(-1)
        moe = _moe(xp, _route_rows(tab, cnt, tile).reshape(-1), cnt, e_gate, e_up, e_down, tile)
        x = _final(x1, moe, p[i].reshape(t, -1), _row(ln2_g[i]), _row(ln2_b[i]),
                   bf(ple_w_gate[i]), bf(ple_w_proj[i]), alpha).reshape(bsz, seq, d)
    return x
```

```python
import functools

import jax
import jax.numpy as jnp
from jax import lax
from jax.experimental import pallas as pl
from jax.experimental.pallas import tpu as pltpu

F32 = jnp.float32
BF16 = jnp.bfloat16

D_MODEL = 1024
PLE_DIM = 256
SSM_D_INNER = 1024
SSM_HEAD_DIM = 64
SSM_HEADS = 16
SSM_GROUPS = 4
SSM_STATE = 128
SSM_CONV = 4
SSM_XBC = SSM_D_INNER + 2 * SSM_GROUPS * SSM_STATE
LSTM_HEADS = 8
LSTM_QK = 64
LSTM_V = 128
LSTM_D_QK = LSTM_HEADS * LSTM_QK
LSTM_D_V = LSTM_HEADS * LSTM_V
CHUNK = 128
MOE_GROUPS = 8
MOE_PER_GROUP = 8
MOE_EXPERTS = 64
MOE_D_FF = 512
NORM_EPS = 1e-5
IN_PROJ_SIZES = (SSM_D_INNER, SSM_XBC, SSM_HEADS, LSTM_D_QK, LSTM_D_QK, LSTM_D_V, LSTM_D_V,
                 LSTM_HEADS, LSTM_HEADS, D_MODEL, D_MODEL)

LANES = 128
SUBLANES = 8
VMEM_LIMIT = 56 * 1024 * 1024
MOE_VMEM_LIMIT = 60 * 1024 * 1024

MIX_ROWS = 256
MIX_SEQS = 2
DENSE_ROWS = 512
FINAL_ROWS = 1024
MOE_TILE = 8192
MOE_CHUNK = 256
MOE_BLOCKS = (192, 256, 320, 384, 448)
MOE_WEIGHT_BUFFERS = 4
MOE_WEIGHT_SPLIT = 4
PACK_SLABS = D_MODEL // 2 // LANES
OUT_SLABS = D_MODEL // LANES


def _dot(a, b):
    return jnp.dot(a, b, preferred_element_type=F32)


def _dot_nt(a, b):
    return lax.dot_general(a, b, (((1,), (1,)), ((), ())), preferred_element_type=F32)


def _dot_tn(a, b):
    return lax.dot_general(a, b, (((0,), (0,)), ((), ())), preferred_element_type=F32)


def _sigmoid(x):
    return 0.5 + 0.5 * jnp.tanh(0.5 * x)


def _silu(x):
    h = 0.5 * x
    return h + h * jnp.tanh(h)


def _softplus(x):
    return jnp.maximum(x, 0.0) + jnp.log(1.0 + jnp.exp(-jnp.abs(x)))


def _split3(x):
    hi = x.astype(BF16)
    r1 = x - hi.astype(F32)
    mid = r1.astype(BF16)
    lo = (r1 - mid.astype(F32)).astype(BF16)
    return hi, mid, lo


def _cumsum_rows(tril, x):
    hi, mid, lo = _split3(x)
    return _dot(tril, hi) + _dot(tril, mid) + _dot(tril, lo)


def _causal_masks():
    r = lax.broadcasted_iota(jnp.int32, (CHUNK, CHUNK), 0)
    c = lax.broadcasted_iota(jnp.int32, (CHUNK, CHUNK), 1)
    causal = r >= c
    return causal, jnp.where(causal, 1.0, 0.0).astype(BF16)


def _pair_cols(v, ha, hb, rows, lane):
    a = jnp.broadcast_to(v[:, ha:ha + 1], (rows, LANES))
    b = jnp.broadcast_to(v[:, hb:hb + 1], (rows, LANES))
    return jnp.where(lane < SSM_HEAD_DIM, a, b)


def _ssd_kernel(x_ref, w_ref, cw_ref, cb_ref, dtb_ref, aneg_ref, dexp_ref, nw_ref, ex_ref,
                y_ref, ext_ref, u_ref, st_ref, *, rows, nseq):
    n_slab = SSM_XBC // LANES
    tail = SUBLANES - SSM_CONV + 1

    @pl.when(pl.program_id(1) == 0)
    def _():
        ext_ref[:, :, 0:SUBLANES, :] = jnp.zeros((nseq, n_slab, SUBLANES, LANES), F32)
        st_ref[...] = jnp.zeros(st_ref.shape, F32)

    xb = x_ref[...].reshape(nseq * rows, x_ref.shape[-1]).astype(BF16)
    z = _dot(xb, w_ref[:, 0:SSM_D_INNER])
    dt_col = SSM_D_INNER + SSM_XBC
    dtr = _dot(xb, w_ref[:, dt_col:dt_col + LANES])
    dtr = jnp.where(lax.broadcasted_iota(jnp.int32, dtr.shape, 1) < SSM_HEADS, dtr, 0.0)
    for c2 in range(n_slab // 2):
        xbc = _dot(xb, w_ref[:, SSM_D_INNER + c2 * 2 * LANES:SSM_D_INNER + (c2 + 1) * 2 * LANES])
        for s in range(nseq):
            for cc in range(2):
                ext_ref[s, 2 * c2 + cc, SUBLANES:SUBLANES + rows, :] = (
                    xbc[s * rows:(s + 1) * rows, cc * LANES:(cc + 1) * LANES])
    for s in range(nseq):
        for c in range(n_slab):
            conv = cb_ref[:, c * LANES:(c + 1) * LANES]
            for k in range(SSM_CONV):
                conv = conv + cw_ref[k:k + 1, c * LANES:(c + 1) * LANES] * ext_ref[s, c, pl.ds(tail + k, rows), :]
            u_ref[s, c] = _silu(conv)
            ext_ref[s, c, 0:SUBLANES, :] = ext_ref[s, c, rows:rows + SUBLANES, :]

    dt = _softplus(dtr + dtb_ref[...])
    da = dt * aneg_ref[...]
    causal, tril = _causal_masks()
    lane = lax.broadcasted_iota(jnp.int32, (CHUNK, LANES), 1)
    lane1 = lax.broadcasted_iota(jnp.int32, (1, LANES), 1)
    gw = SSM_D_INNER // SSM_GROUPS
    b_slab = SSM_D_INNER // LANES
    c_slab = b_slab + SSM_GROUPS

    for ci in range(rows // CHUNK):
        for s in range(nseq):
            r0 = ci * CHUNK
            q0 = s * rows + r0
            dt_c = dt[q0:q0 + CHUNK]
            a_cs = _cumsum_rows(tril, da[q0:q0 + CHUNK])
            key_t = (a_cs - jnp.log(dt_c)).T
            a_last = a_cs[CHUNK - 1:CHUNK, :]
            ea_last = jnp.exp(a_last)
            ea_x = _dot(jnp.exp(a_cs).astype(BF16), ex_ref[...])
            sw_x = _dot((dt_c * jnp.exp(a_last - a_cs)).astype(BF16), ex_ref[...])
            for g in range(SSM_GROUPS):
                bm_g = u_ref[s, b_slab + g, r0:r0 + CHUNK, :].astype(BF16)
                cm_g = u_ref[s, c_slab + g, r0:r0 + CHUNK, :].astype(BF16)
                cb = _dot_nt(cm_g, bm_g)
                st_g = st_ref[s, :, g * gw:(g + 1) * gw]
                y_off = _dot(cm_g, st_g.astype(BF16))
                xw, eal_x = [], []
                for pr in range(2):
                    lo = g * gw + pr * LANES
                    xw.append((u_ref[s, 2 * g + pr, r0:r0 + CHUNK, :] * sw_x[:, lo:lo + LANES]).astype(BF16))
                    eal_x.append(_pair_cols(ea_last, 4 * g + 2 * pr, 4 * g + 2 * pr + 1, 1, lane1))
                st_ref[s, :, g * gw:(g + 1) * gw] = (st_g * jnp.concatenate(eal_x, axis=1)
                                                     + _dot_tn(bm_g, jnp.concatenate(xw, axis=1)))
                yy = []
                for pr in range(2):
                    ha = 4 * g + 2 * pr
                    hb = ha + 1
                    lo = g * gw + pr * LANES
                    xs_p = u_ref[s, 2 * g + pr, r0:r0 + CHUNK, :]
                    xs_pb = xs_p.astype(BF16)
                    rhs = jnp.concatenate([jnp.where(lane < SSM_HEAD_DIM, xs_pb, 0).astype(BF16),
                                           jnp.where(lane >= SSM_HEAD_DIM, xs_pb, 0).astype(BF16)], axis=0)
                    gs = []
                    for h in (ha, hb):
                        colb = jnp.broadcast_to(a_cs[:, h:h + 1], (CHUNK, CHUNK))
                        rowb = jnp.broadcast_to(key_t[h:h + 1, :], (CHUNK, CHUNK))
                        gs.append((cb * jnp.exp(jnp.where(causal, colb - rowb, -jnp.inf))).astype(BF16))
                    y_p = (_dot(jnp.concatenate(gs, axis=1), rhs)
                           + y_off[:, pr * LANES:(pr + 1) * LANES] * ea_x[:, lo:lo + LANES]
                           + dexp_ref[:, lo:lo + LANES] * xs_p)
                    zz = z[q0:q0 + CHUNK, lo:lo + LANES]
                    yy.append(y_p * _silu(zz))
                ms = sum(jnp.sum(t * t, axis=-1, keepdims=True) for t in yy) * (1.0 / gw)
                inv = lax.rsqrt(ms + NORM_EPS)
                for pr in range(2):
                    lo = g * gw + pr * LANES
                    y_ref[s, r0:r0 + CHUNK, lo:lo + LANES] = (yy[pr] * inv * nw_ref[:, lo:lo + LANES]).astype(BF16)


def _ssd_mixer(x, w, cw, cb, dtb, aneg, dexp, nw, ex):
    b, s, d = x.shape
    rows = MIX_ROWS
    nseq = MIX_SEQS
    const = lambda shape: pl.BlockSpec(shape, lambda i, j: (0,) * len(shape), pipeline_mode=pl.Buffered(1))
    return pl.pallas_call(
        functools.partial(_ssd_kernel, rows=rows, nseq=nseq),
        grid=(b // nseq, s // rows),
        in_specs=[pl.BlockSpec((nseq, rows, d), lambda i, j: (i, j, 0)),
                  const(w.shape), const(cw.shape), const(cb.shape),
                  const(dtb.shape), const(aneg.shape), const(dexp.shape), const(nw.shape), const(ex.shape)],
        out_specs=pl.BlockSpec((nseq, rows, SSM_D_INNER), lambda i, j: (i, j, 0)),
        out_shape=jax.ShapeDtypeStruct((b, s, SSM_D_INNER), BF16),
        scratch_shapes=[pltpu.VMEM((nseq, SSM_XBC // LANES, SUBLANES + rows, LANES), F32),
                        pltpu.VMEM((nseq, SSM_XBC // LANES, rows, LANES), F32),
                        pltpu.VMEM((nseq, SSM_STATE, SSM_D_INNER), F32)],
        compiler_params=pltpu.CompilerParams(dimension_semantics=("arbitrary", "arbitrary"),
                                             vmem_limit_bytes=VMEM_LIMIT),
        name="ssd_mixer",
    )(x, w, cw, cb, dtb, aneg, dexp, nw, ex)


def _mlstm_kernel(x_ref, w_ref, bif_ref, nw_ref,
                  y_ref, c_ref, m_ref, *, rows, nseq):
    @pl.when(pl.program_id(1) == 0)
    def _():
        c_ref[...] = jnp.zeros(c_ref.shape, F32)
        m_ref[...] = jnp.full(m_ref.shape, -jnp.inf, F32)

    xb = x_ref[...].reshape(nseq * rows, x_ref.shape[-1]).astype(BF16)
    c_k, c_v, c_o, c_g = LSTM_D_QK, 2 * LSTM_D_QK, 2 * LSTM_D_QK + LSTM_D_V, 2 * LSTM_D_QK + 2 * LSTM_D_V
    q = _dot(xb, w_ref[:, 0:c_k])
    k = _dot(xb, w_ref[:, c_k:c_v])
    v = _dot(xb, w_ref[:, c_v:c_o])
    o = _dot(xb, w_ref[:, c_o:c_g])
    gif = _dot(xb, w_ref[:, c_g:c_g + 2 * LANES]) + bif_ref[...]
    causal, tril = _causal_masks()
    lane = lax.broadcasted_iota(jnp.int32, (CHUNK, LANES), 1)
    lane1 = lax.broadcasted_iota(jnp.int32, (1, LANES), 1)
    row = lax.broadcasted_iota(jnp.int32, (LANES, 1), 0)
    head_lane = lane < LSTM_HEADS
    ones = jnp.ones((CHUNK, LANES), BF16)

    for ci in range(rows // CHUNK):
        r0 = ci * CHUNK
        pre = {}
        for s in range(nseq):
            q0 = s * rows + r0
            log_i = gif[q0:q0 + CHUNK, 0:LANES]
            f_pre = gif[q0:q0 + CHUNK, LANES:2 * LANES]
            log_f = jnp.where(head_lane, jnp.minimum(f_pre, 0.0) - jnp.log(1.0 + jnp.exp(-jnp.abs(f_pre))), 0.0)
            fcum = _cumsum_rows(tril, log_f)
            f_tot = fcum[CHUNK - 1:CHUNK, :]
            gk = log_i - fcum
            m_loc = jnp.max(f_tot + gk, axis=0, keepdims=True)
            g_t = gk.T
            m_prev = m_ref[s]
            m_new = jnp.maximum(f_tot + m_prev, m_loc)
            s_prev = jnp.exp(f_tot + m_prev - m_new)
            s_loc = jnp.exp(m_loc - m_new)
            m_ref[s] = jnp.where(lane1 < LSTM_HEADS, m_new, -jnp.inf)
            w_end = jnp.exp(gk + (f_tot - m_loc))
            pre[s] = (q0, fcum, gk, g_t, m_prev, s_prev, s_loc, w_end)

        for pr in range(LSTM_HEADS // 2):
            st = {}
            for s in range(nseq):
                q0, fcum, gk, g_t, m_prev, s_prev, s_loc, w_end = pre[s]
                q_p = q[q0:q0 + CHUNK, pr * LANES:(pr + 1) * LANES]
                k_p = k[q0:q0 + CHUNK, pr * LANES:(pr + 1) * LANES]
                k_pb = k_p.astype(BF16)
                c_pair = c_ref[s, pr]
                c_pair_b = c_pair.astype(BF16)
                c_new = c_pair * jnp.where(row < LSTM_QK,
                                           jnp.broadcast_to(s_prev[:, 2 * pr:2 * pr + 1], (LANES, 1)),
                                           jnp.broadcast_to(s_prev[:, 2 * pr + 1:2 * pr + 2], (LANES, 1)))
                st[s] = [q_p, k_p, k_pb, c_pair_b, c_new]
            for hh in range(2):
                for s in range(nseq):
                    q0, fcum, gk, g_t, m_prev, s_prev, s_loc, w_end = pre[s]
                    q_p, k_p, k_pb, c_pair_b, c_new = st[s]
                    h = 2 * pr + hh
                    in_head = (lane < LSTM_QK) if hh == 0 else (lane >= LSTM_QK)
                    v_h = v[q0:q0 + CHUNK, h * LSTM_V:(h + 1) * LSTM_V].astype(BF16)
                    v_ext = jnp.concatenate([v_h, ones], axis=1)
                    q_m = jnp.where(in_head, q_p, 0.0)
                    fcol = jnp.broadcast_to(fcum[:, h:h + 1], (CHUNK, CHUNK))
                    rowb = jnp.broadcast_to(g_t[h:h + 1, :], (CHUNK, CHUNK))
                    log_d = jnp.where(causal, fcol + rowb, -jnp.inf)
                    inter_log = fcol + m_prev[:, h:h + 1]
                    m_t = jnp.maximum(inter_log, jnp.max(log_d, axis=-1, keepdims=True))
                    scores = _dot_nt(q_m.astype(BF16), k_pb) * jnp.exp(log_d - m_t)
                    inter_w = jnp.exp(inter_log - m_t)
                    lhs = jnp.concatenate([scores.astype(BF16), (q_m * inter_w).astype(BF16)], axis=1)
                    res = _dot(lhs, jnp.concatenate([v_ext, c_pair_b], axis=0))
                    hv = res[:, 0:LSTM_V] / jnp.maximum(jnp.abs(res[:, LSTM_V:2 * LSTM_V]), jnp.exp(-m_t))
                    ms = jnp.mean(hv * hv, axis=-1, keepdims=True)
                    o_h = o[q0:q0 + CHUNK, h * LSTM_V:(h + 1) * LSTM_V]
                    y_ref[s, r0:r0 + CHUNK, h * LSTM_V:(h + 1) * LSTM_V] = (
                        _sigmoid(o_h) * (hv * lax.rsqrt(ms + NORM_EPS) * nw_ref[:, h * LSTM_V:(h + 1) * LSTM_V])
                    ).astype(BF16)
                    kw = jnp.where(in_head, k_p * jnp.broadcast_to(w_end[:, h:h + 1], (CHUNK, LANES)), 0.0)
                    st[s][4] = c_new + s_loc[:, h:h + 1] * _dot_tn(kw.astype(BF16), v_ext)
            for s in range(nseq):
                c_ref[s, pr] = st[s][4]


def _mlstm_mixer(x, w, bif, nw):
    b, s, d = x.shape
    rows = MIX_ROWS
    nseq = MIX_SEQS
    const = lambda shape: pl.BlockSpec(shape, lambda i, j: (0,) * len(shape), pipeline_mode=pl.Buffered(1))
    return pl.pallas_call(
        functools.partial(_mlstm_kernel, rows=rows, nseq=nseq),
        grid=(b // nseq, s // rows),
        in_specs=[pl.BlockSpec((nseq, rows, d), lambda i, j: (i, j, 0)),
                  const(w.shape), const(bif.shape), const(nw.shape)],
        out_specs=pl.BlockSpec((nseq, rows, LSTM_D_V), lambda i, j: (i, j, 0)),
        out_shape=jax.ShapeDtypeStruct((b, s, LSTM_D_V), BF16),
        scratch_shapes=[pltpu.VMEM((nseq, LSTM_HEADS // 2, 2 * LSTM_QK, 2 * LSTM_V), F32),
                        pltpu.VMEM((nseq, 1, LANES), F32)],
        compiler_params=pltpu.CompilerParams(dimension_semantics=("arbitrary", "arbitrary"),
                                             vmem_limit_bytes=VMEM_LIMIT),
        name="mlstm_mixer",
    )(x, w, bif, nw)


def _layer_norm(t, g, b):
    mu = jnp.mean(t, axis=-1, keepdims=True)
    tc = t - mu
    var = jnp.mean(tc * tc, axis=-1, keepdims=True)
    return tc * lax.rsqrt(var + NORM_EPS) * g + b


def _pack_pair(a, b):
    pa = lax.bitcast_convert_type(a.astype(BF16).astype(F32), jnp.uint32)
    pb = lax.bitcast_convert_type(b.astype(BF16).astype(F32), jnp.uint32)
    return pa | (pb >> 16)


def _unpack_pair(w):
    return (lax.bitcast_convert_type(w & jnp.uint32(0xFFFF0000), F32),
            lax.bitcast_convert_type(w << 16, F32))


def _route(lt):
    n = lt.shape[1]
    sub = lax.broadcasted_iota(jnp.int32, (MOE_PER_GROUP, n), 0).astype(F32)
    big = float(LANES)
    neg = -jnp.inf
    gl = lt[0:MOE_GROUPS, :]
    gmax = jnp.max(gl, axis=0, keepdims=True)
    gidx = jnp.min(jnp.where(gl == gmax, sub, big), axis=0, keepdims=True)
    grp_p = 1.0 / jnp.sum(jnp.exp(gl - gmax), axis=0, keepdims=True)
    el = lt[MOE_GROUPS:MOE_GROUPS + MOE_PER_GROUP, :]
    for g in range(1, MOE_GROUPS):
        lo = MOE_GROUPS + g * MOE_PER_GROUP
        el = jnp.where(gidx == g, lt[lo:lo + MOE_PER_GROUP, :], el)
    m1 = jnp.max(el, axis=0, keepdims=True)
    i1 = jnp.min(jnp.where(el == m1, sub, big), axis=0, keepdims=True)
    el2 = jnp.where(sub == i1, neg, el)
    m2 = jnp.max(el2, axis=0, keepdims=True)
    i2 = jnp.min(jnp.where(el2 == m2, sub, big), axis=0, keepdims=True)
    e21 = jnp.exp(m2 - m1)
    g1 = grp_p / (1.0 + e21)
    g2 = grp_p * e21 / (1.0 + e21)
    return gidx * MOE_PER_GROUP + i1, gidx * MOE_PER_GROUP + i2, g1, g2


def _merge_kernel(x_ref, ys_ref, yl_ref, wg_ref, wa_ref, wb_ref, wo_ref, g_ref, b_ref,
                  wr_ref, br_ref, eg_ref, eu_ref, ed_ref,
                  x1_ref, xp_ref, tab_ref, cnt_ref, egb_ref, eub_ref, edb_ref, carry_ref,
                  *, alpha, steps_per_tile, cast_steps):
    @pl.when(pl.program_id(0) % steps_per_tile == 0)
    def _():
        carry_ref[...] = jnp.zeros(carry_ref.shape, F32)

    @pl.when(pl.program_id(0) < cast_steps)
    def _():
        egb_ref[...] = eg_ref[...].astype(BF16)
        eub_ref[...] = eu_ref[...].astype(BF16)
        edb_ref[...] = ed_ref[...].astype(BF16)

    x = x_ref[...]
    rows = x.shape[0]
    xb = x.astype(BF16)
    d = x.shape[1]
    merged = (_sigmoid(_dot(xb, wg_ref[:, 0:d])) * _dot(ys_ref[...], wa_ref[...])
              + _sigmoid(_dot(xb, wg_ref[:, d:2 * d])) * _dot(yl_ref[...], wb_ref[...]))
    x1 = _layer_norm(alpha * x + _dot(merged.astype(BF16), wo_ref[...]), g_ref[...], b_ref[...])
    x1_ref[...] = x1
    half = x1.shape[1] // 2
    xp = _pack_pair(x1[:, :half], x1[:, half:])
    for c in range(PACK_SLABS):
        xp_ref[pl.ds(c, rows, stride=PACK_SLABS), :] = xp[:, c * LANES:(c + 1) * LANES]
    x_hi = x1.astype(BF16)
    x_lo = (x1 - x_hi.astype(F32)).astype(BF16)
    by_hi = _dot_nt(wr_ref[...], x_hi)
    lt = by_hi[0:LANES] + by_hi[LANES:2 * LANES] + _dot_nt(wr_ref[0:LANES, :], x_lo) + br_ref[...]
    e1, e2, g1, g2 = _route(lt)
    expert = lax.broadcasted_iota(jnp.int32, (LANES, rows), 0).astype(F32)
    oh1 = expert == e1
    oh2 = expert == e2
    ohs = jnp.where(oh1, 1.0, jnp.where(oh2, 1.0, 0.0))
    r_i = lax.broadcasted_iota(jnp.int32, (rows, rows), 0)
    c_i = lax.broadcasted_iota(jnp.int32, (rows, rows), 1)
    before = jnp.where(r_i < c_i, 1.0, 0.0).astype(BF16)
    carry = carry_ref[...]
    seen = _dot(ohs.astype(BF16), before) + jnp.concatenate([carry] * (rows // LANES), axis=1)
    r1 = jnp.sum(jnp.where(oh1, seen, 0.0), axis=0, keepdims=True)
    r2 = jnp.sum(jnp.where(oh2, seen, 0.0), axis=0, keepdims=True)
    carry = carry + jnp.sum(ohs, axis=1, keepdims=True)
    carry_ref[...] = carry
    cnt_ref[...] = carry.astype(jnp.int32)
    as_int = lambda v: v.astype(jnp.int32)
    as_bits = lambda v: lax.bitcast_convert_type(v, jnp.int32)
    tab_ref[...] = jnp.concatenate(
        [as_int(e1), as_int(e2), as_int(r1) * PACK_SLABS, as_int(r2) * PACK_SLABS, as_bits(g1), as_bits(g2),
         jnp.zeros((2, rows), jnp.int32)], axis=0)


def _merge(x2d, ys, yl, wg, wa, wb, wo, g, b, wr, br, eg, eu, ed, alpha, tile):
    t, d = x2d.shape
    rows = DENSE_ROWS
    spt = tile // rows
    n_steps = t // rows
    per_step = -(-MOE_EXPERTS // n_steps)
    cast_steps = MOE_EXPERTS // per_step
    assert per_step * cast_steps == MOE_EXPERTS and cast_steps <= n_steps
    const = lambda shape: pl.BlockSpec(shape, lambda i: (0,) * len(shape), pipeline_mode=pl.Buffered(1))
    rowblk = lambda w: pl.BlockSpec((rows, w), lambda i: (i, 0))
    expert = lambda w: pl.BlockSpec((per_step,) + w.shape[1:], lambda i: (jnp.minimum(i, cast_steps - 1), 0, 0))
    return pl.pallas_call(
        functools.partial(_merge_kernel, alpha=alpha, steps_per_tile=spt, cast_steps=cast_steps),
        grid=(n_steps,),
        in_specs=[rowblk(d), rowblk(SSM_D_INNER), rowblk(LSTM_D_V),
                  const(wg.shape), const(wa.shape), const(wb.shape), const(wo.shape),
                  const(g.shape), const(b.shape), const(wr.shape), const(br.shape),
                  expert(eg), expert(eu), expert(ed)],
        out_specs=[rowblk(d),
                   pl.BlockSpec((rows * PACK_SLABS, LANES), lambda i: (i, 0)),
                   pl.BlockSpec((SUBLANES, rows), lambda i: (i // spt, i % spt)),
                   pl.BlockSpec((LANES, LANES), lambda i: (i // spt, 0)),
                   expert(eg), expert(eu), expert(ed)],
        out_shape=[jax.ShapeDtypeStruct((t, d), F32),
                   jax.ShapeDtypeStruct((t * PACK_SLABS, LANES), jnp.uint32),
                   jax.ShapeDtypeStruct((t // tile * SUBLANES, tile), jnp.int32),
                   jax.ShapeDtypeStruct((t // tile * LANES, LANES), jnp.int32),
                   jax.ShapeDtypeStruct(eg.shape, BF16), jax.ShapeDtypeStruct(eu.shape, BF16),
                   jax.ShapeDtypeStruct(ed.shape, BF16)],
        scratch_shapes=[pltpu.VMEM((LANES, LANES), F32)],
        compiler_params=pltpu.CompilerParams(dimension_semantics=("arbitrary",), vmem_limit_bytes=VMEM_LIMIT),
        name="merge_ln_route",
    )(x2d, ys, yl, wg, wa, wb, wo, g, b, wr, br, eg, eu, ed)


def _segment_rows(count):
    groups = lax.shift_right_logical(count + (SUBLANES - 1), SUBLANES.bit_length() - 1)
    return groups * (SUBLANES * PACK_SLABS)


def _route_rows_kernel(cnt_ref, tab_ref, out_ref):
    i = pl.program_id(0)
    tab = tab_ref[...]
    starts = jnp.zeros(tab.shape, jnp.int32)
    off = jnp.int32(0)
    for e in range(MOE_EXPERTS):
        starts = jnp.where(tab == e, off, starts)
        off = off + _segment_rows(cnt_ref[i * MOE_EXPERTS + e])
    row = lax.broadcasted_iota(jnp.int32, tab.shape, 0)
    res = jnp.where(lax.shift_right_logical(row, 1) == 1, tab + pltpu.roll(starts, 2, axis=0), tab)
    blocks = tab.shape[1] // LANES
    for c in range(blocks):
        out_ref[pl.ds(c, SUBLANES, stride=blocks), :] = res[:, c * LANES:(c + 1) * LANES]


def _route_rows(tab, cnt, tile):
    n_tiles = tab.shape[0] // SUBLANES
    rows = SUBLANES * tile // LANES
    return pl.pallas_call(
        _route_rows_kernel,
        grid_spec=pltpu.PrefetchScalarGridSpec(
            num_scalar_prefetch=1, grid=(n_tiles,),
            in_specs=[pl.BlockSpec((SUBLANES, tile), lambda i, c: (i, 0))],
            out_specs=pl.BlockSpec((rows, LANES), lambda i, c: (i, 0))),
        out_shape=jax.ShapeDtypeStruct((n_tiles * rows, LANES), jnp.int32),
        compiler_params=pltpu.CompilerParams(dimension_semantics=("arbitrary",), vmem_limit_bytes=VMEM_LIMIT),
        name="route_rows",
    )(cnt, tab)


def _moe_kernel(cnt_ref, tab_ref, xp_hbm, wg_hbm, wu_hbm, wd_hbm, out_hbm, xys, ybuf, off_ref,
                wg_buf, wu_buf, wd_buf, xin, xout, wsem, isem, osem, *, tile, chunk, n_steps):
    i = pl.program_id(0)
    group = SUBLANES * PACK_SLABS
    half = PACK_SLABS * LANES
    weights = ((wg_hbm, wg_buf), (wu_hbm, wu_buf), (wd_hbm, wd_buf))

    def weight_copies(expert, into):
        copies = []
        for k, (hbm, buf) in enumerate(weights):
            part = buf.shape[1] // MOE_WEIGHT_SPLIT
            for q in range(MOE_WEIGHT_SPLIT):
                copies.append(pltpu.make_async_copy(hbm.at[expert, pl.ds(q * part, part)],
                                                    buf.at[into, pl.ds(q * part, part)], wsem.at[k, into]))
        return copies

    @pl.when(i == 0)
    def _():
        for ahead in range(MOE_WEIGHT_BUFFERS - 1):
            for cp in weight_copies(ahead % MOE_EXPERTS, ahead):
                cp.start()

    n_chunks = tile // chunk
    rolled = lambda trips: trips + jnp.minimum(i, 0)

    def in_buf(c):
        return pl.multiple_of((c % 2) * (chunk * PACK_SLABS), group)

    def out_buf(c):
        return pl.multiple_of((c % 2) * (chunk * OUT_SLABS), SUBLANES)

    def in_copy(c):
        row0 = pl.multiple_of((i * tile + c * chunk) * PACK_SLABS, group)
        return pltpu.make_async_copy(xp_hbm.at[pl.ds(row0, chunk * PACK_SLABS)],
                                     xin.at[pl.ds(in_buf(c), chunk * PACK_SLABS)], isem.at[c % 2])

    def out_copy(c):
        row0 = pl.multiple_of((i * tile + c * chunk) * OUT_SLABS, SUBLANES)
        return pltpu.make_async_copy(xout.at[pl.ds(out_buf(c), chunk * OUT_SLABS)],
                                     out_hbm.at[pl.ds(row0, chunk * OUT_SLABS)], osem.at[c % 2])

    def sort_rows():
        in_copy(0).start()

        def offsets(e, acc):
            padded = _segment_rows(cnt_ref[i * MOE_EXPERTS + e])
            off_ref[e] = acc
            end = acc + padded

            @pl.when(padded > 0)
            def _():
                xys[pl.ds(pl.multiple_of(end - group, group), group), :] = jnp.zeros((group, LANES), jnp.uint32)
            return end

        total = lax.fori_loop(0, MOE_EXPERTS, offsets, 0)
        off_ref[MOE_EXPERTS] = total
        tail = MOE_BLOCKS[-1] * PACK_SLABS
        xys[pl.ds(pl.multiple_of(total, group), tail), :] = jnp.zeros((tail, LANES), jnp.uint32)

        def distribute_chunk(c, carry):
            @pl.when(c + 1 < n_chunks)
            def _():
                in_copy(c + 1).start()
            in_copy(c).wait()

            g0 = c * (chunk // SUBLANES)
            shift = in_buf(c) - c * (chunk * PACK_SLABS)

            def distribute(tg, carry2):
                for u in range(SUBLANES):
                    t = tg * SUBLANES + u
                    d1 = pl.multiple_of(tab_ref[2 * tile + t], PACK_SLABS)
                    d2 = pl.multiple_of(tab_ref[3 * tile + t], PACK_SLABS)
                    row = xin[pl.ds(pl.multiple_of(shift + t * PACK_SLABS, PACK_SLABS), PACK_SLABS), :]
                    xys[pl.ds(d1, PACK_SLABS), :] = row
                    xys[pl.ds(d2, PACK_SLABS), :] = row
                return carry2

            lax.fori_loop(g0, g0 + rolled(chunk // SUBLANES), distribute, 0)
            return carry

        lax.fori_loop(0, rolled(n_chunks), distribute_chunk, 0)

    def expert_block(refs, base, rows, valid):
        wg_ref, wu_ref, wd_ref = refs
        words = [xys[pl.ds(base + c, rows, stride=PACK_SLABS), :] for c in range(PACK_SLABS)]
        parts = [_unpack_pair(w) for w in words]
        x_hi = jnp.concatenate([p[0].astype(BF16) for p in parts], axis=1)
        x_lo = jnp.concatenate([p[1].astype(BF16) for p in parts], axis=1)
        hg = _dot(x_hi, wg_ref[0:half, :]) + _dot(x_lo, wg_ref[half:2 * half, :])
        hu = _dot(x_hi, wu_ref[0:half, :]) + _dot(x_lo, wu_ref[half:2 * half, :])
        y = _dot((_silu(hg) * hu).astype(BF16), wd_ref[...])
        for c in range(PACK_SLABS):
            ybuf[pl.ds(c, rows, stride=PACK_SLABS), :] = _pack_pair(
                y[:, c * LANES:(c + 1) * LANES], y[:, half + c * LANES:half + (c + 1) * LANES])

        def copy(g, c):
            r = pl.multiple_of(g * group, group)
            xys[pl.ds(base + r, group), :] = ybuf[pl.ds(r, group), :]
            return c

        lax.fori_loop(0, valid // group, copy, 0)

    def expert_step(j, carry):
        step = i * MOE_EXPERTS + j
        slot = step % MOE_WEIGHT_BUFFERS
        nxt = step + MOE_WEIGHT_BUFFERS - 1

        @pl.when(nxt < n_steps)
        def _():
            for cp in weight_copies(nxt % MOE_EXPERTS, nxt % MOE_WEIGHT_BUFFERS):
                cp.start()

        for cp in weight_copies(j, slot):
            cp.wait()
        refs = (wg_buf.at[slot], wu_buf.at[slot], wd_buf.at[slot])
        start = off_ref[j]
        n = off_ref[j + 1] - start
        lo = 0
        for rows in MOE_BLOCKS:
            hi = rows * PACK_SLABS

            @pl.when(jnp.logical_and(n > lo, n <= hi))
            def _(rows=rows):
                expert_block(refs, pl.multiple_of(start, group), rows, n)
            lo = hi

        @pl.when(n > lo)
        def _():
            span = MOE_BLOCKS[0] * PACK_SLABS

            def block(bi, carry2):
                expert_block(refs, pl.multiple_of(start + bi * span, group), MOE_BLOCKS[0],
                             jnp.minimum(span, n - bi * span))
                return carry2

            lax.fori_loop(0, (n + span - 1) // span, block, 0)
        return carry

    def combine_rows():
        def combine_chunk(c, carry):
            @pl.when(c >= 2)
            def _():
                out_copy(c - 2).wait()

            g0 = c * (chunk // SUBLANES)
            shift = out_buf(c) - c * (chunk * OUT_SLABS)

            def combine(tg, carry2):
                for u in range(SUBLANES):
                    t = tg * SUBLANES + u
                    g1 = lax.bitcast_convert_type(tab_ref[4 * tile + t], F32)
                    g2 = lax.bitcast_convert_type(tab_ref[5 * tile + t], F32)
                    a_hi, a_lo = _unpack_pair(
                        xys[pl.ds(pl.multiple_of(tab_ref[2 * tile + t], PACK_SLABS), PACK_SLABS), :])
                    b_hi, b_lo = _unpack_pair(
                        xys[pl.ds(pl.multiple_of(tab_ref[3 * tile + t], PACK_SLABS), PACK_SLABS), :])
                    xout[pl.ds(pl.multiple_of(shift + t * OUT_SLABS, OUT_SLABS), OUT_SLABS), :] = jnp.concatenate(
                        [g1 * a_hi + g2 * b_hi, g1 * a_lo + g2 * b_lo], axis=0)
                return carry2

            lax.fori_loop(g0, g0 + rolled(chunk // SUBLANES), combine, 0)
            out_copy(c).start()
            return carry

        lax.fori_loop(0, rolled(n_chunks), combine_chunk, 0)
        for c in range(max(0, n_chunks - 2), n_chunks):
            out_copy(c).wait()

    sort_rows()
    lax.fori_loop(0, rolled(MOE_EXPERTS), expert_step, 0)
    combine_rows()


def _moe(xp, tab, cnt, wg, wu, wd, tile):
    t = xp.shape[0] // PACK_SLABS
    d = D_MODEL
    n_tiles = t // tile
    chunk = min(MOE_CHUNK, tile)
    seg_rows = 2 * tile + MOE_EXPERTS * SUBLANES + MOE_BLOCKS[-1]
    grid_spec = pltpu.PrefetchScalarGridSpec(
        num_scalar_prefetch=1,
        grid=(n_tiles,),
        in_specs=[pl.BlockSpec((SUBLANES * tile,), lambda i, c: (i,), memory_space=pltpu.SMEM,
                               pipeline_mode=pl.Buffered(1)),
                  pl.BlockSpec(memory_space=pl.ANY), pl.BlockSpec(memory_space=pl.ANY),
                  pl.BlockSpec(memory_space=pl.ANY), pl.BlockSpec(memory_space=pl.ANY)],
        out_specs=pl.BlockSpec(memory_space=pl.ANY),
        scratch_shapes=[pltpu.VMEM((seg_rows * PACK_SLABS, LANES), jnp.uint32),
                        pltpu.VMEM((MOE_BLOCKS[-1] * PACK_SLABS, LANES), jnp.uint32),
                        pltpu.SMEM((LANES,), jnp.int32),
                        pltpu.VMEM((MOE_WEIGHT_BUFFERS, d, MOE_D_FF), BF16),
                        pltpu.VMEM((MOE_WEIGHT_BUFFERS, d, MOE_D_FF), BF16),
                        pltpu.VMEM((MOE_WEIGHT_BUFFERS, MOE_D_FF, d), BF16),
                        pltpu.VMEM((2 * chunk * PACK_SLABS, LANES), jnp.uint32),
                        pltpu.VMEM((2 * chunk * OUT_SLABS, LANES), F32),
                        pltpu.SemaphoreType.DMA((3, MOE_WEIGHT_BUFFERS)),
                        pltpu.SemaphoreType.DMA((2,)), pltpu.SemaphoreType.DMA((2,))],
    )
    return pl.pallas_call(
        functools.partial(_moe_kernel, tile=tile, chunk=chunk, n_steps=n_tiles * MOE_EXPERTS),
        grid_spec=grid_spec,
        out_shape=jax.ShapeDtypeStruct((t * OUT_SLABS, LANES), F32),
        compiler_params=pltpu.CompilerParams(dimension_semantics=("arbitrary",),
                                             vmem_limit_bytes=MOE_VMEM_LIMIT),
        name="moe_experts",
    )(cnt, tab, xp, wg, wu, wd)


def _final_kernel(x1_ref, moe_ref, p_ref, g_ref, b_ref, wpg_ref, wpp_ref, out_ref, *, alpha):
    rows = x1_ref.shape[0]
    moe = jnp.concatenate([moe_ref[pl.ds(c, rows, stride=OUT_SLABS), :] for c in range(OUT_SLABS)], axis=1)
    x2 = _layer_norm(alpha * x1_ref[...] + moe, g_ref[...], b_ref[...])
    gate = _sigmoid(_dot(x2.astype(BF16), wpg_ref[...]))
    out_ref[...] = x2 + gate * _dot(p_ref[...].astype(BF16), wpp_ref[...])


def _final(x1, moe, p2d, g, b, wpg, wpp, alpha):
    t, d = x1.shape
    rows = min(FINAL_ROWS, t)
    const = lambda shape: pl.BlockSpec(shape, lambda i: (0,) * len(shape))
    rowblk = lambda w: pl.BlockSpec((rows, w), lambda i: (i, 0))
    return pl.pallas_call(
        functools.partial(_final_kernel, alpha=alpha),
        grid=(t // rows,),
        in_specs=[rowblk(d), pl.BlockSpec((rows * OUT_SLABS, LANES), lambda i: (i, 0)), rowblk(p2d.shape[1]),
                  const(g.shape), const(b.shape), const(wpg.shape), const(wpp.shape)],
        out_specs=rowblk(d),
        out_shape=jax.ShapeDtypeStruct((t, d), F32),
        compiler_params=pltpu.CompilerParams(dimension_semantics=("arbitrary",), vmem_limit_bytes=VMEM_LIMIT),
        name="final_ln_ple",
    )(x1, moe, p2d, g, b, wpg, wpp)


def _proj_weights_kernel(w_ref, ssd_ref, lstm_ref, gate_ref):
    w = w_ref[...]
    cols = [0]
    for sz in IN_PROJ_SIZES:
        cols.append(cols[-1] + sz)
    lane = lax.broadcasted_iota(jnp.int32, (w.shape[0], LANES), 1)
    gate_block = lambda c: jnp.where(lane < LSTM_HEADS, w[:, c:c + LANES], 0.0)
    ssd_ref[...] = w[:, 0:cols[2] + LANES].astype(BF16)
    lstm_ref[...] = jnp.concatenate(
        [w[:, cols[3]:cols[4]] * (LSTM_QK ** -0.5), w[:, cols[4]:cols[7]], gate_block(cols[7]), gate_block(cols[8])],
        axis=1).astype(BF16)
    gate_ref[...] = w[:, cols[9]:cols[11]].astype(BF16)


def _proj_weights(w_all, layer):
    _, d, n = w_all.shape
    rows = LANES
    widths = (IN_PROJ_SIZES[0] + IN_PROJ_SIZES[1] + LANES, 2 * LSTM_D_QK + 2 * LSTM_D_V + 2 * LANES, 2 * D_MODEL)
    return pl.pallas_call(
        _proj_weights_kernel,
        grid=(d // rows,),
        in_specs=[pl.BlockSpec((None, rows, n), lambda i: (layer, i, 0))],
        out_specs=[pl.BlockSpec((rows, wd), lambda i: (i, 0)) for wd in widths],
        out_shape=[jax.ShapeDtypeStruct((d, wd), BF16) for wd in widths],
        compiler_params=pltpu.CompilerParams(dimension_semantics=("arbitrary",), vmem_limit_bytes=VMEM_LIMIT),
        name="proj_weights",
    )(w_all.astype(F32))


def _pad_lanes(w, width=LANES):
    return jnp.pad(w, ((0, 0), (0, width - w.shape[1])))


def _row(v, width=None):
    v = v.astype(F32).reshape(1, -1)
    return v if width is None else _pad_lanes(v, width)


def kernel(x, p, w_in, ssm_conv_w, ssm_conv_b, ssm_dt_bias, ssm_a_log, ssm_d, ssm_norm_w, lstm_i_bias, lstm_f_bias, lstm_norm_w, w_branch_ssm, w_branch_lstm, w_out, ln1_g, ln1_b, moe_w_group, moe_b_group, moe_w_expert, moe_b_expert, moe_w_gate, moe_w_up, moe_w_down, ln2_g, ln2_b, ple_w_proj, ple_w_gate):
    depth = w_in.shape[0]
    bsz, seq, d = x.shape
    t = bsz * seq
    alpha = (2.0 * depth) ** 0.25
    head_expand = (jnp.arange(LANES, dtype=jnp.int32)[:, None]
                   == jnp.arange(SSM_D_INNER, dtype=jnp.int32)[None, :] // SSM_HEAD_DIM).astype(BF16)
    for i in range(depth):
        bf = lambda w: w.astype(BF16)
        w_ssd, w_lstm, w_gates = _proj_weights(w_in, i)
        y_ssm = _ssd_mixer(
            x, w_ssd,
            ssm_conv_w[i].astype(F32), _row(ssm_conv_b[i]), _row(ssm_dt_bias[i], LANES),
            _row(-jnp.exp(ssm_a_log[i].astype(F32)), LANES),
            _row(jnp.repeat(ssm_d[i].astype(F32), SSM_HEAD_DIM)), _row(ssm_norm_w[i]), head_expand)
        b_if = jnp.concatenate([_row(lstm_i_bias[i], LANES), _row(lstm_f_bias[i], LANES)], axis=1)
        y_lstm = _mlstm_mixer(x, w_lstm, b_if, _row(lstm_norm_w[i]))
        w_r = _pad_lanes(jnp.concatenate([moe_w_group[i], moe_w_expert[i]], axis=1).astype(F32)).T
        w_r_hi = w_r.astype(BF16)
        w_r_cat = jnp.concatenate([w_r_hi, (w_r - w_r_hi.astype(F32)).astype(BF16)], axis=0)
        b_r = jnp.broadcast_to(_row(jnp.concatenate([moe_b_group[i], moe_b_expert[i]]), LANES).T, (LANES, DENSE_ROWS))
        tile = min(MOE_TILE, t)
        x1, xp, tab, cnt, e_gate, e_up, e_down = _merge(
            x.reshape(t, d), y_ssm.reshape(t, -1), y_lstm.reshape(t, -1),
            w_gates, bf(w_branch_ssm[i]), bf(w_branch_lstm[i]), bf(w_out[i]),
            _row(ln1_g[i]), _row(ln1_b[i]), w_r_cat, b_r,
            moe_w_gate[i].astype(F32), moe_w_up[i].astype(F32), moe_w_down[i].astype(F32), alpha, tile)
        cnt = cnt.reshape(t // tile, LANES, LANES)[:, :MOE_EXPERTS, 0].reshape(-1)
        moe = _moe(xp, _route_rows(tab, cnt, tile).reshape(-1), cnt, e_gate, e_up, e_down, tile)
        x = _final(x1, moe, p[i].reshape(t, -1), _row(ln2_g[i]), _row(ln2_b[i]),
                   bf(ple_w_gate[i]), bf(ple_w_proj[i]), alpha).reshape(bsz, seq, d)
    return x
```

```python
import functools

import jax
import jax.numpy as jnp
from jax import lax
from jax.experimental import pallas as pl
from jax.experimental.pallas import tpu as pltpu

F32 = jnp.float32
BF16 = jnp.bfloat16

D_MODEL = 1024
PLE_DIM = 256
SSM_D_INNER = 1024
SSM_HEAD_DIM = 64
SSM_HEADS = 16
SSM_GROUPS = 4
SSM_STATE = 128
SSM_CONV = 4
SSM_XBC = SSM_D_INNER + 2 * SSM_GROUPS * SSM_STATE
LSTM_HEADS = 8
LSTM_QK = 64
LSTM_V = 128
LSTM_D_QK = LSTM_HEADS * LSTM_QK
LSTM_D_V = LSTM_HEADS * LSTM_V
CHUNK = 128
MOE_GROUPS = 8
MOE_PER_GROUP = 8
MOE_EXPERTS = 64
MOE_D_FF = 512
NORM_EPS = 1e-5
IN_PROJ_SIZES = (SSM_D_INNER, SSM_XBC, SSM_HEADS, LSTM_D_QK, LSTM_D_QK, LSTM_D_V, LSTM_D_V,
                 LSTM_HEADS, LSTM_HEADS, D_MODEL, D_MODEL)

LANES = 128
SUBLANES = 8
VMEM_LIMIT = 56 * 1024 * 1024
MOE_VMEM_LIMIT = 60 * 1024 * 1024

MIX_ROWS = 256
MIX_SEQS = 2
DENSE_ROWS = 512
FINAL_ROWS = 1024
MOE_TILE = 8192
MOE_CHUNK = 256
MOE_BLOCKS = (192, 256, 320, 384, 448)
MOE_WEIGHT_BUFFERS = 4
MOE_WEIGHT_SPLIT = 4
PACK_SLABS = D_MODEL // 2 // LANES
OUT_SLABS = D_MODEL // LANES


def _dot(a, b):
    return jnp.dot(a, b, preferred_element_type=F32)


def _dot_nt(a, b):
    return lax.dot_general(a, b, (((1,), (1,)), ((), ())), preferred_element_type=F32)


def _dot_tn(a, b):
    return lax.dot_general(a, b, (((0,), (0,)), ((), ())), preferred_element_type=F32)


def _sigmoid(x):
    return 0.5 + 0.5 * jnp.tanh(0.5 * x)


def _silu(x):
    h = 0.5 * x
    return h + h * jnp.tanh(h)


def _softplus(x):
    return jnp.maximum(x, 0.0) + jnp.log(1.0 + jnp.exp(-jnp.abs(x)))


def _split3(x):
    hi = x.astype(BF16)
    r1 = x - hi.astype(F32)
    mid = r1.astype(BF16)
    lo = (r1 - mid.astype(F32)).astype(BF16)
    return hi, mid, lo


def _cumsum_rows(tril, x):
    hi, mid, lo = _split3(x)
    return _dot(tril, hi) + _dot(tril, mid) + _dot(tril, lo)


def _causal_masks():
    r = lax.broadcasted_iota(jnp.int32, (CHUNK, CHUNK), 0)
    c = lax.broadcasted_iota(jnp.int32, (CHUNK, CHUNK), 1)
    causal = r >= c
    return causal, jnp.where(causal, 1.0, 0.0).astype(BF16)


def _pair_cols(v, ha, hb, rows, lane):
    a = jnp.broadcast_to(v[:, ha:ha + 1], (rows, LANES))
    b = jnp.broadcast_to(v[:, hb:hb + 1], (rows, LANES))
    return jnp.where(lane < SSM_HEAD_DIM, a, b)


def _ssd_kernel(x_ref, w_ref, cw_ref, cb_ref, dtb_ref, aneg_ref, dexp_ref, nw_ref, ex_ref,
                y_ref, ext_ref, u_ref, st_ref, *, rows, nseq):
    n_slab = SSM_XBC // LANES
    tail = SUBLANES - SSM_CONV + 1

    @pl.when(pl.program_id(1) == 0)
    def _():
        ext_ref[:, :, 0:SUBLANES, :] = jnp.zeros((nseq, n_slab, SUBLANES, LANES), F32)
        st_ref[...] = jnp.zeros(st_ref.shape, F32)

    xb = x_ref[...].reshape(nseq * rows, x_ref.shape[-1]).astype(BF16)
    z = _dot(xb, w_ref[:, 0:SSM_D_INNER])
    dt_col = SSM_D_INNER + SSM_XBC
    dtr = _dot(xb, w_ref[:, dt_col:dt_col + LANES])
    dtr = jnp.where(lax.broadcasted_iota(jnp.int32, dtr.shape, 1) < SSM_HEADS, dtr, 0.0)
    for c2 in range(n_slab // 2):
        xbc = _dot(xb, w_ref[:, SSM_D_INNER + c2 * 2 * LANES:SSM_D_INNER + (c2 + 1) * 2 * LANES])
        for s in range(nseq):
            for cc in range(2):
                ext_ref[s, 2 * c2 + cc, SUBLANES:SUBLANES + rows, :] = (
                    xbc[s * rows:(s + 1) * rows, cc * LANES:(cc + 1) * LANES])
    for s in range(nseq):
        for c in range(n_slab):
            conv = cb_ref[:, c * LANES:(c + 1) * LANES]
            for k in range(SSM_CONV):
                conv = conv + cw_ref[k:k + 1, c * LANES:(c + 1) * LANES] * ext_ref[s, c, pl.ds(tail + k, rows), :]
            u_ref[s, c] = _silu(conv)
            ext_ref[s, c, 0:SUBLANES, :] = ext_ref[s, c, rows:rows + SUBLANES, :]

    dt = _softplus(dtr + dtb_ref[...])
    da = dt * aneg_ref[...]
    causal, tril = _causal_masks()
    lane = lax.broadcasted_iota(jnp.int32, (CHUNK, LANES), 1)
    lane1 = lax.broadcasted_iota(jnp.int32, (1, LANES), 1)
    gw = SSM_D_INNER // SSM_GROUPS
    b_slab = SSM_D_INNER // LANES
    c_slab = b_slab + SSM_GROUPS

    for ci in range(rows // CHUNK):
        for s in range(nseq):
            r0 = ci * CHUNK
            q0 = s * rows + r0
            dt_c = dt[q0:q0 + CHUNK]
            a_cs = _cumsum_rows(tril, da[q0:q0 + CHUNK])
            key_t = (a_cs - jnp.log(dt_c)).T
            a_last = a_cs[CHUNK - 1:CHUNK, :]
            ea_last = jnp.exp(a_last)
            ea_x = _dot(jnp.exp(a_cs).astype(BF16), ex_ref[...])
            sw_x = _dot((dt_c * jnp.exp(a_last - a_cs)).astype(BF16), ex_ref[...])
            for g in range(SSM_GROUPS):
                bm_g = u_ref[s, b_slab + g, r0:r0 + CHUNK, :].astype(BF16)
                cm_g = u_ref[s, c_slab + g, r0:r0 + CHUNK, :].astype(BF16)
                cb = _dot_nt(cm_g, bm_g)
                st_g = st_ref[s, :, g * gw:(g + 1) * gw]
                y_off = _dot(cm_g, st_g.astype(BF16))
                xw, eal_x = [], []
                for pr in range(2):
                    lo = g * gw + pr * LANES
                    xw.append((u_ref[s, 2 * g + pr, r0:r0 + CHUNK, :] * sw_x[:, lo:lo + LANES]).astype(BF16))
                    eal_x.append(_pair_cols(ea_last, 4 * g + 2 * pr, 4 * g + 2 * pr + 1, 1, lane1))
                st_ref[s, :, g * gw:(g + 1) * gw] = (st_g * jnp.concatenate(eal_x, axis=1)
                                                     + _dot_tn(bm_g, jnp.concatenate(xw, axis=1)))
                yy = []
                for pr in range(2):
                    ha = 4 * g + 2 * pr
                    hb = ha + 1
                    lo = g * gw + pr * LANES
                    xs_p = u_ref[s, 2 * g + pr, r0:r0 + CHUNK, :]
                    xs_pb = xs_p.astype(BF16)
                    rhs = jnp.concatenate([jnp.where(lane < SSM_HEAD_DIM, xs_pb, 0).astype(BF16),
                                           jnp.where(lane >= SSM_HEAD_DIM, xs_pb, 0).astype(BF16)], axis=0)
                    gs = []
                    for h in (ha, hb):
                        colb = jnp.broadcast_to(a_cs[:, h:h + 1], (CHUNK, CHUNK))
                        rowb = jnp.broadcast_to(key_t[h:h + 1, :], (CHUNK, CHUNK))
                        gs.append((cb * jnp.exp(jnp.where(causal, colb - rowb, -jnp.inf))).astype(BF16))
                    y_p = (_dot(jnp.concatenate(gs, axis=1), rhs)
                           + y_off[:, pr * LANES:(pr + 1) * LANES] * ea_x[:, lo:lo + LANES]
                           + dexp_ref[:, lo:lo + LANES] * xs_p)
                    zz = z[q0:q0 + CHUNK, lo:lo + LANES]
                    yy.append(y_p * _silu(zz))
                ms = sum(jnp.sum(t * t, axis=-1, keepdims=True) for t in yy) * (1.0 / gw)
                inv = lax.rsqrt(ms + NORM_EPS)
                for pr in range(2):
                    lo = g * gw + pr * LANES
                    y_ref[s, r0:r0 + CHUNK, lo:lo + LANES] = (yy[pr] * inv * nw_ref[:, lo:lo + LANES]).astype(BF16)


def _ssd_mixer(x, w, cw, cb, dtb, aneg, dexp, nw, ex):
    b, s, d = x.shape
    rows = MIX_ROWS
    nseq = MIX_SEQS
    const = lambda shape: pl.BlockSpec(shape, lambda i, j: (0,) * len(shape), pipeline_mode=pl.Buffered(1))
    return pl.pallas_call(
        functools.partial(_ssd_kernel, rows=rows, nseq=nseq),
        grid=(b // nseq, s // rows),
        in_specs=[pl.BlockSpec((nseq, rows, d), lambda i, j: (i, j, 0)),
                  const(w.shape), const(cw.shape), const(cb.shape),
                  const(dtb.shape), const(aneg.shape), const(dexp.shape), const(nw.shape), const(ex.shape)],
        out_specs=pl.BlockSpec((nseq, rows, SSM_D_INNER), lambda i, j: (i, j, 0)),
        out_shape=jax.ShapeDtypeStruct((b, s, SSM_D_INNER), BF16),
        scratch_shapes=[pltpu.VMEM((nseq, SSM_XBC // LANES, SUBLANES + rows, LANES), F32),
                        pltpu.VMEM((nseq, SSM_XBC // LANES, rows, LANES), F32),
                        pltpu.VMEM((nseq, SSM_STATE, SSM_D_INNER), F32)],
        compiler_params=pltpu.CompilerParams(dimension_semantics=("arbitrary", "arbitrary"),
                                             vmem_limit_bytes=VMEM_LIMIT),
        name="ssd_mixer",
    )(x, w, cw, cb, dtb, aneg, dexp, nw, ex)


def _mlstm_kernel(x_ref, w_ref, bif_ref, nw_ref,
                  y_ref, bg_ref, c_ref, m_ref, *, rows, nseq):
    @pl.when(pl.program_id(1) == 0)
    def _():
        c_ref[...] = jnp.zeros(c_ref.shape, F32)
        m_ref[...] = jnp.full(m_ref.shape, -jnp.inf, F32)

    xb = x_ref[...].reshape(nseq * rows, x_ref.shape[-1]).astype(BF16)
    c_k, c_v, c_o, c_g = LSTM_D_QK, 2 * LSTM_D_QK, 2 * LSTM_D_QK + LSTM_D_V, 2 * LSTM_D_QK + 2 * LSTM_D_V
    q = _dot(xb, w_ref[:, 0:c_k])
    k = _dot(xb, w_ref[:, c_k:c_v])
    v = _dot(xb, w_ref[:, c_v:c_o])
    o = _dot(xb, w_ref[:, c_o:c_g])
    gif = _dot(xb, w_ref[:, c_g:c_g + 2 * LANES]) + bif_ref[...]
    c_b = c_g + 2 * LANES
    bg_ref[...] = _sigmoid(_dot(xb, w_ref[:, c_b:c_b + 2 * D_MODEL])).reshape(bg_ref.shape)
    causal, tril = _causal_masks()
    lane = lax.broadcasted_iota(jnp.int32, (CHUNK, LANES), 1)
    lane1 = lax.broadcasted_iota(jnp.int32, (1, LANES), 1)
    row = lax.broadcasted_iota(jnp.int32, (LANES, 1), 0)
    head_lane = lane < LSTM_HEADS
    ones = jnp.ones((CHUNK, LANES), BF16)

    for ci in range(rows // CHUNK):
        r0 = ci * CHUNK
        pre = {}
        for s in range(nseq):
            q0 = s * rows + r0
            log_i = gif[q0:q0 + CHUNK, 0:LANES]
            f_pre = gif[q0:q0 + CHUNK, LANES:2 * LANES]
            log_f = jnp.where(head_lane, jnp.minimum(f_pre, 0.0) - jnp.log(1.0 + jnp.exp(-jnp.abs(f_pre))), 0.0)
            fcum = _cumsum_rows(tril, log_f)
            f_tot = fcum[CHUNK - 1:CHUNK, :]
            gk = log_i - fcum
            m_loc = jnp.max(f_tot + gk, axis=0, keepdims=True)
            g_t = gk.T
            m_prev = m_ref[s]
            m_new = jnp.maximum(f_tot + m_prev, m_loc)
            s_prev = jnp.exp(f_tot + m_prev - m_new)
            s_loc = jnp.exp(m_loc - m_new)
            m_ref[s] = jnp.where(lane1 < LSTM_HEADS, m_new, -jnp.inf)
            w_end = jnp.exp(gk + (f_tot - m_loc))
            pre[s] = (q0, fcum, gk, g_t, m_prev, s_prev, s_loc, w_end)

        for pr in range(LSTM_HEADS // 2):
            st = {}
            for s in range(nseq):
                q0, fcum, gk, g_t, m_prev, s_prev, s_loc, w_end = pre[s]
                q_p = q[q0:q0 + CHUNK, pr * LANES:(pr + 1) * LANES]
                k_p = k[q0:q0 + CHUNK, pr * LANES:(pr + 1) * LANES]
                k_pb = k_p.astype(BF16)
                c_pair = c_ref[s, pr]
                c_pair_b = c_pair.astype(BF16)
                c_new = c_pair * jnp.where(row < LSTM_QK,
                                           jnp.broadcast_to(s_prev[:, 2 * pr:2 * pr + 1], (LANES, 1)),
                                           jnp.broadcast_to(s_prev[:, 2 * pr + 1:2 * pr + 2], (LANES, 1)))
                st[s] = [q_p, k_p, k_pb, c_pair_b, c_new]
            for hh in range(2):
                for s in range(nseq):
                    q0, fcum, gk, g_t, m_prev, s_prev, s_loc, w_end = pre[s]
                    q_p, k_p, k_pb, c_pair_b, c_new = st[s]
                    h = 2 * pr + hh
                    in_head = (lane < LSTM_QK) if hh == 0 else (lane >= LSTM_QK)
                    v_h = v[q0:q0 + CHUNK, h * LSTM_V:(h + 1) * LSTM_V].astype(BF16)
                    v_ext = jnp.concatenate([v_h, ones], axis=1)
                    q_m = jnp.where(in_head, q_p, 0.0)
                    fcol = jnp.broadcast_to(fcum[:, h:h + 1], (CHUNK, CHUNK))
                    rowb = jnp.broadcast_to(g_t[h:h + 1, :], (CHUNK, CHUNK))
                    log_d = jnp.where(causal, fcol + rowb, -jnp.inf)
                    inter_log = fcol + m_prev[:, h:h + 1]
                    m_t = jnp.maximum(inter_log, jnp.max(log_d, axis=-1, keepdims=True))
                    scores = _dot_nt(q_m.astype(BF16), k_pb) * jnp.exp(log_d - m_t)
                    inter_w = jnp.exp(inter_log - m_t)
                    lhs = jnp.concatenate([scores.astype(BF16), (q_m * inter_w).astype(BF16)], axis=1)
                    res = _dot(lhs, jnp.concatenate([v_ext, c_pair_b], axis=0))
                    hv = res[:, 0:LSTM_V] / jnp.maximum(jnp.abs(res[:, LSTM_V:2 * LSTM_V]), jnp.exp(-m_t))
                    ms = jnp.mean(hv * hv, axis=-1, keepdims=True)
                    o_h = o[q0:q0 + CHUNK, h * LSTM_V:(h + 1) * LSTM_V]
                    y_ref[s, r0:r0 + CHUNK, h * LSTM_V:(h + 1) * LSTM_V] = (
                        _sigmoid(o_h) * (hv * lax.rsqrt(ms + NORM_EPS) * nw_ref[:, h * LSTM_V:(h + 1) * LSTM_V])
                    ).astype(BF16)
                    kw = jnp.where(in_head, k_p * jnp.broadcast_to(w_end[:, h:h + 1], (CHUNK, LANES)), 0.0)
                    st[s][4] = c_new + s_loc[:, h:h + 1] * _dot_tn(kw.astype(BF16), v_ext)
            for s in range(nseq):
                c_ref[s, pr] = st[s][4]


def _mlstm_mixer(x, w, bif, nw):
    b, s, d = x.shape
    rows = MIX_ROWS
    nseq = MIX_SEQS
    const = lambda shape: pl.BlockSpec(shape, lambda i, j: (0,) * len(shape), pipeline_mode=pl.Buffered(1))
    return pl.pallas_call(
        functools.partial(_mlstm_kernel, rows=rows, nseq=nseq),
        grid=(b // nseq, s // rows),
        in_specs=[pl.BlockSpec((nseq, rows, d), lambda i, j: (i, j, 0)),
                  const(w.shape), const(bif.shape), const(nw.shape)],
        out_specs=[pl.BlockSpec((nseq, rows, LSTM_D_V), lambda i, j: (i, j, 0)),
                   pl.BlockSpec((nseq, rows, 2 * D_MODEL), lambda i, j: (i, j, 0))],
        out_shape=[jax.ShapeDtypeStruct((b, s, LSTM_D_V), BF16), jax.ShapeDtypeStruct((b, s, 2 * D_MODEL), F32)],
        scratch_shapes=[pltpu.VMEM((nseq, LSTM_HEADS // 2, 2 * LSTM_QK, 2 * LSTM_V), F32),
                        pltpu.VMEM((nseq, 1, LANES), F32)],
        compiler_params=pltpu.CompilerParams(dimension_semantics=("arbitrary", "arbitrary"),
                                             vmem_limit_bytes=VMEM_LIMIT),
        name="mlstm_mixer",
    )(x, w, bif, nw)


def _layer_norm(t, g, b):
    mu = jnp.mean(t, axis=-1, keepdims=True)
    tc = t - mu
    var = jnp.mean(tc * tc, axis=-1, keepdims=True)
    return tc * lax.rsqrt(var + NORM_EPS) * g + b


def _pack_pair(a, b):
    pa = lax.bitcast_convert_type(a.astype(BF16).astype(F32), jnp.uint32)
    pb = lax.bitcast_convert_type(b.astype(BF16).astype(F32), jnp.uint32)
    return pa | (pb >> 16)


def _unpack_pair(w):
    return (lax.bitcast_convert_type(w & jnp.uint32(0xFFFF0000), F32),
            lax.bitcast_convert_type(w << 16, F32))


def _route(lt):
    n = lt.shape[1]
    sub = lax.broadcasted_iota(jnp.int32, (MOE_PER_GROUP, n), 0).astype(F32)
    big = float(LANES)
    neg = -jnp.inf
    gl = lt[0:MOE_GROUPS, :]
    gmax = jnp.max(gl, axis=0, keepdims=True)
    gidx = jnp.min(jnp.where(gl == gmax, sub, big), axis=0, keepdims=True)
    grp_p = 1.0 / jnp.sum(jnp.exp(gl - gmax), axis=0, keepdims=True)
    el = lt[MOE_GROUPS:MOE_GROUPS + MOE_PER_GROUP, :]
    for g in range(1, MOE_GROUPS):
        lo = MOE_GROUPS + g * MOE_PER_GROUP
        el = jnp.where(gidx == g, lt[lo:lo + MOE_PER_GROUP, :], el)
    m1 = jnp.max(el, axis=0, keepdims=True)
    i1 = jnp.min(jnp.where(el == m1, sub, big), axis=0, keepdims=True)
    el2 = jnp.where(sub == i1, neg, el)
    m2 = jnp.max(el2, axis=0, keepdims=True)
    i2 = jnp.min(jnp.where(el2 == m2, sub, big), axis=0, keepdims=True)
    e21 = jnp.exp(m2 - m1)
    g1 = grp_p / (1.0 + e21)
    g2 = grp_p * e21 / (1.0 + e21)
    return gidx * MOE_PER_GROUP + i1, gidx * MOE_PER_GROUP + i2, g1, g2


def _merge_kernel(x_ref, ys_ref, yl_ref, bg_ref, wa_ref, wb_ref, wo_ref, g_ref, b_ref,
                  wr_ref, br_ref, eg_ref, eu_ref, ed_ref,
                  x1_ref, xp_ref, tab_ref, cnt_ref, egb_ref, eub_ref, edb_ref, carry_ref,
                  *, alpha, steps_per_tile, cast_steps):
    @pl.when(pl.program_id(0) % steps_per_tile == 0)
    def _():
        carry_ref[...] = jnp.zeros(carry_ref.shape, F32)

    @pl.when(pl.program_id(0) < cast_steps)
    def _():
        egb_ref[...] = eg_ref[...].astype(BF16)
        eub_ref[...] = eu_ref[...].astype(BF16)
        edb_ref[...] = ed_ref[...].astype(BF16)

    x = x_ref[...]
    rows = x.shape[0]
    d = x.shape[1]
    merged = (bg_ref[:, 0:d] * _dot(ys_ref[...], wa_ref[...])
              + bg_ref[:, d:2 * d] * _dot(yl_ref[...], wb_ref[...]))
    x1 = _layer_norm(alpha * x + _dot(merged.astype(BF16), wo_ref[...]), g_ref[...], b_ref[...])
    x1_ref[...] = x1
    half = x1.shape[1] // 2
    xp = _pack_pair(x1[:, :half], x1[:, half:])
    for c in range(PACK_SLABS):
        xp_ref[pl.ds(c, rows, stride=PACK_SLABS), :] = xp[:, c * LANES:(c + 1) * LANES]
    x_hi = x1.astype(BF16)
    x_lo = (x1 - x_hi.astype(F32)).astype(BF16)
    by_hi = _dot_nt(wr_ref[...], x_hi)
    lt = by_hi[0:LANES] + by_hi[LANES:2 * LANES] + _dot_nt(wr_ref[0:LANES, :], x_lo) + br_ref[...]
    e1, e2, g1, g2 = _route(lt)
    expert = lax.broadcasted_iota(jnp.int32, (LANES, rows), 0).astype(F32)
    oh1 = expert == e1
    oh2 = expert == e2
    ohs = jnp.where(oh1, 1.0, jnp.where(oh2, 1.0, 0.0))
    r_i = lax.broadcasted_iota(jnp.int32, (rows, rows), 0)
    c_i = lax.broadcasted_iota(jnp.int32, (rows, rows), 1)
    before = jnp.where(r_i < c_i, 1.0, 0.0).astype(BF16)
    carry = carry_ref[...]
    seen = _dot(ohs.astype(BF16), before) + jnp.concatenate([carry] * (rows // LANES), axis=1)
    r1 = jnp.sum(jnp.where(oh1, seen, 0.0), axis=0, keepdims=True)
    r2 = jnp.sum(jnp.where(oh2, seen, 0.0), axis=0, keepdims=True)
    carry = carry + jnp.sum(ohs, axis=1, keepdims=True)
    carry_ref[...] = carry
    cnt_ref[...] = carry.astype(jnp.int32)
    as_int = lambda v: v.astype(jnp.int32)
    as_bits = lambda v: lax.bitcast_convert_type(v, jnp.int32)
    tab_ref[...] = jnp.concatenate(
        [as_int(e1), as_int(e2), as_int(r1) * PACK_SLABS, as_int(r2) * PACK_SLABS, as_bits(g1), as_bits(g2),
         jnp.zeros((2, rows), jnp.int32)], axis=0)


def _merge(x2d, ys, yl, bg, wa, wb, wo, g, b, wr, br, eg, eu, ed, alpha, tile):
    t, d = x2d.shape
    rows = DENSE_ROWS
    spt = tile // rows
    n_steps = t // rows
    per_step = -(-MOE_EXPERTS // n_steps)
    cast_steps = MOE_EXPERTS // per_step
    assert per_step * cast_steps == MOE_EXPERTS and cast_steps <= n_steps
    const = lambda shape: pl.BlockSpec(shape, lambda i: (0,) * len(shape), pipeline_mode=pl.Buffered(1))
    rowblk = lambda w: pl.BlockSpec((rows, w), lambda i: (i, 0))
    expert = lambda w: pl.BlockSpec((per_step,) + w.shape[1:], lambda i: (jnp.minimum(i, cast_steps - 1), 0, 0))
    return pl.pallas_call(
        functools.partial(_merge_kernel, alpha=alpha, steps_per_tile=spt, cast_steps=cast_steps),
        grid=(n_steps,),
        in_specs=[rowblk(d), rowblk(SSM_D_INNER), rowblk(LSTM_D_V), rowblk(2 * d),
                  const(wa.shape), const(wb.shape), const(wo.shape),
                  const(g.shape), const(b.shape), const(wr.shape), const(br.shape),
                  expert(eg), expert(eu), expert(ed)],
        out_specs=[rowblk(d),
                   pl.BlockSpec((rows * PACK_SLABS, LANES), lambda i: (i, 0)),
                   pl.BlockSpec((SUBLANES, rows), lambda i: (i // spt, i % spt)),
                   pl.BlockSpec((LANES, LANES), lambda i: (i // spt, 0)),
                   expert(eg), expert(eu), expert(ed)],
        out_shape=[jax.ShapeDtypeStruct((t, d), F32),
                   jax.ShapeDtypeStruct((t * PACK_SLABS, LANES), jnp.uint32),
                   jax.ShapeDtypeStruct((t // tile * SUBLANES, tile), jnp.int32),
                   jax.ShapeDtypeStruct((t // tile * LANES, LANES), jnp.int32),
                   jax.ShapeDtypeStruct(eg.shape, BF16), jax.ShapeDtypeStruct(eu.shape, BF16),
                   jax.ShapeDtypeStruct(ed.shape, BF16)],
        scratch_shapes=[pltpu.VMEM((LANES, LANES), F32)],
        compiler_params=pltpu.CompilerParams(dimension_semantics=("arbitrary",), vmem_limit_bytes=VMEM_LIMIT),
        name="merge_ln_route",
    )(x2d, ys, yl, bg, wa, wb, wo, g, b, wr, br, eg, eu, ed)


def _segment_rows(count):
    groups = lax.shift_right_logical(count + (SUBLANES - 1), SUBLANES.bit_length() - 1)
    return groups * (SUBLANES * PACK_SLABS)


def _route_rows_kernel(cnt_ref, tab_ref, out_ref):
    i = pl.program_id(0)
    tab = tab_ref[...]
    starts = jnp.zeros(tab.shape, jnp.int32)
    off = jnp.int32(0)
    for e in range(MOE_EXPERTS):
        starts = jnp.where(tab == e, off, starts)
        off = off + _segment_rows(cnt_ref[i * MOE_EXPERTS + e])
    row = lax.broadcasted_iota(jnp.int32, tab.shape, 0)
    res = jnp.where(lax.shift_right_logical(row, 1) == 1, tab + pltpu.roll(starts, 2, axis=0), tab)
    blocks = tab.shape[1] // LANES
    for c in range(blocks):
        out_ref[pl.ds(c, SUBLANES, stride=blocks), :] = res[:, c * LANES:(c + 1) * LANES]


def _route_rows(tab, cnt, tile):
    n_tiles = tab.shape[0] // SUBLANES
    rows = SUBLANES * tile // LANES
    return pl.pallas_call(
        _route_rows_kernel,
        grid_spec=pltpu.PrefetchScalarGridSpec(
            num_scalar_prefetch=1, grid=(n_tiles,),
            in_specs=[pl.BlockSpec((SUBLANES, tile), lambda i, c: (i, 0))],
            out_specs=pl.BlockSpec((rows, LANES), lambda i, c: (i, 0))),
        out_shape=jax.ShapeDtypeStruct((n_tiles * rows, LANES), jnp.int32),
        compiler_params=pltpu.CompilerParams(dimension_semantics=("arbitrary",), vmem_limit_bytes=VMEM_LIMIT),
        name="route_rows",
    )(cnt, tab)


def _moe_kernel(cnt_ref, tab_ref, xp_hbm, wg_hbm, wu_hbm, wd_hbm, out_hbm, xys, ybuf, off_ref,
                wg_buf, wu_buf, wd_buf, xin, xout, wsem, isem, osem, *, tile, chunk, n_steps):
    i = pl.program_id(0)
    group = SUBLANES * PACK_SLABS
    half = PACK_SLABS * LANES
    weights = ((wg_hbm, wg_buf), (wu_hbm, wu_buf), (wd_hbm, wd_buf))

    def weight_copies(expert, into):
        copies = []
        for k, (hbm, buf) in enumerate(weights):
            part = buf.shape[1] // MOE_WEIGHT_SPLIT
            for q in range(MOE_WEIGHT_SPLIT):
                copies.append(pltpu.make_async_copy(hbm.at[expert, pl.ds(q * part, part)],
                                                    buf.at[into, pl.ds(q * part, part)], wsem.at[k, into]))
        return copies

    @pl.when(i == 0)
    def _():
        for ahead in range(MOE_WEIGHT_BUFFERS - 1):
            for cp in weight_copies(ahead % MOE_EXPERTS, ahead):
                cp.start()

    n_chunks = tile // chunk
    rolled = lambda trips: trips + jnp.minimum(i, 0)

    def in_buf(c):
        return pl.multiple_of((c % 2) * (chunk * PACK_SLABS), group)

    def out_buf(c):
        return pl.multiple_of((c % 2) * (chunk * OUT_SLABS), SUBLANES)

    def in_copy(c):
        row0 = pl.multiple_of((i * tile + c * chunk) * PACK_SLABS, group)
        return pltpu.make_async_copy(xp_hbm.at[pl.ds(row0, chunk * PACK_SLABS)],
                                     xin.at[pl.ds(in_buf(c), chunk * PACK_SLABS)], isem.at[c % 2])

    def out_copy(c):
        row0 = pl.multiple_of((i * tile + c * chunk) * OUT_SLABS, SUBLANES)
        return pltpu.make_async_copy(xout.at[pl.ds(out_buf(c), chunk * OUT_SLABS)],
                                     out_hbm.at[pl.ds(row0, chunk * OUT_SLABS)], osem.at[c % 2])

    def sort_rows():
        in_copy(0).start()

        def offsets(e, acc):
            padded = _segment_rows(cnt_ref[i * MOE_EXPERTS + e])
            off_ref[e] = acc
            end = acc + padded

            @pl.when(padded > 0)
            def _():
                xys[pl.ds(pl.multiple_of(end - group, group), group), :] = jnp.zeros((group, LANES), jnp.uint32)
            return end

        total = lax.fori_loop(0, MOE_EXPERTS, offsets, 0)
        off_ref[MOE_EXPERTS] = total
        tail = MOE_BLOCKS[-1] * PACK_SLABS
        xys[pl.ds(pl.multiple_of(total, group), tail), :] = jnp.zeros((tail, LANES), jnp.uint32)

        def distribute_chunk(c, carry):
            @pl.when(c + 1 < n_chunks)
            def _():
                in_copy(c + 1).start()
            in_copy(c).wait()

            g0 = c * (chunk // SUBLANES)
            shift = in_buf(c) - c * (chunk * PACK_SLABS)

            def distribute(tg, carry2):
                for u in range(SUBLANES):
                    t = tg * SUBLANES + u
                    d1 = pl.multiple_of(tab_ref[2 * tile + t], PACK_SLABS)
                    d2 = pl.multiple_of(tab_ref[3 * tile + t], PACK_SLABS)
                    row = xin[pl.ds(pl.multiple_of(shift + t * PACK_SLABS, PACK_SLABS), PACK_SLABS), :]
                    xys[pl.ds(d1, PACK_SLABS), :] = row
                    xys[pl.ds(d2, PACK_SLABS), :] = row
                return carry2

            lax.fori_loop(g0, g0 + rolled(chunk // SUBLANES), distribute, 0)
            return carry

        lax.fori_loop(0, rolled(n_chunks), distribute_chunk, 0)

    def expert_block(refs, base, rows, valid):
        wg_ref, wu_ref, wd_ref = refs
        words = [xys[pl.ds(base + c, rows, stride=PACK_SLABS), :] for c in range(PACK_SLABS)]
        parts = [_unpack_pair(w) for w in words]
        x_hi = jnp.concatenate([p[0].astype(BF16) for p in parts], axis=1)
        x_lo = jnp.concatenate([p[1].astype(BF16) for p in parts], axis=1)
        hg = _dot(x_hi, wg_ref[0:half, :]) + _dot(x_lo, wg_ref[half:2 * half, :])
        hu = _dot(x_hi, wu_ref[0:half, :]) + _dot(x_lo, wu_ref[half:2 * half, :])
        y = _dot((_silu(hg) * hu).astype(BF16), wd_ref[...])
        for c in range(PACK_SLABS):
            ybuf[pl.ds(c, rows, stride=PACK_SLABS), :] = _pack_pair(
                y[:, c * LANES:(c + 1) * LANES], y[:, half + c * LANES:half + (c + 1) * LANES])

        def copy(g, c):
            r = pl.multiple_of(g * group, group)
            xys[pl.ds(base + r, group), :] = ybuf[pl.ds(r, group), :]
            return c

        lax.fori_loop(0, valid // group, copy, 0)

    def expert_step(j, carry):
        step = i * MOE_EXPERTS + j
        slot = step % MOE_WEIGHT_BUFFERS
        nxt = step + MOE_WEIGHT_BUFFERS - 1

        @pl.when(nxt < n_steps)
        def _():
            for cp in weight_copies(nxt % MOE_EXPERTS, nxt % MOE_WEIGHT_BUFFERS):
                cp.start()

        for cp in weight_copies(j, slot):
            cp.wait()
        refs = (wg_buf.at[slot], wu_buf.at[slot], wd_buf.at[slot])
        start = off_ref[j]
        n = off_ref[j + 1] - start
        lo = 0
        for rows in MOE_BLOCKS:
            hi = rows * PACK_SLABS

            @pl.when(jnp.logical_and(n > lo, n <= hi))
            def _(rows=rows):
                expert_block(refs, pl.multiple_of(start, group), rows, n)
            lo = hi

        @pl.when(n > lo)
        def _():
            span = MOE_BLOCKS[0] * PACK_SLABS

            def block(bi, carry2):
                expert_block(refs, pl.multiple_of(start + bi * span, group), MOE_BLOCKS[0],
                             jnp.minimum(span, n - bi * span))
                return carry2

            lax.fori_loop(0, (n + span - 1) // span, block, 0)
        return carry

    def combine_rows():
        def combine_chunk(c, carry):
            @pl.when(c >= 2)
            def _():
                out_copy(c - 2).wait()

            g0 = c * (chunk // SUBLANES)
            shift = out_buf(c) - c * (chunk * OUT_SLABS)

            def combine(tg, carry2):
                for u in range(SUBLANES):
                    t = tg * SUBLANES + u
                    g1 = lax.bitcast_convert_type(tab_ref[4 * tile + t], F32)
                    g2 = lax.bitcast_convert_type(tab_ref[5 * tile + t], F32)
                    a_hi, a_lo = _unpack_pair(
                        xys[pl.ds(pl.multiple_of(tab_ref[2 * tile + t], PACK_SLABS), PACK_SLABS), :])
                    b_hi, b_lo = _unpack_pair(
                        xys[pl.ds(pl.multiple_of(tab_ref[3 * tile + t], PACK_SLABS), PACK_SLABS), :])
                    xout[pl.ds(pl.multiple_of(shift + t * OUT_SLABS, OUT_SLABS), OUT_SLABS), :] = jnp.concatenate(
                        [g1 * a_hi + g2 * b_hi, g1 * a_lo + g2 * b_lo], axis=0)
                return carry2

            lax.fori_loop(g0, g0 + rolled(chunk // SUBLANES), combine, 0)
            out_copy(c).start()
            return carry

        lax.fori_loop(0, rolled(n_chunks), combine_chunk, 0)
        for c in range(max(0, n_chunks - 2), n_chunks):
            out_copy(c).wait()

    sort_rows()
    lax.fori_loop(0, rolled(MOE_EXPERTS), expert_step, 0)
    combine_rows()


def _moe(xp, tab, cnt, wg, wu, wd, tile):
    t = xp.shape[0] // PACK_SLABS
    d = D_MODEL
    n_tiles = t // tile
    chunk = min(MOE_CHUNK, tile)
    seg_rows = 2 * tile + MOE_EXPERTS * SUBLANES + MOE_BLOCKS[-1]
    grid_spec = pltpu.PrefetchScalarGridSpec(
        num_scalar_prefetch=1,
        grid=(n_tiles,),
        in_specs=[pl.BlockSpec((SUBLANES * tile,), lambda i, c: (i,), memory_space=pltpu.SMEM,
                               pipeline_mode=pl.Buffered(1)),
                  pl.BlockSpec(memory_space=pl.ANY), pl.BlockSpec(memory_space=pl.ANY),
                  pl.BlockSpec(memory_space=pl.ANY), pl.BlockSpec(memory_space=pl.ANY)],
        out_specs=pl.BlockSpec(memory_space=pl.ANY),
        scratch_shapes=[pltpu.VMEM((seg_rows * PACK_SLABS, LANES), jnp.uint32),
                        pltpu.VMEM((MOE_BLOCKS[-1] * PACK_SLABS, LANES), jnp.uint32),
                        pltpu.SMEM((LANES,), jnp.int32),
                        pltpu.VMEM((MOE_WEIGHT_BUFFERS, d, MOE_D_FF), BF16),
                        pltpu.VMEM((MOE_WEIGHT_BUFFERS, d, MOE_D_FF), BF16),
                        pltpu.VMEM((MOE_WEIGHT_BUFFERS, MOE_D_FF, d), BF16),
                        pltpu.VMEM((2 * chunk * PACK_SLABS, LANES), jnp.uint32),
                        pltpu.VMEM((2 * chunk * OUT_SLABS, LANES), F32),
                        pltpu.SemaphoreType.DMA((3, MOE_WEIGHT_BUFFERS)),
                        pltpu.SemaphoreType.DMA((2,)), pltpu.SemaphoreType.DMA((2,))],
    )
    return pl.pallas_call(
        functools.partial(_moe_kernel, tile=tile, chunk=chunk, n_steps=n_tiles * MOE_EXPERTS),
        grid_spec=grid_spec,
        out_shape=jax.ShapeDtypeStruct((t * OUT_SLABS, LANES), F32),
        compiler_params=pltpu.CompilerParams(dimension_semantics=("arbitrary",),
                                             vmem_limit_bytes=MOE_VMEM_LIMIT),
        name="moe_experts",
    )(cnt, tab, xp, wg, wu, wd)


def _final_kernel(x1_ref, moe_ref, p_ref, g_ref, b_ref, wpg_ref, wpp_ref, out_ref, *, alpha):
    rows = x1_ref.shape[0]
    moe = jnp.concatenate([moe_ref[pl.ds(c, rows, stride=OUT_SLABS), :] for c in range(OUT_SLABS)], axis=1)
    x2 = _layer_norm(alpha * x1_ref[...] + moe, g_ref[...], b_ref[...])
    gate = _sigmoid(_dot(x2.astype(BF16), wpg_ref[...]))
    out_ref[...] = x2 + gate * _dot(p_ref[...].astype(BF16), wpp_ref[...])


def _final(x1, moe, p2d, g, b, wpg, wpp, alpha):
    t, d = x1.shape
    rows = min(FINAL_ROWS, t)
    const = lambda shape: pl.BlockSpec(shape, lambda i: (0,) * len(shape))
    rowblk = lambda w: pl.BlockSpec((rows, w), lambda i: (i, 0))
    return pl.pallas_call(
        functools.partial(_final_kernel, alpha=alpha),
        grid=(t // rows,),
        in_specs=[rowblk(d), pl.BlockSpec((rows * OUT_SLABS, LANES), lambda i: (i, 0)), rowblk(p2d.shape[1]),
                  const(g.shape), const(b.shape), const(wpg.shape), const(wpp.shape)],
        out_specs=rowblk(d),
        out_shape=jax.ShapeDtypeStruct((t, d), F32),
        compiler_params=pltpu.CompilerParams(dimension_semantics=("arbitrary",), vmem_limit_bytes=VMEM_LIMIT),
        name="final_ln_ple",
    )(x1, moe, p2d, g, b, wpg, wpp)


def _proj_weights_kernel(w_ref, ssd_ref, lstm_ref):
    w = w_ref[...]
    cols = [0]
    for sz in IN_PROJ_SIZES:
        cols.append(cols[-1] + sz)
    lane = lax.broadcasted_iota(jnp.int32, (w.shape[0], LANES), 1)
    gate_block = lambda c: jnp.where(lane < LSTM_HEADS, w[:, c:c + LANES], 0.0)
    ssd_ref[...] = w[:, 0:cols[2] + LANES].astype(BF16)
    lstm_ref[...] = jnp.concatenate(
        [w[:, cols[3]:cols[4]] * (LSTM_QK ** -0.5), w[:, cols[4]:cols[7]], gate_block(cols[7]), gate_block(cols[8]),
         w[:, cols[9]:cols[11]]],
        axis=1).astype(BF16)


def _proj_weights(w_all, layer):
    _, d, n = w_all.shape
    rows = LANES
    widths = (IN_PROJ_SIZES[0] + IN_PROJ_SIZES[1] + LANES, 2 * LSTM_D_QK + 2 * LSTM_D_V + 2 * LANES + 2 * D_MODEL)
    return pl.pallas_call(
        _proj_weights_kernel,
        grid=(d // rows,),
        in_specs=[pl.BlockSpec((None, rows, n), lambda i: (layer, i, 0))],
        out_specs=[pl.BlockSpec((rows, wd), lambda i: (i, 0)) for wd in widths],
        out_shape=[jax.ShapeDtypeStruct((d, wd), BF16) for wd in widths],
        compiler_params=pltpu.CompilerParams(dimension_semantics=("arbitrary",), vmem_limit_bytes=VMEM_LIMIT),
        name="proj_weights",
    )(w_all.astype(F32))


def _pad_lanes(w, width=LANES):
    return jnp.pad(w, ((0, 0), (0, width - w.shape[1])))


def _row(v, width=None):
    v = v.astype(F32).reshape(1, -1)
    return v if width is None else _pad_lanes(v, width)


def kernel(x, p, w_in, ssm_conv_w, ssm_conv_b, ssm_dt_bias, ssm_a_log, ssm_d, ssm_norm_w, lstm_i_bias, lstm_f_bias, lstm_norm_w, w_branch_ssm, w_branch_lstm, w_out, ln1_g, ln1_b, moe_w_group, moe_b_group, moe_w_expert, moe_b_expert, moe_w_gate, moe_w_up, moe_w_down, ln2_g, ln2_b, ple_w_proj, ple_w_gate):
    depth = w_in.shape[0]
    bsz, seq, d = x.shape
    t = bsz * seq
    alpha = (2.0 * depth) ** 0.25
    head_expand = (jnp.arange(LANES, dtype=jnp.int32)[:, None]
                   == jnp.arange(SSM_D_INNER, dtype=jnp.int32)[None, :] // SSM_HEAD_DIM).astype(BF16)
    for i in range(depth):
        bf = lambda w: w.astype(BF16)
        w_ssd, w_lstm = _proj_weights(w_in, i)
        y_ssm = _ssd_mixer(
            x, w_ssd,
            ssm_conv_w[i].astype(F32), _row(ssm_conv_b[i]), _row(ssm_dt_bias[i], LANES),
            _row(-jnp.exp(ssm_a_log[i].astype(F32)), LANES),
            _row(jnp.repeat(ssm_d[i].astype(F32), SSM_HEAD_DIM)), _row(ssm_norm_w[i]), head_expand)
        b_if = jnp.concatenate([_row(lstm_i_bias[i], LANES), _row(lstm_f_bias[i], LANES)], axis=1)
        y_lstm, branch_gates = _mlstm_mixer(x, w_lstm, b_if, _row(lstm_norm_w[i]))
        w_r = _pad_lanes(jnp.concatenate([moe_w_group[i], moe_w_expert[i]], axis=1).astype(F32)).T
        w_r_hi = w_r.astype(BF16)
        w_r_cat = jnp.concatenate([w_r_hi, (w_r - w_r_hi.astype(F32)).astype(BF16)], axis=0)
        b_r = jnp.broadcast_to(_row(jnp.concatenate([moe_b_group[i], moe_b_expert[i]]), LANES).T, (LANES, DENSE_ROWS))
        tile = min(MOE_TILE, t)
        x1, xp, tab, cnt, e_gate, e_up, e_down = _merge(
            x.reshape(t, d), y_ssm.reshape(t, -1), y_lstm.reshape(t, -1),
            branch_gates.reshape(t, -1), bf(w_branch_ssm[i]), bf(w_branch_lstm[i]), bf(w_out[i]),
            _row(ln1_g[i]), _row(ln1_b[i]), w_r_cat, b_r,
            moe_w_gate[i].astype(F32), moe_w_up[i].astype(F32), moe_w_down[i].astype(F32), alpha, tile)
        cnt = cnt.reshape(t // tile, LANES, LANES)[:, :MOE_EXPERTS, 0].reshape(-1)
        moe = _moe(xp, _route_rows(tab, cnt, tile).reshape(-1), cnt, e_gate, e_up, e_down, tile)
        x = _final(x1, moe, p[i].reshape(t, -1), _row(ln2_g[i]), _row(ln2_b[i]),
                   bf(ple_w_gate[i]), bf(ple_w_proj[i]), alpha).reshape(bsz, seq, d)
    return x
```

```python
import functools

import jax
import jax.numpy as jnp
from jax import lax
from jax.experimental import pallas as pl
from jax.experimental.pallas import tpu as pltpu

F32 = jnp.float32
BF16 = jnp.bfloat16

D_MODEL = 1024
PLE_DIM = 256
SSM_D_INNER = 1024
SSM_HEAD_DIM = 64
SSM_HEADS = 16
SSM_GROUPS = 4
SSM_STATE = 128
SSM_CONV = 4
SSM_XBC = SSM_D_INNER + 2 * SSM_GROUPS * SSM_STATE
LSTM_HEADS = 8
LSTM_QK = 64
LSTM_V = 128
LSTM_D_QK = LSTM_HEADS * LSTM_QK
LSTM_D_V = LSTM_HEADS * LSTM_V
CHUNK = 128
MOE_GROUPS = 8
MOE_PER_GROUP = 8
MOE_EXPERTS = 64
MOE_D_FF = 512
NORM_EPS = 1e-5
IN_PROJ_SIZES = (SSM_D_INNER, SSM_XBC, SSM_HEADS, LSTM_D_QK, LSTM_D_QK, LSTM_D_V, LSTM_D_V,
                 LSTM_HEADS, LSTM_HEADS, D_MODEL, D_MODEL)

LANES = 128
SUBLANES = 8
VMEM_LIMIT = 56 * 1024 * 1024
MOE_VMEM_LIMIT = 60 * 1024 * 1024

MIX_ROWS = 256
MIX_SEQS = 2
DENSE_ROWS = 512
FINAL_ROWS = 1024
MOE_TILE = 8192
MOE_CHUNK = 256
MOE_BLOCKS = (192, 256, 320, 384, 448)
MOE_WEIGHT_BUFFERS = 4
MOE_WEIGHT_SPLIT = 4
PACK_SLABS = D_MODEL // 2 // LANES
OUT_SLABS = D_MODEL // LANES


def _dot(a, b):
    return jnp.dot(a, b, preferred_element_type=F32)


def _dot_nt(a, b):
    return lax.dot_general(a, b, (((1,), (1,)), ((), ())), preferred_element_type=F32)


def _dot_tn(a, b):
    return lax.dot_general(a, b, (((0,), (0,)), ((), ())), preferred_element_type=F32)


def _sigmoid(x):
    return 0.5 + 0.5 * jnp.tanh(0.5 * x)


def _silu(x):
    h = 0.5 * x
    return h + h * jnp.tanh(h)


def _softplus(x):
    return jnp.maximum(x, 0.0) + jnp.log(1.0 + jnp.exp(-jnp.abs(x)))


def _split3(x):
    hi = x.astype(BF16)
    r1 = x - hi.astype(F32)
    mid = r1.astype(BF16)
    lo = (r1 - mid.astype(F32)).astype(BF16)
    return hi, mid, lo


def _cumsum_rows(tril, x):
    hi, mid, lo = _split3(x)
    return _dot(tril, hi) + _dot(tril, mid) + _dot(tril, lo)


def _causal_masks():
    r = lax.broadcasted_iota(jnp.int32, (CHUNK, CHUNK), 0)
    c = lax.broadcasted_iota(jnp.int32, (CHUNK, CHUNK), 1)
    causal = r >= c
    return causal, jnp.where(causal, 1.0, 0.0).astype(BF16)


def _pair_cols(v, ha, hb, rows, lane):
    a = jnp.broadcast_to(v[:, ha:ha + 1], (rows, LANES))
    b = jnp.broadcast_to(v[:, hb:hb + 1], (rows, LANES))
    return jnp.where(lane < SSM_HEAD_DIM, a, b)


def _ssd_kernel(x_ref, w_ref, cw_ref, cb_ref, dtb_ref, aneg_ref, dexp_ref, nw_ref, ex_ref,
                y_ref, ext_ref, u_ref, st_ref, *, rows, nseq):
    n_slab = SSM_XBC // LANES
    tail = SUBLANES - SSM_CONV + 1

    @pl.when(pl.program_id(1) == 0)
    def _():
        ext_ref[:, :, 0:SUBLANES, :] = jnp.zeros((nseq, n_slab, SUBLANES, LANES), F32)
        st_ref[...] = jnp.zeros(st_ref.shape, F32)

    xb = x_ref[...].reshape(nseq * rows, x_ref.shape[-1]).astype(BF16)
    z = _dot(xb, w_ref[:, 0:SSM_D_INNER])
    dt_col = SSM_D_INNER + SSM_XBC
    dtr = _dot(xb, w_ref[:, dt_col:dt_col + LANES])
    dtr = jnp.where(lax.broadcasted_iota(jnp.int32, dtr.shape, 1) < SSM_HEADS, dtr, 0.0)
    for c2 in range(n_slab // 2):
        xbc = _dot(xb, w_ref[:, SSM_D_INNER + c2 * 2 * LANES:SSM_D_INNER + (c2 + 1) * 2 * LANES])
        for s in range(nseq):
            for cc in range(2):
                ext_ref[s, 2 * c2 + cc, SUBLANES:SUBLANES + rows, :] = (
                    xbc[s * rows:(s + 1) * rows, cc * LANES:(cc + 1) * LANES])
    for s in range(nseq):
        for c in range(n_slab):
            conv = cb_ref[:, c * LANES:(c + 1) * LANES]
            for k in range(SSM_CONV):
                conv = conv + cw_ref[k:k + 1, c * LANES:(c + 1) * LANES] * ext_ref[s, c, pl.ds(tail + k, rows), :]
            u_ref[s, c] = _silu(conv)
            ext_ref[s, c, 0:SUBLANES, :] = ext_ref[s, c, rows:rows + SUBLANES, :]

    dt = _softplus(dtr + dtb_ref[...])
    da = dt * aneg_ref[...]
    causal, tril = _causal_masks()
    lane = lax.broadcasted_iota(jnp.int32, (CHUNK, LANES), 1)
    lane1 = lax.broadcasted_iota(jnp.int32, (1, LANES), 1)
    gw = SSM_D_INNER // SSM_GROUPS
    b_slab = SSM_D_INNER // LANES
    c_slab = b_slab + SSM_GROUPS

    for ci in range(rows // CHUNK):
        for s in range(nseq):
            r0 = ci * CHUNK
            q0 = s * rows + r0
            dt_c = dt[q0:q0 + CHUNK]
            a_cs = _cumsum_rows(tril, da[q0:q0 + CHUNK])
            key_t = (a_cs - jnp.log(dt_c)).T
            a_last = a_cs[CHUNK - 1:CHUNK, :]
            ea_last = jnp.exp(a_last)
            ea_x = _dot(jnp.exp(a_cs).astype(BF16), ex_ref[...])
            sw_x = _dot((dt_c * jnp.exp(a_last - a_cs)).astype(BF16), ex_ref[...])
            for g in range(SSM_GROUPS):
                bm_g = u_ref[s, b_slab + g, r0:r0 + CHUNK, :].astype(BF16)
                cm_g = u_ref[s, c_slab + g, r0:r0 + CHUNK, :].astype(BF16)
                cb = _dot_nt(cm_g, bm_g)
                st_g = st_ref[s, :, g * gw:(g + 1) * gw]
                y_off = _dot(cm_g, st_g.astype(BF16))
                xw, eal_x = [], []
                for pr in range(2):
                    lo = g * gw + pr * LANES
                    xw.append((u_ref[s, 2 * g + pr, r0:r0 + CHUNK, :] * sw_x[:, lo:lo + LANES]).astype(BF16))
                    eal_x.append(_pair_cols(ea_last, 4 * g + 2 * pr, 4 * g + 2 * pr + 1, 1, lane1))
                st_ref[s, :, g * gw:(g + 1) * gw] = (st_g * jnp.concatenate(eal_x, axis=1)
                                                     + _dot_tn(bm_g, jnp.concatenate(xw, axis=1)))
                yy = []
                for pr in range(2):
                    ha = 4 * g + 2 * pr
                    hb = ha + 1
                    lo = g * gw + pr * LANES
                    xs_p = u_ref[s, 2 * g + pr, r0:r0 + CHUNK, :]
                    xs_pb = xs_p.astype(BF16)
                    rhs = jnp.concatenate([jnp.where(lane < SSM_HEAD_DIM, xs_pb, 0).astype(BF16),
                                           jnp.where(lane >= SSM_HEAD_DIM, xs_pb, 0).astype(BF16)], axis=0)
                    gs = []
                    for h in (ha, hb):
                        colb = jnp.broadcast_to(a_cs[:, h:h + 1], (CHUNK, CHUNK))
                        rowb = jnp.broadcast_to(key_t[h:h + 1, :], (CHUNK, CHUNK))
                        gs.append((cb * jnp.exp(jnp.where(causal, colb - rowb, -jnp.inf))).astype(BF16))
                    y_p = (_dot(jnp.concatenate(gs, axis=1), rhs)
                           + y_off[:, pr * LANES:(pr + 1) * LANES] * ea_x[:, lo:lo + LANES]
                           + dexp_ref[:, lo:lo + LANES] * xs_p)
                    zz = z[q0:q0 + CHUNK, lo:lo + LANES]
                    yy.append(y_p * _silu(zz))
                ms = sum(jnp.sum(t * t, axis=-1, keepdims=True) for t in yy) * (1.0 / gw)
                inv = lax.rsqrt(ms + NORM_EPS)
                for pr in range(2):
                    lo = g * gw + pr * LANES
                    y_ref[s, r0:r0 + CHUNK, lo:lo + LANES] = (yy[pr] * inv * nw_ref[:, lo:lo + LANES]).astype(BF16)


def _ssd_mixer(x, w, cw, cb, dtb, aneg, dexp, nw, ex):
    b, s, d = x.shape
    rows = MIX_ROWS
    nseq = MIX_SEQS
    const = lambda shape: pl.BlockSpec(shape, lambda i, j: (0,) * len(shape), pipeline_mode=pl.Buffered(1))
    return pl.pallas_call(
        functools.partial(_ssd_kernel, rows=rows, nseq=nseq),
        grid=(b // nseq, s // rows),
        in_specs=[pl.BlockSpec((nseq, rows, d), lambda i, j: (i, j, 0)),
                  const(w.shape), const(cw.shape), const(cb.shape),
                  const(dtb.shape), const(aneg.shape), const(dexp.shape), const(nw.shape), const(ex.shape)],
        out_specs=pl.BlockSpec((nseq, rows, SSM_D_INNER), lambda i, j: (i, j, 0)),
        out_shape=jax.ShapeDtypeStruct((b, s, SSM_D_INNER), BF16),
        scratch_shapes=[pltpu.VMEM((nseq, SSM_XBC // LANES, SUBLANES + rows, LANES), F32),
                        pltpu.VMEM((nseq, SSM_XBC // LANES, rows, LANES), F32),
                        pltpu.VMEM((nseq, SSM_STATE, SSM_D_INNER), F32)],
        compiler_params=pltpu.CompilerParams(dimension_semantics=("arbitrary", "arbitrary"),
                                             vmem_limit_bytes=VMEM_LIMIT),
        name="ssd_mixer",
    )(x, w, cw, cb, dtb, aneg, dexp, nw, ex)


def _mlstm_kernel(x_ref, w_ref, bif_ref, nw_ref, eg_ref, eu_ref, ed_ref,
                  y_ref, egb_ref, eub_ref, edb_ref, c_ref, m_ref, *, rows, nseq, cast_steps):
    @pl.when(pl.program_id(1) == 0)
    def _():
        c_ref[...] = jnp.zeros(c_ref.shape, F32)
        m_ref[...] = jnp.full(m_ref.shape, -jnp.inf, F32)

    @pl.when(pl.program_id(0) * pl.num_programs(1) + pl.program_id(1) < cast_steps)
    def _():
        egb_ref[...] = eg_ref[...].astype(BF16)
        eub_ref[...] = eu_ref[...].astype(BF16)
        edb_ref[...] = ed_ref[...].astype(BF16)

    xb = x_ref[...].reshape(nseq * rows, x_ref.shape[-1]).astype(BF16)
    c_k, c_v, c_o, c_g = LSTM_D_QK, 2 * LSTM_D_QK, 2 * LSTM_D_QK + LSTM_D_V, 2 * LSTM_D_QK + 2 * LSTM_D_V
    q = _dot(xb, w_ref[:, 0:c_k])
    k = _dot(xb, w_ref[:, c_k:c_v])
    v = _dot(xb, w_ref[:, c_v:c_o])
    o = _dot(xb, w_ref[:, c_o:c_g])
    gif = _dot(xb, w_ref[:, c_g:c_g + 2 * LANES]) + bif_ref[...]
    causal, tril = _causal_masks()
    lane = lax.broadcasted_iota(jnp.int32, (CHUNK, LANES), 1)
    lane1 = lax.broadcasted_iota(jnp.int32, (1, LANES), 1)
    row = lax.broadcasted_iota(jnp.int32, (LANES, 1), 0)
    head_lane = lane < LSTM_HEADS
    ones = jnp.ones((CHUNK, LANES), BF16)

    for ci in range(rows // CHUNK):
        r0 = ci * CHUNK
        pre = {}
        for s in range(nseq):
            q0 = s * rows + r0
            log_i = gif[q0:q0 + CHUNK, 0:LANES]
            f_pre = gif[q0:q0 + CHUNK, LANES:2 * LANES]
            log_f = jnp.where(head_lane, jnp.minimum(f_pre, 0.0) - jnp.log(1.0 + jnp.exp(-jnp.abs(f_pre))), 0.0)
            fcum = _cumsum_rows(tril, log_f)
            f_tot = fcum[CHUNK - 1:CHUNK, :]
            gk = log_i - fcum
            m_loc = jnp.max(f_tot + gk, axis=0, keepdims=True)
            g_t = gk.T
            m_prev = m_ref[s]
            m_new = jnp.maximum(f_tot + m_prev, m_loc)
            s_prev = jnp.exp(f_tot + m_prev - m_new)
            s_loc = jnp.exp(m_loc - m_new)
            m_ref[s] = jnp.where(lane1 < LSTM_HEADS, m_new, -jnp.inf)
            w_end = jnp.exp(gk + (f_tot - m_loc))
            pre[s] = (q0, fcum, gk, g_t, m_prev, s_prev, s_loc, w_end)

        for pr in range(LSTM_HEADS // 2):
            st = {}
            for s in range(nseq):
                q0, fcum, gk, g_t, m_prev, s_prev, s_loc, w_end = pre[s]
                q_p = q[q0:q0 + CHUNK, pr * LANES:(pr + 1) * LANES]
                k_p = k[q0:q0 + CHUNK, pr * LANES:(pr + 1) * LANES]
                k_pb = k_p.astype(BF16)
                c_pair = c_ref[s, pr]
                c_pair_b = c_pair.astype(BF16)
                c_new = c_pair * jnp.where(row < LSTM_QK,
                                           jnp.broadcast_to(s_prev[:, 2 * pr:2 * pr + 1], (LANES, 1)),
                                           jnp.broadcast_to(s_prev[:, 2 * pr + 1:2 * pr + 2], (LANES, 1)))
                st[s] = [q_p, k_p, k_pb, c_pair_b, c_new]
            for hh in range(2):
                for s in range(nseq):
                    q0, fcum, gk, g_t, m_prev, s_prev, s_loc, w_end = pre[s]
                    q_p, k_p, k_pb, c_pair_b, c_new = st[s]
                    h = 2 * pr + hh
                    in_head = (lane < LSTM_QK) if hh == 0 else (lane >= LSTM_QK)
                    v_h = v[q0:q0 + CHUNK, h * LSTM_V:(h + 1) * LSTM_V].astype(BF16)
                    v_ext = jnp.concatenate([v_h, ones], axis=1)
                    q_m = jnp.where(in_head, q_p, 0.0)
                    fcol = jnp.broadcast_to(fcum[:, h:h + 1], (CHUNK, CHUNK))
                    rowb = jnp.broadcast_to(g_t[h:h + 1, :], (CHUNK, CHUNK))
                    log_d = jnp.where(causal, fcol + rowb, -jnp.inf)
                    inter_log = fcol + m_prev[:, h:h + 1]
                    m_t = jnp.maximum(inter_log, jnp.max(log_d, axis=-1, keepdims=True))
                    scores = _dot_nt(q_m.astype(BF16), k_pb) * jnp.exp(log_d - m_t)
                    inter_w = jnp.exp(inter_log - m_t)
                    lhs = jnp.concatenate([scores.astype(BF16), (q_m * inter_w).astype(BF16)], axis=1)
                    res = _dot(lhs, jnp.concatenate([v_ext, c_pair_b], axis=0))
                    hv = res[:, 0:LSTM_V] / jnp.maximum(jnp.abs(res[:, LSTM_V:2 * LSTM_V]), jnp.exp(-m_t))
                    ms = jnp.mean(hv * hv, axis=-1, keepdims=True)
                    o_h = o[q0:q0 + CHUNK, h * LSTM_V:(h + 1) * LSTM_V]
                    y_ref[s, r0:r0 + CHUNK, h * LSTM_V:(h + 1) * LSTM_V] = (
                        _sigmoid(o_h) * (hv * lax.rsqrt(ms + NORM_EPS) * nw_ref[:, h * LSTM_V:(h + 1) * LSTM_V])
                    ).astype(BF16)
                    kw = jnp.where(in_head, k_p * jnp.broadcast_to(w_end[:, h:h + 1], (CHUNK, LANES)), 0.0)
                    st[s][4] = c_new + s_loc[:, h:h + 1] * _dot_tn(kw.astype(BF16), v_ext)
            for s in range(nseq):
                c_ref[s, pr] = st[s][4]


def _mlstm_mixer(x, w, bif, nw, eg, eu, ed):
    b, s, d = x.shape
    rows = MIX_ROWS
    nseq = MIX_SEQS
    n_j = s // rows
    n_steps = (b // nseq) * n_j
    per_step = -(-MOE_EXPERTS // n_steps)
    cast_steps = MOE_EXPERTS // per_step
    assert per_step * cast_steps == MOE_EXPERTS and cast_steps <= n_steps
    const = lambda shape: pl.BlockSpec(shape, lambda i, j: (0,) * len(shape), pipeline_mode=pl.Buffered(1))
    expert = lambda a: pl.BlockSpec((per_step,) + a.shape[1:],
                                    lambda i, j: (jnp.minimum(i * n_j + j, cast_steps - 1), 0, 0))
    return pl.pallas_call(
        functools.partial(_mlstm_kernel, rows=rows, nseq=nseq, cast_steps=cast_steps),
        grid=(b // nseq, n_j),
        in_specs=[pl.BlockSpec((nseq, rows, d), lambda i, j: (i, j, 0)),
                  const(w.shape), const(bif.shape), const(nw.shape), expert(eg), expert(eu), expert(ed)],
        out_specs=[pl.BlockSpec((nseq, rows, LSTM_D_V), lambda i, j: (i, j, 0)),
                   expert(eg), expert(eu), expert(ed)],
        out_shape=[jax.ShapeDtypeStruct((b, s, LSTM_D_V), BF16), jax.ShapeDtypeStruct(eg.shape, BF16),
                   jax.ShapeDtypeStruct(eu.shape, BF16), jax.ShapeDtypeStruct(ed.shape, BF16)],
        scratch_shapes=[pltpu.VMEM((nseq, LSTM_HEADS // 2, 2 * LSTM_QK, 2 * LSTM_V), F32),
                        pltpu.VMEM((nseq, 1, LANES), F32)],
        compiler_params=pltpu.CompilerParams(dimension_semantics=("arbitrary", "arbitrary"),
                                             vmem_limit_bytes=VMEM_LIMIT),
        name="mlstm_mixer",
    )(x, w, bif, nw, eg, eu, ed)


def _layer_norm(t, g, b):
    mu = jnp.mean(t, axis=-1, keepdims=True)
    tc = t - mu
    var = jnp.mean(tc * tc, axis=-1, keepdims=True)
    return tc * lax.rsqrt(var + NORM_EPS) * g + b


def _pack_pair(a, b):
    pa = lax.bitcast_convert_type(a.astype(BF16).astype(F32), jnp.uint32)
    pb = lax.bitcast_convert_type(b.astype(BF16).astype(F32), jnp.uint32)
    return pa | (pb >> 16)


def _unpack_pair(w):
    return (lax.bitcast_convert_type(w & jnp.uint32(0xFFFF0000), F32),
            lax.bitcast_convert_type(w << 16, F32))


def _route(lt):
    n = lt.shape[1]
    sub = lax.broadcasted_iota(jnp.int32, (MOE_PER_GROUP, n), 0).astype(F32)
    big = float(LANES)
    neg = -jnp.inf
    gl = lt[0:MOE_GROUPS, :]
    gmax = jnp.max(gl, axis=0, keepdims=True)
    gidx = jnp.min(jnp.where(gl == gmax, sub, big), axis=0, keepdims=True)
    grp_p = 1.0 / jnp.sum(jnp.exp(gl - gmax), axis=0, keepdims=True)
    el = lt[MOE_GROUPS:MOE_GROUPS + MOE_PER_GROUP, :]
    for g in range(1, MOE_GROUPS):
        lo = MOE_GROUPS + g * MOE_PER_GROUP
        el = jnp.where(gidx == g, lt[lo:lo + MOE_PER_GROUP, :], el)
    m1 = jnp.max(el, axis=0, keepdims=True)
    i1 = jnp.min(jnp.where(el == m1, sub, big), axis=0, keepdims=True)
    el2 = jnp.where(sub == i1, neg, el)
    m2 = jnp.max(el2, axis=0, keepdims=True)
    i2 = jnp.min(jnp.where(el2 == m2, sub, big), axis=0, keepdims=True)
    e21 = jnp.exp(m2 - m1)
    g1 = grp_p / (1.0 + e21)
    g2 = grp_p * e21 / (1.0 + e21)
    return gidx * MOE_PER_GROUP + i1, gidx * MOE_PER_GROUP + i2, g1, g2


def _merge_kernel(x_ref, ys_ref, yl_ref, wg_ref, wa_ref, wb_ref, wo_ref, g_ref, b_ref,
                  wr_ref, br_ref, x1_ref, xp_ref, tab_ref, cnt_ref, carry_ref,
                  *, alpha, steps_per_tile):
    @pl.when(pl.program_id(0) % steps_per_tile == 0)
    def _():
        carry_ref[...] = jnp.zeros(carry_ref.shape, F32)

    x = x_ref[...]
    rows = x.shape[0]
    xb = x.astype(BF16)
    d = x.shape[1]
    merged = (_sigmoid(_dot(xb, wg_ref[:, 0:d])) * _dot(ys_ref[...], wa_ref[...])
              + _sigmoid(_dot(xb, wg_ref[:, d:2 * d])) * _dot(yl_ref[...], wb_ref[...]))
    x1 = _layer_norm(alpha * x + _dot(merged.astype(BF16), wo_ref[...]), g_ref[...], b_ref[...])
    x1_ref[...] = x1
    half = x1.shape[1] // 2
    xp = _pack_pair(x1[:, :half], x1[:, half:])
    for c in range(PACK_SLABS):
        xp_ref[pl.ds(c, rows, stride=PACK_SLABS), :] = xp[:, c * LANES:(c + 1) * LANES]
    x_hi = x1.astype(BF16)
    x_lo = (x1 - x_hi.astype(F32)).astype(BF16)
    by_hi = _dot_nt(wr_ref[...], x_hi)
    lt = by_hi[0:LANES] + by_hi[LANES:2 * LANES] + _dot_nt(wr_ref[0:LANES, :], x_lo) + br_ref[...]
    e1, e2, g1, g2 = _route(lt)
    expert = lax.broadcasted_iota(jnp.int32, (LANES, rows), 0).astype(F32)
    oh1 = expert == e1
    oh2 = expert == e2
    ohs = jnp.where(oh1, 1.0, jnp.where(oh2, 1.0, 0.0))
    r_i = lax.broadcasted_iota(jnp.int32, (rows, rows), 0)
    c_i = lax.broadcasted_iota(jnp.int32, (rows, rows), 1)
    before = jnp.where(r_i < c_i, 1.0, 0.0).astype(BF16)
    carry = carry_ref[...]
    seen = _dot(ohs.astype(BF16), before) + jnp.concatenate([carry] * (rows // LANES), axis=1)
    r1 = jnp.sum(jnp.where(oh1, seen, 0.0), axis=0, keepdims=True)
    r2 = jnp.sum(jnp.where(oh2, seen, 0.0), axis=0, keepdims=True)
    carry = carry + jnp.sum(ohs, axis=1, keepdims=True)
    carry_ref[...] = carry
    cnt_ref[...] = carry.astype(jnp.int32)
    as_int = lambda v: v.astype(jnp.int32)
    as_bits = lambda v: lax.bitcast_convert_type(v, jnp.int32)
    tab_ref[...] = jnp.concatenate(
        [as_int(e1), as_int(e2), as_int(r1) * PACK_SLABS, as_int(r2) * PACK_SLABS, as_bits(g1), as_bits(g2),
         jnp.zeros((2, rows), jnp.int32)], axis=0)


def _merge(x2d, ys, yl, wg, wa, wb, wo, g, b, wr, br, alpha, tile):
    t, d = x2d.shape
    rows = DENSE_ROWS
    spt = tile // rows
    const = lambda shape: pl.BlockSpec(shape, lambda i: (0,) * len(shape), pipeline_mode=pl.Buffered(1))
    rowblk = lambda w: pl.BlockSpec((rows, w), lambda i: (i, 0))
    return pl.pallas_call(
        functools.partial(_merge_kernel, alpha=alpha, steps_per_tile=spt),
        grid=(t // rows,),
        in_specs=[rowblk(d), rowblk(SSM_D_INNER), rowblk(LSTM_D_V),
                  const(wg.shape), const(wa.shape), const(wb.shape), const(wo.shape),
                  const(g.shape), const(b.shape), const(wr.shape), const(br.shape)],
        out_specs=[rowblk(d),
                   pl.BlockSpec((rows * PACK_SLABS, LANES), lambda i: (i, 0)),
                   pl.BlockSpec((SUBLANES, rows), lambda i: (i // spt, i % spt)),
                   pl.BlockSpec((LANES, LANES), lambda i: (i // spt, 0))],
        out_shape=[jax.ShapeDtypeStruct((t, d), F32),
                   jax.ShapeDtypeStruct((t * PACK_SLABS, LANES), jnp.uint32),
                   jax.ShapeDtypeStruct((t // tile * SUBLANES, tile), jnp.int32),
                   jax.ShapeDtypeStruct((t // tile * LANES, LANES), jnp.int32)],
        scratch_shapes=[pltpu.VMEM((LANES, LANES), F32)],
        compiler_params=pltpu.CompilerParams(dimension_semantics=("arbitrary",), vmem_limit_bytes=VMEM_LIMIT),
        name="merge_ln_route",
    )(x2d, ys, yl, wg, wa, wb, wo, g, b, wr, br)


def _segment_rows(count):
    groups = lax.shift_right_logical(count + (SUBLANES - 1), SUBLANES.bit_length() - 1)
    return groups * (SUBLANES * PACK_SLABS)


def _route_rows_kernel(cnt_ref, tab_ref, out_ref):
    i = pl.program_id(0)
    tab = tab_ref[...]
    starts = jnp.zeros(tab.shape, jnp.int32)
    off = jnp.int32(0)
    for e in range(MOE_EXPERTS):
        starts = jnp.where(tab == e, off, starts)
        off = off + _segment_rows(cnt_ref[i * MOE_EXPERTS + e])
    row = lax.broadcasted_iota(jnp.int32, tab.shape, 0)
    res = jnp.where(lax.shift_right_logical(row, 1) == 1, tab + pltpu.roll(starts, 2, axis=0), tab)
    blocks = tab.shape[1] // LANES
    for c in range(blocks):
        out_ref[pl.ds(c, SUBLANES, stride=blocks), :] = res[:, c * LANES:(c + 1) * LANES]


def _route_rows(tab, cnt, tile):
    n_tiles = tab.shape[0] // SUBLANES
    rows = SUBLANES * tile // LANES
    return pl.pallas_call(
        _route_rows_kernel,
        grid_spec=pltpu.PrefetchScalarGridSpec(
            num_scalar_prefetch=1, grid=(n_tiles,),
            in_specs=[pl.BlockSpec((SUBLANES, tile), lambda i, c: (i, 0))],
            out_specs=pl.BlockSpec((rows, LANES), lambda i, c: (i, 0))),
        out_shape=jax.ShapeDtypeStruct((n_tiles * rows, LANES), jnp.int32),
        compiler_params=pltpu.CompilerParams(dimension_semantics=("arbitrary",), vmem_limit_bytes=VMEM_LIMIT),
        name="route_rows",
    )(cnt, tab)


def _moe_kernel(cnt_ref, tab_ref, xp_hbm, wg_hbm, wu_hbm, wd_hbm, out_hbm, xys, ybuf, off_ref,
                wg_buf, wu_buf, wd_buf, xin, xout, wsem, isem, osem, *, tile, chunk, n_steps):
    i = pl.program_id(0)
    group = SUBLANES * PACK_SLABS
    half = PACK_SLABS * LANES
    weights = ((wg_hbm, wg_buf), (wu_hbm, wu_buf), (wd_hbm, wd_buf))

    def weight_copies(expert, into):
        copies = []
        for k, (hbm, buf) in enumerate(weights):
            part = buf.shape[1] // MOE_WEIGHT_SPLIT
            for q in range(MOE_WEIGHT_SPLIT):
                copies.append(pltpu.make_async_copy(hbm.at[expert, pl.ds(q * part, part)],
                                                    buf.at[into, pl.ds(q * part, part)], wsem.at[k, into]))
        return copies

    @pl.when(i == 0)
    def _():
        for ahead in range(MOE_WEIGHT_BUFFERS - 1):
            for cp in weight_copies(ahead % MOE_EXPERTS, ahead):
                cp.start()

    n_chunks = tile // chunk
    rolled = lambda trips: trips + jnp.minimum(i, 0)

    def in_buf(c):
        return pl.multiple_of((c % 2) * (chunk * PACK_SLABS), group)

    def out_buf(c):
        return pl.multiple_of((c % 2) * (chunk * OUT_SLABS), SUBLANES)

    def in_copy(c):
        row0 = pl.multiple_of((i * tile + c * chunk) * PACK_SLABS, group)
        return pltpu.make_async_copy(xp_hbm.at[pl.ds(row0, chunk * PACK_SLABS)],
                                     xin.at[pl.ds(in_buf(c), chunk * PACK_SLABS)], isem.at[c % 2])

    def out_copy(c):
        row0 = pl.multiple_of((i * tile + c * chunk) * OUT_SLABS, SUBLANES)
        return pltpu.make_async_copy(xout.at[pl.ds(out_buf(c), chunk * OUT_SLABS)],
                                     out_hbm.at[pl.ds(row0, chunk * OUT_SLABS)], osem.at[c % 2])

    def sort_rows():
        in_copy(0).start()

        def offsets(e, acc):
            padded = _segment_rows(cnt_ref[i * MOE_EXPERTS + e])
            off_ref[e] = acc
            end = acc + padded

            @pl.when(padded > 0)
            def _():
                xys[pl.ds(pl.multiple_of(end - group, group), group), :] = jnp.zeros((group, LANES), jnp.uint32)
            return end

        total = lax.fori_loop(0, MOE_EXPERTS, offsets, 0)
        off_ref[MOE_EXPERTS] = total
        tail = MOE_BLOCKS[-1] * PACK_SLABS
        xys[pl.ds(pl.multiple_of(total, group), tail), :] = jnp.zeros((tail, LANES), jnp.uint32)

        def distribute_chunk(c, carry):
            @pl.when(c + 1 < n_chunks)
            def _():
                in_copy(c + 1).start()
            in_copy(c).wait()

            g0 = c * (chunk // SUBLANES)
            shift = in_buf(c) - c * (chunk * PACK_SLABS)

            def distribute(tg, carry2):
                for u in range(SUBLANES):
                    t = tg * SUBLANES + u
                    d1 = pl.multiple_of(tab_ref[2 * tile + t], PACK_SLABS)
                    d2 = pl.multiple_of(tab_ref[3 * tile + t], PACK_SLABS)
                    row = xin[pl.ds(pl.multiple_of(shift + t * PACK_SLABS, PACK_SLABS), PACK_SLABS), :]
                    xys[pl.ds(d1, PACK_SLABS), :] = row
                    xys[pl.ds(d2, PACK_SLABS), :] = row
                return carry2

            lax.fori_loop(g0, g0 + rolled(chunk // SUBLANES), distribute, 0)
            return carry

        lax.fori_loop(0, rolled(n_chunks), distribute_chunk, 0)

    def expert_block(refs, base, rows, valid):
        wg_ref, wu_ref, wd_ref = refs
        words = [xys[pl.ds(base + c, rows, stride=PACK_SLABS), :] for c in range(PACK_SLABS)]
        parts = [_unpack_pair(w) for w in words]
        x_hi = jnp.concatenate([p[0].astype(BF16) for p in parts], axis=1)
        x_lo = jnp.concatenate([p[1].astype(BF16) for p in parts], axis=1)
        hg = _dot(x_hi, wg_ref[0:half, :]) + _dot(x_lo, wg_ref[half:2 * half, :])
        hu = _dot(x_hi, wu_ref[0:half, :]) + _dot(x_lo, wu_ref[half:2 * half, :])
        y = _dot((_silu(hg) * hu).astype(BF16), wd_ref[...])
        for c in range(PACK_SLABS):
            ybuf[pl.ds(c, rows, stride=PACK_SLABS), :] = _pack_pair(
                y[:, c * LANES:(c + 1) * LANES], y[:, half + c * LANES:half + (c + 1) * LANES])

        def copy(g, c):
            r = pl.multiple_of(g * group, group)
            xys[pl.ds(base + r, group), :] = ybuf[pl.ds(r, group), :]
            return c

        lax.fori_loop(0, valid // group, copy, 0)

    def expert_step(j, carry):
        step = i * MOE_EXPERTS + j
        slot = step % MOE_WEIGHT_BUFFERS
        nxt = step + MOE_WEIGHT_BUFFERS - 1

        @pl.when(nxt < n_steps)
        def _():
            for cp in weight_copies(nxt % MOE_EXPERTS, nxt % MOE_WEIGHT_BUFFERS):
                cp.start()

        for cp in weight_copies(j, slot):
            cp.wait()
        refs = (wg_buf.at[slot], wu_buf.at[slot], wd_buf.at[slot])
        start = off_ref[j]
        n = off_ref[j + 1] - start
        lo = 0
        for rows in MOE_BLOCKS:
            hi = rows * PACK_SLABS

            @pl.when(jnp.logical_and(n > lo, n <= hi))
            def _(rows=rows):
                expert_block(refs, pl.multiple_of(start, group), rows, n)
            lo = hi

        @pl.when(n > lo)
        def _():
            span = MOE_BLOCKS[0] * PACK_SLABS

            def block(bi, carry2):
                expert_block(refs, pl.multiple_of(start + bi * span, group), MOE_BLOCKS[0],
                             jnp.minimum(span, n - bi * span))
                return carry2

            lax.fori_loop(0, (n + span - 1) // span, block, 0)
        return carry

    def combine_rows():
        def combine_chunk(c, carry):
            @pl.when(c >= 2)
            def _():
                out_copy(c - 2).wait()

            g0 = c * (chunk // SUBLANES)
            shift = out_buf(c) - c * (chunk * OUT_SLABS)

            def combine(tg, carry2):
                for u in range(SUBLANES):
                    t = tg * SUBLANES + u
                    g1 = lax.bitcast_convert_type(tab_ref[4 * tile + t], F32)
                    g2 = lax.bitcast_convert_type(tab_ref[5 * tile + t], F32)
                    a_hi, a_lo = _unpack_pair(
                        xys[pl.ds(pl.multiple_of(tab_ref[2 * tile + t], PACK_SLABS), PACK_SLABS), :])
                    b_hi, b_lo = _unpack_pair(
                        xys[pl.ds(pl.multiple_of(tab_ref[3 * tile + t], PACK_SLABS), PACK_SLABS), :])
                    xout[pl.ds(pl.multiple_of(shift + t * OUT_SLABS, OUT_SLABS), OUT_SLABS), :] = jnp.concatenate(
                        [g1 * a_hi + g2 * b_hi, g1 * a_lo + g2 * b_lo], axis=0)
                return carry2

            lax.fori_loop(g0, g0 + rolled(chunk // SUBLANES), combine, 0)
            out_copy(c).start()
            return carry

        lax.fori_loop(0, rolled(n_chunks), combine_chunk, 0)
        for c in range(max(0, n_chunks - 2), n_chunks):
            out_copy(c).wait()

    sort_rows()
    lax.fori_loop(0, rolled(MOE_EXPERTS), expert_step, 0)
    combine_rows()


def _moe(xp, tab, cnt, wg, wu, wd, tile):
    t = xp.shape[0] // PACK_SLABS
    d = D_MODEL
    n_tiles = t // tile
    chunk = min(MOE_CHUNK, tile)
    seg_rows = 2 * tile + MOE_EXPERTS * SUBLANES + MOE_BLOCKS[-1]
    grid_spec = pltpu.PrefetchScalarGridSpec(
        num_scalar_prefetch=1,
        grid=(n_tiles,),
        in_specs=[pl.BlockSpec((SUBLANES * tile,), lambda i, c: (i,), memory_space=pltpu.SMEM,
                               pipeline_mode=pl.Buffered(1)),
                  pl.BlockSpec(memory_space=pl.ANY), pl.BlockSpec(memory_space=pl.ANY),
                  pl.BlockSpec(memory_space=pl.ANY), pl.BlockSpec(memory_space=pl.ANY)],
        out_specs=pl.BlockSpec(memory_space=pl.ANY),
        scratch_shapes=[pltpu.VMEM((seg_rows * PACK_SLABS, LANES), jnp.uint32),
                        pltpu.VMEM((MOE_BLOCKS[-1] * PACK_SLABS, LANES), jnp.uint32),
                        pltpu.SMEM((LANES,), jnp.int32),
                        pltpu.VMEM((MOE_WEIGHT_BUFFERS, d, MOE_D_FF), BF16),
                        pltpu.VMEM((MOE_WEIGHT_BUFFERS, d, MOE_D_FF), BF16),
                        pltpu.VMEM((MOE_WEIGHT_BUFFERS, MOE_D_FF, d), BF16),
                        pltpu.VMEM((2 * chunk * PACK_SLABS, LANES), jnp.uint32),
                        pltpu.VMEM((2 * chunk * OUT_SLABS, LANES), F32),
                        pltpu.SemaphoreType.DMA((3, MOE_WEIGHT_BUFFERS)),
                        pltpu.SemaphoreType.DMA((2,)), pltpu.SemaphoreType.DMA((2,))],
    )
    return pl.pallas_call(
        functools.partial(_moe_kernel, tile=tile, chunk=chunk, n_steps=n_tiles * MOE_EXPERTS),
        grid_spec=grid_spec,
        out_shape=jax.ShapeDtypeStruct((t * OUT_SLABS, LANES), F32),
        compiler_params=pltpu.CompilerParams(dimension_semantics=("arbitrary",),
                                             vmem_limit_bytes=MOE_VMEM_LIMIT),
        name="moe_experts",
    )(cnt, tab, xp, wg, wu, wd)


def _final_kernel(x1_ref, moe_ref, p_ref, g_ref, b_ref, wpg_ref, wpp_ref, out_ref, *, alpha):
    rows = x1_ref.shape[0]
    moe = jnp.concatenate([moe_ref[pl.ds(c, rows, stride=OUT_SLABS), :] for c in range(OUT_SLABS)], axis=1)
    x2 = _layer_norm(alpha * x1_ref[...] + moe, g_ref[...], b_ref[...])
    gate = _sigmoid(_dot(x2.astype(BF16), wpg_ref[...]))
    out_ref[...] = x2 + gate * _dot(p_ref[...].astype(BF16), wpp_ref[...])


def _final(x1, moe, p2d, g, b, wpg, wpp, alpha):
    t, d = x1.shape
    rows = min(FINAL_ROWS, t)
    const = lambda shape: pl.BlockSpec(shape, lambda i: (0,) * len(shape))
    rowblk = lambda w: pl.BlockSpec((rows, w), lambda i: (i, 0))
    return pl.pallas_call(
        functools.partial(_final_kernel, alpha=alpha),
        grid=(t // rows,),
        in_specs=[rowblk(d), pl.BlockSpec((rows * OUT_SLABS, LANES), lambda i: (i, 0)), rowblk(p2d.shape[1]),
                  const(g.shape), const(b.shape), const(wpg.shape), const(wpp.shape)],
        out_specs=rowblk(d),
        out_shape=jax.ShapeDtypeStruct((t, d), F32),
        compiler_params=pltpu.CompilerParams(dimension_semantics=("arbitrary",), vmem_limit_bytes=VMEM_LIMIT),
        name="final_ln_ple",
    )(x1, moe, p2d, g, b, wpg, wpp)


def _proj_weights_kernel(w_ref, ssd_ref, lstm_ref, gate_ref):
    w = w_ref[...]
    cols = [0]
    for sz in IN_PROJ_SIZES:
        cols.append(cols[-1] + sz)
    lane = lax.broadcasted_iota(jnp.int32, (w.shape[0], LANES), 1)
    gate_block = lambda c: jnp.where(lane < LSTM_HEADS, w[:, c:c + LANES], 0.0)
    ssd_ref[...] = w[:, 0:cols[2] + LANES].astype(BF16)
    lstm_ref[...] = jnp.concatenate(
        [w[:, cols[3]:cols[4]] * (LSTM_QK ** -0.5), w[:, cols[4]:cols[7]], gate_block(cols[7]), gate_block(cols[8])],
        axis=1).astype(BF16)
    gate_ref[...] = w[:, cols[9]:cols[11]].astype(BF16)


def _proj_weights(w_all, layer):
    _, d, n = w_all.shape
    rows = LANES
    widths = (IN_PROJ_SIZES[0] + IN_PROJ_SIZES[1] + LANES, 2 * LSTM_D_QK + 2 * LSTM_D_V + 2 * LANES, 2 * D_MODEL)
    return pl.pallas_call(
        _proj_weights_kernel,
        grid=(d // rows,),
        in_specs=[pl.BlockSpec((None, rows, n), lambda i: (layer, i, 0))],
        out_specs=[pl.BlockSpec((rows, wd), lambda i: (i, 0)) for wd in widths],
        out_shape=[jax.ShapeDtypeStruct((d, wd), BF16) for wd in widths],
        compiler_params=pltpu.CompilerParams(dimension_semantics=("arbitrary",), vmem_limit_bytes=VMEM_LIMIT),
        name="proj_weights",
    )(w_all.astype(F32))


def _pad_lanes(w, width=LANES):
    return jnp.pad(w, ((0, 0), (0, width - w.shape[1])))


def _row(v, width=None):
    v = v.astype(F32).reshape(1, -1)
    return v if width is None else _pad_lanes(v, width)


def kernel(x, p, w_in, ssm_conv_w, ssm_conv_b, ssm_dt_bias, ssm_a_log, ssm_d, ssm_norm_w, lstm_i_bias, lstm_f_bias, lstm_norm_w, w_branch_ssm, w_branch_lstm, w_out, ln1_g, ln1_b, moe_w_group, moe_b_group, moe_w_expert, moe_b_expert, moe_w_gate, moe_w_up, moe_w_down, ln2_g, ln2_b, ple_w_proj, ple_w_gate):
    depth = w_in.shape[0]
    bsz, seq, d = x.shape
    t = bsz * seq
    alpha = (2.0 * depth) ** 0.25
    head_expand = (jnp.arange(LANES, dtype=jnp.int32)[:, None]
                   == jnp.arange(SSM_D_INNER, dtype=jnp.int32)[None, :] // SSM_HEAD_DIM).astype(BF16)
    for i in range(depth):
        bf = lambda w: w.astype(BF16)
        w_ssd, w_lstm, w_gates = _proj_weights(w_in, i)
        y_ssm = _ssd_mixer(
            x, w_ssd,
            ssm_conv_w[i].astype(F32), _row(ssm_conv_b[i]), _row(ssm_dt_bias[i], LANES),
            _row(-jnp.exp(ssm_a_log[i].astype(F32)), LANES),
            _row(jnp.repeat(ssm_d[i].astype(F32), SSM_HEAD_DIM)), _row(ssm_norm_w[i]), head_expand)
        b_if = jnp.concatenate([_row(lstm_i_bias[i], LANES), _row(lstm_f_bias[i], LANES)], axis=1)
        y_lstm, e_gate, e_up, e_down = _mlstm_mixer(
            x, w_lstm, b_if, _row(lstm_norm_w[i]),
            moe_w_gate[i].astype(F32), moe_w_up[i].astype(F32), moe_w_down[i].astype(F32))
        w_r = _pad_lanes(jnp.concatenate([moe_w_group[i], moe_w_expert[i]], axis=1).astype(F32)).T
        w_r_hi = w_r.astype(BF16)
        w_r_cat = jnp.concatenate([w_r_hi, (w_r - w_r_hi.astype(F32)).astype(BF16)], axis=0)
        b_r = jnp.broadcast_to(_row(jnp.concatenate([moe_b_group[i], moe_b_expert[i]]), LANES).T, (LANES, DENSE_ROWS))
        tile = min(MOE_TILE, t)
        x1, xp, tab, cnt = _merge(
            x.reshape(t, d), y_ssm.reshape(t, -1), y_lstm.reshape(t, -1),
            w_gates, bf(w_branch_ssm[i]), bf(w_branch_lstm[i]), bf(w_out[i]),
            _row(ln1_g[i]), _row(ln1_b[i]), w_r_cat, b_r, alpha, tile)
        cnt = cnt.reshape(t // tile, LANES, LANES)[:, :MOE_EXPERTS, 0].reshape(-1)
        moe = _moe(xp, _route_rows(tab, cnt, tile).reshape(-1), cnt, e_gate, e_up, e_down, tile)
        x = _final(x1, moe, p[i].reshape(t, -1), _row(ln2_g[i]), _row(ln2_b[i]),
                   bf(ple_w_gate[i]), bf(ple_w_proj[i]), alpha).reshape(bsz, seq, d)
    return x
```
